```python
import math, functools
import jax, jax.numpy as jnp
from jax import lax
import numpy as np

D_MODEL = 1024
BATCH = 8
SEQ = 2048
DEPTH = 1
DEC_BATCH = 32
DEC_SEQ = 8
PAST_LEN = 8192
PAGE_SIZE = 128

RET_HEADS = 4
RET_DK = 64
RET_DV = 128
RET_CHUNK = 128
MOBA_HEADS = 4
MOBA_HEAD_DIM = 128
MOBA_BLOCK = 256
MOBA_TOPK = 3
Q_BLOCK = 128
D_FF = 4 * D_MODEL
ROPE_THETA = 10000.0
LN_EPS = 1e-5
GN_EPS = 1e-6
ALPHA = (2 * DEPTH) ** 0.25
BETA = (8 * DEPTH) ** -0.25
RET_WIDTH = RET_HEADS * RET_DV
MOBA_WIDTH = MOBA_HEADS * MOBA_HEAD_DIM
MIX_WIDTH = RET_WIDTH + MOBA_WIDTH
SPLIT_SIZES = (RET_HEADS * RET_DK, RET_HEADS * RET_DK, RET_WIDTH, RET_WIDTH,
               MOBA_WIDTH, MOBA_WIDTH, MOBA_WIDTH)
IN_WIDTH = sum(SPLIT_SIZES)

kernel_name = "hymba_retention_moba_adaln_deepnorm_step"


def layer_norm(x, g, b):
    xf = x.astype(jnp.float32)
    mu = jnp.mean(xf, axis=-1, keepdims=True)
    var = jnp.mean(jnp.square(xf - mu), axis=-1, keepdims=True)
    y = (xf - mu) * lax.rsqrt(var + LN_EPS) * g.astype(jnp.float32) + b.astype(jnp.float32)
    return y.astype(x.dtype)


def rope(x, pos):
    half = x.shape[-1] // 2
    inv_freq = jnp.power(ROPE_THETA, -jnp.arange(half, dtype=jnp.float32) / half)
    ang = pos.astype(jnp.float32)[:, None] * inv_freq[None, :]
    cos = jnp.cos(ang)[None, :, None, :]
    sin = jnp.sin(ang)[None, :, None, :]
    xf = x.astype(jnp.float32)
    x1, x2 = xf[..., :half], xf[..., half:]
    return jnp.concatenate([x1 * cos - x2 * sin, x2 * cos + x1 * sin], axis=-1).astype(x.dtype)


def adaln(c, w, b):
    m = jax.nn.silu(c) @ w + b
    return jnp.split(m[:, None, :], 6, axis=-1)


def project(h, w_in, pos):
    bsz, s, _ = h.shape
    z = h @ w_in
    idx = [int(i) for i in np.cumsum(SPLIT_SIZES)[:-1]]
    rq, rk, rv, rg, mq, mk, mv = jnp.split(z, idx, axis=-1)
    rq = rope(rq.reshape(bsz, s, RET_HEADS, RET_DK), pos)
    rk = rope(rk.reshape(bsz, s, RET_HEADS, RET_DK), pos) * (RET_DK ** -0.5)
    rv = rv.reshape(bsz, s, RET_HEADS, RET_DV)
    rg = rg.reshape(bsz, s, RET_HEADS, RET_DV)
    mq = rope(mq.reshape(bsz, s, MOBA_HEADS, MOBA_HEAD_DIM), pos)
    mk = rope(mk.reshape(bsz, s, MOBA_HEADS, MOBA_HEAD_DIM), pos)
    mv = mv.reshape(bsz, s, MOBA_HEADS, MOBA_HEAD_DIM)
    return rq, rk, rv, rg, mq, mk, mv


def retention_log_decay():
    return jnp.log1p(-jnp.exp2(-5.0 - jnp.arange(RET_HEADS, dtype=jnp.float32)))


def retention_chunk(state, q, k, v, log_g):
    q = q.astype(jnp.float32)
    k = k.astype(jnp.float32)
    v = v.astype(jnp.float32)
    state = state.astype(jnp.float32)
    c = q.shape[1]
    i = jnp.arange(c, dtype=jnp.float32)
    diff = i[:, None] - i[None, :]
    causal = diff >= 0
    dmat = jnp.where(causal[None], jnp.exp(jnp.where(causal, diff, 0.0)[None] * log_g[:, None, None]), 0.0)
    scores = jnp.einsum("bihd,bjhd->bhij", q, k) * dmat[None]
    o_inner = jnp.einsum("bhij,bjhe->bihe", scores, v)
    q_decay = jnp.exp((i[:, None] + 1.0) * log_g[None, :])
    o_cross = jnp.einsum("bihd,bhde->bihe", q * q_decay[None, :, :, None], state)
    k_decay = jnp.exp((c - 1.0 - i)[:, None] * log_g[None, :])
    new_state = (jnp.exp(c * log_g)[None, :, None, None] * state
                 + jnp.einsum("bjhd,bjhe->bhde", k * k_decay[None, :, :, None], v))
    return o_inner + o_cross, new_state


def retention_prompt(q, k, v, log_g):
    bsz, s, h, dk = q.shape
    n_chunks = s // RET_CHUNK

    def to_chunks(t):
        return t.reshape(bsz, n_chunks, RET_CHUNK, h, t.shape[-1]).swapaxes(0, 1)

    def step(state, qkv):
        o, new_state = retention_chunk(state, qkv[0], qkv[1], qkv[2], log_g)
        return new_state, o

    state0 = jnp.zeros((bsz, h, dk, RET_DV), jnp.float32)
    state, o = lax.scan(step, state0, (to_chunks(q), to_chunks(k), to_chunks(v)))
    return o.swapaxes(0, 1).reshape(bsz, s, h, RET_DV), state


def retention_readout(o, g, dtype):
    bsz, s = o.shape[0], o.shape[1]
    mu = jnp.mean(o, axis=-1, keepdims=True)
    var = jnp.mean(jnp.square(o - mu), axis=-1, keepdims=True)
    y = (o - mu) * lax.rsqrt(var + GN_EPS) * jax.nn.silu(g.astype(jnp.float32))
    return y.reshape(bsz, s, RET_WIDTH).astype(dtype)


def moba_blocks(k, v):
    bsz, length, h, d = k.shape
    n_blocks = -(-length // MOBA_BLOCK)
    pad = n_blocks * MOBA_BLOCK - length

    def blockify(t):
        t = jnp.pad(t, ((0, 0), (0, pad), (0, 0), (0, 0)))
        return t.reshape(bsz, n_blocks, MOBA_BLOCK, h, d).transpose(0, 3, 1, 2, 4)

    kb = blockify(k)
    vb = blockify(v)
    kmean = jnp.mean(kb.astype(jnp.float32), axis=3)
    return kb, vb, kmean


def moba_attend(q, q_pos, kb, vb, kmean):
    bsz, nq, h, d = q.shape
    n_blocks = kb.shape[2]
    n_top = min(MOBA_TOPK, n_blocks)
    own = q_pos // MOBA_BLOCK
    gate = jnp.einsum("bqhd,bhnd->bqhn", q.astype(jnp.float32), kmean)
    past_ok = jnp.arange(n_blocks, dtype=jnp.int32)[None, :] < own[:, None]
    gate = jnp.where(past_ok[None, :, None, :], gate, -jnp.inf)
    _, top = lax.top_k(gate, n_top)
    top_valid = top < own[None, :, None, None]
    own_idx = jnp.broadcast_to(own[None, :, None, None], (bsz, nq, h, 1)).astype(top.dtype)
    idx = jnp.concatenate([top, own_idx], axis=-1)
    valid = jnp.concatenate([top_valid, jnp.ones((bsz, nq, h, 1), bool)], axis=-1)
    bi = jnp.arange(bsz)[:, None, None, None]
    hi = jnp.arange(h)[None, None, :, None]
    ksel = kb[bi, hi, idx]
    vsel = vb[bi, hi, idx]
    key_pos = idx[..., None] * MOBA_BLOCK + jnp.arange(MOBA_BLOCK, dtype=idx.dtype)
    mask = valid[..., None] & (key_pos <= q_pos[None, :, None, None, None])
    logits = jnp.einsum("bqhd,bqhsjd->bqhsj", q, ksel, preferred_element_type=jnp.float32) * (d ** -0.5)
    logits = jnp.where(mask, logits, -jnp.inf).reshape(bsz, nq, h, -1)
    p = jax.nn.softmax(logits, axis=-1).reshape(mask.shape)
    out = jnp.einsum("bqhsj,bqhsjd->bqhd", p.astype(vsel.dtype), vsel, preferred_element_type=jnp.float32)
    return out.astype(q.dtype)


def moba_prompt(q, k, v):
    bsz, s, h, d = q.shape
    kb, vb, kmean = moba_blocks(k, v)
    n_qb = s // Q_BLOCK
    qb = q.reshape(bsz, n_qb, Q_BLOCK, h, d).swapaxes(0, 1)
    posb = jnp.arange(s, dtype=jnp.int32).reshape(n_qb, Q_BLOCK)
    out = lax.map(lambda a: moba_attend(a[0], a[1], kb, vb, kmean), (qb, posb))
    return out.swapaxes(0, 1).reshape(bsz, s, h, d)


def moba_sample(q, k_new, v_new, cache_k, cache_v, page_table):
    dbs, t, h, d = q.shape
    past_len = page_table.shape[1] * PAGE_SIZE
    k_past = cache_k[page_table].reshape(dbs, past_len, h, d)
    v_past = cache_v[page_table].reshape(dbs, past_len, h, d)
    k_all = jnp.concatenate([k_past, k_new.astype(k_past.dtype)], axis=1)
    v_all = jnp.concatenate([v_past, v_new.astype(v_past.dtype)], axis=1)
    kb, vb, kmean = moba_blocks(k_all, v_all)
    q_pos = past_len + jnp.arange(t, dtype=jnp.int32)
    return moba_attend(q, q_pos, kb, vb, kmean)


def mixer_sublayer(x, shift, scale, gate, pos, w_in, w_o, ln_g, ln_b, log_g, retention_fn, moba_fn):
    bsz, s, _ = x.shape
    h = x * (1.0 + scale) + shift
    rq, rk, rv, rg, mq, mk, mv = project(h, w_in, pos)
    o_ret, ret_state = retention_fn(rq, rk, rv, log_g)
    o_moba = moba_fn(mq, mk, mv)
    mixed = jnp.concatenate([retention_readout(o_ret, rg, x.dtype),
                             o_moba.reshape(bsz, s, MOBA_WIDTH)], axis=-1) @ w_o
    x = layer_norm(ALPHA * x + gate * mixed, ln_g, ln_b)
    return x, mk, mv, ret_state


def ffn_sublayer(x, shift, scale, gate, w_up, w_down, ln_g, ln_b):
    h = x * (1.0 + scale) + shift
    u = jnp.square(jax.nn.relu(h @ w_up))
    return layer_norm(ALPHA * x + gate * (u @ w_down), ln_g, ln_b)


def setup_inputs(seed: int = 0) -> dict:
    key = jax.random.key(seed)
    ks = jax.random.split(key, 20)
    n_pages = PAST_LEN // PAGE_SIZE
    n_used = DEC_BATCH * n_pages
    n_pool = n_used + (n_used + 3) // 4
    f32 = jnp.float32
    x_prompt = jax.random.normal(ks[0], (BATCH, SEQ, D_MODEL), f32)
    x_sample = jax.random.normal(ks[1], (DEC_BATCH, DEC_SEQ, D_MODEL), f32)
    cache_k = jax.random.normal(ks[2], (DEPTH, n_pool, PAGE_SIZE, MOBA_HEADS, MOBA_HEAD_DIM), f32)
    cache_v = jax.random.normal(ks[3], (DEPTH, n_pool, PAGE_SIZE, MOBA_HEADS, MOBA_HEAD_DIM), f32)
    state_ret = jax.random.normal(ks[4], (DEPTH, DEC_BATCH, RET_HEADS, RET_DK, RET_DV), f32)
    page_table = jax.random.permutation(ks[5], n_pool)[:n_used].reshape(DEC_BATCH, n_pages).astype(jnp.int32)
    c_prompt = jax.random.normal(ks[6], (BATCH, D_MODEL), f32)
    c_sample = jax.random.normal(ks[7], (DEC_BATCH, D_MODEL), f32)
    w_ada = jax.random.normal(ks[8], (DEPTH, D_MODEL, 6 * D_MODEL), f32) * D_MODEL ** -0.5
    b_ada = 0.02 * jax.random.normal(ks[9], (DEPTH, 6 * D_MODEL), f32)
    col_scale = jnp.concatenate([
        jnp.ones((2 * RET_HEADS * RET_DK,), f32), jnp.full((RET_WIDTH,), BETA, f32),
        jnp.ones((RET_WIDTH + 2 * MOBA_WIDTH,), f32), jnp.full((MOBA_WIDTH,), BETA, f32)])
    w_in = jax.random.normal(ks[10], (DEPTH, D_MODEL, IN_WIDTH), f32) * D_MODEL ** -0.5 * col_scale
    w_o = jax.random.normal(ks[11], (DEPTH, MIX_WIDTH, D_MODEL), f32) * MIX_WIDTH ** -0.5 * BETA
    ln1_g = 1.0 + 0.02 * jax.random.normal(ks[12], (DEPTH, D_MODEL), f32)
    ln1_b = 0.02 * jax.random.normal(ks[13], (DEPTH, D_MODEL), f32)
    w_up = jax.random.normal(ks[14], (DEPTH, D_MODEL, D_FF), f32) * D_MODEL ** -0.5
    w_down = jax.random.normal(ks[15], (DEPTH, D_FF, D_MODEL), f32) * D_FF ** -0.5 * BETA
    ln2_g = 1.0 + 0.02 * jax.random.normal(ks[16], (DEPTH, D_MODEL), f32)
    ln2_b = 0.02 * jax.random.normal(ks[17], (DEPTH, D_MODEL), f32)
    return {"x_prompt": x_prompt, "x_sample": x_sample, "cache_k": cache_k, "cache_v": cache_v,
            "state_ret": state_ret, "page_table": page_table, "c_prompt": c_prompt,
            "c_sample": c_sample, "w_ada": w_ada, "b_ada": b_ada, "w_in": w_in, "w_o": w_o,
            "ln1_g": ln1_g, "ln1_b": ln1_b, "w_up": w_up, "w_down": w_down,
            "ln2_g": ln2_g, "ln2_b": ln2_b}


def reference(x_prompt, x_sample, cache_k, cache_v, state_ret, page_table, c_prompt, c_sample,
              w_ada, b_ada, w_in, w_o, ln1_g, ln1_b, w_up, w_down, ln2_g, ln2_b):
    log_g = retention_log_decay()
    pos_p = jnp.arange(x_prompt.shape[1], dtype=jnp.int32)
    pos_s = page_table.shape[1] * PAGE_SIZE + jnp.arange(x_sample.shape[1], dtype=jnp.int32)
    xp, xs = x_prompt, x_sample
    k_p, v_p, s_p, k_s, v_s, s_s = [], [], [], [], [], []
    for layer in range(DEPTH):
        mod_p = adaln(c_prompt, w_ada[layer], b_ada[layer])
        mod_s = adaln(c_sample, w_ada[layer], b_ada[layer])
        xp, kp, vp, sp = mixer_sublayer(
            xp, mod_p[0], mod_p[1], mod_p[2], pos_p, w_in[layer], w_o[layer],
            ln1_g[layer], ln1_b[layer], log_g, retention_prompt, moba_prompt)
        st_l, ck_l, cv_l = state_ret[layer], cache_k[layer], cache_v[layer]
        xs, ksn, vsn, ssn = mixer_sublayer(
            xs, mod_s[0], mod_s[1], mod_s[2], pos_s, w_in[layer], w_o[layer],
            ln1_g[layer], ln1_b[layer], log_g,
            lambda q, k, v, lg: retention_chunk(st_l, q, k, v, lg),
            lambda q, k, v: moba_sample(q, k, v, ck_l, cv_l, page_table))
        xp = ffn_sublayer(xp, mod_p[3], mod_p[4], mod_p[5], w_up[layer], w_down[layer], ln2_g[layer], ln2_b[layer])
        xs = ffn_sublayer(xs, mod_s[3], mod_s[4], mod_s[5], w_up[layer], w_down[layer], ln2_g[layer], ln2_b[layer])
        k_p.append(kp)
        v_p.append(vp)
        s_p.append(sp)
        k_s.append(ksn)
        v_s.append(vsn)
        s_s.append(ssn)
    return (xp, xs, jnp.stack(k_p), jnp.stack(v_p), jnp.stack(s_p), jnp.stack(k_s), jnp.stack(v_s), jnp.stack(s_s))
```

```python
import functools

import numpy as np
import jax
import jax.numpy as jnp
from jax import lax
from jax.experimental import pallas as pl
from jax.experimental.pallas import tpu as pltpu

F32 = jnp.float32
BF16 = jnp.bfloat16

D_MODEL = 1024
BATCH = 8
SEQ = 2048
DEC_BATCH = 32
DEC_SEQ = 8
PAGE_SIZE = 128
RET_HEADS = 4
RET_DK = 64
RET_DV = 128
RET_CHUNK = 128
MOBA_HEADS = 4
MOBA_HEAD_DIM = 128
MOBA_BLOCK = 256
MOBA_TOPK = 3
D_FF = 4 * D_MODEL
ROPE_THETA = 10000.0
LN_EPS = 1e-5
GN_EPS = 1e-6
DEPTH = 1
ALPHA = (2 * DEPTH) ** 0.25
RET_QK = RET_HEADS * RET_DK
RET_WIDTH = RET_HEADS * RET_DV
MOBA_WIDTH = MOBA_HEADS * MOBA_HEAD_DIM
IN_WIDTH = 2 * RET_QK + 2 * RET_WIDTH + 3 * MOBA_WIDTH
OFF_RQ, OFF_RK, OFF_RV, OFF_RG = 0, 256, 512, 1024
OFF_MQ, OFF_MK, OFF_MV = 1536, 2048, 2560
LANES = 128
N_SAMPLE_TOK = DEC_BATCH * DEC_SEQ
PAGES_PER_SEQ = 64
PAGES_PER_STEP = 8
PAGE_ROWS = PAGE_SIZE * MOBA_HEADS
NEG_INF = float("-inf")

_NT = (((1,), (1,)), ((), ()))
_TN = (((0,), (0,)), ((), ()))


def _log_decay():
    return np.log1p(-np.exp2(-5.0 - np.arange(RET_HEADS, dtype=np.float64)))


def _adaln_body(c_ref, w_ref, b_ref, o_ref):
    c = c_ref[...]
    a = (c * jax.nn.sigmoid(c)).astype(BF16)
    o_ref[...] = jnp.dot(a, w_ref[...].astype(BF16), preferred_element_type=F32) + b_ref[...]


def _adaln(c_all, w_ada, b_ada):
    n = c_all.shape[0]
    tn = 1024
    return pl.pallas_call(
        _adaln_body,
        grid=(6 * D_MODEL // tn,),
        in_specs=[pl.BlockSpec((n, D_MODEL), lambda j: (0, 0)),
                  pl.BlockSpec((D_MODEL, tn), lambda j: (0, j)),
                  pl.BlockSpec((1, tn), lambda j: (0, j))],
        out_specs=pl.BlockSpec((n, tn), lambda j: (0, j)),
        out_shape=jax.ShapeDtypeStruct((n, 6 * D_MODEL), F32),
        name="adaln",
    )(c_all, w_ada, b_ada)


def _rope_tables(pos, head_dim):
    half = head_dim // 2
    inv_freq = jnp.power(ROPE_THETA, -jnp.arange(half, dtype=F32) / half)
    ang = pos.astype(F32)[:, None] * inv_freq[None, :]
    cos, sin = jnp.cos(ang), jnp.sin(ang)
    reps = LANES // head_dim
    cos_t = jnp.tile(jnp.concatenate([cos, cos], axis=-1), (1, reps))
    sin_t = jnp.tile(jnp.concatenate([-sin, sin], axis=-1), (1, reps))
    return cos_t, sin_t


def _inproj_body(x_ref, sc_ref, sh_ref, w_ref, cm_ref, sm_ref, cr_ref, sr_ref,
                 rq_ref, rk_ref, rv_ref, rg_ref, mq_ref, ko_ref, vo_ref):
    tm = x_ref.shape[0]
    h = (x_ref[...] * (1.0 + sc_ref[...]) + sh_ref[...]).astype(BF16)

    def proj(lo, width):
        return jnp.dot(h, w_ref[:, lo:lo + width], preferred_element_type=F32)

    lane = lax.broadcasted_iota(jnp.int32, (tm, LANES), 1)
    low_half = (lane & (RET_DK - 1)) < (RET_DK // 2)
    cr, sr = cr_ref[...], sr_ref[...]
    cm, sm = cm_ref[...], sm_ref[...]

    def rope_ret(z):
        rot = jnp.where(low_half, pltpu.roll(z, LANES - RET_DK // 2, 1), pltpu.roll(z, RET_DK // 2, 1))
        return z * cr + rot * sr

    def rope_moba(z):
        return z * cm + pltpu.roll(z, MOBA_HEAD_DIM // 2, 1) * sm

    zq = proj(OFF_RQ, RET_QK)
    zk = proj(OFF_RK, RET_QK)
    for s in range(RET_QK // LANES):
        sl = slice(s * LANES, (s + 1) * LANES)
        rq_ref[:, sl] = rope_ret(zq[:, sl])
        rk_ref[:, sl] = rope_ret(zk[:, sl]) * (RET_DK ** -0.5)
    rv_ref[...] = proj(OFF_RV, RET_WIDTH).astype(rv_ref.dtype)
    rg_ref[...] = proj(OFF_RG, RET_WIDTH)
    zq = proj(OFF_MQ, MOBA_WIDTH)
    zk = proj(OFF_MK, MOBA_WIDTH)
    zv = proj(OFF_MV, MOBA_WIDTH)
    for hd in range(MOBA_HEADS):
        sl = slice(hd * LANES, (hd + 1) * LANES)
        mq_ref[:, sl] = rope_moba(zq[:, sl]).astype(mq_ref.dtype)
        ko_ref[pl.ds(hd, tm, stride=MOBA_HEADS), :] = rope_moba(zk[:, sl])
        vo_ref[pl.ds(hd, tm, stride=MOBA_HEADS), :] = zv[:, sl]


def _inproj(x2d, mod_specs, mod_args, tab_specs, tabs, w_in, grid, row_map, tm, act_dtype):
    t = x2d.shape[0]
    wide = lambda w: pl.BlockSpec((tm, w), row_map)
    return pl.pallas_call(
        _inproj_body,
        grid=grid,
        in_specs=[wide(D_MODEL)] + mod_specs + [pl.BlockSpec((D_MODEL, IN_WIDTH), lambda *_: (0, 0))] + tab_specs,
        out_specs=[wide(RET_QK), wide(RET_QK), wide(RET_WIDTH), wide(RET_WIDTH), wide(MOBA_WIDTH),
                   pl.BlockSpec((tm * MOBA_HEADS, LANES), row_map),
                   pl.BlockSpec((tm * MOBA_HEADS, LANES), row_map)],
        out_shape=[jax.ShapeDtypeStruct((t, RET_QK), F32), jax.ShapeDtypeStruct((t, RET_QK), F32),
                   jax.ShapeDtypeStruct((t, RET_WIDTH), act_dtype), jax.ShapeDtypeStruct((t, RET_WIDTH), F32),
                   jax.ShapeDtypeStruct((t, MOBA_WIDTH), act_dtype),
                   jax.ShapeDtypeStruct((t * MOBA_HEADS, LANES), F32),
                   jax.ShapeDtypeStruct((t * MOBA_HEADS, LANES), F32)],
        compiler_params=pltpu.CompilerParams(dimension_semantics=("arbitrary",) * len(grid),
                                             vmem_limit_bytes=48 * 1024 * 1024),
        name="inproj",
    )(x2d, *mod_args, w_in, *tabs)


def _group_norm_gate(o, g):
    mu = jnp.mean(o, axis=-1, keepdims=True)
    d = o - mu
    var = jnp.mean(d * d, axis=-1, keepdims=True)
    return d * lax.rsqrt(var + GN_EPS) * (g * jax.nn.sigmoid(g))


def _ret_prompt_tables():
    lg = _log_decay()
    i = np.arange(RET_CHUNK, dtype=np.float64)
    diff = i[:, None] - i[None, :]
    dmat = np.where(diff >= 0, np.exp(np.maximum(diff, 0.0)[None] * lg[:, None, None]), 0.0)
    lane_head = np.arange(RET_QK) // RET_DK
    qd = np.exp((i[:, None] + 1.0) * lg[lane_head][None, :])
    kd = np.exp((RET_CHUNK - 1.0 - i)[:, None] * lg[lane_head][None, :])
    row_head = np.arange(RET_QK) // RET_DK
    col_head = np.arange(RET_WIDTH) // RET_DV
    same = row_head[:, None] == col_head[None, :]
    cdec = np.where(same, np.exp(RET_CHUNK * lg[row_head])[:, None], 0.0)
    return [jnp.asarray(a, dtype=F32) for a in (dmat, qd, kd, cdec, same.astype(np.float64))]


def _ret_prompt_body(q_ref, k_ref, v_ref, g_ref, dmat_ref, qd_ref, kd_ref, cdec_ref, bdm_ref,
                     o_ref, st_ref, state_scr):
    state_scr[...] = jnp.zeros_like(state_scr)
    lane_head = lax.broadcasted_iota(jnp.int32, (RET_CHUNK, RET_QK), 1) >> 6

    def chunk(c, carry):
        rows = pl.ds(pl.multiple_of(c * RET_CHUNK, RET_CHUNK), RET_CHUNK)
        q = q_ref[rows, :]
        k = k_ref[rows, :]
        v = v_ref[rows, :]
        g = g_ref[rows, :]
        kb = k.astype(BF16)
        state = state_scr[...]
        cross = jnp.dot((q * qd_ref[...]).astype(BF16), state.astype(BF16), preferred_element_type=F32)
        for hd in range(RET_HEADS):
            sl = slice(hd * RET_DV, (hd + 1) * RET_DV)
            qm = jnp.where(lane_head == hd, q, 0.0).astype(BF16)
            s = lax.dot_general(qm, kb, _NT, preferred_element_type=F32) * dmat_ref[hd]
            inner = jnp.dot(s.astype(BF16), v[:, sl], preferred_element_type=F32)
            o_ref[rows, sl] = _group_norm_gate(inner + cross[:, sl], g[:, sl]).astype(o_ref.dtype)
        kv = lax.dot_general((k * kd_ref[...]).astype(BF16), v, _TN, preferred_element_type=F32)
        state_scr[...] = cdec_ref[...] * state + bdm_ref[...] * kv
        return carry

    lax.fori_loop(0, SEQ // RET_CHUNK, chunk, 0)
    for hd in range(RET_HEADS):
        st_ref[hd] = state_scr[hd * RET_DK:(hd + 1) * RET_DK, hd * RET_DV:(hd + 1) * RET_DV]


def _ret_prompt(rq, rk, rv, rg):
    tabs = _ret_prompt_tables()
    seq = lambda w: pl.BlockSpec((SEQ, w), lambda b: (b, 0))
    const = lambda a: pl.BlockSpec(a.shape, lambda b: (0,) * a.ndim)
    return pl.pallas_call(
        _ret_prompt_body,
        grid=(BATCH,),
        in_specs=[seq(RET_QK), seq(RET_QK), seq(RET_WIDTH), seq(RET_WIDTH)] + [const(a) for a in tabs],
        out_specs=[seq(RET_WIDTH), pl.BlockSpec((None, RET_HEADS, RET_DK, RET_DV), lambda b: (b, 0, 0, 0))],
        out_shape=[jax.ShapeDtypeStruct((BATCH * SEQ, RET_WIDTH), BF16),
                   jax.ShapeDtypeStruct((BATCH, RET_HEADS, RET_DK, RET_DV), F32)],
        scratch_shapes=[pltpu.VMEM((RET_QK, RET_WIDTH), F32)],
        compiler_params=pltpu.CompilerParams(dimension_semantics=("arbitrary",),
                                             vmem_limit_bytes=48 * 1024 * 1024),
        name="ret_prompt",
    )(rq, rk, rv, rg, *tabs)


def _ret_sample_tables():
    lg = _log_decay()
    t = np.arange(N_SAMPLE_TOK) % DEC_SEQ
    seq_id = np.arange(N_SAMPLE_TOK) // DEC_SEQ
    diff = (t[:, None] - t[None, :]).astype(np.float64)
    same_seq = seq_id[:, None] == seq_id[None, :]
    dmat = np.where(same_seq[None] & (diff >= 0)[None],
                    np.exp(np.maximum(diff, 0.0)[None] * lg[:, None, None]), 0.0)
    lane_head = np.arange(RET_QK) // RET_DK
    qd = np.exp((t[:, None] + 1.0) * lg[lane_head][None, :])
    kd = np.exp((DEC_SEQ - 1.0 - t)[:, None] * lg[lane_head][None, :])
    return [jnp.asarray(a, dtype=F32) for a in (dmat, qd, kd)]


def _ret_sample_body(q_ref, k_ref, v_ref, g_ref, st_ref, dmat_ref, qd_ref, kd_ref, o_ref, sto_ref):
    lg = _log_decay()
    q = q_ref[...]
    k = k_ref[...]
    kb = k.astype(BF16)
    qdec = q * qd_ref[...]
    kdec = k * kd_ref[...]
    vb = v_ref[...].astype(BF16)
    g = g_ref[...]
    lane = lax.broadcasted_iota(jnp.int32, (N_SAMPLE_TOK, LANES), 1)
    lane_head = lax.broadcasted_iota(jnp.int32, (N_SAMPLE_TOK, RET_QK), 1) >> 6
    n_state_rows = DEC_BATCH * RET_DK
    own_seq = ((lax.broadcasted_iota(jnp.int32, (N_SAMPLE_TOK, n_state_rows), 0) >> 3)
               == (lax.broadcasted_iota(jnp.int32, (N_SAMPLE_TOK, n_state_rows), 1) >> 6))
    for hd in range(RET_HEADS):
        sl = slice(hd * RET_DV, (hd + 1) * RET_DV)
        qm = jnp.where(lane_head == hd, q, 0.0).astype(BF16)
        s = lax.dot_general(qm, kb, _NT, preferred_element_type=F32) * dmat_ref[hd]
        inner = jnp.dot(s.astype(BF16), vb[:, sl], preferred_element_type=F32)

        def expand(z):
            slab = z[:, (hd // 2) * LANES:(hd // 2 + 1) * LANES]
            other = pltpu.roll(slab, RET_DK, 1)
            in_low = lane < RET_DK
            both = jnp.where(in_low, slab, other) if hd % 2 == 0 else jnp.where(in_low, other, slab)
            tiled = jnp.concatenate([both] * (n_state_rows // LANES), axis=1)
            return jnp.where(own_seq, tiled, 0.0).astype(BF16)

        st = st_ref[:, hd].reshape(n_state_rows, RET_DV)
        cross = jnp.dot(expand(qdec), st.astype(BF16), preferred_element_type=F32)
        o_ref[:, sl] = _group_norm_gate(inner + cross, g[:, sl])
        kv = lax.dot_general(expand(kdec), vb[:, sl], _TN, preferred_element_type=F32)
        new = float(np.exp(DEC_SEQ * lg[hd])) * st + kv
        sto_ref[:, hd] = new.reshape(DEC_BATCH, RET_DK, RET_DV)


def _ret_sample(rq, rk, rv, rg, state):
    tabs = _ret_sample_tables()
    full = lambda a: pl.BlockSpec(a.shape, lambda i: (0,) * a.ndim)
    args = (rq, rk, rv, rg, state, *tabs)
    return pl.pallas_call(
        _ret_sample_body,
        grid=(1,),
        in_specs=[full(a) for a in args],
        out_specs=[pl.BlockSpec((N_SAMPLE_TOK, RET_WIDTH), lambda i: (0, 0)),
                   pl.BlockSpec(state.shape, lambda i: (0, 0, 0, 0))],
        out_shape=[jax.ShapeDtypeStruct((N_SAMPLE_TOK, RET_WIDTH), F32),
                   jax.ShapeDtypeStruct(state.shape, F32)],
        compiler_params=pltpu.CompilerParams(dimension_semantics=("arbitrary",),
                                             vmem_limit_bytes=56 * 1024 * 1024),
        name="ret_sample",
    )(*args)


def _moba_prompt_body(q_ref, k_ref, v_ref, e_ref, o_ref):
    hd = pl.program_id(1)
    n_blocks = SEQ // MOBA_BLOCK
    scale = MOBA_HEAD_DIM ** -0.5
    k32 = k_ref[pl.ds(hd, SEQ, stride=MOBA_HEADS), :]
    kb = k32.astype(BF16)
    vb = v_ref[pl.ds(hd, SEQ, stride=MOBA_HEADS), :].astype(BF16)
    kmean = jnp.sum(k32.reshape(n_blocks, MOBA_BLOCK, MOBA_HEAD_DIM), axis=1) * (1.0 / MOBA_BLOCK)
    kmb = kmean.astype(BF16)
    row = lax.broadcasted_iota(jnp.int32, (MOBA_BLOCK, MOBA_BLOCK), 0)
    col = lax.broadcasted_iota(jnp.int32, (MOBA_BLOCK, MOBA_BLOCK), 1)
    causal = col <= row
    blk_row = lax.broadcasted_iota(jnp.int32, (n_blocks, MOBA_BLOCK), 0)

    for i in range(n_blocks):
        qi = q_ref[i * MOBA_BLOCK:(i + 1) * MOBA_BLOCK, :]
        own = slice(i * MOBA_BLOCK, (i + 1) * MOBA_BLOCK)
        s_own = jnp.where(causal, lax.dot_general(qi, kb[own], _NT, preferred_element_type=F32), NEG_INF)
        m = jnp.max(s_own, axis=-1, keepdims=True)
        if i > 0:
            past = slice(0, i * MOBA_BLOCK)
            s_past = lax.dot_general(qi, kb[past], _NT, preferred_element_type=F32)
            if i > MOBA_TOPK:
                gt = lax.dot_general(kmb, qi, _NT, preferred_element_type=F32)
                sel_t = jnp.zeros((n_blocks, MOBA_BLOCK), F32)
                for n in range(i):
                    beats = jnp.zeros((1, MOBA_BLOCK), F32)
                    for mm in range(i):
                        if mm == n:
                            continue
                        win = (gt[mm:mm + 1] >= gt[n:n + 1]) if mm < n else (gt[mm:mm + 1] > gt[n:n + 1])
                        beats = beats + win.astype(F32)
                    sel_t = jnp.where(blk_row == n, (beats < MOBA_TOPK).astype(F32), sel_t)
                keep = lax.dot_general(sel_t.astype(BF16), e_ref[:, past], _TN, preferred_element_type=F32)
                s_past = jnp.where(keep > 0.5, s_past, NEG_INF)
            m = jnp.maximum(m, jnp.max(s_past, axis=-1, keepdims=True))
        e_own = jnp.exp((s_own - m) * scale)
        l = jnp.sum(e_own, axis=-1, keepdims=True)
        acc = jnp.dot(e_own.astype(BF16), vb[own], preferred_element_type=F32)
        if i > 0:
            e_past = jnp.exp((s_past - m) * scale)
            l = l + jnp.sum(e_past, axis=-1, keepdims=True)
            acc = acc + jnp.dot(e_past.astype(BF16), vb[past], preferred_element_type=F32)
        o_ref[own, :] = (acc / l).astype(o_ref.dtype)


def _moba_prompt(mq, k2d, v2d):
    n_blocks = SEQ // MOBA_BLOCK
    e = jnp.asarray(np.arange(SEQ)[None, :] // MOBA_BLOCK == np.arange(n_blocks)[:, None], dtype=BF16)
    kv_spec = pl.BlockSpec((SEQ * MOBA_HEADS, LANES), lambda b, h: (b, 0))
    return pl.pallas_call(
        _moba_prompt_body,
        grid=(BATCH, MOBA_HEADS),
        in_specs=[pl.BlockSpec((SEQ, MOBA_HEAD_DIM), lambda b, h: (b, h)), kv_spec, kv_spec,
                  pl.BlockSpec((n_blocks, SEQ), lambda b, h: (0, 0))],
        out_specs=pl.BlockSpec((SEQ, MOBA_HEAD_DIM), lambda b, h: (b, h)),
        out_shape=jax.ShapeDtypeStruct((BATCH * SEQ, MOBA_WIDTH), BF16),
        compiler_params=pltpu.CompilerParams(dimension_semantics=("arbitrary", "arbitrary"),
                                             vmem_limit_bytes=56 * 1024 * 1024),
        name="moba_prompt",
    )(mq, k2d, v2d, e)


def _moba_sample_body(pt_ref, q_ref, kn_ref, vn_ref, *refs):
    kp = refs[:PAGES_PER_STEP]
    vp = refs[PAGES_PER_STEP:2 * PAGES_PER_STEP]
    o_ref, s_scr, p_scr, acc_scr, l_scr = refs[2 * PAGES_PER_STEP:]
    j = pl.program_id(1)
    k_steps = PAGES_PER_SEQ // PAGES_PER_STEP
    n_rows = MOBA_HEADS * DEC_SEQ
    n_blocks = PAGES_PER_SEQ * PAGE_SIZE // MOBA_BLOCK
    pages_per_block = MOBA_BLOCK // PAGE_SIZE
    scale = MOBA_HEAD_DIM ** -0.5

    def head_q(hd):
        return q_ref[:, hd * MOBA_HEAD_DIM:(hd + 1) * MOBA_HEAD_DIM].astype(BF16)

    @pl.when(j < k_steps)
    def _():
        for r in range(PAGES_PER_STEP):
            parts = []
            for hd in range(MOBA_HEADS):
                kh = kp[r][pl.ds(hd, PAGE_SIZE, stride=MOBA_HEADS), :].astype(BF16)
                parts.append(lax.dot_general(head_q(hd), kh, _NT, preferred_element_type=F32))
            s_scr[j * PAGES_PER_STEP + r] = jnp.concatenate(parts, axis=0)

    @pl.when(j == k_steps - 1)
    def _():
        s = s_scr[...].reshape(n_blocks, pages_per_block, n_rows, PAGE_SIZE)
        gs = jnp.sum(s[:, 0] + s[:, 1], axis=-1, keepdims=True)
        lane = lax.broadcasted_iota(jnp.int32, (n_rows, LANES), 1)
        g_all = jnp.full((n_rows, LANES), NEG_INF, F32)
        for n in range(n_blocks):
            g_all = jnp.where(lane == n, gs[n], g_all)
        sel = []
        for n in range(n_blocks):
            wins = (g_all > gs[n]) | ((g_all == gs[n]) & (lane < n))
            beats = jnp.sum(wins.astype(F32), axis=-1, keepdims=True)
            sel.append((beats < MOBA_TOPK).astype(F32)[None])
        sel = jnp.concatenate(sel, axis=0)
        s = jnp.where(sel[:, None] > 0.5, s, NEG_INF)
        m_past = jnp.max(jnp.max(s, axis=(0, 1)), axis=-1, keepdims=True)
        parts = []
        for hd in range(MOBA_HEADS):
            kh = kn_ref[pl.ds(hd, DEC_SEQ, stride=MOBA_HEADS), :].astype(BF16)
            parts.append(lax.dot_general(head_q(hd), kh, _NT, preferred_element_type=F32))
        s_own = jnp.concatenate(parts, axis=0)
        r_id = lax.broadcasted_iota(jnp.int32, (n_rows, DEC_SEQ), 0)
        c_id = lax.broadcasted_iota(jnp.int32, (n_rows, DEC_SEQ), 1)
        s_own = jnp.where(c_id <= (r_id & (DEC_SEQ - 1)), s_own, NEG_INF)
        m = jnp.maximum(m_past, jnp.max(s_own, axis=-1, keepdims=True))
        e = jnp.exp((s - m[None, None]) * scale)
        l = jnp.sum(jnp.sum(e, axis=(0, 1)), axis=-1, keepdims=True)
        p_scr[...] = e.reshape(PAGES_PER_SEQ, n_rows, PAGE_SIZE)
        e_own = jnp.exp((s_own - m) * scale)
        l = l + jnp.sum(e_own, axis=-1, keepdims=True)
        l_scr[...] = jnp.broadcast_to(l, (n_rows, LANES))
        eo = e_own.astype(BF16).astype(F32)
        accs = []
        for hd in range(MOBA_HEADS):
            vh = vn_ref[pl.ds(hd, DEC_SEQ, stride=MOBA_HEADS), :].astype(BF16).astype(F32)
            eh = eo[hd * DEC_SEQ:(hd + 1) * DEC_SEQ]
            a = jnp.zeros((DEC_SEQ, MOBA_HEAD_DIM), F32)
            for t in range(DEC_SEQ):
                a = a + eh[:, t:t + 1] * vh[t:t + 1, :]
            accs.append(a)
        acc_scr[...] = jnp.concatenate(accs, axis=0)

    @pl.when(j >= k_steps)
    def _():
        for r in range(PAGES_PER_STEP):
            p = p_scr[(j - k_steps) * PAGES_PER_STEP + r]
            for hd in range(MOBA_HEADS):
                vh = vp[r][pl.ds(hd, PAGE_SIZE, stride=MOBA_HEADS), :].astype(BF16)
                rs = slice(hd * DEC_SEQ, (hd + 1) * DEC_SEQ)
                acc_scr[rs, :] += jnp.dot(p[rs].astype(BF16), vh, preferred_element_type=F32)

    @pl.when(j == 2 * k_steps - 1)
    def _():
        out = acc_scr[...] / l_scr[...]
        for hd in range(MOBA_HEADS):
            o_ref[:, hd * MOBA_HEAD_DIM:(hd + 1) * MOBA_HEAD_DIM] = out[hd * DEC_SEQ:(hd + 1) * DEC_SEQ]


def _moba_sample(page_table, mq_s, kn2d, vn2d, cache_k2d, cache_v2d):
    k_steps = PAGES_PER_SEQ // PAGES_PER_STEP
    n_rows = MOBA_HEADS * DEC_SEQ

    def k_map(r):
        return lambda b, j, pt: (pt[b * PAGES_PER_SEQ + jnp.minimum(j, k_steps - 1) * PAGES_PER_STEP + r], 0)

    def v_map(r):
        return lambda b, j, pt: (pt[b * PAGES_PER_SEQ + jnp.maximum(j - k_steps, 0) * PAGES_PER_STEP + r], 0)

    seq_rows = lambda w, n: pl.BlockSpec((n, w), lambda b, j, pt: (b, 0))
    grid_spec = pltpu.PrefetchScalarGridSpec(
        num_scalar_prefetch=1,
        grid=(DEC_BATCH, 2 * k_steps),
        in_specs=[seq_rows(MOBA_WIDTH, DEC_SEQ), seq_rows(LANES, n_rows), seq_rows(LANES, n_rows)]
        + [pl.BlockSpec((PAGE_ROWS, LANES), k_map(r)) for r in range(PAGES_PER_STEP)]
        + [pl.BlockSpec((PAGE_ROWS, LANES), v_map(r)) for r in range(PAGES_PER_STEP)],
        out_specs=seq_rows(MOBA_WIDTH, DEC_SEQ),
        scratch_shapes=[pltpu.VMEM((PAGES_PER_SEQ, n_rows, PAGE_SIZE), F32),
                        pltpu.VMEM((PAGES_PER_SEQ, n_rows, PAGE_SIZE), F32),
                        pltpu.VMEM((n_rows, MOBA_HEAD_DIM), F32),
                        pltpu.VMEM((n_rows, LANES), F32)],
    )
    return pl.pallas_call(
        _moba_sample_body,
        grid_spec=grid_spec,
        out_shape=jax.ShapeDtypeStruct((N_SAMPLE_TOK, MOBA_WIDTH), F32),
        compiler_params=pltpu.CompilerParams(dimension_semantics=("arbitrary", "arbitrary"),
                                             vmem_limit_bytes=48 * 1024 * 1024),
        name="moba_sample",
    )(page_table.reshape(-1), mq_s, kn2d, vn2d, *([cache_k2d] * PAGES_PER_STEP), *([cache_v2d] * PAGES_PER_STEP))


def _layer_norm(x, g, b):
    mu = jnp.mean(x, axis=-1, keepdims=True)
    d = x - mu
    var = jnp.mean(d * d, axis=-1, keepdims=True)
    return d * lax.rsqrt(var + LN_EPS) * g + b


def _out_ffn_body(ar_ref, am_ref, x_ref, ga_ref, shf_ref, scf_ref, gf_ref, wo_ref, g1_ref, b1_ref,
                  wu_ref, wd_ref, g2_ref, b2_ref, y_ref):
    mixed = (jnp.dot(ar_ref[...].astype(BF16), wo_ref[:RET_WIDTH, :], preferred_element_type=F32)
             + jnp.dot(am_ref[...].astype(BF16), wo_ref[RET_WIDTH:, :], preferred_element_type=F32))
    x1 = _layer_norm(ALPHA * x_ref[...] + ga_ref[...] * mixed, g1_ref[...], b1_ref[...])
    h = (x1 * (1.0 + scf_ref[...]) + shf_ref[...]).astype(BF16)
    acc = jnp.zeros(x1.shape, F32)
    for c in range(D_FF // D_MODEL):
        cols = slice(c * D_MODEL, (c + 1) * D_MODEL)
        u = jnp.maximum(jnp.dot(h, wu_ref[:, cols], preferred_element_type=F32), 0.0)
        acc = acc + jnp.dot((u * u).astype(BF16), wd_ref[cols, :], preferred_element_type=F32)
    y_ref[...] = _layer_norm(ALPHA * x1 + gf_ref[...] * acc, g2_ref[...], b2_ref[...])


def _out_ffn(a_ret, a_moba, x2d, mod_specs, mod_args, weights, grid, row_map, tm):
    w_o, ln1_g, ln1_b, w_up, w_down, ln2_g, ln2_b = weights
    t = x2d.shape[0]
    wide = lambda w: pl.BlockSpec((tm, w), row_map)
    const = lambda a: pl.BlockSpec(a.shape, lambda *_: (0,) * a.ndim, pipeline_mode=pl.Buffered(1))
    return pl.pallas_call(
        _out_ffn_body,
        grid=grid,
        in_specs=[wide(RET_WIDTH), wide(MOBA_WIDTH), wide(D_MODEL)] + mod_specs + [const(a) for a in weights],
        out_specs=wide(D_MODEL),
        out_shape=jax.ShapeDtypeStruct((t, D_MODEL), F32),
        compiler_params=pltpu.CompilerParams(dimension_semantics=("arbitrary",) * len(grid),
                                             vmem_limit_bytes=56 * 1024 * 1024),
        name="out_ffn",
    )(a_ret, a_moba, x2d, *mod_args, *weights)


def kernel(x_prompt, x_sample, cache_k, cache_v, state_ret, page_table, c_prompt, c_sample,
           w_ada, b_ada, w_in, w_o, ln1_g, ln1_b, w_up, w_down, ln2_g, ln2_b):
    n_prompt_tok = BATCH * SEQ
    past_len = page_table.shape[1] * PAGE_SIZE

    mod = _adaln(jnp.concatenate([c_prompt, c_sample], axis=0), w_ada[0], b_ada)
    mod_rows = mod.reshape((BATCH + DEC_BATCH) * 6, 1, D_MODEL)
    mod_s = jnp.repeat(mod[BATCH:], DEC_SEQ, axis=0)

    def prompt_mod(chunk):
        return pl.BlockSpec((None, 1, D_MODEL), lambda b, i: (b * 6 + chunk, 0, 0))

    def sample_mod(chunk):
        return pl.BlockSpec((N_SAMPLE_TOK, D_MODEL), lambda i: (0, chunk))

    w_in_b = w_in[0].astype(BF16)
    weights = (w_o[0].astype(BF16), ln1_g, ln1_b, w_up[0].astype(BF16), w_down[0].astype(BF16), ln2_g, ln2_b)

    tm = 512
    nt = SEQ // tm
    p_row = lambda b, i: (b * nt + i, 0)
    p_tab = pl.BlockSpec((tm, LANES), lambda b, i: (i, 0))
    s_row = lambda i: (0, 0)
    s_tab = pl.BlockSpec((N_SAMPLE_TOK, LANES), s_row)

    pos_p = jnp.arange(SEQ, dtype=jnp.int32)
    pos_s = jnp.tile(past_len + jnp.arange(DEC_SEQ, dtype=jnp.int32), DEC_BATCH)
    tabs_p = _rope_tables(pos_p, MOBA_HEAD_DIM) + _rope_tables(pos_p, RET_DK)
    tabs_s = _rope_tables(pos_s, MOBA_HEAD_DIM) + _rope_tables(pos_s, RET_DK)

    xp = x_prompt.reshape(n_prompt_tok, D_MODEL)
    rq, rk, rv, rg, mq, k_p, v_p = _inproj(
        xp, [prompt_mod(1), prompt_mod(0)], [mod_rows, mod_rows], [p_tab] * 4, tabs_p, w_in_b,
        (BATCH, nt), p_row, tm, BF16)
    a_ret, state_p = _ret_prompt(rq, rk, rv, rg)
    a_moba = _moba_prompt(mq, k_p, v_p)
    y_p = _out_ffn(a_ret, a_moba, xp, [prompt_mod(c) for c in (2, 3, 4, 5)], [mod_rows] * 4, weights,
                   (BATCH, nt), p_row, tm)

    xs = x_sample.reshape(N_SAMPLE_TOK, D_MODEL)
    rq_s, rk_s, rv_s, rg_s, mq_s, k_s, v_s = _inproj(
        xs, [sample_mod(1), sample_mod(0)], [mod_s, mod_s], [s_tab] * 4, tabs_s, w_in_b,
        (1,), s_row, N_SAMPLE_TOK, F32)
    a_ret_s, state_s = _ret_sample(rq_s, rk_s, rv_s, rg_s, state_ret[0])
    cache_rows = cache_k.shape[1] * PAGE_ROWS
    a_moba_s = _moba_sample(page_table, mq_s, k_s, v_s,
                            cache_k.reshape(cache_rows, LANES), cache_v.reshape(cache_rows, LANES))
    y_s = _out_ffn(a_ret_s, a_moba_s, xs, [sample_mod(c) for c in (2, 3, 4, 5)], [mod_s] * 4, weights,
                   (1,), s_row, N_SAMPLE_TOK)

    kv_p_shape = (DEPTH, BATCH, SEQ, MOBA_HEADS, MOBA_HEAD_DIM)
    kv_s_shape = (DEPTH, DEC_BATCH, DEC_SEQ, MOBA_HEADS, MOBA_HEAD_DIM)
    return (y_p.reshape(BATCH, SEQ, D_MODEL),
            y_s.reshape(DEC_BATCH, DEC_SEQ, D_MODEL),
            k_p.reshape(kv_p_shape), v_p.reshape(kv_p_shape), state_p[None],
            k_s.reshape(kv_s_shape), v_s.reshape(kv_s_shape), state_s[None])
```

```python
import functools

import numpy as np
import jax
import jax.numpy as jnp
from jax import lax
from jax.experimental import pallas as pl
from jax.experimental.pallas import tpu as pltpu

F32 = jnp.float32
BF16 = jnp.bfloat16

D_MODEL = 1024
BATCH = 8
SEQ = 2048
DEC_BATCH = 32
DEC_SEQ = 8
PAGE_SIZE = 128
RET_HEADS = 4
RET_DK = 64
RET_DV = 128
RET_CHUNK = 128
MOBA_HEADS = 4
MOBA_HEAD_DIM = 128
MOBA_BLOCK = 256
MOBA_TOPK = 3
D_FF = 4 * D_MODEL
ROPE_THETA = 10000.0
LN_EPS = 1e-5
GN_EPS = 1e-6
DEPTH = 1
ALPHA = (2 * DEPTH) ** 0.25
RET_QK = RET_HEADS * RET_DK
RET_WIDTH = RET_HEADS * RET_DV
MOBA_WIDTH = MOBA_HEADS * MOBA_HEAD_DIM
IN_WIDTH = 2 * RET_QK + 2 * RET_WIDTH + 3 * MOBA_WIDTH
OFF_RQ, OFF_RK, OFF_RV, OFF_RG = 0, 256, 512, 1024
OFF_MQ, OFF_MK, OFF_MV = 1536, 2048, 2560
LANES = 128
N_SAMPLE_TOK = DEC_BATCH * DEC_SEQ
PAGES_PER_SEQ = 64
PAGES_PER_STEP = 16
PAGE_ROWS = PAGE_SIZE * MOBA_HEADS
NEG_INF = float("-inf")

_NT = (((1,), (1,)), ((), ()))
_TN = (((0,), (0,)), ((), ()))


def _log_decay():
    return np.log1p(-np.exp2(-5.0 - np.arange(RET_HEADS, dtype=np.float64)))


def _adaln_body(c_ref, w_ref, b_ref, o_ref):
    c = c_ref[...]
    a = (c * jax.nn.sigmoid(c)).astype(BF16)
    o_ref[...] = jnp.dot(a, w_ref[...].astype(BF16), preferred_element_type=F32) + b_ref[...]


def _adaln(c_all, w_ada, b_ada):
    n = c_all.shape[0]
    tn = 1024
    return pl.pallas_call(
        _adaln_body,
        grid=(6 * D_MODEL // tn,),
        in_specs=[pl.BlockSpec((n, D_MODEL), lambda j: (0, 0)),
                  pl.BlockSpec((D_MODEL, tn), lambda j: (0, j)),
                  pl.BlockSpec((1, tn), lambda j: (0, j))],
        out_specs=pl.BlockSpec((n, tn), lambda j: (0, j)),
        out_shape=jax.ShapeDtypeStruct((n, 6 * D_MODEL), F32),
        name="adaln",
    )(c_all, w_ada, b_ada)


def _rope_tables(pos, head_dim):
    half = head_dim // 2
    inv_freq = np.power(ROPE_THETA, -np.arange(half, dtype=np.float64) / half)
    ang = pos.astype(np.float64)[:, None] * inv_freq[None, :]
    cos, sin = np.cos(ang), np.sin(ang)
    reps = LANES // head_dim
    cos_t = np.tile(np.concatenate([cos, cos], axis=-1), (1, reps))
    sin_t = np.tile(np.concatenate([-sin, sin], axis=-1), (1, reps))
    return jnp.asarray(cos_t, dtype=F32), jnp.asarray(sin_t, dtype=F32)


def _inproj_body(x_ref, sc_ref, sh_ref, w_ref, cm_ref, sm_ref, cr_ref, sr_ref,
                 rq_ref, rk_ref, rv_ref, rg_ref, mq_ref, ko_ref, vo_ref):
    tm = x_ref.shape[0]
    h = (x_ref[...] * (1.0 + sc_ref[...]) + sh_ref[...]).astype(BF16)

    def proj(lo, width):
        return jnp.dot(h, w_ref[:, lo:lo + width], preferred_element_type=F32)

    lane = lax.broadcasted_iota(jnp.int32, (tm, LANES), 1)
    low_half = (lane & (RET_DK - 1)) < (RET_DK // 2)
    cr, sr = cr_ref[...], sr_ref[...]
    cm, sm = cm_ref[...], sm_ref[...]

    def rope_ret(z):
        rot = jnp.where(low_half, pltpu.roll(z, LANES - RET_DK // 2, 1), pltpu.roll(z, RET_DK // 2, 1))
        return z * cr + rot * sr

    def rope_moba(z):
        return z * cm + pltpu.roll(z, MOBA_HEAD_DIM // 2, 1) * sm

    zq = proj(OFF_RQ, RET_QK)
    zk = proj(OFF_RK, RET_QK)
    for s in range(RET_QK // LANES):
        sl = slice(s * LANES, (s + 1) * LANES)
        rq_ref[:, sl] = rope_ret(zq[:, sl])
        rk_ref[:, sl] = rope_ret(zk[:, sl]) * (RET_DK ** -0.5)
    rv_ref[...] = proj(OFF_RV, RET_WIDTH).astype(rv_ref.dtype)
    rg_ref[...] = proj(OFF_RG, RET_WIDTH)
    zq = proj(OFF_MQ, MOBA_WIDTH)
    zk = proj(OFF_MK, MOBA_WIDTH)
    zv = proj(OFF_MV, MOBA_WIDTH)
    for hd in range(MOBA_HEADS):
        sl = slice(hd * LANES, (hd + 1) * LANES)
        mq_ref[:, sl] = rope_moba(zq[:, sl]).astype(mq_ref.dtype)
        ko_ref[pl.ds(hd, tm, stride=MOBA_HEADS), :] = rope_moba(zk[:, sl])
        vo_ref[pl.ds(hd, tm, stride=MOBA_HEADS), :] = zv[:, sl]


def _inproj(x2d, mod_specs, mod_args, tab_specs, tabs, w_in, grid, row_map, tm, act_dtype):
    t = x2d.shape[0]
    wide = lambda w: pl.BlockSpec((tm, w), row_map)
    return pl.pallas_call(
        _inproj_body,
        grid=grid,
        in_specs=[wide(D_MODEL)] + mod_specs + [pl.BlockSpec((D_MODEL, IN_WIDTH), lambda *_: (0, 0))] + tab_specs,
        out_specs=[wide(RET_QK), wide(RET_QK), wide(RET_WIDTH), wide(RET_WIDTH), wide(MOBA_WIDTH),
                   pl.BlockSpec((tm * MOBA_HEADS, LANES), row_map),
                   pl.BlockSpec((tm * MOBA_HEADS, LANES), row_map)],
        out_shape=[jax.ShapeDtypeStruct((t, RET_QK), F32), jax.ShapeDtypeStruct((t, RET_QK), F32),
                   jax.ShapeDtypeStruct((t, RET_WIDTH), act_dtype), jax.ShapeDtypeStruct((t, RET_WIDTH), F32),
                   jax.ShapeDtypeStruct((t, MOBA_WIDTH), act_dtype),
                   jax.ShapeDtypeStruct((t * MOBA_HEADS, LANES), F32),
                   jax.ShapeDtypeStruct((t * MOBA_HEADS, LANES), F32)],
        compiler_params=pltpu.CompilerParams(dimension_semantics=("arbitrary",) * len(grid),
                                             vmem_limit_bytes=48 * 1024 * 1024),
        name="inproj",
    )(x2d, *mod_args, w_in, *tabs)


def _group_norm_gate(o, g):
    mu = jnp.mean(o, axis=-1, keepdims=True)
    d = o - mu
    var = jnp.mean(d * d, axis=-1, keepdims=True)
    return d * lax.rsqrt(var + GN_EPS) * (g * jax.nn.sigmoid(g))


def _ret_prompt_tables():
    lg = _log_decay()
    i = np.arange(RET_CHUNK, dtype=np.float64)
    diff = i[:, None] - i[None, :]
    dmat = np.where(diff >= 0, np.exp(np.maximum(diff, 0.0)[None] * lg[:, None, None]), 0.0)
    lane_head = np.arange(RET_QK) // RET_DK
    qd = np.exp((i[:, None] + 1.0) * lg[lane_head][None, :])
    kd = np.exp((RET_CHUNK - 1.0 - i)[:, None] * lg[lane_head][None, :])
    row_head = np.arange(RET_QK) // RET_DK
    col_head = np.arange(RET_WIDTH) // RET_DV
    same = row_head[:, None] == col_head[None, :]
    cdec = np.where(same, np.exp(RET_CHUNK * lg[row_head])[:, None], 0.0)
    return [jnp.asarray(a, dtype=F32) for a in (dmat, qd, kd, cdec, same.astype(np.float64))]


def _ret_prompt_body(q_ref, k_ref, v_ref, g_ref, dmat_ref, qd_ref, kd_ref, cdec_ref, bdm_ref,
                     o_ref, st_ref, state_scr):
    state_scr[...] = jnp.zeros_like(state_scr)
    lane_head = lax.broadcasted_iota(jnp.int32, (RET_CHUNK, RET_QK), 1) >> 6

    def chunk(c, carry):
        rows = pl.ds(pl.multiple_of(c * RET_CHUNK, RET_CHUNK), RET_CHUNK)
        q = q_ref[rows, :]
        k = k_ref[rows, :]
        v = v_ref[rows, :]
        g = g_ref[rows, :]
        kb = k.astype(BF16)
        state = state_scr[...]
        cross = jnp.dot((q * qd_ref[...]).astype(BF16), state.astype(BF16), preferred_element_type=F32)
        for hd in range(RET_HEADS):
            sl = slice(hd * RET_DV, (hd + 1) * RET_DV)
            qm = jnp.where(lane_head == hd, q, 0.0).astype(BF16)
            s = lax.dot_general(qm, kb, _NT, preferred_element_type=F32) * dmat_ref[hd]
            inner = jnp.dot(s.astype(BF16), v[:, sl], preferred_element_type=F32)
            o_ref[rows, sl] = _group_norm_gate(inner + cross[:, sl], g[:, sl]).astype(o_ref.dtype)
        kv = lax.dot_general((k * kd_ref[...]).astype(BF16), v, _TN, preferred_element_type=F32)
        state_scr[...] = cdec_ref[...] * state + bdm_ref[...] * kv
        return carry

    lax.fori_loop(0, SEQ // RET_CHUNK, chunk, 0)
    for hd in range(RET_HEADS):
        st_ref[hd] = state_scr[hd * RET_DK:(hd + 1) * RET_DK, hd * RET_DV:(hd + 1) * RET_DV]


def _ret_prompt(rq, rk, rv, rg):
    tabs = _ret_prompt_tables()
    seq = lambda w: pl.BlockSpec((SEQ, w), lambda b: (b, 0))
    const = lambda a: pl.BlockSpec(a.shape, lambda b: (0,) * a.ndim)
    return pl.pallas_call(
        _ret_prompt_body,
        grid=(BATCH,),
        in_specs=[seq(RET_QK), seq(RET_QK), seq(RET_WIDTH), seq(RET_WIDTH)] + [const(a) for a in tabs],
        out_specs=[seq(RET_WIDTH), pl.BlockSpec((None, RET_HEADS, RET_DK, RET_DV), lambda b: (b, 0, 0, 0))],
        out_shape=[jax.ShapeDtypeStruct((BATCH * SEQ, RET_WIDTH), BF16),
                   jax.ShapeDtypeStruct((BATCH, RET_HEADS, RET_DK, RET_DV), F32)],
        scratch_shapes=[pltpu.VMEM((RET_QK, RET_WIDTH), F32)],
        compiler_params=pltpu.CompilerParams(dimension_semantics=("arbitrary",),
                                             vmem_limit_bytes=48 * 1024 * 1024),
        name="ret_prompt",
    )(rq, rk, rv, rg, *tabs)


def _ret_sample_tables():
    lg = _log_decay()
    t = np.arange(N_SAMPLE_TOK) % DEC_SEQ
    seq_id = np.arange(N_SAMPLE_TOK) // DEC_SEQ
    diff = (t[:, None] - t[None, :]).astype(np.float64)
    same_seq = seq_id[:, None] == seq_id[None, :]
    dmat = np.where(same_seq[None] & (diff >= 0)[None],
                    np.exp(np.maximum(diff, 0.0)[None] * lg[:, None, None]), 0.0)
    lane_head = np.arange(RET_QK) // RET_DK
    qd = np.exp((t[:, None] + 1.0) * lg[lane_head][None, :])
    kd = np.exp((DEC_SEQ - 1.0 - t)[:, None] * lg[lane_head][None, :])
    return [jnp.asarray(a, dtype=F32) for a in (dmat, qd, kd)]


def _ret_sample_body(q_ref, k_ref, v_ref, g_ref, st_ref, dmat_ref, qd_ref, kd_ref, o_ref, sto_ref):
    lg = _log_decay()
    q = q_ref[...]
    k = k_ref[...]
    kb = k.astype(BF16)
    qdec = q * qd_ref[...]
    kdec = k * kd_ref[...]
    vb = v_ref[...].astype(BF16)
    g = g_ref[...]
    lane = lax.broadcasted_iota(jnp.int32, (N_SAMPLE_TOK, LANES), 1)
    lane_head = lax.broadcasted_iota(jnp.int32, (N_SAMPLE_TOK, RET_QK), 1) >> 6
    n_state_rows = DEC_BATCH * RET_DK
    own_seq = ((lax.broadcasted_iota(jnp.int32, (N_SAMPLE_TOK, n_state_rows), 0) >> 3)
               == (lax.broadcasted_iota(jnp.int32, (N_SAMPLE_TOK, n_state_rows), 1) >> 6))
    for hd in range(RET_HEADS):
        sl = slice(hd * RET_DV, (hd + 1) * RET_DV)
        qm = jnp.where(lane_head == hd, q, 0.0).astype(BF16)
        s = lax.dot_general(qm, kb, _NT, preferred_element_type=F32) * dmat_ref[hd]
        inner = jnp.dot(s.astype(BF16), vb[:, sl], preferred_element_type=F32)

        def expand(z):
            slab = z[:, (hd // 2) * LANES:(hd // 2 + 1) * LANES]
            other = pltpu.roll(slab, RET_DK, 1)
            in_low = lane < RET_DK
            both = jnp.where(in_low, slab, other) if hd % 2 == 0 else jnp.where(in_low, other, slab)
            tiled = jnp.concatenate([both] * (n_state_rows // LANES), axis=1)
            return jnp.where(own_seq, tiled, 0.0).astype(BF16)

        st = st_ref[:, hd].reshape(n_state_rows, RET_DV)
        cross = jnp.dot(expand(qdec), st.astype(BF16), preferred_element_type=F32)
        o_ref[:, sl] = _group_norm_gate(inner + cross, g[:, sl])
        kv = lax.dot_general(expand(kdec), vb[:, sl], _TN, preferred_element_type=F32)
        new = float(np.exp(DEC_SEQ * lg[hd])) * st + kv
        sto_ref[:, hd] = new.reshape(DEC_BATCH, RET_DK, RET_DV)


def _ret_sample(rq, rk, rv, rg, state):
    tabs = _ret_sample_tables()
    full = lambda a: pl.BlockSpec(a.shape, lambda i: (0,) * a.ndim)
    args = (rq, rk, rv, rg, state, *tabs)
    return pl.pallas_call(
        _ret_sample_body,
        grid=(1,),
        in_specs=[full(a) for a in args],
        out_specs=[pl.BlockSpec((N_SAMPLE_TOK, RET_WIDTH), lambda i: (0, 0)),
                   pl.BlockSpec(state.shape, lambda i: (0, 0, 0, 0))],
        out_shape=[jax.ShapeDtypeStruct((N_SAMPLE_TOK, RET_WIDTH), F32),
                   jax.ShapeDtypeStruct(state.shape, F32)],
        compiler_params=pltpu.CompilerParams(dimension_semantics=("arbitrary",),
                                             vmem_limit_bytes=56 * 1024 * 1024),
        name="ret_sample",
    )(*args)


def _moba_prompt_body(q_ref, k_ref, v_ref, e_ref, o_ref):
    hd = pl.program_id(1)
    n_blocks = SEQ // MOBA_BLOCK
    scale = MOBA_HEAD_DIM ** -0.5
    k32 = k_ref[pl.ds(hd, SEQ, stride=MOBA_HEADS), :]
    kb = k32.astype(BF16)
    vb = v_ref[pl.ds(hd, SEQ, stride=MOBA_HEADS), :].astype(BF16)
    kmean = jnp.sum(k32.reshape(n_blocks, MOBA_BLOCK, MOBA_HEAD_DIM), axis=1) * (1.0 / MOBA_BLOCK)
    kmb = kmean.astype(BF16)
    row = lax.broadcasted_iota(jnp.int32, (MOBA_BLOCK, MOBA_BLOCK), 0)
    col = lax.broadcasted_iota(jnp.int32, (MOBA_BLOCK, MOBA_BLOCK), 1)
    causal = col <= row
    blk_row = lax.broadcasted_iota(jnp.int32, (n_blocks, MOBA_BLOCK), 0)

    for i in range(n_blocks):
        qi = q_ref[i * MOBA_BLOCK:(i + 1) * MOBA_BLOCK, :]
        own = slice(i * MOBA_BLOCK, (i + 1) * MOBA_BLOCK)
        s_own = jnp.where(causal, lax.dot_general(qi, kb[own], _NT, preferred_element_type=F32), NEG_INF)
        m = jnp.max(s_own, axis=-1, keepdims=True)
        if i > 0:
            past = slice(0, i * MOBA_BLOCK)
            s_past = lax.dot_general(qi, kb[past], _NT, preferred_element_type=F32)
            if i > MOBA_TOPK:
                gt = lax.dot_general(kmb, qi, _NT, preferred_element_type=F32)
                sel_t = jnp.zeros((n_blocks, MOBA_BLOCK), F32)
                for n in range(i):
                    beats = jnp.zeros((1, MOBA_BLOCK), F32)
                    for mm in range(i):
                        if mm == n:
                            continue
                        win = (gt[mm:mm + 1] >= gt[n:n + 1]) if mm < n else (gt[mm:mm + 1] > gt[n:n + 1])
                        beats = beats + win.astype(F32)
                    sel_t = jnp.where(blk_row == n, (beats < MOBA_TOPK).astype(F32), sel_t)
                keep = lax.dot_general(sel_t.astype(BF16), e_ref[:, past], _TN, preferred_element_type=F32)
                s_past = jnp.where(keep > 0.5, s_past, NEG_INF)
            m = jnp.maximum(m, jnp.max(s_past, axis=-1, keepdims=True))
        e_own = jnp.exp((s_own - m) * scale)
        l = jnp.sum(e_own, axis=-1, keepdims=True)
        acc = jnp.dot(e_own.astype(BF16), vb[own], preferred_element_type=F32)
        if i > 0:
            e_past = jnp.exp((s_past - m) * scale)
            l = l + jnp.sum(e_past, axis=-1, keepdims=True)
            acc = acc + jnp.dot(e_past.astype(BF16), vb[past], preferred_element_type=F32)
        o_ref[own, :] = (acc / l).astype(o_ref.dtype)


def _moba_prompt(mq, k2d, v2d):
    n_blocks = SEQ // MOBA_BLOCK
    e = jnp.asarray(np.arange(SEQ)[None, :] // MOBA_BLOCK == np.arange(n_blocks)[:, None], dtype=BF16)
    kv_spec = pl.BlockSpec((SEQ * MOBA_HEADS, LANES), lambda b, h: (b, 0))
    return pl.pallas_call(
        _moba_prompt_body,
        grid=(BATCH, MOBA_HEADS),
        in_specs=[pl.BlockSpec((SEQ, MOBA_HEAD_DIM), lambda b, h: (b, h)), kv_spec, kv_spec,
                  pl.BlockSpec((n_blocks, SEQ), lambda b, h: (0, 0))],
        out_specs=pl.BlockSpec((SEQ, MOBA_HEAD_DIM), lambda b, h: (b, h)),
        out_shape=jax.ShapeDtypeStruct((BATCH * SEQ, MOBA_WIDTH), BF16),
        compiler_params=pltpu.CompilerParams(dimension_semantics=("arbitrary", "arbitrary"),
                                             vmem_limit_bytes=56 * 1024 * 1024),
        name="moba_prompt",
    )(mq, k2d, v2d, e)


def _moba_sample_body(pt_ref, q_ref, kn_ref, vn_ref, *refs):
    kp = refs[:PAGES_PER_STEP]
    vp = refs[PAGES_PER_STEP:2 * PAGES_PER_STEP]
    o_ref, q_scr, s_scr, acc_scr, l_scr = refs[2 * PAGES_PER_STEP:]
    j = pl.program_id(1)
    k_steps = PAGES_PER_SEQ // PAGES_PER_STEP
    n_rows = MOBA_HEADS * DEC_SEQ
    n_blocks = PAGES_PER_SEQ * PAGE_SIZE // MOBA_BLOCK
    pages_per_block = MOBA_BLOCK // PAGE_SIZE
    scale = MOBA_HEAD_DIM ** -0.5

    @pl.when(j == 0)
    def _():
        q_scr[...] = jnp.concatenate(
            [q_ref[:, hd * MOBA_HEAD_DIM:(hd + 1) * MOBA_HEAD_DIM] for hd in range(MOBA_HEADS)], axis=0
        ).astype(BF16)

    @pl.when(j < k_steps)
    def _():
        q = q_scr[...]
        for r in range(PAGES_PER_STEP):
            s_scr[j * PAGES_PER_STEP + r] = lax.dot_general(
                q, kp[r][...].astype(BF16), _NT, preferred_element_type=F32)

    @pl.when(j == k_steps - 1)
    def _():
        s = s_scr[...].reshape(n_blocks, pages_per_block, n_rows, PAGE_ROWS)
        row_head = lax.broadcasted_iota(jnp.int32, (n_rows, PAGE_ROWS), 0) >> 3
        col_head = lax.broadcasted_iota(jnp.int32, (n_rows, PAGE_ROWS), 1) & (MOBA_HEADS - 1)
        same_head = row_head == col_head
        sz = jnp.where(same_head, s, 0.0)
        gs = jnp.sum(sz[:, 0] + sz[:, 1], axis=-1, keepdims=True)
        lane = lax.broadcasted_iota(jnp.int32, (n_rows, LANES), 1)
        g_all = jnp.full((n_rows, LANES), NEG_INF, F32)
        for n in range(n_blocks):
            g_all = jnp.where(lane == n, gs[n], g_all)
        sel = []
        for n in range(n_blocks):
            wins = (g_all > gs[n]) | ((g_all == gs[n]) & (lane < n))
            beats = jnp.sum(wins.astype(F32), axis=-1, keepdims=True)
            sel.append((beats < MOBA_TOPK).astype(F32)[None])
        sel = jnp.concatenate(sel, axis=0)
        s = jnp.where(same_head & (sel[:, None] > 0.5), s, NEG_INF)
        m_past = jnp.max(jnp.max(s, axis=(0, 1)), axis=-1, keepdims=True)
        q = q_scr[...]
        s_own = lax.dot_general(q, kn_ref[...].astype(BF16), _NT, preferred_element_type=F32)
        r_id = lax.broadcasted_iota(jnp.int32, (n_rows, n_rows), 0)
        c_id = lax.broadcasted_iota(jnp.int32, (n_rows, n_rows), 1)
        own_ok = ((c_id & (MOBA_HEADS - 1)) == (r_id >> 3)) & ((c_id >> 2) <= (r_id & (DEC_SEQ - 1)))
        s_own = jnp.where(own_ok, s_own, NEG_INF)
        m = jnp.maximum(m_past, jnp.max(s_own, axis=-1, keepdims=True))
        e = jnp.exp((s - m[None, None]) * scale)
        l = jnp.sum(jnp.sum(e, axis=(0, 1)), axis=-1, keepdims=True)
        s_scr[...] = e.reshape(PAGES_PER_SEQ, n_rows, PAGE_ROWS)
        e_own = jnp.exp((s_own - m) * scale)
        l = l + jnp.sum(e_own, axis=-1, keepdims=True)
        l_scr[...] = jnp.broadcast_to(l, (n_rows, LANES))
        acc_scr[...] = jnp.dot(e_own.astype(BF16), vn_ref[...].astype(BF16), preferred_element_type=F32)

    @pl.when(j >= k_steps)
    def _():
        acc = acc_scr[...]
        for r in range(PAGES_PER_STEP):
            p = s_scr[(j - k_steps) * PAGES_PER_STEP + r].astype(BF16)
            acc = acc + jnp.dot(p, vp[r][...].astype(BF16), preferred_element_type=F32)
        acc_scr[...] = acc

    @pl.when(j == 2 * k_steps - 1)
    def _():
        out = acc_scr[...] / l_scr[...]
        for hd in range(MOBA_HEADS):
            o_ref[:, hd * MOBA_HEAD_DIM:(hd + 1) * MOBA_HEAD_DIM] = out[hd * DEC_SEQ:(hd + 1) * DEC_SEQ]


def _moba_sample(page_table, mq_s, kn2d, vn2d, cache_k2d, cache_v2d):
    k_steps = PAGES_PER_SEQ // PAGES_PER_STEP
    n_rows = MOBA_HEADS * DEC_SEQ

    def k_map(r):
        return lambda b, j, pt: (pt[b * PAGES_PER_SEQ + jnp.minimum(j, k_steps - 1) * PAGES_PER_STEP + r], 0)

    def v_map(r):
        return lambda b, j, pt: (pt[b * PAGES_PER_SEQ + jnp.maximum(j - k_steps, 0) * PAGES_PER_STEP + r], 0)

    seq_rows = lambda w, n: pl.BlockSpec((n, w), lambda b, j, pt: (b, 0))
    grid_spec = pltpu.PrefetchScalarGridSpec(
        num_scalar_prefetch=1,
        grid=(DEC_BATCH, 2 * k_steps),
        in_specs=[seq_rows(MOBA_WIDTH, DEC_SEQ), seq_rows(LANES, n_rows), seq_rows(LANES, n_rows)]
        + [pl.BlockSpec((PAGE_ROWS, LANES), k_map(r)) for r in range(PAGES_PER_STEP)]
        + [pl.BlockSpec((PAGE_ROWS, LANES), v_map(r)) for r in range(PAGES_PER_STEP)],
        out_specs=seq_rows(MOBA_WIDTH, DEC_SEQ),
        scratch_shapes=[pltpu.VMEM((n_rows, MOBA_HEAD_DIM), BF16),
                        pltpu.VMEM((PAGES_PER_SEQ, n_rows, PAGE_ROWS), F32),
                        pltpu.VMEM((n_rows, MOBA_HEAD_DIM), F32),
                        pltpu.VMEM((n_rows, LANES), F32)],
    )
    return pl.pallas_call(
        _moba_sample_body,
        grid_spec=grid_spec,
        out_shape=jax.ShapeDtypeStruct((N_SAMPLE_TOK, MOBA_WIDTH), F32),
        compiler_params=pltpu.CompilerParams(dimension_semantics=("arbitrary", "arbitrary"),
                                             vmem_limit_bytes=56 * 1024 * 1024),
        name="moba_sample",
    )(page_table.reshape(-1), mq_s, kn2d, vn2d, *([cache_k2d] * PAGES_PER_STEP), *([cache_v2d] * PAGES_PER_STEP))


def _layer_norm(x, g, b):
    mu = jnp.mean(x, axis=-1, keepdims=True)
    d = x - mu
    var = jnp.mean(d * d, axis=-1, keepdims=True)
    return d * lax.rsqrt(var + LN_EPS) * g + b


def _out_ffn_body(ar_ref, am_ref, x_ref, ga_ref, shf_ref, scf_ref, gf_ref, wo_ref, g1_ref, b1_ref,
                  wu_ref, wd_ref, g2_ref, b2_ref, y_ref):
    mixed = (jnp.dot(ar_ref[...].astype(BF16), wo_ref[:RET_WIDTH, :], preferred_element_type=F32)
             + jnp.dot(am_ref[...].astype(BF16), wo_ref[RET_WIDTH:, :], preferred_element_type=F32))
    x1 = _layer_norm(ALPHA * x_ref[...] + ga_ref[...] * mixed, g1_ref[...], b1_ref[...])
    h = (x1 * (1.0 + scf_ref[...]) + shf_ref[...]).astype(BF16)
    acc = jnp.zeros(x1.shape, F32)
    for c in range(D_FF // D_MODEL):
        cols = slice(c * D_MODEL, (c + 1) * D_MODEL)
        u = jnp.maximum(jnp.dot(h, wu_ref[:, cols], preferred_element_type=F32), 0.0)
        acc = acc + jnp.dot((u * u).astype(BF16), wd_ref[cols, :], preferred_element_type=F32)
    y_ref[...] = _layer_norm(ALPHA * x1 + gf_ref[...] * acc, g2_ref[...], b2_ref[...])


def _out_ffn(a_ret, a_moba, x2d, mod_specs, mod_args, weights, grid, row_map, tm):
    w_o, ln1_g, ln1_b, w_up, w_down, ln2_g, ln2_b = weights
    t = x2d.shape[0]
    wide = lambda w: pl.BlockSpec((tm, w), row_map)
    const = lambda a: pl.BlockSpec(a.shape, lambda *_: (0,) * a.ndim, pipeline_mode=pl.Buffered(1))
    return pl.pallas_call(
        _out_ffn_body,
        grid=grid,
        in_specs=[wide(RET_WIDTH), wide(MOBA_WIDTH), wide(D_MODEL)] + mod_specs + [const(a) for a in weights],
        out_specs=wide(D_MODEL),
        out_shape=jax.ShapeDtypeStruct((t, D_MODEL), F32),
        compiler_params=pltpu.CompilerParams(dimension_semantics=("arbitrary",) * len(grid),
                                             vmem_limit_bytes=56 * 1024 * 1024),
        name="out_ffn",
    )(a_ret, a_moba, x2d, *mod_args, *weights)


def kernel(x_prompt, x_sample, cache_k, cache_v, state_ret, page_table, c_prompt, c_sample,
           w_ada, b_ada, w_in, w_o, ln1_g, ln1_b, w_up, w_down, ln2_g, ln2_b):
    n_prompt_tok = BATCH * SEQ
    past_len = page_table.shape[1] * PAGE_SIZE

    mod = _adaln(jnp.concatenate([c_prompt, c_sample], axis=0), w_ada[0], b_ada)
    mod_rows = mod.reshape((BATCH + DEC_BATCH) * 6, 1, D_MODEL)
    mod_s = jnp.repeat(mod[BATCH:], DEC_SEQ, axis=0)

    def prompt_mod(chunk):
        return pl.BlockSpec((None, 1, D_MODEL), lambda b, i: (b * 6 + chunk, 0, 0))

    def sample_mod(chunk):
        return pl.BlockSpec((N_SAMPLE_TOK, D_MODEL), lambda i: (0, chunk))

    w_in_b = w_in[0].astype(BF16)
    weights = (w_o[0].astype(BF16), ln1_g, ln1_b, w_up[0].astype(BF16), w_down[0].astype(BF16), ln2_g, ln2_b)

    tm = 512
    nt = SEQ // tm
    p_row = lambda b, i: (b * nt + i, 0)
    p_tab = pl.BlockSpec((tm, LANES), lambda b, i: (i, 0))
    s_row = lambda i: (0, 0)
    s_tab = pl.BlockSpec((N_SAMPLE_TOK, LANES), s_row)

    pos_p = np.arange(SEQ, dtype=np.int32)
    pos_s = np.tile(past_len + np.arange(DEC_SEQ, dtype=np.int32), DEC_BATCH)
    tabs_p = _rope_tables(pos_p, MOBA_HEAD_DIM) + _rope_tables(pos_p, RET_DK)
    tabs_s = _rope_tables(pos_s, MOBA_HEAD_DIM) + _rope_tables(pos_s, RET_DK)

    xp = x_prompt.reshape(n_prompt_tok, D_MODEL)
    rq, rk, rv, rg, mq, k_p, v_p = _inproj(
        xp, [prompt_mod(1), prompt_mod(0)], [mod_rows, mod_rows], [p_tab] * 4, tabs_p, w_in_b,
        (BATCH, nt), p_row, tm, BF16)
    a_ret, state_p = _ret_prompt(rq, rk, rv, rg)
    a_moba = _moba_prompt(mq, k_p, v_p)
    y_p = _out_ffn(a_ret, a_moba, xp, [prompt_mod(c) for c in (2, 3, 4, 5)], [mod_rows] * 4, weights,
                   (BATCH, nt), p_row, tm)

    xs = x_sample.reshape(N_SAMPLE_TOK, D_MODEL)
    rq_s, rk_s, rv_s, rg_s, mq_s, k_s, v_s = _inproj(
        xs, [sample_mod(1), sample_mod(0)], [mod_s, mod_s], [s_tab] * 4, tabs_s, w_in_b,
        (1,), s_row, N_SAMPLE_TOK, F32)
    a_ret_s, state_s = _ret_sample(rq_s, rk_s, rv_s, rg_s, state_ret[0])
    cache_rows = cache_k.shape[1] * PAGE_ROWS
    a_moba_s = _moba_sample(page_table, mq_s, k_s, v_s,
                            cache_k.reshape(cache_rows, LANES), cache_v.reshape(cache_rows, LANES))
    y_s = _out_ffn(a_ret_s, a_moba_s, xs, [sample_mod(c) for c in (2, 3, 4, 5)], [mod_s] * 4, weights,
                   (1,), s_row, N_SAMPLE_TOK)

    kv_p_shape = (DEPTH, BATCH, SEQ, MOBA_HEADS, MOBA_HEAD_DIM)
    kv_s_shape = (DEPTH, DEC_BATCH, DEC_SEQ, MOBA_HEADS, MOBA_HEAD_DIM)
    return (y_p.reshape(BATCH, SEQ, D_MODEL),
            y_s.reshape(DEC_BATCH, DEC_SEQ, D_MODEL),
            k_p.reshape(kv_p_shape), v_p.reshape(kv_p_shape), state_p[None],
            k_s.reshape(kv_s_shape), v_s.reshape(kv_s_shape), state_s[None])
```

```python
import functools

import numpy as np
import jax
import jax.numpy as jnp
from jax import lax
from jax.experimental import pallas as pl
from jax.experimental.pallas import tpu as pltpu

F32 = jnp.float32
BF16 = jnp.bfloat16

D_MODEL = 1024
BATCH = 8
SEQ = 2048
DEC_BATCH = 32
DEC_SEQ = 8
PAGE_SIZE = 128
RET_HEADS = 4
RET_DK = 64
RET_DV = 128
RET_CHUNK = 128
MOBA_HEADS = 4
MOBA_HEAD_DIM = 128
MOBA_BLOCK = 256
MOBA_TOPK = 3
D_FF = 4 * D_MODEL
ROPE_THETA = 10000.0
LN_EPS = 1e-5
GN_EPS = 1e-6
DEPTH = 1
ALPHA = (2 * DEPTH) ** 0.25
RET_QK = RET_HEADS * RET_DK
RET_WIDTH = RET_HEADS * RET_DV
MOBA_WIDTH = MOBA_HEADS * MOBA_HEAD_DIM
IN_WIDTH = 2 * RET_QK + 2 * RET_WIDTH + 3 * MOBA_WIDTH
OFF_RQ, OFF_RK, OFF_RV, OFF_RG = 0, 256, 512, 1024
OFF_MQ, OFF_MK, OFF_MV = 1536, 2048, 2560
LANES = 128
N_SAMPLE_TOK = DEC_BATCH * DEC_SEQ
PAGES_PER_SEQ = 64
CHUNK_PAGES = 16
RING_PAGES = 2 * CHUNK_PAGES
PAGE_ROWS = PAGE_SIZE * MOBA_HEADS
NEG_INF = float("-inf")

_NT = (((1,), (1,)), ((), ()))
_TN = (((0,), (0,)), ((), ()))


def _log_decay():
    return np.log1p(-np.exp2(-5.0 - np.arange(RET_HEADS, dtype=np.float64)))


def _adaln_body(c_ref, w_ref, b_ref, o_ref):
    c = c_ref[...]
    a = (c * jax.nn.sigmoid(c)).astype(BF16)
    o_ref[...] = jnp.dot(a, w_ref[...].astype(BF16), preferred_element_type=F32) + b_ref[...]


def _adaln(c_all, w_ada, b_ada):
    n = c_all.shape[0]
    tn = 1024
    return pl.pallas_call(
        _adaln_body,
        grid=(6 * D_MODEL // tn,),
        in_specs=[pl.BlockSpec((n, D_MODEL), lambda j: (0, 0)),
                  pl.BlockSpec((D_MODEL, tn), lambda j: (0, j)),
                  pl.BlockSpec((1, tn), lambda j: (0, j))],
        out_specs=pl.BlockSpec((n, tn), lambda j: (0, j)),
        out_shape=jax.ShapeDtypeStruct((n, 6 * D_MODEL), F32),
        name="adaln",
    )(c_all, w_ada, b_ada)


def _rope_tables(pos, head_dim):
    half = head_dim // 2
    inv_freq = np.power(ROPE_THETA, -np.arange(half, dtype=np.float64) / half)
    ang = pos.astype(np.float64)[:, None] * inv_freq[None, :]
    cos, sin = np.cos(ang), np.sin(ang)
    reps = LANES // head_dim
    cos_t = np.tile(np.concatenate([cos, cos], axis=-1), (1, reps))
    sin_t = np.tile(np.concatenate([-sin, sin], axis=-1), (1, reps))
    return jnp.asarray(cos_t, dtype=F32), jnp.asarray(sin_t, dtype=F32)


def _inproj_body(x_ref, sc_ref, sh_ref, w_ref, cm_ref, sm_ref, cr_ref, sr_ref,
                 rq_ref, rk_ref, rv_ref, rg_ref, mq_ref, ko_ref, vo_ref):
    tm = x_ref.shape[0]
    h = (x_ref[...] * (1.0 + sc_ref[...]) + sh_ref[...]).astype(BF16)

    def proj(lo, width):
        return jnp.dot(h, w_ref[:, lo:lo + width], preferred_element_type=F32)

    lane = lax.broadcasted_iota(jnp.int32, (tm, LANES), 1)
    low_half = (lane & (RET_DK - 1)) < (RET_DK // 2)
    cr, sr = cr_ref[...], sr_ref[...]
    cm, sm = cm_ref[...], sm_ref[...]

    def rope_ret(z):
        rot = jnp.where(low_half, pltpu.roll(z, LANES - RET_DK // 2, 1), pltpu.roll(z, RET_DK // 2, 1))
        return z * cr + rot * sr

    def rope_moba(z):
        return z * cm + pltpu.roll(z, MOBA_HEAD_DIM // 2, 1) * sm

    zq = proj(OFF_RQ, RET_QK)
    zk = proj(OFF_RK, RET_QK)
    for s in range(RET_QK // LANES):
        sl = slice(s * LANES, (s + 1) * LANES)
        rq_ref[:, sl] = rope_ret(zq[:, sl])
        rk_ref[:, sl] = rope_ret(zk[:, sl]) * (RET_DK ** -0.5)
    rv_ref[...] = proj(OFF_RV, RET_WIDTH).astype(rv_ref.dtype)
    rg_ref[...] = proj(OFF_RG, RET_WIDTH)
    zq = proj(OFF_MQ, MOBA_WIDTH)
    zk = proj(OFF_MK, MOBA_WIDTH)
    zv = proj(OFF_MV, MOBA_WIDTH)
    for hd in range(MOBA_HEADS):
        sl = slice(hd * LANES, (hd + 1) * LANES)
        mq_ref[:, sl] = rope_moba(zq[:, sl]).astype(mq_ref.dtype)
        ko_ref[pl.ds(hd, tm, stride=MOBA_HEADS), :] = rope_moba(zk[:, sl])
        vo_ref[pl.ds(hd, tm, stride=MOBA_HEADS), :] = zv[:, sl]


def _inproj(x2d, mod_specs, mod_args, tab_specs, tabs, w_in, grid, row_map, tm, act_dtype):
    t = x2d.shape[0]
    wide = lambda w: pl.BlockSpec((tm, w), row_map)
    return pl.pallas_call(
        _inproj_body,
        grid=grid,
        in_specs=[wide(D_MODEL)] + mod_specs + [pl.BlockSpec((D_MODEL, IN_WIDTH), lambda *_: (0, 0))] + tab_specs,
        out_specs=[wide(RET_QK), wide(RET_QK), wide(RET_WIDTH), wide(RET_WIDTH), wide(MOBA_WIDTH),
                   pl.BlockSpec((tm * MOBA_HEADS, LANES), row_map),
                   pl.BlockSpec((tm * MOBA_HEADS, LANES), row_map)],
        out_shape=[jax.ShapeDtypeStruct((t, RET_QK), F32), jax.ShapeDtypeStruct((t, RET_QK), F32),
                   jax.ShapeDtypeStruct((t, RET_WIDTH), act_dtype), jax.ShapeDtypeStruct((t, RET_WIDTH), F32),
                   jax.ShapeDtypeStruct((t, MOBA_WIDTH), act_dtype),
                   jax.ShapeDtypeStruct((t * MOBA_HEADS, LANES), F32),
                   jax.ShapeDtypeStruct((t * MOBA_HEADS, LANES), F32)],
        compiler_params=pltpu.CompilerParams(dimension_semantics=("arbitrary",) * len(grid),
                                             vmem_limit_bytes=48 * 1024 * 1024),
        name="inproj",
    )(x2d, *mod_args, w_in, *tabs)


def _group_norm_gate(o, g):
    mu = jnp.mean(o, axis=-1, keepdims=True)
    d = o - mu
    var = jnp.mean(d * d, axis=-1, keepdims=True)
    return d * lax.rsqrt(var + GN_EPS) * (g * jax.nn.sigmoid(g))


def _ret_prompt_tables():
    lg = _log_decay()
    i = np.arange(RET_CHUNK, dtype=np.float64)
    diff = i[:, None] - i[None, :]
    dmat = np.where(diff >= 0, np.exp(np.maximum(diff, 0.0)[None] * lg[:, None, None]), 0.0)
    lane_head = np.arange(RET_QK) // RET_DK
    qd = np.exp((i[:, None] + 1.0) * lg[lane_head][None, :])
    kd = np.exp((RET_CHUNK - 1.0 - i)[:, None] * lg[lane_head][None, :])
    row_head = np.arange(RET_QK) // RET_DK
    col_head = np.arange(RET_WIDTH) // RET_DV
    same = row_head[:, None] == col_head[None, :]
    cdec = np.where(same, np.exp(RET_CHUNK * lg[row_head])[:, None], 0.0)
    return [jnp.asarray(a, dtype=F32) for a in (dmat, qd, kd, cdec, same.astype(np.float64))]


def _ret_prompt_body(q_ref, k_ref, v_ref, g_ref, dmat_ref, qd_ref, kd_ref, cdec_ref, bdm_ref,
                     o_ref, st_ref, state_scr):
    state_scr[...] = jnp.zeros_like(state_scr)
    lane_head = lax.broadcasted_iota(jnp.int32, (RET_CHUNK, RET_QK), 1) >> 6

    def chunk(c, carry):
        rows = pl.ds(pl.multiple_of(c * RET_CHUNK, RET_CHUNK), RET_CHUNK)
        q = q_ref[rows, :]
        k = k_ref[rows, :]
        v = v_ref[rows, :]
        g = g_ref[rows, :]
        kb = k.astype(BF16)
        state = state_scr[...]
        cross = jnp.dot((q * qd_ref[...]).astype(BF16), state.astype(BF16), preferred_element_type=F32)
        for hd in range(RET_HEADS):
            sl = slice(hd * RET_DV, (hd + 1) * RET_DV)
            qm = jnp.where(lane_head == hd, q, 0.0).astype(BF16)
            s = lax.dot_general(qm, kb, _NT, preferred_element_type=F32) * dmat_ref[hd]
            inner = jnp.dot(s.astype(BF16), v[:, sl], preferred_element_type=F32)
            o_ref[rows, sl] = _group_norm_gate(inner + cross[:, sl], g[:, sl]).astype(o_ref.dtype)
        kv = lax.dot_general((k * kd_ref[...]).astype(BF16), v, _TN, preferred_element_type=F32)
        state_scr[...] = cdec_ref[...] * state + bdm_ref[...] * kv
        return carry

    lax.fori_loop(0, SEQ // RET_CHUNK, chunk, 0)
    for hd in range(RET_HEADS):
        st_ref[hd] = state_scr[hd * RET_DK:(hd + 1) * RET_DK, hd * RET_DV:(hd + 1) * RET_DV]


def _ret_prompt(rq, rk, rv, rg):
    tabs = _ret_prompt_tables()
    seq = lambda w: pl.BlockSpec((SEQ, w), lambda b: (b, 0))
    const = lambda a: pl.BlockSpec(a.shape, lambda b: (0,) * a.ndim)
    return pl.pallas_call(
        _ret_prompt_body,
        grid=(BATCH,),
        in_specs=[seq(RET_QK), seq(RET_QK), seq(RET_WIDTH), seq(RET_WIDTH)] + [const(a) for a in tabs],
        out_specs=[seq(RET_WIDTH), pl.BlockSpec((None, RET_HEADS, RET_DK, RET_DV), lambda b: (b, 0, 0, 0))],
        out_shape=[jax.ShapeDtypeStruct((BATCH * SEQ, RET_WIDTH), BF16),
                   jax.ShapeDtypeStruct((BATCH, RET_HEADS, RET_DK, RET_DV), F32)],
        scratch_shapes=[pltpu.VMEM((RET_QK, RET_WIDTH), F32)],
        compiler_params=pltpu.CompilerParams(dimension_semantics=("arbitrary",),
                                             vmem_limit_bytes=48 * 1024 * 1024),
        name="ret_prompt",
    )(rq, rk, rv, rg, *tabs)


def _ret_sample_tables():
    lg = _log_decay()
    t = np.arange(N_SAMPLE_TOK) % DEC_SEQ
    seq_id = np.arange(N_SAMPLE_TOK) // DEC_SEQ
    diff = (t[:, None] - t[None, :]).astype(np.float64)
    same_seq = seq_id[:, None] == seq_id[None, :]
    dmat = np.where(same_seq[None] & (diff >= 0)[None],
                    np.exp(np.maximum(diff, 0.0)[None] * lg[:, None, None]), 0.0)
    lane_head = np.arange(RET_QK) // RET_DK
    qd = np.exp((t[:, None] + 1.0) * lg[lane_head][None, :])
    kd = np.exp((DEC_SEQ - 1.0 - t)[:, None] * lg[lane_head][None, :])
    return [jnp.asarray(a, dtype=F32) for a in (dmat, qd, kd)]


def _ret_sample_body(q_ref, k_ref, v_ref, g_ref, st_ref, dmat_ref, qd_ref, kd_ref, o_ref, sto_ref):
    lg = _log_decay()
    q = q_ref[...]
    k = k_ref[...]
    kb = k.astype(BF16)
    qdec = q * qd_ref[...]
    kdec = k * kd_ref[...]
    vb = v_ref[...].astype(BF16)
    g = g_ref[...]
    lane = lax.broadcasted_iota(jnp.int32, (N_SAMPLE_TOK, LANES), 1)
    lane_head = lax.broadcasted_iota(jnp.int32, (N_SAMPLE_TOK, RET_QK), 1) >> 6
    n_state_rows = DEC_BATCH * RET_DK
    own_seq = ((lax.broadcasted_iota(jnp.int32, (N_SAMPLE_TOK, n_state_rows), 0) >> 3)
               == (lax.broadcasted_iota(jnp.int32, (N_SAMPLE_TOK, n_state_rows), 1) >> 6))
    for hd in range(RET_HEADS):
        sl = slice(hd * RET_DV, (hd + 1) * RET_DV)
        qm = jnp.where(lane_head == hd, q, 0.0).astype(BF16)
        s = lax.dot_general(qm, kb, _NT, preferred_element_type=F32) * dmat_ref[hd]
        inner = jnp.dot(s.astype(BF16), vb[:, sl], preferred_element_type=F32)

        def expand(z):
            slab = z[:, (hd // 2) * LANES:(hd // 2 + 1) * LANES]
            other = pltpu.roll(slab, RET_DK, 1)
            in_low = lane < RET_DK
            both = jnp.where(in_low, slab, other) if hd % 2 == 0 else jnp.where(in_low, other, slab)
            tiled = jnp.concatenate([both] * (n_state_rows // LANES), axis=1)
            return jnp.where(own_seq, tiled, 0.0).astype(BF16)

        st = st_ref[:, hd].reshape(n_state_rows, RET_DV)
        cross = jnp.dot(expand(qdec), st.astype(BF16), preferred_element_type=F32)
        o_ref[:, sl] = _group_norm_gate(inner + cross, g[:, sl])
        kv = lax.dot_general(expand(kdec), vb[:, sl], _TN, preferred_element_type=F32)
        new = float(np.exp(DEC_SEQ * lg[hd])) * st + kv
        sto_ref[:, hd] = new.reshape(DEC_BATCH, RET_DK, RET_DV)


def _ret_sample(rq, rk, rv, rg, state):
    tabs = _ret_sample_tables()
    full = lambda a: pl.BlockSpec(a.shape, lambda i: (0,) * a.ndim)
    args = (rq, rk, rv, rg, state, *tabs)
    return pl.pallas_call(
        _ret_sample_body,
        grid=(1,),
        in_specs=[full(a) for a in args],
        out_specs=[pl.BlockSpec((N_SAMPLE_TOK, RET_WIDTH), lambda i: (0, 0)),
                   pl.BlockSpec(state.shape, lambda i: (0, 0, 0, 0))],
        out_shape=[jax.ShapeDtypeStruct((N_SAMPLE_TOK, RET_WIDTH), F32),
                   jax.ShapeDtypeStruct(state.shape, F32)],
        compiler_params=pltpu.CompilerParams(dimension_semantics=("arbitrary",),
                                             vmem_limit_bytes=56 * 1024 * 1024),
        name="ret_sample",
    )(*args)


def _moba_prompt_body(q_ref, k_ref, v_ref, e_ref, o_ref):
    hd = pl.program_id(1)
    n_blocks = SEQ // MOBA_BLOCK
    scale = MOBA_HEAD_DIM ** -0.5
    k32 = k_ref[pl.ds(hd, SEQ, stride=MOBA_HEADS), :]
    kb = k32.astype(BF16)
    vb = v_ref[pl.ds(hd, SEQ, stride=MOBA_HEADS), :].astype(BF16)
    kmean = jnp.sum(k32.reshape(n_blocks, MOBA_BLOCK, MOBA_HEAD_DIM), axis=1) * (1.0 / MOBA_BLOCK)
    kmb = kmean.astype(BF16)
    row = lax.broadcasted_iota(jnp.int32, (MOBA_BLOCK, MOBA_BLOCK), 0)
    col = lax.broadcasted_iota(jnp.int32, (MOBA_BLOCK, MOBA_BLOCK), 1)
    causal = col <= row
    blk_row = lax.broadcasted_iota(jnp.int32, (n_blocks, MOBA_BLOCK), 0)

    for i in range(n_blocks):
        qi = q_ref[i * MOBA_BLOCK:(i + 1) * MOBA_BLOCK, :]
        own = slice(i * MOBA_BLOCK, (i + 1) * MOBA_BLOCK)
        s_own = jnp.where(causal, lax.dot_general(qi, kb[own], _NT, preferred_element_type=F32), NEG_INF)
        m = jnp.max(s_own, axis=-1, keepdims=True)
        if i > 0:
            past = slice(0, i * MOBA_BLOCK)
            s_past = lax.dot_general(qi, kb[past], _NT, preferred_element_type=F32)
            if i > MOBA_TOPK:
                gt = lax.dot_general(kmb, qi, _NT, preferred_element_type=F32)
                sel_t = jnp.zeros((n_blocks, MOBA_BLOCK), F32)
                for n in range(i):
                    beats = jnp.zeros((1, MOBA_BLOCK), F32)
                    for mm in range(i):
                        if mm == n:
                            continue
                        win = (gt[mm:mm + 1] >= gt[n:n + 1]) if mm < n else (gt[mm:mm + 1] > gt[n:n + 1])
                        beats = beats + win.astype(F32)
                    sel_t = jnp.where(blk_row == n, (beats < MOBA_TOPK).astype(F32), sel_t)
                keep = lax.dot_general(sel_t.astype(BF16), e_ref[:, past], _TN, preferred_element_type=F32)
                s_past = jnp.where(keep > 0.5, s_past, NEG_INF)
            m = jnp.maximum(m, jnp.max(s_past, axis=-1, keepdims=True))
        e_own = jnp.exp((s_own - m) * scale)
        l = jnp.sum(e_own, axis=-1, keepdims=True)
        acc = jnp.dot(e_own.astype(BF16), vb[own], preferred_element_type=F32)
        if i > 0:
            e_past = jnp.exp((s_past - m) * scale)
            l = l + jnp.sum(e_past, axis=-1, keepdims=True)
            acc = acc + jnp.dot(e_past.astype(BF16), vb[past], preferred_element_type=F32)
        o_ref[own, :] = (acc / l).astype(o_ref.dtype)


def _moba_prompt(mq, k2d, v2d):
    n_blocks = SEQ // MOBA_BLOCK
    e = jnp.asarray(np.arange(SEQ)[None, :] // MOBA_BLOCK == np.arange(n_blocks)[:, None], dtype=BF16)
    kv_spec = pl.BlockSpec((SEQ * MOBA_HEADS, LANES), lambda b, h: (b, 0))
    return pl.pallas_call(
        _moba_prompt_body,
        grid=(BATCH, MOBA_HEADS),
        in_specs=[pl.BlockSpec((SEQ, MOBA_HEAD_DIM), lambda b, h: (b, h)), kv_spec, kv_spec,
                  pl.BlockSpec((n_blocks, SEQ), lambda b, h: (0, 0))],
        out_specs=pl.BlockSpec((SEQ, MOBA_HEAD_DIM), lambda b, h: (b, h)),
        out_shape=jax.ShapeDtypeStruct((BATCH * SEQ, MOBA_WIDTH), BF16),
        compiler_params=pltpu.CompilerParams(dimension_semantics=("arbitrary", "arbitrary"),
                                             vmem_limit_bytes=56 * 1024 * 1024),
        name="moba_prompt",
    )(mq, k2d, v2d, e)


def _moba_sample_body(pt_ref, q_ref, kn_ref, vn_ref, *refs):
    kc_ref, vc_ref, o_ref, ring, sem, s_scr = refs
    b = pl.program_id(0)
    n_chunks = PAGES_PER_SEQ // CHUNK_PAGES
    n_rows = MOBA_HEADS * DEC_SEQ
    n_blocks = PAGES_PER_SEQ * PAGE_SIZE // MOBA_BLOCK
    pages_per_block = MOBA_BLOCK // PAGE_SIZE
    scale = MOBA_HEAD_DIM ** -0.5

    def page_copy(cache_ref, row0, slot):
        return pltpu.make_async_copy(cache_ref.at[pl.ds(row0, PAGE_ROWS)], ring.at[slot], sem.at[slot])

    def start_chunk(cache_ref, seq, chunk):
        for r in range(CHUNK_PAGES):
            page = pt_ref[seq * PAGES_PER_SEQ + chunk * CHUNK_PAGES + r]
            page_copy(cache_ref, pl.multiple_of(page * PAGE_ROWS, PAGE_ROWS), (chunk % 2) * CHUNK_PAGES + r).start()

    def wait_chunk(cache_ref, chunk):
        for r in range(CHUNK_PAGES):
            page_copy(cache_ref, 0, (chunk % 2) * CHUNK_PAGES + r).wait()

    @pl.when(b == 0)
    def _():
        start_chunk(kc_ref, b, 0)
        start_chunk(kc_ref, b, 1)

    q = jnp.concatenate(
        [q_ref[:, hd * MOBA_HEAD_DIM:(hd + 1) * MOBA_HEAD_DIM] for hd in range(MOBA_HEADS)], axis=0
    ).astype(BF16)

    for c in range(n_chunks):
        wait_chunk(kc_ref, c)
        for r in range(CHUNK_PAGES):
            page = ring[(c % 2) * CHUNK_PAGES + r].astype(BF16)
            s_scr[c * CHUNK_PAGES + r] = lax.dot_general(q, page, _NT, preferred_element_type=F32)
        if c + 2 < n_chunks:
            start_chunk(kc_ref, b, c + 2)
        else:
            start_chunk(vc_ref, b, c + 2 - n_chunks)

    def softmax_stage():
        s = s_scr[...].reshape(n_blocks, pages_per_block, n_rows, PAGE_ROWS)
        row_head = lax.broadcasted_iota(jnp.int32, (n_rows, PAGE_ROWS), 0) >> 3
        col_head = lax.broadcasted_iota(jnp.int32, (n_rows, PAGE_ROWS), 1) & (MOBA_HEADS - 1)
        same_head = row_head == col_head
        sz = jnp.where(same_head, s, 0.0)
        gs = jnp.sum(sz[:, 0] + sz[:, 1], axis=-1, keepdims=True)
        lane = lax.broadcasted_iota(jnp.int32, (n_rows, LANES), 1)
        g_all = jnp.full((n_rows, LANES), NEG_INF, F32)
        for n in range(n_blocks):
            g_all = jnp.where(lane == n, gs[n], g_all)
        sel = []
        for n in range(n_blocks):
            wins = (g_all > gs[n]) | ((g_all == gs[n]) & (lane < n))
            beats = jnp.sum(wins.astype(F32), axis=-1, keepdims=True)
            sel.append((beats < MOBA_TOPK).astype(F32)[None])
        sel = jnp.concatenate(sel, axis=0)
        s = jnp.where(same_head & (sel[:, None] > 0.5), s, NEG_INF)
        m_past = jnp.max(jnp.max(s, axis=(0, 1)), axis=-1, keepdims=True)
        s_own = lax.dot_general(q, kn_ref[...].astype(BF16), _NT, preferred_element_type=F32)
        r_id = lax.broadcasted_iota(jnp.int32, (n_rows, n_rows), 0)
        c_id = lax.broadcasted_iota(jnp.int32, (n_rows, n_rows), 1)
        own_ok = ((c_id & (MOBA_HEADS - 1)) == (r_id >> 3)) & ((c_id >> 2) <= (r_id & (DEC_SEQ - 1)))
        s_own = jnp.where(own_ok, s_own, NEG_INF)
        m = jnp.maximum(m_past, jnp.max(s_own, axis=-1, keepdims=True))
        e = jnp.exp((s - m[None, None]) * scale)
        l = jnp.sum(jnp.sum(e, axis=(0, 1)), axis=-1, keepdims=True)
        s_scr[...] = e.reshape(PAGES_PER_SEQ, n_rows, PAGE_ROWS)
        e_own = jnp.exp((s_own - m) * scale)
        l = l + jnp.sum(e_own, axis=-1, keepdims=True)
        return l, jnp.dot(e_own.astype(BF16), vn_ref[...].astype(BF16), preferred_element_type=F32)

    l, acc = softmax_stage()

    for c in range(n_chunks):
        wait_chunk(vc_ref, c)
        for r in range(CHUNK_PAGES):
            p = s_scr[c * CHUNK_PAGES + r].astype(BF16)
            page = ring[(c % 2) * CHUNK_PAGES + r].astype(BF16)
            acc = acc + jnp.dot(p, page, preferred_element_type=F32)
        if c + 2 < n_chunks:
            start_chunk(vc_ref, b, c + 2)
        else:
            @pl.when(b + 1 < DEC_BATCH)
            def _():
                start_chunk(kc_ref, b + 1, c + 2 - n_chunks)

    out = acc / l
    for hd in range(MOBA_HEADS):
        o_ref[:, hd * MOBA_HEAD_DIM:(hd + 1) * MOBA_HEAD_DIM] = out[hd * DEC_SEQ:(hd + 1) * DEC_SEQ]


def _moba_sample(page_table, mq_s, kn2d, vn2d, cache_k2d, cache_v2d):
    n_rows = MOBA_HEADS * DEC_SEQ
    seq_rows = lambda w, n: pl.BlockSpec((n, w), lambda b, pt: (b, 0))
    hbm = pl.BlockSpec(memory_space=pl.ANY)
    grid_spec = pltpu.PrefetchScalarGridSpec(
        num_scalar_prefetch=1,
        grid=(DEC_BATCH,),
        in_specs=[seq_rows(MOBA_WIDTH, DEC_SEQ), seq_rows(LANES, n_rows), seq_rows(LANES, n_rows), hbm, hbm],
        out_specs=seq_rows(MOBA_WIDTH, DEC_SEQ),
        scratch_shapes=[pltpu.VMEM((RING_PAGES, PAGE_ROWS, LANES), F32),
                        pltpu.SemaphoreType.DMA((RING_PAGES,)),
                        pltpu.VMEM((PAGES_PER_SEQ, n_rows, PAGE_ROWS), F32)],
    )
    return pl.pallas_call(
        _moba_sample_body,
        grid_spec=grid_spec,
        out_shape=jax.ShapeDtypeStruct((N_SAMPLE_TOK, MOBA_WIDTH), F32),
        compiler_params=pltpu.CompilerParams(dimension_semantics=("arbitrary",),
                                             vmem_limit_bytes=48 * 1024 * 1024),
        name="moba_sample",
    )(page_table.reshape(-1), mq_s, kn2d, vn2d, cache_k2d, cache_v2d)


def _layer_norm(x, g, b):
    mu = jnp.mean(x, axis=-1, keepdims=True)
    d = x - mu
    var = jnp.mean(d * d, axis=-1, keepdims=True)
    return d * lax.rsqrt(var + LN_EPS) * g + b


def _out_ffn_body(ar_ref, am_ref, x_ref, ga_ref, shf_ref, scf_ref, gf_ref, wo_ref, g1_ref, b1_ref,
                  wu_ref, wd_ref, g2_ref, b2_ref, y_ref):
    mixed = (jnp.dot(ar_ref[...].astype(BF16), wo_ref[:RET_WIDTH, :], preferred_element_type=F32)
             + jnp.dot(am_ref[...].astype(BF16), wo_ref[RET_WIDTH:, :], preferred_element_type=F32))
    x1 = _layer_norm(ALPHA * x_ref[...] + ga_ref[...] * mixed, g1_ref[...], b1_ref[...])
    h = (x1 * (1.0 + scf_ref[...]) + shf_ref[...]).astype(BF16)
    acc = jnp.zeros(x1.shape, F32)
    for c in range(D_FF // D_MODEL):
        cols = slice(c * D_MODEL, (c + 1) * D_MODEL)
        u = jnp.maximum(jnp.dot(h, wu_ref[:, cols], preferred_element_type=F32), 0.0)
        acc = acc + jnp.dot((u * u).astype(BF16), wd_ref[cols, :], preferred_element_type=F32)
    y_ref[...] = _layer_norm(ALPHA * x1 + gf_ref[...] * acc, g2_ref[...], b2_ref[...])


def _out_ffn(a_ret, a_moba, x2d, mod_specs, mod_args, weights, grid, row_map, tm):
    w_o, ln1_g, ln1_b, w_up, w_down, ln2_g, ln2_b = weights
    t = x2d.shape[0]
    wide = lambda w: pl.BlockSpec((tm, w), row_map)
    const = lambda a: pl.BlockSpec(a.shape, lambda *_: (0,) * a.ndim, pipeline_mode=pl.Buffered(1))
    return pl.pallas_call(
        _out_ffn_body,
        grid=grid,
        in_specs=[wide(RET_WIDTH), wide(MOBA_WIDTH), wide(D_MODEL)] + mod_specs + [const(a) for a in weights],
        out_specs=wide(D_MODEL),
        out_shape=jax.ShapeDtypeStruct((t, D_MODEL), F32),
        compiler_params=pltpu.CompilerParams(dimension_semantics=("arbitrary",) * len(grid),
                                             vmem_limit_bytes=56 * 1024 * 1024),
        name="out_ffn",
    )(a_ret, a_moba, x2d, *mod_args, *weights)


def kernel(x_prompt, x_sample, cache_k, cache_v, state_ret, page_table, c_prompt, c_sample,
           w_ada, b_ada, w_in, w_o, ln1_g, ln1_b, w_up, w_down, ln2_g, ln2_b):
    n_prompt_tok = BATCH * SEQ
    past_len = page_table.shape[1] * PAGE_SIZE

    mod = _adaln(jnp.concatenate([c_prompt, c_sample], axis=0), w_ada[0], b_ada)
    mod_rows = mod.reshape((BATCH + DEC_BATCH) * 6, 1, D_MODEL)
    mod_s = jnp.repeat(mod[BATCH:], DEC_SEQ, axis=0)

    def prompt_mod(chunk):
        return pl.BlockSpec((None, 1, D_MODEL), lambda b, i: (b * 6 + chunk, 0, 0))

    def sample_mod(chunk):
        return pl.BlockSpec((N_SAMPLE_TOK, D_MODEL), lambda i: (0, chunk))

    w_in_b = w_in[0].astype(BF16)
    weights = (w_o[0].astype(BF16), ln1_g, ln1_b, w_up[0].astype(BF16), w_down[0].astype(BF16), ln2_g, ln2_b)

    tm = 512
    nt = SEQ // tm
    p_row = lambda b, i: (b * nt + i, 0)
    p_tab = pl.BlockSpec((tm, LANES), lambda b, i: (i, 0))
    s_row = lambda i: (0, 0)
    s_tab = pl.BlockSpec((N_SAMPLE_TOK, LANES), s_row)

    pos_p = np.arange(SEQ, dtype=np.int32)
    pos_s = np.tile(past_len + np.arange(DEC_SEQ, dtype=np.int32), DEC_BATCH)
    tabs_p = _rope_tables(pos_p, MOBA_HEAD_DIM) + _rope_tables(pos_p, RET_DK)
    tabs_s = _rope_tables(pos_s, MOBA_HEAD_DIM) + _rope_tables(pos_s, RET_DK)

    xp = x_prompt.reshape(n_prompt_tok, D_MODEL)
    rq, rk, rv, rg, mq, k_p, v_p = _inproj(
        xp, [prompt_mod(1), prompt_mod(0)], [mod_rows, mod_rows], [p_tab] * 4, tabs_p, w_in_b,
        (BATCH, nt), p_row, tm, BF16)
    a_ret, state_p = _ret_prompt(rq, rk, rv, rg)
    a_moba = _moba_prompt(mq, k_p, v_p)
    y_p = _out_ffn(a_ret, a_moba, xp, [prompt_mod(c) for c in (2, 3, 4, 5)], [mod_rows] * 4, weights,
                   (BATCH, nt), p_row, tm)

    xs = x_sample.reshape(N_SAMPLE_TOK, D_MODEL)
    rq_s, rk_s, rv_s, rg_s, mq_s, k_s, v_s = _inproj(
        xs, [sample_mod(1), sample_mod(0)], [mod_s, mod_s], [s_tab] * 4, tabs_s, w_in_b,
        (1,), s_row, N_SAMPLE_TOK, F32)
    a_ret_s, state_s = _ret_sample(rq_s, rk_s, rv_s, rg_s, state_ret[0])
    cache_rows = cache_k.shape[1] * PAGE_ROWS
    a_moba_s = _moba_sample(page_table, mq_s, k_s, v_s,
                            cache_k.reshape(cache_rows, LANES), cache_v.reshape(cache_rows, LANES))
    y_s = _out_ffn(a_ret_s, a_moba_s, xs, [sample_mod(c) for c in (2, 3, 4, 5)], [mod_s] * 4, weights,
                   (1,), s_row, N_SAMPLE_TOK)

    kv_p_shape = (DEPTH, BATCH, SEQ, MOBA_HEADS, MOBA_HEAD_DIM)
    kv_s_shape = (DEPTH, DEC_BATCH, DEC_SEQ, MOBA_HEADS, MOBA_HEAD_DIM)
    return (y_p.reshape(BATCH, SEQ, D_MODEL),
            y_s.reshape(DEC_BATCH, DEC_SEQ, D_MODEL),
            k_p.reshape(kv_p_shape), v_p.reshape(kv_p_shape), state_p[None],
            k_s.reshape(kv_s_shape), v_s.reshape(kv_s_shape), state_s[None])
```

```python
import functools

import numpy as np
import jax
import jax.numpy as jnp
from jax import lax
from jax.experimental import pallas as pl
from jax.experimental.pallas import tpu as pltpu

F32 = jnp.float32
BF16 = jnp.bfloat16

D_MODEL = 1024
BATCH = 8
SEQ = 2048
DEC_BATCH = 32
DEC_SEQ = 8
PAGE_SIZE = 128
RET_HEADS = 4
RET_DK = 64
RET_DV = 128
RET_CHUNK = 128
MOBA_HEADS = 4
MOBA_HEAD_DIM = 128
MOBA_BLOCK = 256
MOBA_TOPK = 3
D_FF = 4 * D_MODEL
ROPE_THETA = 10000.0
LN_EPS = 1e-5
GN_EPS = 1e-6
DEPTH = 1
ALPHA = (2 * DEPTH) ** 0.25
RET_QK = RET_HEADS * RET_DK
RET_WIDTH = RET_HEADS * RET_DV
MOBA_WIDTH = MOBA_HEADS * MOBA_HEAD_DIM
IN_WIDTH = 2 * RET_QK + 2 * RET_WIDTH + 3 * MOBA_WIDTH
OFF_RQ, OFF_RK, OFF_RV, OFF_RG = 0, 256, 512, 1024
OFF_MQ, OFF_MK, OFF_MV = 1536, 2048, 2560
LANES = 128
N_SAMPLE_TOK = DEC_BATCH * DEC_SEQ
PAGES_PER_SEQ = 64
CHUNK_PAGES = 16
RING_PAGES = 2 * CHUNK_PAGES
PAGE_ROWS = PAGE_SIZE * MOBA_HEADS
NEG_INF = float("-inf")
LOG2_E = 1.4426950408889634

_NT = (((1,), (1,)), ((), ()))
_TN = (((0,), (0,)), ((), ()))


def _log_decay():
    return np.log1p(-np.exp2(-5.0 - np.arange(RET_HEADS, dtype=np.float64)))


def _adaln_body(c_ref, w_ref, b_ref, o_ref):
    c = c_ref[...]
    a = (c * jax.nn.sigmoid(c)).astype(BF16)
    o_ref[...] = jnp.dot(a, w_ref[...].astype(BF16), preferred_element_type=F32) + b_ref[...]


def _adaln(c_all, w_ada, b_ada):
    n = c_all.shape[0]
    tn = 1024
    return pl.pallas_call(
        _adaln_body,
        grid=(6 * D_MODEL // tn,),
        in_specs=[pl.BlockSpec((n, D_MODEL), lambda j: (0, 0)),
                  pl.BlockSpec((D_MODEL, tn), lambda j: (0, j)),
                  pl.BlockSpec((1, tn), lambda j: (0, j))],
        out_specs=pl.BlockSpec((n, tn), lambda j: (0, j)),
        out_shape=jax.ShapeDtypeStruct((n, 6 * D_MODEL), F32),
        name="adaln",
    )(c_all, w_ada, b_ada)


def _rope_tables(pos, head_dim):
    half = head_dim // 2
    inv_freq = np.power(ROPE_THETA, -np.arange(half, dtype=np.float64) / half)
    ang = pos.astype(np.float64)[:, None] * inv_freq[None, :]
    cos, sin = np.cos(ang), np.sin(ang)
    reps = LANES // head_dim
    cos_t = np.tile(np.concatenate([cos, cos], axis=-1), (1, reps))
    sin_t = np.tile(np.concatenate([-sin, sin], axis=-1), (1, reps))
    return jnp.asarray(cos_t, dtype=F32), jnp.asarray(sin_t, dtype=F32)


def _inproj_body(x_ref, sc_ref, sh_ref, w_ref, cm_ref, sm_ref, cr_ref, sr_ref,
                 rq_ref, rk_ref, rv_ref, rg_ref, mq_ref, ko_ref, vo_ref):
    tm = x_ref.shape[0]
    h = (x_ref[...] * (1.0 + sc_ref[...]) + sh_ref[...]).astype(BF16)

    def proj(lo, width):
        return jnp.dot(h, w_ref[:, lo:lo + width], preferred_element_type=F32)

    lane = lax.broadcasted_iota(jnp.int32, (tm, LANES), 1)
    low_half = (lane & (RET_DK - 1)) < (RET_DK // 2)
    cr, sr = cr_ref[...], sr_ref[...]
    cm, sm = cm_ref[...], sm_ref[...]

    def rope_ret(z):
        rot = jnp.where(low_half, pltpu.roll(z, LANES - RET_DK // 2, 1), pltpu.roll(z, RET_DK // 2, 1))
        return z * cr + rot * sr

    def rope_moba(z):
        return z * cm + pltpu.roll(z, MOBA_HEAD_DIM // 2, 1) * sm

    zq = proj(OFF_RQ, RET_QK)
    zk = proj(OFF_RK, RET_QK)
    for s in range(RET_QK // LANES):
        sl = slice(s * LANES, (s + 1) * LANES)
        rq_ref[:, sl] = rope_ret(zq[:, sl])
        rk_ref[:, sl] = rope_ret(zk[:, sl]) * (RET_DK ** -0.5)
    rv_ref[...] = proj(OFF_RV, RET_WIDTH).astype(rv_ref.dtype)
    rg_ref[...] = proj(OFF_RG, RET_WIDTH)
    zq = proj(OFF_MQ, MOBA_WIDTH)
    zk = proj(OFF_MK, MOBA_WIDTH)
    zv = proj(OFF_MV, MOBA_WIDTH)
    for hd in range(MOBA_HEADS):
        sl = slice(hd * LANES, (hd + 1) * LANES)
        mq_ref[:, sl] = rope_moba(zq[:, sl]).astype(mq_ref.dtype)
        ko_ref[pl.ds(hd, tm, stride=MOBA_HEADS), :] = rope_moba(zk[:, sl])
        vo_ref[pl.ds(hd, tm, stride=MOBA_HEADS), :] = zv[:, sl]


def _inproj(x2d, mod_specs, mod_args, tab_specs, tabs, w_in, grid, row_map, tm, act_dtype):
    t = x2d.shape[0]
    wide = lambda w: pl.BlockSpec((tm, w), row_map)
    return pl.pallas_call(
        _inproj_body,
        grid=grid,
        in_specs=[wide(D_MODEL)] + mod_specs + [pl.BlockSpec((D_MODEL, IN_WIDTH), lambda *_: (0, 0))] + tab_specs,
        out_specs=[wide(RET_QK), wide(RET_QK), wide(RET_WIDTH), wide(RET_WIDTH), wide(MOBA_WIDTH),
                   pl.BlockSpec((tm * MOBA_HEADS, LANES), row_map),
                   pl.BlockSpec((tm * MOBA_HEADS, LANES), row_map)],
        out_shape=[jax.ShapeDtypeStruct((t, RET_QK), F32), jax.ShapeDtypeStruct((t, RET_QK), F32),
                   jax.ShapeDtypeStruct((t, RET_WIDTH), act_dtype), jax.ShapeDtypeStruct((t, RET_WIDTH), F32),
                   jax.ShapeDtypeStruct((t, MOBA_WIDTH), act_dtype),
                   jax.ShapeDtypeStruct((t * MOBA_HEADS, LANES), F32),
                   jax.ShapeDtypeStruct((t * MOBA_HEADS, LANES), F32)],
        compiler_params=pltpu.CompilerParams(dimension_semantics=("arbitrary",) * len(grid),
                                             vmem_limit_bytes=48 * 1024 * 1024),
        name="inproj",
    )(x2d, *mod_args, w_in, *tabs)


def _group_norm_gate(o, g):
    mu = jnp.mean(o, axis=-1, keepdims=True)
    d = o - mu
    var = jnp.mean(d * d, axis=-1, keepdims=True)
    return d * lax.rsqrt(var + GN_EPS) * (g * jax.nn.sigmoid(g))


def _ret_prompt_tables():
    lg = _log_decay()
    i = np.arange(RET_CHUNK, dtype=np.float64)
    diff = i[:, None] - i[None, :]
    dmat = np.where(diff >= 0, np.exp(np.maximum(diff, 0.0)[None] * lg[:, None, None]), 0.0)
    lane_head = np.arange(RET_QK) // RET_DK
    qd = np.exp((i[:, None] + 1.0) * lg[lane_head][None, :])
    kd = np.exp((RET_CHUNK - 1.0 - i)[:, None] * lg[lane_head][None, :])
    row_head = np.arange(RET_QK) // RET_DK
    col_head = np.arange(RET_WIDTH) // RET_DV
    same = row_head[:, None] == col_head[None, :]
    cdec = np.where(same, np.exp(RET_CHUNK * lg[row_head])[:, None], 0.0)
    return [jnp.asarray(a, dtype=F32) for a in (dmat, qd, kd, cdec, same.astype(np.float64))]


def _ret_prompt_body(q_ref, k_ref, v_ref, g_ref, dmat_ref, qd_ref, kd_ref, cdec_ref, bdm_ref,
                     o_ref, st_ref, state_scr):
    state_scr[...] = jnp.zeros_like(state_scr)
    lane_head = lax.broadcasted_iota(jnp.int32, (RET_CHUNK, RET_QK), 1) >> 6

    def chunk(c, carry):
        rows = pl.ds(pl.multiple_of(c * RET_CHUNK, RET_CHUNK), RET_CHUNK)
        q = q_ref[rows, :]
        k = k_ref[rows, :]
        v = v_ref[rows, :]
        g = g_ref[rows, :]
        kb = k.astype(BF16)
        state = state_scr[...]
        cross = jnp.dot((q * qd_ref[...]).astype(BF16), state.astype(BF16), preferred_element_type=F32)
        for hd in range(RET_HEADS):
            sl = slice(hd * RET_DV, (hd + 1) * RET_DV)
            qm = jnp.where(lane_head == hd, q, 0.0).astype(BF16)
            s = lax.dot_general(qm, kb, _NT, preferred_element_type=F32) * dmat_ref[hd]
            inner = jnp.dot(s.astype(BF16), v[:, sl], preferred_element_type=F32)
            o_ref[rows, sl] = _group_norm_gate(inner + cross[:, sl], g[:, sl]).astype(o_ref.dtype)
        kv = lax.dot_general((k * kd_ref[...]).astype(BF16), v, _TN, preferred_element_type=F32)
        state_scr[...] = cdec_ref[...] * state + bdm_ref[...] * kv
        return carry

    lax.fori_loop(0, SEQ // RET_CHUNK, chunk, 0)
    for hd in range(RET_HEADS):
        st_ref[hd] = state_scr[hd * RET_DK:(hd + 1) * RET_DK, hd * RET_DV:(hd + 1) * RET_DV]


def _ret_prompt(rq, rk, rv, rg):
    tabs = _ret_prompt_tables()
    seq = lambda w: pl.BlockSpec((SEQ, w), lambda b: (b, 0))
    const = lambda a: pl.BlockSpec(a.shape, lambda b: (0,) * a.ndim)
    return pl.pallas_call(
        _ret_prompt_body,
        grid=(BATCH,),
        in_specs=[seq(RET_QK), seq(RET_QK), seq(RET_WIDTH), seq(RET_WIDTH)] + [const(a) for a in tabs],
        out_specs=[seq(RET_WIDTH), pl.BlockSpec((None, RET_HEADS, RET_DK, RET_DV), lambda b: (b, 0, 0, 0))],
        out_shape=[jax.ShapeDtypeStruct((BATCH * SEQ, RET_WIDTH), BF16),
                   jax.ShapeDtypeStruct((BATCH, RET_HEADS, RET_DK, RET_DV), F32)],
        scratch_shapes=[pltpu.VMEM((RET_QK, RET_WIDTH), F32)],
        compiler_params=pltpu.CompilerParams(dimension_semantics=("arbitrary",),
                                             vmem_limit_bytes=48 * 1024 * 1024),
        name="ret_prompt",
    )(rq, rk, rv, rg, *tabs)


def _ret_sample_tables():
    lg = _log_decay()
    t = np.arange(N_SAMPLE_TOK) % DEC_SEQ
    seq_id = np.arange(N_SAMPLE_TOK) // DEC_SEQ
    diff = (t[:, None] - t[None, :]).astype(np.float64)
    same_seq = seq_id[:, None] == seq_id[None, :]
    dmat = np.where(same_seq[None] & (diff >= 0)[None],
                    np.exp(np.maximum(diff, 0.0)[None] * lg[:, None, None]), 0.0)
    lane_head = np.arange(RET_QK) // RET_DK
    qd = np.exp((t[:, None] + 1.0) * lg[lane_head][None, :])
    kd = np.exp((DEC_SEQ - 1.0 - t)[:, None] * lg[lane_head][None, :])
    return [jnp.asarray(a, dtype=F32) for a in (dmat, qd, kd)]


def _ret_sample_body(q_ref, k_ref, v_ref, g_ref, st_ref, dmat_ref, qd_ref, kd_ref, o_ref, sto_ref):
    lg = _log_decay()
    q = q_ref[...]
    k = k_ref[...]
    kb = k.astype(BF16)
    qdec = q * qd_ref[...]
    kdec = k * kd_ref[...]
    vb = v_ref[...].astype(BF16)
    g = g_ref[...]
    lane = lax.broadcasted_iota(jnp.int32, (N_SAMPLE_TOK, LANES), 1)
    lane_head = lax.broadcasted_iota(jnp.int32, (N_SAMPLE_TOK, RET_QK), 1) >> 6
    n_state_rows = DEC_BATCH * RET_DK
    own_seq = ((lax.broadcasted_iota(jnp.int32, (N_SAMPLE_TOK, n_state_rows), 0) >> 3)
               == (lax.broadcasted_iota(jnp.int32, (N_SAMPLE_TOK, n_state_rows), 1) >> 6))
    for hd in range(RET_HEADS):
        sl = slice(hd * RET_DV, (hd + 1) * RET_DV)
        qm = jnp.where(lane_head == hd, q, 0.0).astype(BF16)
        s = lax.dot_general(qm, kb, _NT, preferred_element_type=F32) * dmat_ref[hd]
        inner = jnp.dot(s.astype(BF16), vb[:, sl], preferred_element_type=F32)

        def expand(z):
            slab = z[:, (hd // 2) * LANES:(hd // 2 + 1) * LANES]
            other = pltpu.roll(slab, RET_DK, 1)
            in_low = lane < RET_DK
            both = jnp.where(in_low, slab, other) if hd % 2 == 0 else jnp.where(in_low, other, slab)
            tiled = jnp.concatenate([both] * (n_state_rows // LANES), axis=1)
            return jnp.where(own_seq, tiled, 0.0).astype(BF16)

        st = st_ref[:, hd].reshape(n_state_rows, RET_DV)
        cross = jnp.dot(expand(qdec), st.astype(BF16), preferred_element_type=F32)
        o_ref[:, sl] = _group_norm_gate(inner + cross, g[:, sl])
        kv = lax.dot_general(expand(kdec), vb[:, sl], _TN, preferred_element_type=F32)
        new = float(np.exp(DEC_SEQ * lg[hd])) * st + kv
        sto_ref[:, hd] = new.reshape(DEC_BATCH, RET_DK, RET_DV)


def _ret_sample(rq, rk, rv, rg, state):
    tabs = _ret_sample_tables()
    full = lambda a: pl.BlockSpec(a.shape, lambda i: (0,) * a.ndim)
    args = (rq, rk, rv, rg, state, *tabs)
    return pl.pallas_call(
        _ret_sample_body,
        grid=(1,),
        in_specs=[full(a) for a in args],
        out_specs=[pl.BlockSpec((N_SAMPLE_TOK, RET_WIDTH), lambda i: (0, 0)),
                   pl.BlockSpec(state.shape, lambda i: (0, 0, 0, 0))],
        out_shape=[jax.ShapeDtypeStruct((N_SAMPLE_TOK, RET_WIDTH), F32),
                   jax.ShapeDtypeStruct(state.shape, F32)],
        compiler_params=pltpu.CompilerParams(dimension_semantics=("arbitrary",),
                                             vmem_limit_bytes=56 * 1024 * 1024),
        name="ret_sample",
    )(*args)


def _moba_prompt_body(q_ref, k_ref, v_ref, o_ref):
    hd = pl.program_id(1)
    n_blocks = SEQ // MOBA_BLOCK
    exp2_scale = MOBA_HEAD_DIM ** -0.5 * LOG2_E
    k32 = k_ref[pl.ds(hd, SEQ, stride=MOBA_HEADS), :]
    kb = k32.astype(BF16)
    vt = v_ref[pl.ds(hd, SEQ, stride=MOBA_HEADS), :].T.astype(BF16)
    kmean = jnp.sum(k32.reshape(n_blocks, MOBA_BLOCK, MOBA_HEAD_DIM), axis=1) * (1.0 / MOBA_BLOCK)
    kmb = kmean.astype(BF16)
    key_id = lax.broadcasted_iota(jnp.int32, (MOBA_BLOCK, MOBA_BLOCK), 0)
    qry_id = lax.broadcasted_iota(jnp.int32, (MOBA_BLOCK, MOBA_BLOCK), 1)
    causal = key_id <= qry_id

    for i in range(n_blocks):
        qi = q_ref[i * MOBA_BLOCK:(i + 1) * MOBA_BLOCK, :]
        blk = lambda n: slice(n * MOBA_BLOCK, (n + 1) * MOBA_BLOCK)
        st = [lax.dot_general(kb[blk(n)], qi, _NT, preferred_element_type=F32) for n in range(i + 1)]
        st[i] = jnp.where(causal, st[i], NEG_INF)
        if i > MOBA_TOPK:
            gt = lax.dot_general(kmb, qi, _NT, preferred_element_type=F32)
            for n in range(i):
                beats = jnp.zeros((1, MOBA_BLOCK), F32)
                for mm in range(i):
                    if mm == n:
                        continue
                    win = (gt[mm:mm + 1] >= gt[n:n + 1]) if mm < n else (gt[mm:mm + 1] > gt[n:n + 1])
                    beats = beats + win.astype(F32)
                st[n] = st[n] + jnp.where(beats < MOBA_TOPK, 0.0, NEG_INF)
        m = functools.reduce(jnp.maximum, [jnp.max(s, axis=0, keepdims=True) for s in st])
        l = jnp.zeros((1, MOBA_BLOCK), F32)
        acc = jnp.zeros((MOBA_HEAD_DIM, MOBA_BLOCK), F32)
        for n in range(i + 1):
            e = jnp.exp2((st[n] - m) * exp2_scale)
            l = l + jnp.sum(e, axis=0, keepdims=True)
            acc = acc + jnp.dot(vt[:, blk(n)], e.astype(BF16), preferred_element_type=F32)
        o_ref[blk(i), :] = (acc / l).T.astype(o_ref.dtype)


def _moba_prompt(mq, k2d, v2d):
    kv_spec = pl.BlockSpec((SEQ * MOBA_HEADS, LANES), lambda b, h: (b, 0))
    return pl.pallas_call(
        _moba_prompt_body,
        grid=(BATCH, MOBA_HEADS),
        in_specs=[pl.BlockSpec((SEQ, MOBA_HEAD_DIM), lambda b, h: (b, h)), kv_spec, kv_spec],
        out_specs=pl.BlockSpec((SEQ, MOBA_HEAD_DIM), lambda b, h: (b, h)),
        out_shape=jax.ShapeDtypeStruct((BATCH * SEQ, MOBA_WIDTH), BF16),
        compiler_params=pltpu.CompilerParams(dimension_semantics=("arbitrary", "arbitrary"),
                                             vmem_limit_bytes=56 * 1024 * 1024),
        name="moba_prompt",
    )(mq, k2d, v2d)


def _moba_sample_body(pt_ref, q_ref, kn_ref, vn_ref, *refs):
    kc_ref, vc_ref, o_ref, ring, sem, s_scr = refs
    b = pl.program_id(0)
    n_chunks = PAGES_PER_SEQ // CHUNK_PAGES
    n_rows = MOBA_HEADS * DEC_SEQ
    n_blocks = PAGES_PER_SEQ * PAGE_SIZE // MOBA_BLOCK
    pages_per_block = MOBA_BLOCK // PAGE_SIZE
    scale = MOBA_HEAD_DIM ** -0.5

    def page_copy(cache_ref, row0, slot):
        return pltpu.make_async_copy(cache_ref.at[pl.ds(row0, PAGE_ROWS)], ring.at[slot], sem.at[slot])

    def start_chunk(cache_ref, seq, chunk):
        for r in range(CHUNK_PAGES):
            page = pt_ref[seq * PAGES_PER_SEQ + chunk * CHUNK_PAGES + r]
            page_copy(cache_ref, pl.multiple_of(page * PAGE_ROWS, PAGE_ROWS),
                      (chunk % 2) * CHUNK_PAGES + r).start(priority=r % 2)

    def wait_chunk(cache_ref, chunk):
        for r in range(CHUNK_PAGES):
            page_copy(cache_ref, 0, (chunk % 2) * CHUNK_PAGES + r).wait()

    @pl.when(b == 0)
    def _():
        start_chunk(kc_ref, b, 0)
        start_chunk(kc_ref, b, 1)

    q = jnp.concatenate(
        [q_ref[:, hd * MOBA_HEAD_DIM:(hd + 1) * MOBA_HEAD_DIM] for hd in range(MOBA_HEADS)], axis=0
    ).astype(BF16)

    for c in range(n_chunks):
        wait_chunk(kc_ref, c)
        for r in range(CHUNK_PAGES):
            page = ring[(c % 2) * CHUNK_PAGES + r].astype(BF16)
            s_scr[c * CHUNK_PAGES + r] = lax.dot_general(q, page, _NT, preferred_element_type=F32)
        if c + 2 < n_chunks:
            start_chunk(kc_ref, b, c + 2)
        else:
            start_chunk(vc_ref, b, c + 2 - n_chunks)

    def softmax_stage():
        s = s_scr[...].reshape(n_blocks, pages_per_block, n_rows, PAGE_ROWS)
        row_head = lax.broadcasted_iota(jnp.int32, (n_rows, PAGE_ROWS), 0) >> 3
        col_head = lax.broadcasted_iota(jnp.int32, (n_rows, PAGE_ROWS), 1) & (MOBA_HEADS - 1)
        same_head = row_head == col_head
        sz = jnp.where(same_head, s, 0.0)
        gs = jnp.sum(sz[:, 0] + sz[:, 1], axis=-1, keepdims=True)
        lane = lax.broadcasted_iota(jnp.int32, (n_rows, LANES), 1)
        g_all = jnp.full((n_rows, LANES), NEG_INF, F32)
        for n in range(n_blocks):
            g_all = jnp.where(lane == n, gs[n], g_all)
        sel = []
        for n in range(n_blocks):
            wins = (g_all > gs[n]) | ((g_all == gs[n]) & (lane < n))
            beats = jnp.sum(wins.astype(F32), axis=-1, keepdims=True)
            sel.append((beats < MOBA_TOPK).astype(F32)[None])
        sel = jnp.concatenate(sel, axis=0)
        s = jnp.where(same_head & (sel[:, None] > 0.5), s, NEG_INF)
        m_past = jnp.max(jnp.max(s, axis=(0, 1)), axis=-1, keepdims=True)
        s_own = lax.dot_general(q, kn_ref[...].astype(BF16), _NT, preferred_element_type=F32)
        r_id = lax.broadcasted_iota(jnp.int32, (n_rows, n_rows), 0)
        c_id = lax.broadcasted_iota(jnp.int32, (n_rows, n_rows), 1)
        own_ok = ((c_id & (MOBA_HEADS - 1)) == (r_id >> 3)) & ((c_id >> 2) <= (r_id & (DEC_SEQ - 1)))
        s_own = jnp.where(own_ok, s_own, NEG_INF)
        m = jnp.maximum(m_past, jnp.max(s_own, axis=-1, keepdims=True))
        e = jnp.exp((s - m[None, None]) * scale)
        l = jnp.sum(jnp.sum(e, axis=(0, 1)), axis=-1, keepdims=True)
        s_scr[...] = e.reshape(PAGES_PER_SEQ, n_rows, PAGE_ROWS)
        e_own = jnp.exp((s_own - m) * scale)
        l = l + jnp.sum(e_own, axis=-1, keepdims=True)
        return l, jnp.dot(e_own.astype(BF16), vn_ref[...].astype(BF16), preferred_element_type=F32)

    l, acc = softmax_stage()

    for c in range(n_chunks):
        wait_chunk(vc_ref, c)
        for r in range(CHUNK_PAGES):
            p = s_scr[c * CHUNK_PAGES + r].astype(BF16)
            page = ring[(c % 2) * CHUNK_PAGES + r].astype(BF16)
            acc = acc + jnp.dot(p, page, preferred_element_type=F32)
        if c + 2 < n_chunks:
            start_chunk(vc_ref, b, c + 2)
        else:
            @pl.when(b + 1 < DEC_BATCH)
            def _():
                start_chunk(kc_ref, b + 1, c + 2 - n_chunks)

    out = acc / l
    for hd in range(MOBA_HEADS):
        o_ref[:, hd * MOBA_HEAD_DIM:(hd + 1) * MOBA_HEAD_DIM] = out[hd * DEC_SEQ:(hd + 1) * DEC_SEQ]


def _moba_sample(page_table, mq_s, kn2d, vn2d, cache_k2d, cache_v2d):
    n_rows = MOBA_HEADS * DEC_SEQ
    seq_rows = lambda w, n: pl.BlockSpec((n, w), lambda b, pt: (b, 0))
    hbm = pl.BlockSpec(memory_space=pl.ANY)
    grid_spec = pltpu.PrefetchScalarGridSpec(
        num_scalar_prefetch=1,
        grid=(DEC_BATCH,),
        in_specs=[seq_rows(MOBA_WIDTH, DEC_SEQ), seq_rows(LANES, n_rows), seq_rows(LANES, n_rows), hbm, hbm],
        out_specs=seq_rows(MOBA_WIDTH, DEC_SEQ),
        scratch_shapes=[pltpu.VMEM((RING_PAGES, PAGE_ROWS, LANES), F32),
                        pltpu.SemaphoreType.DMA((RING_PAGES,)),
                        pltpu.VMEM((PAGES_PER_SEQ, n_rows, PAGE_ROWS), F32)],
    )
    return pl.pallas_call(
        _moba_sample_body,
        grid_spec=grid_spec,
        out_shape=jax.ShapeDtypeStruct((N_SAMPLE_TOK, MOBA_WIDTH), F32),
        compiler_params=pltpu.CompilerParams(dimension_semantics=("arbitrary",),
                                             vmem_limit_bytes=48 * 1024 * 1024),
        name="moba_sample",
    )(page_table.reshape(-1), mq_s, kn2d, vn2d, cache_k2d, cache_v2d)


def _layer_norm(x, g, b):
    mu = jnp.mean(x, axis=-1, keepdims=True)
    d = x - mu
    var = jnp.mean(d * d, axis=-1, keepdims=True)
    return d * lax.rsqrt(var + LN_EPS) * g + b


def _out_ffn_body(ar_ref, am_ref, x_ref, ga_ref, shf_ref, scf_ref, gf_ref, wo_ref, g1_ref, b1_ref,
                  wu_ref, wd_ref, g2_ref, b2_ref, y_ref):
    mixed = (jnp.dot(ar_ref[...].astype(BF16), wo_ref[:RET_WIDTH, :], preferred_element_type=F32)
             + jnp.dot(am_ref[...].astype(BF16), wo_ref[RET_WIDTH:, :], preferred_element_type=F32))
    x1 = _layer_norm(ALPHA * x_ref[...] + ga_ref[...] * mixed, g1_ref[...], b1_ref[...])
    h = (x1 * (1.0 + scf_ref[...]) + shf_ref[...]).astype(BF16)
    acc = jnp.zeros(x1.shape, F32)
    for c in range(D_FF // D_MODEL):
        cols = slice(c * D_MODEL, (c + 1) * D_MODEL)
        u = jnp.maximum(jnp.dot(h, wu_ref[:, cols], preferred_element_type=F32), 0.0)
        acc = acc + jnp.dot((u * u).astype(BF16), wd_ref[cols, :], preferred_element_type=F32)
    y_ref[...] = _layer_norm(ALPHA * x1 + gf_ref[...] * acc, g2_ref[...], b2_ref[...])


def _out_ffn(a_ret, a_moba, x2d, mod_specs, mod_args, weights, grid, row_map, tm):
    w_o, ln1_g, ln1_b, w_up, w_down, ln2_g, ln2_b = weights
    t = x2d.shape[0]
    wide = lambda w: pl.BlockSpec((tm, w), row_map)
    const = lambda a: pl.BlockSpec(a.shape, lambda *_: (0,) * a.ndim, pipeline_mode=pl.Buffered(1))
    return pl.pallas_call(
        _out_ffn_body,
        grid=grid,
        in_specs=[wide(RET_WIDTH), wide(MOBA_WIDTH), wide(D_MODEL)] + mod_specs + [const(a) for a in weights],
        out_specs=wide(D_MODEL),
        out_shape=jax.ShapeDtypeStruct((t, D_MODEL), F32),
        compiler_params=pltpu.CompilerParams(dimension_semantics=("arbitrary",) * len(grid),
                                             vmem_limit_bytes=56 * 1024 * 1024),
        name="out_ffn",
    )(a_ret, a_moba, x2d, *mod_args, *weights)


def kernel(x_prompt, x_sample, cache_k, cache_v, state_ret, page_table, c_prompt, c_sample,
           w_ada, b_ada, w_in, w_o, ln1_g, ln1_b, w_up, w_down, ln2_g, ln2_b):
    n_prompt_tok = BATCH * SEQ
    past_len = page_table.shape[1] * PAGE_SIZE

    mod = _adaln(jnp.concatenate([c_prompt, c_sample], axis=0), w_ada[0], b_ada)
    mod_rows = mod.reshape((BATCH + DEC_BATCH) * 6, 1, D_MODEL)
    mod_s = jnp.repeat(mod[BATCH:], DEC_SEQ, axis=0)

    def prompt_mod(chunk):
        return pl.BlockSpec((None, 1, D_MODEL), lambda b, i: (b * 6 + chunk, 0, 0))

    def sample_mod(chunk):
        return pl.BlockSpec((N_SAMPLE_TOK, D_MODEL), lambda i: (0, chunk))

    w_in_b = w_in[0].astype(BF16)
    weights = (w_o[0].astype(BF16), ln1_g, ln1_b, w_up[0].astype(BF16), w_down[0].astype(BF16), ln2_g, ln2_b)

    tm = 512
    nt = SEQ // tm
    p_row = lambda b, i: (b * nt + i, 0)
    p_tab = pl.BlockSpec((tm, LANES), lambda b, i: (i, 0))
    s_row = lambda i: (0, 0)
    s_tab = pl.BlockSpec((N_SAMPLE_TOK, LANES), s_row)

    pos_p = np.arange(SEQ, dtype=np.int32)
    pos_s = np.tile(past_len + np.arange(DEC_SEQ, dtype=np.int32), DEC_BATCH)
    tabs_p = _rope_tables(pos_p, MOBA_HEAD_DIM) + _rope_tables(pos_p, RET_DK)
    tabs_s = _rope_tables(pos_s, MOBA_HEAD_DIM) + _rope_tables(pos_s, RET_DK)

    xp = x_prompt.reshape(n_prompt_tok, D_MODEL)
    rq, rk, rv, rg, mq, k_p, v_p = _inproj(
        xp, [prompt_mod(1), prompt_mod(0)], [mod_rows, mod_rows], [p_tab] * 4, tabs_p, w_in_b,
        (BATCH, nt), p_row, tm, BF16)
    a_ret, state_p = _ret_prompt(rq, rk, rv, rg)
    a_moba = _moba_prompt(mq, k_p, v_p)
    y_p = _out_ffn(a_ret, a_moba, xp, [prompt_mod(c) for c in (2, 3, 4, 5)], [mod_rows] * 4, weights,
                   (BATCH, nt), p_row, tm)

    xs = x_sample.reshape(N_SAMPLE_TOK, D_MODEL)
    rq_s, rk_s, rv_s, rg_s, mq_s, k_s, v_s = _inproj(
        xs, [sample_mod(1), sample_mod(0)], [mod_s, mod_s], [s_tab] * 4, tabs_s, w_in_b,
        (1,), s_row, N_SAMPLE_TOK, F32)
    a_ret_s, state_s = _ret_sample(rq_s, rk_s, rv_s, rg_s, state_ret[0])
    cache_rows = cache_k.shape[1] * PAGE_ROWS
    a_moba_s = _moba_sample(page_table, mq_s, k_s, v_s,
                            cache_k.reshape(cache_rows, LANES), cache_v.reshape(cache_rows, LANES))
    y_s = _out_ffn(a_ret_s, a_moba_s, xs, [sample_mod(c) for c in (2, 3, 4, 5)], [mod_s] * 4, weights,
                   (1,), s_row, N_SAMPLE_TOK)

    kv_p_shape = (DEPTH, BATCH, SEQ, MOBA_HEADS, MOBA_HEAD_DIM)
    kv_s_shape = (DEPTH, DEC_BATCH, DEC_SEQ, MOBA_HEADS, MOBA_HEAD_DIM)
    return (y_p.reshape(BATCH, SEQ, D_MODEL),
            y_s.reshape(DEC_BATCH, DEC_SEQ, D_MODEL),
            k_p.reshape(kv_p_shape), v_p.reshape(kv_p_shape), state_p[None],
            k_s.reshape(kv_s_shape), v_s.reshape(kv_s_shape), state_s[None])
```

```python
import functools

import numpy as np
import jax
import jax.numpy as jnp
from jax import lax
from jax.experimental import pallas as pl
from jax.experimental.pallas import tpu as pltpu

F32 = jnp.float32
BF16 = jnp.bfloat16

D_MODEL = 1024
BATCH = 8
SEQ = 2048
DEC_BATCH = 32
DEC_SEQ = 8
PAGE_SIZE = 128
RET_HEADS = 4
RET_DK = 64
RET_DV = 128
RET_CHUNK = 128
MOBA_HEADS = 4
MOBA_HEAD_DIM = 128
MOBA_BLOCK = 256
MOBA_TOPK = 3
D_FF = 4 * D_MODEL
ROPE_THETA = 10000.0
LN_EPS = 1e-5
GN_EPS = 1e-6
DEPTH = 1
ALPHA = (2 * DEPTH) ** 0.25
RET_QK = RET_HEADS * RET_DK
RET_WIDTH = RET_HEADS * RET_DV
MOBA_WIDTH = MOBA_HEADS * MOBA_HEAD_DIM
IN_WIDTH = 2 * RET_QK + 2 * RET_WIDTH + 3 * MOBA_WIDTH
OFF_RQ, OFF_RK, OFF_RV, OFF_RG = 0, 256, 512, 1024
OFF_MQ, OFF_MK, OFF_MV = 1536, 2048, 2560
LANES = 128
N_SAMPLE_TOK = DEC_BATCH * DEC_SEQ
PAGES_PER_SEQ = 64
CHUNK_PAGES = 16
RING_PAGES = 2 * CHUNK_PAGES
PAGE_ROWS = PAGE_SIZE * MOBA_HEADS
NEG_INF = float("-inf")
LOG2_E = 1.4426950408889634

_NT = (((1,), (1,)), ((), ()))
_TN = (((0,), (0,)), ((), ()))


def _log_decay():
    return np.log1p(-np.exp2(-5.0 - np.arange(RET_HEADS, dtype=np.float64)))


def _adaln_body(c_ref, w_ref, b_ref, o_ref):
    c = c_ref[...]
    a = (c * jax.nn.sigmoid(c)).astype(BF16)
    o_ref[...] = jnp.dot(a, w_ref[...].astype(BF16), preferred_element_type=F32) + b_ref[...]


def _adaln(c_all, w_ada, b_ada):
    n = c_all.shape[0]
    tn = 1024
    return pl.pallas_call(
        _adaln_body,
        grid=(6 * D_MODEL // tn,),
        in_specs=[pl.BlockSpec((n, D_MODEL), lambda j: (0, 0)),
                  pl.BlockSpec((D_MODEL, tn), lambda j: (0, j)),
                  pl.BlockSpec((1, tn), lambda j: (0, j))],
        out_specs=pl.BlockSpec((n, tn), lambda j: (0, j)),
        out_shape=jax.ShapeDtypeStruct((n, 6 * D_MODEL), F32),
        name="adaln",
    )(c_all, w_ada, b_ada)


def _rope_tables(pos, head_dim):
    half = head_dim // 2
    inv_freq = np.power(ROPE_THETA, -np.arange(half, dtype=np.float64) / half)
    ang = pos.astype(np.float64)[:, None] * inv_freq[None, :]
    cos, sin = np.cos(ang), np.sin(ang)
    reps = LANES // head_dim
    cos_t = np.tile(np.concatenate([cos, cos], axis=-1), (1, reps))
    sin_t = np.tile(np.concatenate([-sin, sin], axis=-1), (1, reps))
    return jnp.asarray(cos_t, dtype=F32), jnp.asarray(sin_t, dtype=F32)


def _inproj_body(x_ref, sc_ref, sh_ref, w_ref, cm_ref, sm_ref, cr_ref, sr_ref,
                 rq_ref, rk_ref, rv_ref, rg_ref, mq_ref, ko_ref, vo_ref):
    tm = x_ref.shape[0]
    h = (x_ref[...] * (1.0 + sc_ref[...]) + sh_ref[...]).astype(BF16)

    def proj(lo, width):
        return jnp.dot(h, w_ref[:, lo:lo + width], preferred_element_type=F32)

    lane = lax.broadcasted_iota(jnp.int32, (tm, LANES), 1)
    low_half = (lane & (RET_DK - 1)) < (RET_DK // 2)
    cr, sr = cr_ref[...], sr_ref[...]
    cm, sm = cm_ref[...], sm_ref[...]

    def rope_ret(z):
        rot = jnp.where(low_half, pltpu.roll(z, LANES - RET_DK // 2, 1), pltpu.roll(z, RET_DK // 2, 1))
        return z * cr + rot * sr

    def rope_moba(z):
        return z * cm + pltpu.roll(z, MOBA_HEAD_DIM // 2, 1) * sm

    zq = proj(OFF_RQ, RET_QK)
    zk = proj(OFF_RK, RET_QK)
    for s in range(RET_QK // LANES):
        sl = slice(s * LANES, (s + 1) * LANES)
        rq_ref[:, sl] = rope_ret(zq[:, sl])
        rk_ref[:, sl] = rope_ret(zk[:, sl]) * (RET_DK ** -0.5)
    rv_ref[...] = proj(OFF_RV, RET_WIDTH).astype(rv_ref.dtype)
    rg_ref[...] = proj(OFF_RG, RET_WIDTH)
    zq = proj(OFF_MQ, MOBA_WIDTH)
    zk = proj(OFF_MK, MOBA_WIDTH)
    zv = proj(OFF_MV, MOBA_WIDTH)
    for hd in range(MOBA_HEADS):
        sl = slice(hd * LANES, (hd + 1) * LANES)
        mq_ref[:, sl] = rope_moba(zq[:, sl]).astype(mq_ref.dtype)
        ko_ref[pl.ds(hd, tm, stride=MOBA_HEADS), :] = rope_moba(zk[:, sl])
        vo_ref[pl.ds(hd, tm, stride=MOBA_HEADS), :] = zv[:, sl]


def _inproj(x2d, mod_specs, mod_args, tab_specs, tabs, w_in, grid, row_map, tm, act_dtype):
    t = x2d.shape[0]
    wide = lambda w: pl.BlockSpec((tm, w), row_map)
    return pl.pallas_call(
        _inproj_body,
        grid=grid,
        in_specs=[wide(D_MODEL)] + mod_specs + [pl.BlockSpec((D_MODEL, IN_WIDTH), lambda *_: (0, 0))] + tab_specs,
        out_specs=[wide(RET_QK), wide(RET_QK), wide(RET_WIDTH), wide(RET_WIDTH), wide(MOBA_WIDTH),
                   pl.BlockSpec((tm * MOBA_HEADS, LANES), row_map),
                   pl.BlockSpec((tm * MOBA_HEADS, LANES), row_map)],
        out_shape=[jax.ShapeDtypeStruct((t, RET_QK), F32), jax.ShapeDtypeStruct((t, RET_QK), F32),
                   jax.ShapeDtypeStruct((t, RET_WIDTH), act_dtype), jax.ShapeDtypeStruct((t, RET_WIDTH), F32),
                   jax.ShapeDtypeStruct((t, MOBA_WIDTH), act_dtype),
                   jax.ShapeDtypeStruct((t * MOBA_HEADS, LANES), F32),
                   jax.ShapeDtypeStruct((t * MOBA_HEADS, LANES), F32)],
        compiler_params=pltpu.CompilerParams(dimension_semantics=("arbitrary",) * len(grid),
                                             vmem_limit_bytes=48 * 1024 * 1024),
        name="inproj",
    )(x2d, *mod_args, w_in, *tabs)


def _group_norm_gate(o, g):
    mu = jnp.mean(o, axis=-1, keepdims=True)
    d = o - mu
    var = jnp.mean(d * d, axis=-1, keepdims=True)
    return d * lax.rsqrt(var + GN_EPS) * (g * jax.nn.sigmoid(g))


def _ret_prompt_tables():
    lg = _log_decay()
    i = np.arange(RET_CHUNK, dtype=np.float64)
    diff = i[:, None] - i[None, :]
    dmat = np.where(diff >= 0, np.exp(np.maximum(diff, 0.0)[None] * lg[:, None, None]), 0.0)
    lane_head = np.arange(RET_QK) // RET_DK
    qd = np.exp((i[:, None] + 1.0) * lg[lane_head][None, :])
    kd = np.exp((RET_CHUNK - 1.0 - i)[:, None] * lg[lane_head][None, :])
    row_head = np.arange(RET_QK) // RET_DK
    col_head = np.arange(RET_WIDTH) // RET_DV
    same = row_head[:, None] == col_head[None, :]
    cdec = np.where(same, np.exp(RET_CHUNK * lg[row_head])[:, None], 0.0)
    return [jnp.asarray(a, dtype=F32) for a in (dmat, qd, kd, cdec, same.astype(np.float64))]


def _ret_prompt_body(q_ref, k_ref, v_ref, g_ref, dmat_ref, qd_ref, kd_ref, cdec_ref, bdm_ref,
                     o_ref, st_ref, state_scr):
    lane_head = lax.broadcasted_iota(jnp.int32, (RET_CHUNK, RET_QK), 1) >> 6
    chunk_rows = lambda c: slice(c * RET_CHUNK, (c + 1) * RET_CHUNK)

    def first_matmuls(c):
        rows = chunk_rows(c)
        q = q_ref[rows, :]
        k = k_ref[rows, :]
        v = v_ref[rows, :]
        kb = k.astype(BF16)
        state = state_scr[...]
        scores = [lax.dot_general(jnp.where(lane_head == hd, q, 0.0).astype(BF16), kb, _NT,
                                  preferred_element_type=F32) for hd in range(RET_HEADS)]
        cross = jnp.dot((q * qd_ref[...]).astype(BF16), state.astype(BF16), preferred_element_type=F32)
        kv = lax.dot_general((k * kd_ref[...]).astype(BF16), v, _TN, preferred_element_type=F32)
        state_scr[...] = cdec_ref[...] * state + bdm_ref[...] * kv
        return scores, cross, v

    def second_matmuls(c, scores, cross, v):
        rows = chunk_rows(c)
        g = g_ref[rows, :]
        decayed = [(scores[hd] * dmat_ref[hd]).astype(BF16) for hd in range(RET_HEADS)]
        for hd in range(RET_HEADS):
            sl = slice(hd * RET_DV, (hd + 1) * RET_DV)
            inner = jnp.dot(decayed[hd], v[:, sl], preferred_element_type=F32)
            o_ref[rows, sl] = _group_norm_gate(inner + cross[:, sl], g[:, sl]).astype(o_ref.dtype)

    state_scr[...] = jnp.zeros_like(state_scr)
    n_chunks = SEQ // RET_CHUNK
    ahead = first_matmuls(0)
    for c in range(n_chunks):
        current = ahead
        if c + 1 < n_chunks:
            ahead = first_matmuls(c + 1)
        second_matmuls(c, *current)
    for hd in range(RET_HEADS):
        st_ref[hd] = state_scr[hd * RET_DK:(hd + 1) * RET_DK, hd * RET_DV:(hd + 1) * RET_DV]


def _ret_prompt(rq, rk, rv, rg):
    tabs = _ret_prompt_tables()
    seq = lambda w: pl.BlockSpec((SEQ, w), lambda b: (b, 0))
    const = lambda a: pl.BlockSpec(a.shape, lambda b: (0,) * a.ndim)
    return pl.pallas_call(
        _ret_prompt_body,
        grid=(BATCH,),
        in_specs=[seq(RET_QK), seq(RET_QK), seq(RET_WIDTH), seq(RET_WIDTH)] + [const(a) for a in tabs],
        out_specs=[seq(RET_WIDTH), pl.BlockSpec((None, RET_HEADS, RET_DK, RET_DV), lambda b: (b, 0, 0, 0))],
        out_shape=[jax.ShapeDtypeStruct((BATCH * SEQ, RET_WIDTH), BF16),
                   jax.ShapeDtypeStruct((BATCH, RET_HEADS, RET_DK, RET_DV), F32)],
        scratch_shapes=[pltpu.VMEM((RET_QK, RET_WIDTH), F32)],
        compiler_params=pltpu.CompilerParams(dimension_semantics=("arbitrary",),
                                             vmem_limit_bytes=48 * 1024 * 1024),
        name="ret_prompt",
    )(rq, rk, rv, rg, *tabs)


def _ret_sample_tables():
    lg = _log_decay()
    t = np.arange(N_SAMPLE_TOK) % DEC_SEQ
    seq_id = np.arange(N_SAMPLE_TOK) // DEC_SEQ
    diff = (t[:, None] - t[None, :]).astype(np.float64)
    same_seq = seq_id[:, None] == seq_id[None, :]
    dmat = np.where(same_seq[None] & (diff >= 0)[None],
                    np.exp(np.maximum(diff, 0.0)[None] * lg[:, None, None]), 0.0)
    lane_head = np.arange(RET_QK) // RET_DK
    qd = np.exp((t[:, None] + 1.0) * lg[lane_head][None, :])
    kd = np.exp((DEC_SEQ - 1.0 - t)[:, None] * lg[lane_head][None, :])
    return [jnp.asarray(a, dtype=F32) for a in (dmat, qd, kd)]


def _ret_sample_body(q_ref, k_ref, v_ref, g_ref, st_ref, dmat_ref, qd_ref, kd_ref, o_ref, sto_ref):
    lg = _log_decay()
    q = q_ref[...]
    k = k_ref[...]
    kb = k.astype(BF16)
    qdec = q * qd_ref[...]
    kdec = k * kd_ref[...]
    vb = v_ref[...].astype(BF16)
    g = g_ref[...]
    lane = lax.broadcasted_iota(jnp.int32, (N_SAMPLE_TOK, LANES), 1)
    lane_head = lax.broadcasted_iota(jnp.int32, (N_SAMPLE_TOK, RET_QK), 1) >> 6
    n_state_rows = DEC_BATCH * RET_DK
    own_seq = ((lax.broadcasted_iota(jnp.int32, (N_SAMPLE_TOK, n_state_rows), 0) >> 3)
               == (lax.broadcasted_iota(jnp.int32, (N_SAMPLE_TOK, n_state_rows), 1) >> 6))
    for hd in range(RET_HEADS):
        sl = slice(hd * RET_DV, (hd + 1) * RET_DV)
        qm = jnp.where(lane_head == hd, q, 0.0).astype(BF16)
        s = lax.dot_general(qm, kb, _NT, preferred_element_type=F32) * dmat_ref[hd]
        inner = jnp.dot(s.astype(BF16), vb[:, sl], preferred_element_type=F32)

        def expand(z):
            slab = z[:, (hd // 2) * LANES:(hd // 2 + 1) * LANES]
            other = pltpu.roll(slab, RET_DK, 1)
            in_low = lane < RET_DK
            both = jnp.where(in_low, slab, other) if hd % 2 == 0 else jnp.where(in_low, other, slab)
            tiled = jnp.concatenate([both] * (n_state_rows // LANES), axis=1)
            return jnp.where(own_seq, tiled, 0.0).astype(BF16)

        st = st_ref[:, hd].reshape(n_state_rows, RET_DV)
        cross = jnp.dot(expand(qdec), st.astype(BF16), preferred_element_type=F32)
        o_ref[:, sl] = _group_norm_gate(inner + cross, g[:, sl])
        kv = lax.dot_general(expand(kdec), vb[:, sl], _TN, preferred_element_type=F32)
        new = float(np.exp(DEC_SEQ * lg[hd])) * st + kv
        sto_ref[:, hd] = new.reshape(DEC_BATCH, RET_DK, RET_DV)


def _ret_sample(rq, rk, rv, rg, state):
    tabs = _ret_sample_tables()
    full = lambda a: pl.BlockSpec(a.shape, lambda i: (0,) * a.ndim)
    args = (rq, rk, rv, rg, state, *tabs)
    return pl.pallas_call(
        _ret_sample_body,
        grid=(1,),
        in_specs=[full(a) for a in args],
        out_specs=[pl.BlockSpec((N_SAMPLE_TOK, RET_WIDTH), lambda i: (0, 0)),
                   pl.BlockSpec(state.shape, lambda i: (0, 0, 0, 0))],
        out_shape=[jax.ShapeDtypeStruct((N_SAMPLE_TOK, RET_WIDTH), F32),
                   jax.ShapeDtypeStruct(state.shape, F32)],
        compiler_params=pltpu.CompilerParams(dimension_semantics=("arbitrary",),
                                             vmem_limit_bytes=56 * 1024 * 1024),
        name="ret_sample",
    )(*args)


def _moba_prompt_body(q_ref, k_ref, v_ref, o_ref):
    hd = pl.program_id(1)
    n_blocks = SEQ // MOBA_BLOCK
    exp2_scale = MOBA_HEAD_DIM ** -0.5 * LOG2_E
    k32 = k_ref[pl.ds(hd, SEQ, stride=MOBA_HEADS), :]
    kb = k32.astype(BF16)
    vt = v_ref[pl.ds(hd, SEQ, stride=MOBA_HEADS), :].T.astype(BF16)
    kmean = jnp.sum(k32.reshape(n_blocks, MOBA_BLOCK, MOBA_HEAD_DIM), axis=1) * (1.0 / MOBA_BLOCK)
    kmb = kmean.astype(BF16)
    key_id = lax.broadcasted_iota(jnp.int32, (MOBA_BLOCK, MOBA_BLOCK), 0)
    qry_id = lax.broadcasted_iota(jnp.int32, (MOBA_BLOCK, MOBA_BLOCK), 1)
    causal = key_id <= qry_id

    blk = lambda n: slice(n * MOBA_BLOCK, (n + 1) * MOBA_BLOCK)

    def score_matmuls(i):
        qi = q_ref[blk(i), :]
        st = [lax.dot_general(kb[blk(n)], qi, _NT, preferred_element_type=F32) for n in range(i + 1)]
        gt = lax.dot_general(kmb, qi, _NT, preferred_element_type=F32) if i > MOBA_TOPK else None
        return st, gt

    ahead = score_matmuls(0)
    for i in range(n_blocks):
        st, gt = ahead
        if i + 1 < n_blocks:
            ahead = score_matmuls(i + 1)
        st[i] = jnp.where(causal, st[i], NEG_INF)
        if i > MOBA_TOPK:
            for n in range(i):
                beats = jnp.zeros((1, MOBA_BLOCK), F32)
                for mm in range(i):
                    if mm == n:
                        continue
                    win = (gt[mm:mm + 1] >= gt[n:n + 1]) if mm < n else (gt[mm:mm + 1] > gt[n:n + 1])
                    beats = beats + win.astype(F32)
                st[n] = st[n] + jnp.where(beats < MOBA_TOPK, 0.0, NEG_INF)
        m = functools.reduce(jnp.maximum, [jnp.max(s, axis=0, keepdims=True) for s in st])
        l = jnp.zeros((1, MOBA_BLOCK), F32)
        acc = jnp.zeros((MOBA_HEAD_DIM, MOBA_BLOCK), F32)
        for n in range(i + 1):
            e = jnp.exp2((st[n] - m) * exp2_scale)
            l = l + jnp.sum(e, axis=0, keepdims=True)
            acc = acc + jnp.dot(vt[:, blk(n)], e.astype(BF16), preferred_element_type=F32)
        o_ref[blk(i), :] = (acc / l).T.astype(o_ref.dtype)


def _moba_prompt(mq, k2d, v2d):
    kv_spec = pl.BlockSpec((SEQ * MOBA_HEADS, LANES), lambda b, h: (b, 0))
    return pl.pallas_call(
        _moba_prompt_body,
        grid=(BATCH, MOBA_HEADS),
        in_specs=[pl.BlockSpec((SEQ, MOBA_HEAD_DIM), lambda b, h: (b, h)), kv_spec, kv_spec],
        out_specs=pl.BlockSpec((SEQ, MOBA_HEAD_DIM), lambda b, h: (b, h)),
        out_shape=jax.ShapeDtypeStruct((BATCH * SEQ, MOBA_WIDTH), BF16),
        compiler_params=pltpu.CompilerParams(dimension_semantics=("arbitrary", "arbitrary"),
                                             vmem_limit_bytes=56 * 1024 * 1024),
        name="moba_prompt",
    )(mq, k2d, v2d)


def _moba_sample_body(pt_ref, q_ref, kn_ref, vn_ref, *refs):
    kc_ref, vc_ref, o_ref, ring, sem, s_scr = refs
    b = pl.program_id(0)
    n_chunks = PAGES_PER_SEQ // CHUNK_PAGES
    n_rows = MOBA_HEADS * DEC_SEQ
    n_blocks = PAGES_PER_SEQ * PAGE_SIZE // MOBA_BLOCK
    pages_per_block = MOBA_BLOCK // PAGE_SIZE
    scale = MOBA_HEAD_DIM ** -0.5

    def page_copy(cache_ref, row0, slot):
        return pltpu.make_async_copy(cache_ref.at[pl.ds(row0, PAGE_ROWS)], ring.at[slot], sem.at[slot])

    def start_chunk(cache_ref, seq, chunk):
        for r in range(CHUNK_PAGES):
            page = pt_ref[seq * PAGES_PER_SEQ + chunk * CHUNK_PAGES + r]
            page_copy(cache_ref, pl.multiple_of(page * PAGE_ROWS, PAGE_ROWS), (chunk % 2) * CHUNK_PAGES + r).start()

    def wait_chunk(cache_ref, chunk):
        for r in range(CHUNK_PAGES):
            page_copy(cache_ref, 0, (chunk % 2) * CHUNK_PAGES + r).wait()

    @pl.when(b == 0)
    def _():
        start_chunk(kc_ref, b, 0)
        start_chunk(kc_ref, b, 1)

    q = jnp.concatenate(
        [q_ref[:, hd * MOBA_HEAD_DIM:(hd + 1) * MOBA_HEAD_DIM] for hd in range(MOBA_HEADS)], axis=0
    ).astype(BF16)

    for c in range(n_chunks):
        wait_chunk(kc_ref, c)
        for r in range(CHUNK_PAGES):
            page = ring[(c % 2) * CHUNK_PAGES + r].astype(BF16)
            s_scr[c * CHUNK_PAGES + r] = lax.dot_general(q, page, _NT, preferred_element_type=F32)
        if c + 2 < n_chunks:
            start_chunk(kc_ref, b, c + 2)
        else:
            start_chunk(vc_ref, b, c + 2 - n_chunks)

    def softmax_stage():
        s = s_scr[...].reshape(n_blocks, pages_per_block, n_rows, PAGE_ROWS)
        row_head = lax.broadcasted_iota(jnp.int32, (n_rows, PAGE_ROWS), 0) >> 3
        col_head = lax.broadcasted_iota(jnp.int32, (n_rows, PAGE_ROWS), 1) & (MOBA_HEADS - 1)
        same_head = row_head == col_head
        sz = jnp.where(same_head, s, 0.0)
        gs = jnp.sum(sz[:, 0] + sz[:, 1], axis=-1, keepdims=True)
        lane = lax.broadcasted_iota(jnp.int32, (n_rows, LANES), 1)
        g_all = jnp.full((n_rows, LANES), NEG_INF, F32)
        for n in range(n_blocks):
            g_all = jnp.where(lane == n, gs[n], g_all)
        sel = []
        for n in range(n_blocks):
            wins = (g_all > gs[n]) | ((g_all == gs[n]) & (lane < n))
            beats = jnp.sum(wins.astype(F32), axis=-1, keepdims=True)
            sel.append((beats < MOBA_TOPK).astype(F32)[None])
        sel = jnp.concatenate(sel, axis=0)
        s = jnp.where(same_head & (sel[:, None] > 0.5), s, NEG_INF)
        m_past = jnp.max(jnp.max(s, axis=(0, 1)), axis=-1, keepdims=True)
        s_own = lax.dot_general(q, kn_ref[...].astype(BF16), _NT, preferred_element_type=F32)
        r_id = lax.broadcasted_iota(jnp.int32, (n_rows, n_rows), 0)
        c_id = lax.broadcasted_iota(jnp.int32, (n_rows, n_rows), 1)
        own_ok = ((c_id & (MOBA_HEADS - 1)) == (r_id >> 3)) & ((c_id >> 2) <= (r_id & (DEC_SEQ - 1)))
        s_own = jnp.where(own_ok, s_own, NEG_INF)
        m = jnp.maximum(m_past, jnp.max(s_own, axis=-1, keepdims=True))
        e = jnp.exp((s - m[None, None]) * scale)
        l = jnp.sum(jnp.sum(e, axis=(0, 1)), axis=-1, keepdims=True)
        s_scr[...] = e.reshape(PAGES_PER_SEQ, n_rows, PAGE_ROWS)
        e_own = jnp.exp((s_own - m) * scale)
        l = l + jnp.sum(e_own, axis=-1, keepdims=True)
        return l, jnp.dot(e_own.astype(BF16), vn_ref[...].astype(BF16), preferred_element_type=F32)

    l, acc = softmax_stage()

    for c in range(n_chunks):
        wait_chunk(vc_ref, c)
        for r in range(CHUNK_PAGES):
            p = s_scr[c * CHUNK_PAGES + r].astype(BF16)
            page = ring[(c % 2) * CHUNK_PAGES + r].astype(BF16)
            acc = acc + jnp.dot(p, page, preferred_element_type=F32)
        if c + 2 < n_chunks:
            start_chunk(vc_ref, b, c + 2)
        else:
            @pl.when(b + 1 < DEC_BATCH)
            def _():
                start_chunk(kc_ref, b + 1, c + 2 - n_chunks)

    out = acc / l
    for hd in range(MOBA_HEADS):
        o_ref[:, hd * MOBA_HEAD_DIM:(hd + 1) * MOBA_HEAD_DIM] = out[hd * DEC_SEQ:(hd + 1) * DEC_SEQ]


def _moba_sample(page_table, mq_s, kn2d, vn2d, cache_k2d, cache_v2d):
    n_rows = MOBA_HEADS * DEC_SEQ
    seq_rows = lambda w, n: pl.BlockSpec((n, w), lambda b, pt: (b, 0))
    hbm = pl.BlockSpec(memory_space=pl.ANY)
    grid_spec = pltpu.PrefetchScalarGridSpec(
        num_scalar_prefetch=1,
        grid=(DEC_BATCH,),
        in_specs=[seq_rows(MOBA_WIDTH, DEC_SEQ), seq_rows(LANES, n_rows), seq_rows(LANES, n_rows), hbm, hbm],
        out_specs=seq_rows(MOBA_WIDTH, DEC_SEQ),
        scratch_shapes=[pltpu.VMEM((RING_PAGES, PAGE_ROWS, LANES), F32),
                        pltpu.SemaphoreType.DMA((RING_PAGES,)),
                        pltpu.VMEM((PAGES_PER_SEQ, n_rows, PAGE_ROWS), F32)],
    )
    return pl.pallas_call(
        _moba_sample_body,
        grid_spec=grid_spec,
        out_shape=jax.ShapeDtypeStruct((N_SAMPLE_TOK, MOBA_WIDTH), F32),
        compiler_params=pltpu.CompilerParams(dimension_semantics=("arbitrary",),
                                             vmem_limit_bytes=48 * 1024 * 1024),
        name="moba_sample",
    )(page_table.reshape(-1), mq_s, kn2d, vn2d, cache_k2d, cache_v2d)


def _layer_norm(x, g, b):
    mu = jnp.mean(x, axis=-1, keepdims=True)
    d = x - mu
    var = jnp.mean(d * d, axis=-1, keepdims=True)
    return d * lax.rsqrt(var + LN_EPS) * g + b


def _out_ffn_body(ar_ref, am_ref, x_ref, ga_ref, shf_ref, scf_ref, gf_ref, wo_ref, g1_ref, b1_ref,
                  wu_ref, wd_ref, g2_ref, b2_ref, y_ref):
    mixed = (jnp.dot(ar_ref[...].astype(BF16), wo_ref[:RET_WIDTH, :], preferred_element_type=F32)
             + jnp.dot(am_ref[...].astype(BF16), wo_ref[RET_WIDTH:, :], preferred_element_type=F32))
    x1 = _layer_norm(ALPHA * x_ref[...] + ga_ref[...] * mixed, g1_ref[...], b1_ref[...])
    h = (x1 * (1.0 + scf_ref[...]) + shf_ref[...]).astype(BF16)
    acc = jnp.zeros(x1.shape, F32)
    for c in range(D_FF // D_MODEL):
        cols = slice(c * D_MODEL, (c + 1) * D_MODEL)
        u = jnp.maximum(jnp.dot(h, wu_ref[:, cols], preferred_element_type=F32), 0.0)
        acc = acc + jnp.dot((u * u).astype(BF16), wd_ref[cols, :], preferred_element_type=F32)
    y_ref[...] = _layer_norm(ALPHA * x1 + gf_ref[...] * acc, g2_ref[...], b2_ref[...])


def _out_ffn(a_ret, a_moba, x2d, mod_specs, mod_args, weights, grid, row_map, tm):
    w_o, ln1_g, ln1_b, w_up, w_down, ln2_g, ln2_b = weights
    t = x2d.shape[0]
    wide = lambda w: pl.BlockSpec((tm, w), row_map)
    const = lambda a: pl.BlockSpec(a.shape, lambda *_: (0,) * a.ndim, pipeline_mode=pl.Buffered(1))
    return pl.pallas_call(
        _out_ffn_body,
        grid=grid,
        in_specs=[wide(RET_WIDTH), wide(MOBA_WIDTH), wide(D_MODEL)] + mod_specs + [const(a) for a in weights],
        out_specs=wide(D_MODEL),
        out_shape=jax.ShapeDtypeStruct((t, D_MODEL), F32),
        compiler_params=pltpu.CompilerParams(dimension_semantics=("arbitrary",) * len(grid),
                                             vmem_limit_bytes=56 * 1024 * 1024),
        name="out_ffn",
    )(a_ret, a_moba, x2d, *mod_args, *weights)


def kernel(x_prompt, x_sample, cache_k, cache_v, state_ret, page_table, c_prompt, c_sample,
           w_ada, b_ada, w_in, w_o, ln1_g, ln1_b, w_up, w_down, ln2_g, ln2_b):
    n_prompt_tok = BATCH * SEQ
    past_len = page_table.shape[1] * PAGE_SIZE

    mod = _adaln(jnp.concatenate([c_prompt, c_sample], axis=0), w_ada[0], b_ada)
    mod_rows = mod.reshape((BATCH + DEC_BATCH) * 6, 1, D_MODEL)
    mod_s = jnp.repeat(mod[BATCH:], DEC_SEQ, axis=0)

    def prompt_mod(chunk):
        return pl.BlockSpec((None, 1, D_MODEL), lambda b, i: (b * 6 + chunk, 0, 0))

    def sample_mod(chunk):
        return pl.BlockSpec((N_SAMPLE_TOK, D_MODEL), lambda i: (0, chunk))

    w_in_b = w_in[0].astype(BF16)
    weights = (w_o[0].astype(BF16), ln1_g, ln1_b, w_up[0].astype(BF16), w_down[0].astype(BF16), ln2_g, ln2_b)

    tm = 512
    nt = SEQ // tm
    p_row = lambda b, i: (b * nt + i, 0)
    p_tab = pl.BlockSpec((tm, LANES), lambda b, i: (i, 0))
    s_row = lambda i: (0, 0)
    s_tab = pl.BlockSpec((N_SAMPLE_TOK, LANES), s_row)

    pos_p = np.arange(SEQ, dtype=np.int32)
    pos_s = np.tile(past_len + np.arange(DEC_SEQ, dtype=np.int32), DEC_BATCH)
    tabs_p = _rope_tables(pos_p, MOBA_HEAD_DIM) + _rope_tables(pos_p, RET_DK)
    tabs_s = _rope_tables(pos_s, MOBA_HEAD_DIM) + _rope_tables(pos_s, RET_DK)

    xp = x_prompt.reshape(n_prompt_tok, D_MODEL)
    rq, rk, rv, rg, mq, k_p, v_p = _inproj(
        xp, [prompt_mod(1), prompt_mod(0)], [mod_rows, mod_rows], [p_tab] * 4, tabs_p, w_in_b,
        (BATCH, nt), p_row, tm, BF16)
    a_ret, state_p = _ret_prompt(rq, rk, rv, rg)
    a_moba = _moba_prompt(mq, k_p, v_p)
    y_p = _out_ffn(a_ret, a_moba, xp, [prompt_mod(c) for c in (2, 3, 4, 5)], [mod_rows] * 4, weights,
                   (BATCH, nt), p_row, tm)

    xs = x_sample.reshape(N_SAMPLE_TOK, D_MODEL)
    rq_s, rk_s, rv_s, rg_s, mq_s, k_s, v_s = _inproj(
        xs, [sample_mod(1), sample_mod(0)], [mod_s, mod_s], [s_tab] * 4, tabs_s, w_in_b,
        (1,), s_row, N_SAMPLE_TOK, F32)
    a_ret_s, state_s = _ret_sample(rq_s, rk_s, rv_s, rg_s, state_ret[0])
    cache_rows = cache_k.shape[1] * PAGE_ROWS
    a_moba_s = _moba_sample(page_table, mq_s, k_s, v_s,
                            cache_k.reshape(cache_rows, LANES), cache_v.reshape(cache_rows, LANES))
    y_s = _out_ffn(a_ret_s, a_moba_s, xs, [sample_mod(c) for c in (2, 3, 4, 5)], [mod_s] * 4, weights,
                   (1,), s_row, N_SAMPLE_TOK)

    kv_p_shape = (DEPTH, BATCH, SEQ, MOBA_HEADS, MOBA_HEAD_DIM)
    kv_s_shape = (DEPTH, DEC_BATCH, DEC_SEQ, MOBA_HEADS, MOBA_HEAD_DIM)
    return (y_p.reshape(BATCH, SEQ, D_MODEL),
            y_s.reshape(DEC_BATCH, DEC_SEQ, D_MODEL),
            k_p.reshape(kv_p_shape), v_p.reshape(kv_p_shape), state_p[None],
            k_s.reshape(kv_s_shape), v_s.reshape(kv_s_shape), state_s[None])
```

```python
import functools

import numpy as np
import jax
import jax.numpy as jnp
from jax import lax
from jax.experimental import pallas as pl
from jax.experimental.pallas import tpu as pltpu

F32 = jnp.float32
BF16 = jnp.bfloat16

D_MODEL = 1024
BATCH = 8
SEQ = 2048
DEC_BATCH = 32
DEC_SEQ = 8
PAGE_SIZE = 128
RET_HEADS = 4
RET_DK = 64
RET_DV = 128
RET_CHUNK = 128
MOBA_HEADS = 4
MOBA_HEAD_DIM = 128
MOBA_BLOCK = 256
MOBA_TOPK = 3
D_FF = 4 * D_MODEL
ROPE_THETA = 10000.0
LN_EPS = 1e-5
GN_EPS = 1e-6
DEPTH = 1
ALPHA = (2 * DEPTH) ** 0.25
RET_QK = RET_HEADS * RET_DK
RET_WIDTH = RET_HEADS * RET_DV
MOBA_WIDTH = MOBA_HEADS * MOBA_HEAD_DIM
IN_WIDTH = 2 * RET_QK + 2 * RET_WIDTH + 3 * MOBA_WIDTH
OFF_RQ, OFF_RK, OFF_RV, OFF_RG = 0, 256, 512, 1024
OFF_MQ, OFF_MK, OFF_MV = 1536, 2048, 2560
LANES = 128
N_SAMPLE_TOK = DEC_BATCH * DEC_SEQ
PAGES_PER_SEQ = 64
CHUNK_PAGES = 16
RING_PAGES = 2 * CHUNK_PAGES
PAGE_ROWS = PAGE_SIZE * MOBA_HEADS
NEG_INF = float("-inf")
LOG2_E = 1.4426950408889634

_NT = (((1,), (1,)), ((), ()))
_TN = (((0,), (0,)), ((), ()))


def _log_decay():
    return np.log1p(-np.exp2(-5.0 - np.arange(RET_HEADS, dtype=np.float64)))


def _adaln_body(c_ref, w_ref, b_ref, o_ref):
    c = c_ref[...]
    a = (c * jax.nn.sigmoid(c)).astype(BF16)
    o_ref[...] = jnp.dot(a, w_ref[...].astype(BF16), preferred_element_type=F32) + b_ref[...]


def _adaln(c_all, w_ada, b_ada):
    n = c_all.shape[0]
    tn = 1024
    return pl.pallas_call(
        _adaln_body,
        grid=(6 * D_MODEL // tn,),
        in_specs=[pl.BlockSpec((n, D_MODEL), lambda j: (0, 0)),
                  pl.BlockSpec((D_MODEL, tn), lambda j: (0, j)),
                  pl.BlockSpec((1, tn), lambda j: (0, j))],
        out_specs=pl.BlockSpec((n, tn), lambda j: (0, j)),
        out_shape=jax.ShapeDtypeStruct((n, 6 * D_MODEL), F32),
        name="adaln",
    )(c_all, w_ada, b_ada)


def _rope_tables(pos, head_dim):
    half = head_dim // 2
    inv_freq = np.power(ROPE_THETA, -np.arange(half, dtype=np.float64) / half)
    ang = pos.astype(np.float64)[:, None] * inv_freq[None, :]
    cos, sin = np.cos(ang), np.sin(ang)
    reps = LANES // head_dim
    cos_t = np.tile(np.concatenate([cos, cos], axis=-1), (1, reps))
    sin_t = np.tile(np.concatenate([-sin, sin], axis=-1), (1, reps))
    return jnp.asarray(cos_t, dtype=F32), jnp.asarray(sin_t, dtype=F32)


def _inproj_body(x_ref, sc_ref, sh_ref, w_ref, cm_ref, sm_ref, cr_ref, sr_ref,
                 rq_ref, rk_ref, rv_ref, rg_ref, mq_ref, ko_ref, vo_ref):
    tm = x_ref.shape[0]
    h = (x_ref[...] * (1.0 + sc_ref[...]) + sh_ref[...]).astype(BF16)

    def proj(lo, width):
        return jnp.dot(h, w_ref[:, lo:lo + width], preferred_element_type=F32)

    lane = lax.broadcasted_iota(jnp.int32, (tm, LANES), 1)
    low_half = (lane & (RET_DK - 1)) < (RET_DK // 2)
    cr, sr = cr_ref[...], sr_ref[...]
    cm, sm = cm_ref[...], sm_ref[...]

    def rope_ret(z):
        rot = jnp.where(low_half, pltpu.roll(z, LANES - RET_DK // 2, 1), pltpu.roll(z, RET_DK // 2, 1))
        return z * cr + rot * sr

    def rope_moba(z):
        return z * cm + pltpu.roll(z, MOBA_HEAD_DIM // 2, 1) * sm

    zq = proj(OFF_RQ, RET_QK)
    zk = proj(OFF_RK, RET_QK)
    for s in range(RET_QK // LANES):
        sl = slice(s * LANES, (s + 1) * LANES)
        rq_ref[:, sl] = rope_ret(zq[:, sl])
        rk_ref[:, sl] = rope_ret(zk[:, sl]) * (RET_DK ** -0.5)
    rv_ref[...] = proj(OFF_RV, RET_WIDTH).astype(rv_ref.dtype)
    rg_ref[...] = proj(OFF_RG, RET_WIDTH)
    zq = proj(OFF_MQ, MOBA_WIDTH)
    zk = proj(OFF_MK, MOBA_WIDTH)
    zv = proj(OFF_MV, MOBA_WIDTH)
    for hd in range(MOBA_HEADS):
        sl = slice(hd * LANES, (hd + 1) * LANES)
        mq_ref[:, sl] = rope_moba(zq[:, sl]).astype(mq_ref.dtype)
        ko_ref[pl.ds(hd, tm, stride=MOBA_HEADS), :] = rope_moba(zk[:, sl])
        vo_ref[pl.ds(hd, tm, stride=MOBA_HEADS), :] = zv[:, sl]


def _inproj(x2d, mod_specs, mod_args, tab_specs, tabs, w_in, grid, row_map, tm, act_dtype):
    t = x2d.shape[0]
    wide = lambda w: pl.BlockSpec((tm, w), row_map)
    return pl.pallas_call(
        _inproj_body,
        grid=grid,
        in_specs=[wide(D_MODEL)] + mod_specs + [pl.BlockSpec((D_MODEL, IN_WIDTH), lambda *_: (0, 0))] + tab_specs,
        out_specs=[wide(RET_QK), wide(RET_QK), wide(RET_WIDTH), wide(RET_WIDTH), wide(MOBA_WIDTH),
                   pl.BlockSpec((tm * MOBA_HEADS, LANES), row_map),
                   pl.BlockSpec((tm * MOBA_HEADS, LANES), row_map)],
        out_shape=[jax.ShapeDtypeStruct((t, RET_QK), F32), jax.ShapeDtypeStruct((t, RET_QK), F32),
                   jax.ShapeDtypeStruct((t, RET_WIDTH), act_dtype), jax.ShapeDtypeStruct((t, RET_WIDTH), F32),
                   jax.ShapeDtypeStruct((t, MOBA_WIDTH), act_dtype),
                   jax.ShapeDtypeStruct((t * MOBA_HEADS, LANES), F32),
                   jax.ShapeDtypeStruct((t * MOBA_HEADS, LANES), F32)],
        compiler_params=pltpu.CompilerParams(dimension_semantics=("arbitrary",) * len(grid),
                                             vmem_limit_bytes=48 * 1024 * 1024),
        name="inproj",
    )(x2d, *mod_args, w_in, *tabs)


def _group_norm_gate(o, g):
    mu = jnp.mean(o, axis=-1, keepdims=True)
    d = o - mu
    var = jnp.mean(d * d, axis=-1, keepdims=True)
    return d * lax.rsqrt(var + GN_EPS) * (g * jax.nn.sigmoid(g))


def _ret_prompt_tables():
    lg = _log_decay()
    i = np.arange(RET_CHUNK, dtype=np.float64)
    diff = i[:, None] - i[None, :]
    dmat = np.where(diff >= 0, np.exp(np.maximum(diff, 0.0)[None] * lg[:, None, None]), 0.0)
    lane_head = np.arange(RET_QK) // RET_DK
    qd = np.exp((i[:, None] + 1.0) * lg[lane_head][None, :])
    kd = np.exp((RET_CHUNK - 1.0 - i)[:, None] * lg[lane_head][None, :])
    row_head = np.arange(RET_QK) // RET_DK
    col_head = np.arange(RET_WIDTH) // RET_DV
    same = row_head[:, None] == col_head[None, :]
    cdec = np.where(same, np.exp(RET_CHUNK * lg[row_head])[:, None], 0.0)
    return [jnp.asarray(a, dtype=F32) for a in (dmat, qd, kd, cdec, same.astype(np.float64))]


def _ret_prompt_body(q_ref, k_ref, v_ref, g_ref, dmat_ref, qd_ref, kd_ref, cdec_ref, bdm_ref,
                     o_ref, st_ref, state_scr):
    lane_head = lax.broadcasted_iota(jnp.int32, (RET_CHUNK, RET_QK), 1) >> 6
    chunk_rows = lambda c: slice(c * RET_CHUNK, (c + 1) * RET_CHUNK)

    def first_matmuls(c):
        rows = chunk_rows(c)
        q = q_ref[rows, :]
        k = k_ref[rows, :]
        v = v_ref[rows, :]
        kb = k.astype(BF16)
        state = state_scr[...]
        scores = [lax.dot_general(jnp.where(lane_head == hd, q, 0.0).astype(BF16), kb, _NT,
                                  preferred_element_type=F32) for hd in range(RET_HEADS)]
        cross = jnp.dot((q * qd_ref[...]).astype(BF16), state.astype(BF16), preferred_element_type=F32)
        kv = lax.dot_general((k * kd_ref[...]).astype(BF16), v, _TN, preferred_element_type=F32)
        state_scr[...] = cdec_ref[...] * state + bdm_ref[...] * kv
        return scores, cross, v

    def second_matmuls(c, scores, cross, v):
        rows = chunk_rows(c)
        g = g_ref[rows, :]
        decayed = [(scores[hd] * dmat_ref[hd]).astype(BF16) for hd in range(RET_HEADS)]
        for hd in range(RET_HEADS):
            sl = slice(hd * RET_DV, (hd + 1) * RET_DV)
            inner = jnp.dot(decayed[hd], v[:, sl], preferred_element_type=F32)
            o_ref[rows, sl] = _group_norm_gate(inner + cross[:, sl], g[:, sl]).astype(o_ref.dtype)

    state_scr[...] = jnp.zeros_like(state_scr)
    n_chunks = SEQ // RET_CHUNK
    ahead = first_matmuls(0)
    for c in range(n_chunks):
        current = ahead
        if c + 1 < n_chunks:
            ahead = first_matmuls(c + 1)
        second_matmuls(c, *current)
    for hd in range(RET_HEADS):
        st_ref[hd] = state_scr[hd * RET_DK:(hd + 1) * RET_DK, hd * RET_DV:(hd + 1) * RET_DV]


def _ret_prompt(rq, rk, rv, rg):
    tabs = _ret_prompt_tables()
    seq = lambda w: pl.BlockSpec((SEQ, w), lambda b: (b, 0))
    const = lambda a: pl.BlockSpec(a.shape, lambda b: (0,) * a.ndim)
    return pl.pallas_call(
        _ret_prompt_body,
        grid=(BATCH,),
        in_specs=[seq(RET_QK), seq(RET_QK), seq(RET_WIDTH), seq(RET_WIDTH)] + [const(a) for a in tabs],
        out_specs=[seq(RET_WIDTH), pl.BlockSpec((None, RET_HEADS, RET_DK, RET_DV), lambda b: (b, 0, 0, 0))],
        out_shape=[jax.ShapeDtypeStruct((BATCH * SEQ, RET_WIDTH), BF16),
                   jax.ShapeDtypeStruct((BATCH, RET_HEADS, RET_DK, RET_DV), F32)],
        scratch_shapes=[pltpu.VMEM((RET_QK, RET_WIDTH), F32)],
        compiler_params=pltpu.CompilerParams(dimension_semantics=("arbitrary",),
                                             vmem_limit_bytes=48 * 1024 * 1024),
        name="ret_prompt",
    )(rq, rk, rv, rg, *tabs)


def _ret_sample_tables():
    lg = _log_decay()
    t = np.arange(N_SAMPLE_TOK) % DEC_SEQ
    seq_id = np.arange(N_SAMPLE_TOK) // DEC_SEQ
    diff = (t[:, None] - t[None, :]).astype(np.float64)
    same_seq = seq_id[:, None] == seq_id[None, :]
    dmat = np.where(same_seq[None] & (diff >= 0)[None],
                    np.exp(np.maximum(diff, 0.0)[None] * lg[:, None, None]), 0.0)
    lane_head = np.arange(RET_QK) // RET_DK
    qd = np.exp((t[:, None] + 1.0) * lg[lane_head][None, :])
    kd = np.exp((DEC_SEQ - 1.0 - t)[:, None] * lg[lane_head][None, :])
    return [jnp.asarray(a, dtype=F32) for a in (dmat, qd, kd)]


def _ret_sample_body(q_ref, k_ref, v_ref, g_ref, st_ref, dmat_ref, qd_ref, kd_ref, o_ref, sto_ref):
    lg = _log_decay()
    q = q_ref[...]
    k = k_ref[...]
    kb = k.astype(BF16)
    qdec = q * qd_ref[...]
    kdec = k * kd_ref[...]
    vb = v_ref[...].astype(BF16)
    g = g_ref[...]
    lane = lax.broadcasted_iota(jnp.int32, (N_SAMPLE_TOK, LANES), 1)
    lane_head = lax.broadcasted_iota(jnp.int32, (N_SAMPLE_TOK, RET_QK), 1) >> 6
    n_state_rows = DEC_BATCH * RET_DK
    own_seq = ((lax.broadcasted_iota(jnp.int32, (N_SAMPLE_TOK, n_state_rows), 0) >> 3)
               == (lax.broadcasted_iota(jnp.int32, (N_SAMPLE_TOK, n_state_rows), 1) >> 6))
    for hd in range(RET_HEADS):
        sl = slice(hd * RET_DV, (hd + 1) * RET_DV)
        qm = jnp.where(lane_head == hd, q, 0.0).astype(BF16)
        s = lax.dot_general(qm, kb, _NT, preferred_element_type=F32) * dmat_ref[hd]
        inner = jnp.dot(s.astype(BF16), vb[:, sl], preferred_element_type=F32)

        def expand(z):
            slab = z[:, (hd // 2) * LANES:(hd // 2 + 1) * LANES]
            other = pltpu.roll(slab, RET_DK, 1)
            in_low = lane < RET_DK
            both = jnp.where(in_low, slab, other) if hd % 2 == 0 else jnp.where(in_low, other, slab)
            tiled = jnp.concatenate([both] * (n_state_rows // LANES), axis=1)
            return jnp.where(own_seq, tiled, 0.0).astype(BF16)

        st = st_ref[:, hd].reshape(n_state_rows, RET_DV)
        cross = jnp.dot(expand(qdec), st.astype(BF16), preferred_element_type=F32)
        o_ref[:, sl] = _group_norm_gate(inner + cross, g[:, sl])
        kv = lax.dot_general(expand(kdec), vb[:, sl], _TN, preferred_element_type=F32)
        new = float(np.exp(DEC_SEQ * lg[hd])) * st + kv
        sto_ref[:, hd] = new.reshape(DEC_BATCH, RET_DK, RET_DV)


def _ret_sample(rq, rk, rv, rg, state):
    tabs = _ret_sample_tables()
    full = lambda a: pl.BlockSpec(a.shape, lambda i: (0,) * a.ndim)
    args = (rq, rk, rv, rg, state, *tabs)
    return pl.pallas_call(
        _ret_sample_body,
        grid=(1,),
        in_specs=[full(a) for a in args],
        out_specs=[pl.BlockSpec((N_SAMPLE_TOK, RET_WIDTH), lambda i: (0, 0)),
                   pl.BlockSpec(state.shape, lambda i: (0, 0, 0, 0))],
        out_shape=[jax.ShapeDtypeStruct((N_SAMPLE_TOK, RET_WIDTH), F32),
                   jax.ShapeDtypeStruct(state.shape, F32)],
        compiler_params=pltpu.CompilerParams(dimension_semantics=("arbitrary",),
                                             vmem_limit_bytes=56 * 1024 * 1024),
        name="ret_sample",
    )(*args)


def _moba_prompt_body(q_ref, k_ref, v_ref, o_ref):
    hd = pl.program_id(1)
    n_blocks = SEQ // MOBA_BLOCK
    exp2_scale = MOBA_HEAD_DIM ** -0.5 * LOG2_E
    k32 = k_ref[pl.ds(hd, SEQ, stride=MOBA_HEADS), :]
    kb = k32.astype(BF16)
    vt = v_ref[pl.ds(hd, SEQ, stride=MOBA_HEADS), :].T.astype(BF16)
    kmean = jnp.sum(k32.reshape(n_blocks, MOBA_BLOCK, MOBA_HEAD_DIM), axis=1) * (1.0 / MOBA_BLOCK)
    kmb = kmean.astype(BF16)
    key_id = lax.broadcasted_iota(jnp.int32, (MOBA_BLOCK, MOBA_BLOCK), 0)
    qry_id = lax.broadcasted_iota(jnp.int32, (MOBA_BLOCK, MOBA_BLOCK), 1)
    causal = key_id <= qry_id

    blk = lambda n: slice(n * MOBA_BLOCK, (n + 1) * MOBA_BLOCK)

    def score_matmuls(i):
        qi = q_ref[blk(i), :]
        st = [lax.dot_general(kb[blk(n)], qi, _NT, preferred_element_type=F32) for n in range(i + 1)]
        gt = lax.dot_general(kmb, qi, _NT, preferred_element_type=F32) if i > MOBA_TOPK else None
        return st, gt

    ahead = score_matmuls(0)
    for i in range(n_blocks):
        st, gt = ahead
        if i + 1 < n_blocks:
            ahead = score_matmuls(i + 1)
        st[i] = jnp.where(causal, st[i], NEG_INF)
        if i > MOBA_TOPK:
            for n in range(i):
                beats = jnp.zeros((1, MOBA_BLOCK), F32)
                for mm in range(i):
                    if mm == n:
                        continue
                    win = (gt[mm:mm + 1] >= gt[n:n + 1]) if mm < n else (gt[mm:mm + 1] > gt[n:n + 1])
                    beats = beats + win.astype(F32)
                st[n] = st[n] + jnp.where(beats < MOBA_TOPK, 0.0, NEG_INF)
        m = functools.reduce(jnp.maximum, [jnp.max(s, axis=0, keepdims=True) for s in st])
        l = jnp.zeros((1, MOBA_BLOCK), F32)
        acc = jnp.zeros((MOBA_HEAD_DIM, MOBA_BLOCK), F32)
        for n in range(i + 1):
            e = jnp.exp2((st[n] - m) * exp2_scale)
            l = l + jnp.sum(e, axis=0, keepdims=True)
            acc = acc + jnp.dot(vt[:, blk(n)], e.astype(BF16), preferred_element_type=F32)
        o_ref[blk(i), :] = (acc / l).T.astype(o_ref.dtype)


def _moba_prompt(mq, k2d, v2d):
    kv_spec = pl.BlockSpec((SEQ * MOBA_HEADS, LANES), lambda b, h: (b, 0))
    return pl.pallas_call(
        _moba_prompt_body,
        grid=(BATCH, MOBA_HEADS),
        in_specs=[pl.BlockSpec((SEQ, MOBA_HEAD_DIM), lambda b, h: (b, h)), kv_spec, kv_spec],
        out_specs=pl.BlockSpec((SEQ, MOBA_HEAD_DIM), lambda b, h: (b, h)),
        out_shape=jax.ShapeDtypeStruct((BATCH * SEQ, MOBA_WIDTH), BF16),
        compiler_params=pltpu.CompilerParams(dimension_semantics=("arbitrary", "arbitrary"),
                                             vmem_limit_bytes=56 * 1024 * 1024),
        name="moba_prompt",
    )(mq, k2d, v2d)


class _SampleMoba:
    n_chunks = PAGES_PER_SEQ // CHUNK_PAGES
    n_rows = MOBA_HEADS * DEC_SEQ
    n_blocks = PAGES_PER_SEQ * PAGE_SIZE // MOBA_BLOCK
    pages_per_block = MOBA_BLOCK // PAGE_SIZE

    def __init__(self, pt_ref, seq, n_seqs, q_ref, kn_ref, vn_ref, kc_ref, vc_ref, o_ref, ring, sem, s_scr):
        self.pt_ref, self.seq, self.n_seqs = pt_ref, seq, n_seqs
        self.q_ref, self.kn_ref, self.vn_ref = q_ref, kn_ref, vn_ref
        self.kc_ref, self.vc_ref, self.o_ref = kc_ref, vc_ref, o_ref
        self.ring, self.sem, self.s_scr = ring, sem, s_scr

    def _page_copy(self, cache_ref, row0, slot):
        return pltpu.make_async_copy(cache_ref.at[pl.ds(row0, PAGE_ROWS)], self.ring.at[slot], self.sem.at[slot])

    def _start_chunk(self, cache_ref, seq, chunk):
        for r in range(CHUNK_PAGES):
            page = self.pt_ref[seq * PAGES_PER_SEQ + chunk * CHUNK_PAGES + r]
            self._page_copy(cache_ref, pl.multiple_of(page * PAGE_ROWS, PAGE_ROWS),
                            (chunk % 2) * CHUNK_PAGES + r).start()

    def _wait_chunk(self, cache_ref, chunk):
        for r in range(CHUNK_PAGES):
            self._page_copy(cache_ref, 0, (chunk % 2) * CHUNK_PAGES + r).wait()

    def prologue(self):
        @pl.when(self.seq == 0)
        def _():
            self._start_chunk(self.kc_ref, self.seq, 0)
            self._start_chunk(self.kc_ref, self.seq, 1)

        self.q = jnp.concatenate(
            [self.q_ref[:, hd * MOBA_HEAD_DIM:(hd + 1) * MOBA_HEAD_DIM] for hd in range(MOBA_HEADS)], axis=0
        ).astype(BF16)

    def k_chunk(self, c):
        self._wait_chunk(self.kc_ref, c)
        for r in range(CHUNK_PAGES):
            page = self.ring[(c % 2) * CHUNK_PAGES + r].astype(BF16)
            self.s_scr[c * CHUNK_PAGES + r] = lax.dot_general(self.q, page, _NT, preferred_element_type=F32)
        if c + 2 < self.n_chunks:
            self._start_chunk(self.kc_ref, self.seq, c + 2)
        else:
            self._start_chunk(self.vc_ref, self.seq, c + 2 - self.n_chunks)

    def softmax(self):
        s_scr, n_rows = self.s_scr, self.n_rows
        exp2_scale = MOBA_HEAD_DIM ** -0.5 * LOG2_E
        row_head = lax.broadcasted_iota(jnp.int32, (n_rows, PAGE_ROWS), 0) >> 3
        col_head = lax.broadcasted_iota(jnp.int32, (n_rows, PAGE_ROWS), 1) & (MOBA_HEADS - 1)
        same_head = row_head == col_head
        head_bias = jnp.where(same_head, 0.0, NEG_INF)
        block_pages = lambda n: range(n * self.pages_per_block, (n + 1) * self.pages_per_block)
        gs = []
        for n in range(self.n_blocks):
            sz = functools.reduce(jnp.add, [jnp.where(same_head, s_scr[pg], 0.0) for pg in block_pages(n)])
            gs.append(jnp.sum(sz, axis=-1, keepdims=True))
        lane = lax.broadcasted_iota(jnp.int32, (n_rows, LANES), 1)
        g_all = jnp.full((n_rows, LANES), NEG_INF, F32)
        for n in range(self.n_blocks):
            g_all = jnp.where(lane == n, gs[n], g_all)
        keep_bias = []
        for n in range(self.n_blocks):
            wins = (g_all > gs[n]) | ((g_all == gs[n]) & (lane < n))
            beats = jnp.sum(wins.astype(F32), axis=-1, keepdims=True)
            keep_bias.append(jnp.where(beats < MOBA_TOPK, 0.0, NEG_INF))
        s_own = lax.dot_general(self.q, self.kn_ref[...].astype(BF16), _NT, preferred_element_type=F32)
        r_id = lax.broadcasted_iota(jnp.int32, (n_rows, n_rows), 0)
        c_id = lax.broadcasted_iota(jnp.int32, (n_rows, n_rows), 1)
        own_ok = ((c_id & (MOBA_HEADS - 1)) == (r_id >> 3)) & ((c_id >> 2) <= (r_id & (DEC_SEQ - 1)))
        s_own = jnp.where(own_ok, s_own, NEG_INF)
        mx = jnp.full((n_rows, PAGE_ROWS), NEG_INF, F32)
        for n in range(self.n_blocks):
            bias = head_bias + keep_bias[n]
            for pg in block_pages(n):
                mx = jnp.maximum(mx, s_scr[pg] + bias)
        m = jnp.maximum(jnp.max(mx, axis=-1, keepdims=True), jnp.max(s_own, axis=-1, keepdims=True))
        lsum = jnp.zeros((n_rows, PAGE_ROWS), F32)
        for n in range(self.n_blocks):
            bias = head_bias + keep_bias[n]
            for pg in block_pages(n):
                e = jnp.exp2((s_scr[pg] + bias - m) * exp2_scale)
                s_scr[pg] = e
                lsum = lsum + e
        e_own = jnp.exp2((s_own - m) * exp2_scale)
        l = jnp.sum(lsum, axis=-1, keepdims=True) + jnp.sum(e_own, axis=-1, keepdims=True)
        return l, jnp.dot(e_own.astype(BF16), self.vn_ref[...].astype(BF16), preferred_element_type=F32)

    def v_chunk(self, c, acc):
        self._wait_chunk(self.vc_ref, c)
        for r in range(CHUNK_PAGES):
            p = self.s_scr[c * CHUNK_PAGES + r].astype(BF16)
            page = self.ring[(c % 2) * CHUNK_PAGES + r].astype(BF16)
            acc = acc + jnp.dot(p, page, preferred_element_type=F32)
        if c + 2 < self.n_chunks:
            self._start_chunk(self.vc_ref, self.seq, c + 2)
        else:
            @pl.when(self.seq + 1 < self.n_seqs)
            def _():
                self._start_chunk(self.kc_ref, self.seq + 1, c + 2 - self.n_chunks)
        return acc

    def finish(self, l, acc):
        out = acc / l
        for hd in range(MOBA_HEADS):
            self.o_ref[:, hd * MOBA_HEAD_DIM:(hd + 1) * MOBA_HEAD_DIM] = out[hd * DEC_SEQ:(hd + 1) * DEC_SEQ]


def _layer_norm(x, g, b):
    mu = jnp.mean(x, axis=-1, keepdims=True)
    d = x - mu
    var = jnp.mean(d * d, axis=-1, keepdims=True)
    return d * lax.rsqrt(var + LN_EPS) * g + b


def _out_ffn_body(ar_ref, am_ref, x_ref, ga_ref, shf_ref, scf_ref, gf_ref, wo_ref, g1_ref, b1_ref,
                  wu_ref, wd_ref, g2_ref, b2_ref, y_ref):
    mixed = (jnp.dot(ar_ref[...].astype(BF16), wo_ref[:RET_WIDTH, :], preferred_element_type=F32)
             + jnp.dot(am_ref[...].astype(BF16), wo_ref[RET_WIDTH:, :], preferred_element_type=F32))
    x1 = _layer_norm(ALPHA * x_ref[...] + ga_ref[...] * mixed, g1_ref[...], b1_ref[...])
    h = (x1 * (1.0 + scf_ref[...]) + shf_ref[...]).astype(BF16)
    acc = jnp.zeros(x1.shape, F32)
    for c in range(D_FF // D_MODEL):
        cols = slice(c * D_MODEL, (c + 1) * D_MODEL)
        u = jnp.maximum(jnp.dot(h, wu_ref[:, cols], preferred_element_type=F32), 0.0)
        acc = acc + jnp.dot((u * u).astype(BF16), wd_ref[cols, :], preferred_element_type=F32)
    y_ref[...] = _layer_norm(ALPHA * x1 + gf_ref[...] * acc, g2_ref[...], b2_ref[...])


def _out_ffn(a_ret, a_moba, x2d, mod_specs, mod_args, weights, grid, row_map, tm):
    t = x2d.shape[0]
    wide = lambda w: pl.BlockSpec((tm, w), row_map)
    const = lambda a: pl.BlockSpec(a.shape, lambda *_: (0,) * a.ndim, pipeline_mode=pl.Buffered(1))
    return pl.pallas_call(
        _out_ffn_body,
        grid=grid,
        in_specs=[wide(RET_WIDTH), wide(MOBA_WIDTH), wide(D_MODEL)] + mod_specs + [const(a) for a in weights],
        out_specs=wide(D_MODEL),
        out_shape=jax.ShapeDtypeStruct((t, D_MODEL), F32),
        compiler_params=pltpu.CompilerParams(dimension_semantics=("arbitrary",) * len(grid),
                                             vmem_limit_bytes=56 * 1024 * 1024),
        name="out_ffn",
    )(a_ret, a_moba, x2d, *mod_args, *weights)


def _out_ffn_moba_body(pt_ref, ar_ref, am_ref, x_ref, ga_ref, shf_ref, scf_ref, gf_ref, wo_ref, g1_ref, b1_ref,
                       wu_ref, wd_ref, g2_ref, b2_ref, q_ref, kn_ref, vn_ref, kc_ref, vc_ref,
                       y_ref, o_ref, ring, sem, s_scr):
    n_seqs = pl.num_programs(0) * pl.num_programs(1)
    seq = pl.program_id(0) * pl.num_programs(1) + pl.program_id(1)
    sm = _SampleMoba(pt_ref, seq, n_seqs, q_ref, kn_ref, vn_ref, kc_ref, vc_ref, o_ref, ring, sem, s_scr)

    def up(c, h):
        u = jnp.maximum(jnp.dot(h, wu_ref[:, c * D_MODEL:(c + 1) * D_MODEL], preferred_element_type=F32), 0.0)
        return (u * u).astype(BF16)

    def down(c, u):
        return jnp.dot(u, wd_ref[c * D_MODEL:(c + 1) * D_MODEL, :], preferred_element_type=F32)

    sm.prologue()
    sm.k_chunk(0)
    mixed = (jnp.dot(ar_ref[...].astype(BF16), wo_ref[:RET_WIDTH, :], preferred_element_type=F32)
             + jnp.dot(am_ref[...].astype(BF16), wo_ref[RET_WIDTH:, :], preferred_element_type=F32))
    x1 = _layer_norm(ALPHA * x_ref[...] + ga_ref[...] * mixed, g1_ref[...], b1_ref[...])
    h = (x1 * (1.0 + scf_ref[...]) + shf_ref[...]).astype(BF16)
    sm.k_chunk(1)
    u = up(0, h)
    sm.k_chunk(2)
    acc = down(0, u)
    u = up(1, h)
    sm.k_chunk(3)
    acc = acc + down(1, u)
    u = up(2, h)
    l, acc_s = sm.softmax()
    acc = acc + down(2, u)
    acc_s = sm.v_chunk(0, acc_s)
    u = up(3, h)
    acc_s = sm.v_chunk(1, acc_s)
    acc = acc + down(3, u)
    acc_s = sm.v_chunk(2, acc_s)
    y_ref[...] = _layer_norm(ALPHA * x1 + gf_ref[...] * acc, g2_ref[...], b2_ref[...])
    acc_s = sm.v_chunk(3, acc_s)
    sm.finish(l, acc_s)


def _out_ffn_moba(a_ret, a_moba, x2d, mod_rows, weights, page_table, mq_s, kn2d, vn2d, cache_k2d, cache_v2d, tm):
    t = x2d.shape[0]
    nt = SEQ // tm
    assert BATCH * nt == DEC_BATCH
    n_rows = MOBA_HEADS * DEC_SEQ
    tile = lambda b, i, pt: (b * nt + i, 0)
    wide = lambda w: pl.BlockSpec((tm, w), tile)
    mod = lambda chunk: pl.BlockSpec((None, 1, D_MODEL), lambda b, i, pt: (b * 6 + chunk, 0, 0))
    const = lambda a: pl.BlockSpec(a.shape, lambda b, i, pt: (0,) * a.ndim, pipeline_mode=pl.Buffered(1))
    seq_rows = lambda w, n: pl.BlockSpec((n, w), tile)
    hbm = pl.BlockSpec(memory_space=pl.ANY)
    grid_spec = pltpu.PrefetchScalarGridSpec(
        num_scalar_prefetch=1,
        grid=(BATCH, nt),
        in_specs=[wide(RET_WIDTH), wide(MOBA_WIDTH), wide(D_MODEL)] + [mod(c) for c in (2, 3, 4, 5)]
        + [const(a) for a in weights]
        + [seq_rows(MOBA_WIDTH, DEC_SEQ), seq_rows(LANES, n_rows), seq_rows(LANES, n_rows), hbm, hbm],
        out_specs=[wide(D_MODEL), seq_rows(MOBA_WIDTH, DEC_SEQ)],
        scratch_shapes=[pltpu.VMEM((RING_PAGES, PAGE_ROWS, LANES), F32),
                        pltpu.SemaphoreType.DMA((RING_PAGES,)),
                        pltpu.VMEM((PAGES_PER_SEQ, n_rows, PAGE_ROWS), F32)],
    )
    return pl.pallas_call(
        _out_ffn_moba_body,
        grid_spec=grid_spec,
        out_shape=[jax.ShapeDtypeStruct((t, D_MODEL), F32),
                   jax.ShapeDtypeStruct((N_SAMPLE_TOK, MOBA_WIDTH), F32)],
        compiler_params=pltpu.CompilerParams(dimension_semantics=("arbitrary", "arbitrary"),
                                             vmem_limit_bytes=58 * 1024 * 1024),
        name="out_ffn_moba",
    )(page_table.reshape(-1), a_ret, a_moba, x2d, *([mod_rows] * 4), *weights,
      mq_s, kn2d, vn2d, cache_k2d, cache_v2d)


def kernel(x_prompt, x_sample, cache_k, cache_v, state_ret, page_table, c_prompt, c_sample,
           w_ada, b_ada, w_in, w_o, ln1_g, ln1_b, w_up, w_down, ln2_g, ln2_b):
    n_prompt_tok = BATCH * SEQ
    past_len = page_table.shape[1] * PAGE_SIZE

    mod = _adaln(jnp.concatenate([c_prompt, c_sample], axis=0), w_ada[0], b_ada)
    mod_rows = mod.reshape((BATCH + DEC_BATCH) * 6, 1, D_MODEL)
    mod_s = jnp.repeat(mod[BATCH:], DEC_SEQ, axis=0)

    def prompt_mod(chunk):
        return pl.BlockSpec((None, 1, D_MODEL), lambda b, i: (b * 6 + chunk, 0, 0))

    def sample_mod(chunk):
        return pl.BlockSpec((N_SAMPLE_TOK, D_MODEL), lambda i: (0, chunk))

    w_in_b = w_in[0].astype(BF16)
    weights = (w_o[0].astype(BF16), ln1_g, ln1_b, w_up[0].astype(BF16), w_down[0].astype(BF16), ln2_g, ln2_b)

    tm = 512
    nt = SEQ // tm
    p_row = lambda b, i: (b * nt + i, 0)
    p_tab = pl.BlockSpec((tm, LANES), lambda b, i: (i, 0))
    s_row = lambda i: (0, 0)
    s_tab = pl.BlockSpec((N_SAMPLE_TOK, LANES), s_row)

    pos_p = np.arange(SEQ, dtype=np.int32)
    pos_s = np.tile(past_len + np.arange(DEC_SEQ, dtype=np.int32), DEC_BATCH)
    tabs_p = _rope_tables(pos_p, MOBA_HEAD_DIM) + _rope_tables(pos_p, RET_DK)
    tabs_s = _rope_tables(pos_s, MOBA_HEAD_DIM) + _rope_tables(pos_s, RET_DK)

    xp = x_prompt.reshape(n_prompt_tok, D_MODEL)
    rq, rk, rv, rg, mq, k_p, v_p = _inproj(
        xp, [prompt_mod(1), prompt_mod(0)], [mod_rows, mod_rows], [p_tab] * 4, tabs_p, w_in_b,
        (BATCH, nt), p_row, tm, BF16)
    a_ret, state_p = _ret_prompt(rq, rk, rv, rg)
    a_moba = _moba_prompt(mq, k_p, v_p)

    xs = x_sample.reshape(N_SAMPLE_TOK, D_MODEL)
    rq_s, rk_s, rv_s, rg_s, mq_s, k_s, v_s = _inproj(
        xs, [sample_mod(1), sample_mod(0)], [mod_s, mod_s], [s_tab] * 4, tabs_s, w_in_b,
        (1,), s_row, N_SAMPLE_TOK, F32)
    a_ret_s, state_s = _ret_sample(rq_s, rk_s, rv_s, rg_s, state_ret[0])

    cache_rows = cache_k.shape[1] * PAGE_ROWS
    y_p, a_moba_s = _out_ffn_moba(a_ret, a_moba, xp, mod_rows, weights, page_table, mq_s, k_s, v_s,
                                  cache_k.reshape(cache_rows, LANES), cache_v.reshape(cache_rows, LANES), tm)
    y_s = _out_ffn(a_ret_s, a_moba_s, xs, [sample_mod(c) for c in (2, 3, 4, 5)], [mod_s] * 4, weights,
                   (1,), s_row, N_SAMPLE_TOK)

    kv_p_shape = (DEPTH, BATCH, SEQ, MOBA_HEADS, MOBA_HEAD_DIM)
    kv_s_shape = (DEPTH, DEC_BATCH, DEC_SEQ, MOBA_HEADS, MOBA_HEAD_DIM)
    return (y_p.reshape(BATCH, SEQ, D_MODEL),
            y_s.reshape(DEC_BATCH, DEC_SEQ, D_MODEL),
            k_p.reshape(kv_p_shape), v_p.reshape(kv_p_shape), state_p[None],
            k_s.reshape(kv_s_shape), v_s.reshape(kv_s_shape), state_s[None])
```

```python
import functools

import numpy as np
import jax
import jax.numpy as jnp
from jax import lax
from jax.experimental import pallas as pl
from jax.experimental.pallas import tpu as pltpu

F32 = jnp.float32
BF16 = jnp.bfloat16

D_MODEL = 1024
BATCH = 8
SEQ = 2048
DEC_BATCH = 32
DEC_SEQ = 8
PAGE_SIZE = 128
RET_HEADS = 4
RET_DK = 64
RET_DV = 128
RET_CHUNK = 128
MOBA_HEADS = 4
MOBA_HEAD_DIM = 128
MOBA_BLOCK = 256
MOBA_TOPK = 3
D_FF = 4 * D_MODEL
ROPE_THETA = 10000.0
LN_EPS = 1e-5
GN_EPS = 1e-6
DEPTH = 1
ALPHA = (2 * DEPTH) ** 0.25
RET_QK = RET_HEADS * RET_DK
RET_WIDTH = RET_HEADS * RET_DV
MOBA_WIDTH = MOBA_HEADS * MOBA_HEAD_DIM
IN_WIDTH = 2 * RET_QK + 2 * RET_WIDTH + 3 * MOBA_WIDTH
OFF_RQ, OFF_RK, OFF_RV, OFF_RG = 0, 256, 512, 1024
OFF_MQ, OFF_MK, OFF_MV = 1536, 2048, 2560
LANES = 128
N_SAMPLE_TOK = DEC_BATCH * DEC_SEQ
PAGES_PER_SEQ = 64
CHUNK_PAGES = 16
RING_PAGES = 2 * CHUNK_PAGES
PAGE_ROWS = PAGE_SIZE * MOBA_HEADS
NEG_INF = float("-inf")
LOG2_E = 1.4426950408889634

_NT = (((1,), (1,)), ((), ()))
_TN = (((0,), (0,)), ((), ()))


def _log_decay():
    return np.log1p(-np.exp2(-5.0 - np.arange(RET_HEADS, dtype=np.float64)))


def _adaln_body(c_ref, w_ref, b_ref, o_ref):
    c = c_ref[...]
    a = (c * jax.nn.sigmoid(c)).astype(BF16)
    o_ref[...] = jnp.dot(a, w_ref[...].astype(BF16), preferred_element_type=F32) + b_ref[...]


def _adaln(c_all, w_ada, b_ada):
    n = c_all.shape[0]
    tn = 1024
    return pl.pallas_call(
        _adaln_body,
        grid=(6 * D_MODEL // tn,),
        in_specs=[pl.BlockSpec((n, D_MODEL), lambda j: (0, 0)),
                  pl.BlockSpec((D_MODEL, tn), lambda j: (0, j)),
                  pl.BlockSpec((1, tn), lambda j: (0, j))],
        out_specs=pl.BlockSpec((n, tn), lambda j: (0, j)),
        out_shape=jax.ShapeDtypeStruct((n, 6 * D_MODEL), F32),
        name="adaln",
    )(c_all, w_ada, b_ada)


def _rope_tables(pos, head_dim):
    half = head_dim // 2
    inv_freq = np.power(ROPE_THETA, -np.arange(half, dtype=np.float64) / half)
    ang = pos.astype(np.float64)[:, None] * inv_freq[None, :]
    cos, sin = np.cos(ang), np.sin(ang)
    reps = LANES // head_dim
    cos_t = np.tile(np.concatenate([cos, cos], axis=-1), (1, reps))
    sin_t = np.tile(np.concatenate([-sin, sin], axis=-1), (1, reps))
    return jnp.asarray(cos_t, dtype=F32), jnp.asarray(sin_t, dtype=F32)


def _inproj_body(x_ref, sc_ref, sh_ref, w_ref, cm_ref, sm_ref, cr_ref, sr_ref,
                 rq_ref, rk_ref, rv_ref, rg_ref, mq_ref, ko_ref, vo_ref):
    tm = x_ref.shape[0]
    h = (x_ref[...] * (1.0 + sc_ref[...]) + sh_ref[...]).astype(BF16)

    def proj(lo, width):
        return jnp.dot(h, w_ref[:, lo:lo + width], preferred_element_type=F32)

    lane = lax.broadcasted_iota(jnp.int32, (tm, LANES), 1)
    low_half = (lane & (RET_DK - 1)) < (RET_DK // 2)
    cr, sr = cr_ref[...], sr_ref[...]
    cm, sm = cm_ref[...], sm_ref[...]

    def rope_ret(z):
        rot = jnp.where(low_half, pltpu.roll(z, LANES - RET_DK // 2, 1), pltpu.roll(z, RET_DK // 2, 1))
        return z * cr + rot * sr

    def rope_moba(z):
        return z * cm + pltpu.roll(z, MOBA_HEAD_DIM // 2, 1) * sm

    zq = proj(OFF_RQ, RET_QK)
    zk = proj(OFF_RK, RET_QK)
    for s in range(RET_QK // LANES):
        sl = slice(s * LANES, (s + 1) * LANES)
        rq_ref[:, sl] = rope_ret(zq[:, sl])
        rk_ref[:, sl] = rope_ret(zk[:, sl]) * (RET_DK ** -0.5)
    rv_ref[...] = proj(OFF_RV, RET_WIDTH).astype(rv_ref.dtype)
    rg_ref[...] = proj(OFF_RG, RET_WIDTH)
    zq = proj(OFF_MQ, MOBA_WIDTH)
    zk = proj(OFF_MK, MOBA_WIDTH)
    zv = proj(OFF_MV, MOBA_WIDTH)
    for hd in range(MOBA_HEADS):
        sl = slice(hd * LANES, (hd + 1) * LANES)
        mq_ref[:, sl] = rope_moba(zq[:, sl]).astype(mq_ref.dtype)
        ko_ref[pl.ds(hd, tm, stride=MOBA_HEADS), :] = rope_moba(zk[:, sl])
        vo_ref[pl.ds(hd, tm, stride=MOBA_HEADS), :] = zv[:, sl]


def _inproj(x2d, mod_specs, mod_args, tab_specs, tabs, w_in, grid, row_map, tm, act_dtype):
    t = x2d.shape[0]
    wide = lambda w: pl.BlockSpec((tm, w), row_map)
    return pl.pallas_call(
        _inproj_body,
        grid=grid,
        in_specs=[wide(D_MODEL)] + mod_specs + [pl.BlockSpec((D_MODEL, IN_WIDTH), lambda *_: (0, 0))] + tab_specs,
        out_specs=[wide(RET_QK), wide(RET_QK), wide(RET_WIDTH), wide(RET_WIDTH), wide(MOBA_WIDTH),
                   pl.BlockSpec((tm * MOBA_HEADS, LANES), row_map),
                   pl.BlockSpec((tm * MOBA_HEADS, LANES), row_map)],
        out_shape=[jax.ShapeDtypeStruct((t, RET_QK), F32), jax.ShapeDtypeStruct((t, RET_QK), F32),
                   jax.ShapeDtypeStruct((t, RET_WIDTH), act_dtype), jax.ShapeDtypeStruct((t, RET_WIDTH), F32),
                   jax.ShapeDtypeStruct((t, MOBA_WIDTH), act_dtype),
                   jax.ShapeDtypeStruct((t * MOBA_HEADS, LANES), F32),
                   jax.ShapeDtypeStruct((t * MOBA_HEADS, LANES), F32)],
        compiler_params=pltpu.CompilerParams(dimension_semantics=("arbitrary",) * len(grid),
                                             vmem_limit_bytes=48 * 1024 * 1024),
        name="inproj",
    )(x2d, *mod_args, w_in, *tabs)


def _group_norm_gate(o, g):
    mu = jnp.mean(o, axis=-1, keepdims=True)
    d = o - mu
    var = jnp.mean(d * d, axis=-1, keepdims=True)
    return d * lax.rsqrt(var + GN_EPS) * (g * jax.nn.sigmoid(g))


def _ret_prompt_tables():
    lg = _log_decay()
    i = np.arange(RET_CHUNK, dtype=np.float64)
    diff = i[:, None] - i[None, :]
    dmat = np.where(diff >= 0, np.exp(np.maximum(diff, 0.0)[None] * lg[:, None, None]), 0.0)
    lane_head = np.arange(RET_QK) // RET_DK
    qd = np.exp((i[:, None] + 1.0) * lg[lane_head][None, :])
    kd = np.exp((RET_CHUNK - 1.0 - i)[:, None] * lg[lane_head][None, :])
    row_head = np.arange(RET_QK) // RET_DK
    col_head = np.arange(RET_WIDTH) // RET_DV
    same = row_head[:, None] == col_head[None, :]
    cdec = np.where(same, np.exp(RET_CHUNK * lg[row_head])[:, None], 0.0)
    return [jnp.asarray(a, dtype=F32) for a in (dmat, qd, kd, cdec, same.astype(np.float64))]


def _ret_prompt_body(q_ref, k_ref, v_ref, g_ref, dmat_ref, qd_ref, kd_ref, cdec_ref, bdm_ref,
                     o_ref, st_ref, state_scr):
    lane_head = lax.broadcasted_iota(jnp.int32, (RET_CHUNK, RET_QK), 1) >> 6
    chunk_rows = lambda c: slice(c * RET_CHUNK, (c + 1) * RET_CHUNK)

    def first_matmuls(c):
        rows = chunk_rows(c)
        q = q_ref[rows, :]
        k = k_ref[rows, :]
        v = v_ref[rows, :]
        kb = k.astype(BF16)
        state = state_scr[...]
        scores = [lax.dot_general(jnp.where(lane_head == hd, q, 0.0).astype(BF16), kb, _NT,
                                  preferred_element_type=F32) for hd in range(RET_HEADS)]
        cross = jnp.dot((q * qd_ref[...]).astype(BF16), state.astype(BF16), preferred_element_type=F32)
        kv = lax.dot_general((k * kd_ref[...]).astype(BF16), v, _TN, preferred_element_type=F32)
        state_scr[...] = cdec_ref[...] * state + bdm_ref[...] * kv
        return scores, cross, v

    def second_matmuls(c, scores, cross, v):
        rows = chunk_rows(c)
        g = g_ref[rows, :]
        decayed = [(scores[hd] * dmat_ref[hd]).astype(BF16) for hd in range(RET_HEADS)]
        for hd in range(RET_HEADS):
            sl = slice(hd * RET_DV, (hd + 1) * RET_DV)
            inner = jnp.dot(decayed[hd], v[:, sl], preferred_element_type=F32)
            o_ref[rows, sl] = _group_norm_gate(inner + cross[:, sl], g[:, sl]).astype(o_ref.dtype)

    state_scr[...] = jnp.zeros_like(state_scr)
    n_chunks = SEQ // RET_CHUNK
    ahead = first_matmuls(0)
    for c in range(n_chunks):
        current = ahead
        if c + 1 < n_chunks:
            ahead = first_matmuls(c + 1)
        second_matmuls(c, *current)
    for hd in range(RET_HEADS):
        st_ref[hd] = state_scr[hd * RET_DK:(hd + 1) * RET_DK, hd * RET_DV:(hd + 1) * RET_DV]


def _ret_prompt(rq, rk, rv, rg):
    tabs = _ret_prompt_tables()
    seq = lambda w: pl.BlockSpec((SEQ, w), lambda b: (b, 0))
    const = lambda a: pl.BlockSpec(a.shape, lambda b: (0,) * a.ndim)
    return pl.pallas_call(
        _ret_prompt_body,
        grid=(BATCH,),
        in_specs=[seq(RET_QK), seq(RET_QK), seq(RET_WIDTH), seq(RET_WIDTH)] + [const(a) for a in tabs],
        out_specs=[seq(RET_WIDTH), pl.BlockSpec((None, RET_HEADS, RET_DK, RET_DV), lambda b: (b, 0, 0, 0))],
        out_shape=[jax.ShapeDtypeStruct((BATCH * SEQ, RET_WIDTH), BF16),
                   jax.ShapeDtypeStruct((BATCH, RET_HEADS, RET_DK, RET_DV), F32)],
        scratch_shapes=[pltpu.VMEM((RET_QK, RET_WIDTH), F32)],
        compiler_params=pltpu.CompilerParams(dimension_semantics=("arbitrary",),
                                             vmem_limit_bytes=48 * 1024 * 1024),
        name="ret_prompt",
    )(rq, rk, rv, rg, *tabs)


def _ret_sample_tables():
    lg = _log_decay()
    t = np.arange(N_SAMPLE_TOK) % DEC_SEQ
    seq_id = np.arange(N_SAMPLE_TOK) // DEC_SEQ
    diff = (t[:, None] - t[None, :]).astype(np.float64)
    same_seq = seq_id[:, None] == seq_id[None, :]
    dmat = np.where(same_seq[None] & (diff >= 0)[None],
                    np.exp(np.maximum(diff, 0.0)[None] * lg[:, None, None]), 0.0)
    lane_head = np.arange(RET_QK) // RET_DK
    qd = np.exp((t[:, None] + 1.0) * lg[lane_head][None, :])
    kd = np.exp((DEC_SEQ - 1.0 - t)[:, None] * lg[lane_head][None, :])
    return [jnp.asarray(a, dtype=F32) for a in (dmat, qd, kd)]


def _ret_sample_body(q_ref, k_ref, v_ref, g_ref, st_ref, dmat_ref, qd_ref, kd_ref, o_ref, sto_ref):
    lg = _log_decay()
    q = q_ref[...]
    k = k_ref[...]
    kb = k.astype(BF16)
    qdec = q * qd_ref[...]
    kdec = k * kd_ref[...]
    vb = v_ref[...].astype(BF16)
    g = g_ref[...]
    lane = lax.broadcasted_iota(jnp.int32, (N_SAMPLE_TOK, LANES), 1)
    lane_head = lax.broadcasted_iota(jnp.int32, (N_SAMPLE_TOK, RET_QK), 1) >> 6
    n_state_rows = DEC_BATCH * RET_DK
    own_seq = ((lax.broadcasted_iota(jnp.int32, (N_SAMPLE_TOK, n_state_rows), 0) >> 3)
               == (lax.broadcasted_iota(jnp.int32, (N_SAMPLE_TOK, n_state_rows), 1) >> 6))
    for hd in range(RET_HEADS):
        sl = slice(hd * RET_DV, (hd + 1) * RET_DV)
        qm = jnp.where(lane_head == hd, q, 0.0).astype(BF16)
        s = lax.dot_general(qm, kb, _NT, preferred_element_type=F32) * dmat_ref[hd]
        inner = jnp.dot(s.astype(BF16), vb[:, sl], preferred_element_type=F32)

        def expand(z):
            slab = z[:, (hd // 2) * LANES:(hd // 2 + 1) * LANES]
            other = pltpu.roll(slab, RET_DK, 1)
            in_low = lane < RET_DK
            both = jnp.where(in_low, slab, other) if hd % 2 == 0 else jnp.where(in_low, other, slab)
            tiled = jnp.concatenate([both] * (n_state_rows // LANES), axis=1)
            return jnp.where(own_seq, tiled, 0.0).astype(BF16)

        st = st_ref[:, hd].reshape(n_state_rows, RET_DV)
        cross = jnp.dot(expand(qdec), st.astype(BF16), preferred_element_type=F32)
        o_ref[:, sl] = _group_norm_gate(inner + cross, g[:, sl])
        kv = lax.dot_general(expand(kdec), vb[:, sl], _TN, preferred_element_type=F32)
        new = float(np.exp(DEC_SEQ * lg[hd])) * st + kv
        sto_ref[:, hd] = new.reshape(DEC_BATCH, RET_DK, RET_DV)


def _ret_sample(rq, rk, rv, rg, state):
    tabs = _ret_sample_tables()
    full = lambda a: pl.BlockSpec(a.shape, lambda i: (0,) * a.ndim)
    args = (rq, rk, rv, rg, state, *tabs)
    return pl.pallas_call(
        _ret_sample_body,
        grid=(1,),
        in_specs=[full(a) for a in args],
        out_specs=[pl.BlockSpec((N_SAMPLE_TOK, RET_WIDTH), lambda i: (0, 0)),
                   pl.BlockSpec(state.shape, lambda i: (0, 0, 0, 0))],
        out_shape=[jax.ShapeDtypeStruct((N_SAMPLE_TOK, RET_WIDTH), F32),
                   jax.ShapeDtypeStruct(state.shape, F32)],
        compiler_params=pltpu.CompilerParams(dimension_semantics=("arbitrary",),
                                             vmem_limit_bytes=56 * 1024 * 1024),
        name="ret_sample",
    )(*args)


def _moba_prompt_body(q_ref, k_ref, v_ref, o_ref):
    hd = pl.program_id(1)
    n_blocks = SEQ // MOBA_BLOCK
    exp2_scale = MOBA_HEAD_DIM ** -0.5 * LOG2_E
    k32 = k_ref[pl.ds(hd, SEQ, stride=MOBA_HEADS), :]
    kb = k32.astype(BF16)
    vt = v_ref[pl.ds(hd, SEQ, stride=MOBA_HEADS), :].T.astype(BF16)
    kmean = jnp.sum(k32.reshape(n_blocks, MOBA_BLOCK, MOBA_HEAD_DIM), axis=1) * (1.0 / MOBA_BLOCK)
    kmb = kmean.astype(BF16)
    key_id = lax.broadcasted_iota(jnp.int32, (MOBA_BLOCK, MOBA_BLOCK), 0)
    qry_id = lax.broadcasted_iota(jnp.int32, (MOBA_BLOCK, MOBA_BLOCK), 1)
    causal = key_id <= qry_id

    blk = lambda n: slice(n * MOBA_BLOCK, (n + 1) * MOBA_BLOCK)

    def score_matmuls(i):
        qi = q_ref[blk(i), :]
        st = [lax.dot_general(kb[blk(n)], qi, _NT, preferred_element_type=F32) for n in range(i + 1)]
        gt = lax.dot_general(kmb, qi, _NT, preferred_element_type=F32) if i > MOBA_TOPK else None
        return st, gt

    ahead = score_matmuls(0)
    for i in range(n_blocks):
        st, gt = ahead
        if i + 1 < n_blocks:
            ahead = score_matmuls(i + 1)
        st[i] = jnp.where(causal, st[i], NEG_INF)
        if i > MOBA_TOPK:
            for n in range(i):
                beats = jnp.zeros((1, MOBA_BLOCK), F32)
                for mm in range(i):
                    if mm == n:
                        continue
                    win = (gt[mm:mm + 1] >= gt[n:n + 1]) if mm < n else (gt[mm:mm + 1] > gt[n:n + 1])
                    beats = beats + win.astype(F32)
                st[n] = st[n] + jnp.where(beats < MOBA_TOPK, 0.0, NEG_INF)
        m = functools.reduce(jnp.maximum, [jnp.max(s, axis=0, keepdims=True) for s in st])
        l = jnp.zeros((1, MOBA_BLOCK), F32)
        acc = jnp.zeros((MOBA_HEAD_DIM, MOBA_BLOCK), F32)
        for n in range(i + 1):
            e = jnp.exp2((st[n] - m) * exp2_scale)
            l = l + jnp.sum(e, axis=0, keepdims=True)
            acc = acc + jnp.dot(vt[:, blk(n)], e.astype(BF16), preferred_element_type=F32)
        o_ref[blk(i), :] = (acc / l).T.astype(o_ref.dtype)


def _moba_prompt(mq, k2d, v2d):
    kv_spec = pl.BlockSpec((SEQ * MOBA_HEADS, LANES), lambda b, h: (b, 0))
    return pl.pallas_call(
        _moba_prompt_body,
        grid=(BATCH, MOBA_HEADS),
        in_specs=[pl.BlockSpec((SEQ, MOBA_HEAD_DIM), lambda b, h: (b, h)), kv_spec, kv_spec],
        out_specs=pl.BlockSpec((SEQ, MOBA_HEAD_DIM), lambda b, h: (b, h)),
        out_shape=jax.ShapeDtypeStruct((BATCH * SEQ, MOBA_WIDTH), BF16),
        compiler_params=pltpu.CompilerParams(dimension_semantics=("arbitrary", "arbitrary"),
                                             vmem_limit_bytes=56 * 1024 * 1024),
        name="moba_prompt",
    )(mq, k2d, v2d)


class _SampleMoba:
    n_chunks = PAGES_PER_SEQ // CHUNK_PAGES
    n_rows = MOBA_HEADS * DEC_SEQ
    n_blocks = PAGES_PER_SEQ * PAGE_SIZE // MOBA_BLOCK
    pages_per_block = MOBA_BLOCK // PAGE_SIZE

    def __init__(self, pt_ref, seq, n_seqs, q_ref, kn_ref, vn_ref, kc_ref, vc_ref, o_ref, ring, sem, s_scr):
        self.pt_ref, self.seq, self.n_seqs = pt_ref, seq, n_seqs
        self.q_ref, self.kn_ref, self.vn_ref = q_ref, kn_ref, vn_ref
        self.kc_ref, self.vc_ref, self.o_ref = kc_ref, vc_ref, o_ref
        self.ring, self.sem, self.s_scr = ring, sem, s_scr

    def _page_copy(self, cache_ref, row0, slot):
        return pltpu.make_async_copy(cache_ref.at[pl.ds(row0, PAGE_ROWS)], self.ring.at[slot], self.sem.at[slot])

    def _start_chunk(self, cache_ref, seq, chunk):
        for r in range(CHUNK_PAGES):
            page = self.pt_ref[seq * PAGES_PER_SEQ + chunk * CHUNK_PAGES + r]
            self._page_copy(cache_ref, pl.multiple_of(page * PAGE_ROWS, PAGE_ROWS),
                            (chunk % 2) * CHUNK_PAGES + r).start()

    def _wait_chunk(self, cache_ref, chunk):
        for r in range(CHUNK_PAGES):
            self._page_copy(cache_ref, 0, (chunk % 2) * CHUNK_PAGES + r).wait()

    def prologue(self):
        @pl.when(self.seq == 0)
        def _():
            self._start_chunk(self.kc_ref, self.seq, 0)
            self._start_chunk(self.kc_ref, self.seq, 1)

        self.q = jnp.concatenate(
            [self.q_ref[:, hd * MOBA_HEAD_DIM:(hd + 1) * MOBA_HEAD_DIM] for hd in range(MOBA_HEADS)], axis=0
        ).astype(BF16)
        row_head = lax.broadcasted_iota(jnp.int32, (self.n_rows, LANES), 0) >> 3
        col_head = lax.broadcasted_iota(jnp.int32, (self.n_rows, LANES), 1) & (MOBA_HEADS - 1)
        self.same_head = row_head == col_head
        self.head_bias = jnp.where(self.same_head, 0.0, NEG_INF)
        self.block_sum, self.block_max = [], []

    @staticmethod
    def _slabs(x):
        return [x[:, j * LANES:(j + 1) * LANES] for j in range(PAGE_ROWS // LANES)]

    def wait_k(self, c):
        self._wait_chunk(self.kc_ref, c)

    def k_pages(self, c, first, last):
        assert first % self.pages_per_block == 0 and last % self.pages_per_block == 0
        for r0 in range(first, last, self.pages_per_block):
            tot = jnp.zeros((self.n_rows, LANES), F32)
            top = jnp.full((self.n_rows, LANES), NEG_INF, F32)
            for r in range(r0, r0 + self.pages_per_block):
                page = self.ring[(c % 2) * CHUNK_PAGES + r].astype(BF16)
                s = lax.dot_general(self.q, page, _NT, preferred_element_type=F32)
                self.s_scr[c * CHUNK_PAGES + r] = s
                for slab in self._slabs(s):
                    tot = tot + jnp.where(self.same_head, slab, 0.0)
                    top = jnp.maximum(top, slab + self.head_bias)
            self.block_sum.append(jnp.sum(tot, axis=-1, keepdims=True))
            self.block_max.append(jnp.max(top, axis=-1, keepdims=True))

    def refill_after_k(self, c):
        if c + 2 < self.n_chunks:
            self._start_chunk(self.kc_ref, self.seq, c + 2)
        else:
            self._start_chunk(self.vc_ref, self.seq, c + 2 - self.n_chunks)

    def select(self):
        n_rows, gs = self.n_rows, self.block_sum
        self.exp2_scale = MOBA_HEAD_DIM ** -0.5 * LOG2_E
        lane = lax.broadcasted_iota(jnp.int32, (n_rows, LANES), 1)
        g_all = jnp.full((n_rows, LANES), NEG_INF, F32)
        for n in range(self.n_blocks):
            g_all = jnp.where(lane == n, gs[n], g_all)
        self.keep_bias = []
        for n in range(self.n_blocks):
            wins = (g_all > gs[n]) | ((g_all == gs[n]) & (lane < n))
            beats = jnp.sum(wins.astype(F32), axis=-1, keepdims=True)
            self.keep_bias.append(jnp.where(beats < MOBA_TOPK, 0.0, NEG_INF))
        s_own = lax.dot_general(self.q, self.kn_ref[...].astype(BF16), _NT, preferred_element_type=F32)
        r_id = lax.broadcasted_iota(jnp.int32, (n_rows, n_rows), 0)
        c_id = lax.broadcasted_iota(jnp.int32, (n_rows, n_rows), 1)
        own_ok = ((c_id & (MOBA_HEADS - 1)) == (r_id >> 3)) & ((c_id >> 2) <= (r_id & (DEC_SEQ - 1)))
        s_own = jnp.where(own_ok, s_own, NEG_INF)
        m = jnp.max(s_own, axis=-1, keepdims=True)
        for n in range(self.n_blocks):
            m = jnp.maximum(m, self.block_max[n] + self.keep_bias[n])
        self.m = m
        self.lsum = jnp.zeros((n_rows, LANES), F32)
        e_own = jnp.exp2((s_own - m) * self.exp2_scale)
        return (jnp.sum(e_own, axis=-1, keepdims=True),
                jnp.dot(e_own.astype(BF16), self.vn_ref[...].astype(BF16), preferred_element_type=F32))

    def wait_v(self, c):
        self._wait_chunk(self.vc_ref, c)

    def v_pages(self, c, first, last, acc):
        assert first % self.pages_per_block == 0 and last % self.pages_per_block == 0
        for r0 in range(first, last, self.pages_per_block):
            shift = self.head_bias + (self.keep_bias[(c * CHUNK_PAGES + r0) // self.pages_per_block] - self.m)
            for r in range(r0, r0 + self.pages_per_block):
                e = [jnp.exp2((slab + shift) * self.exp2_scale) for slab in self._slabs(self.s_scr[c * CHUNK_PAGES + r])]
                self.lsum = self.lsum + functools.reduce(jnp.add, e)
                page = self.ring[(c % 2) * CHUNK_PAGES + r].astype(BF16)
                acc = acc + jnp.dot(jnp.concatenate(e, axis=1).astype(BF16), page, preferred_element_type=F32)
        return acc

    def refill_after_v(self, c):
        if c + 2 < self.n_chunks:
            self._start_chunk(self.vc_ref, self.seq, c + 2)
        else:
            @pl.when(self.seq + 1 < self.n_seqs)
            def _():
                self._start_chunk(self.kc_ref, self.seq + 1, c + 2 - self.n_chunks)

    def finish(self, l_own, acc):
        out = acc / (l_own + jnp.sum(self.lsum, axis=-1, keepdims=True))
        for hd in range(MOBA_HEADS):
            self.o_ref[:, hd * MOBA_HEAD_DIM:(hd + 1) * MOBA_HEAD_DIM] = out[hd * DEC_SEQ:(hd + 1) * DEC_SEQ]


def _layer_norm(x, g, b):
    mu = jnp.mean(x, axis=-1, keepdims=True)
    d = x - mu
    var = jnp.mean(d * d, axis=-1, keepdims=True)
    return d * lax.rsqrt(var + LN_EPS) * g + b


def _out_ffn_body(ar_ref, am_ref, x_ref, ga_ref, shf_ref, scf_ref, gf_ref, wo_ref, g1_ref, b1_ref,
                  wu_ref, wd_ref, g2_ref, b2_ref, y_ref):
    mixed = (jnp.dot(ar_ref[...].astype(BF16), wo_ref[:RET_WIDTH, :], preferred_element_type=F32)
             + jnp.dot(am_ref[...].astype(BF16), wo_ref[RET_WIDTH:, :], preferred_element_type=F32))
    x1 = _layer_norm(ALPHA * x_ref[...] + ga_ref[...] * mixed, g1_ref[...], b1_ref[...])
    h = (x1 * (1.0 + scf_ref[...]) + shf_ref[...]).astype(BF16)
    acc = jnp.zeros(x1.shape, F32)
    for c in range(D_FF // D_MODEL):
        cols = slice(c * D_MODEL, (c + 1) * D_MODEL)
        u = jnp.maximum(jnp.dot(h, wu_ref[:, cols], preferred_element_type=F32), 0.0)
        acc = acc + jnp.dot((u * u).astype(BF16), wd_ref[cols, :], preferred_element_type=F32)
    y_ref[...] = _layer_norm(ALPHA * x1 + gf_ref[...] * acc, g2_ref[...], b2_ref[...])


def _out_ffn(a_ret, a_moba, x2d, mod_specs, mod_args, weights, grid, row_map, tm):
    t = x2d.shape[0]
    wide = lambda w: pl.BlockSpec((tm, w), row_map)
    const = lambda a: pl.BlockSpec(a.shape, lambda *_: (0,) * a.ndim, pipeline_mode=pl.Buffered(1))
    return pl.pallas_call(
        _out_ffn_body,
        grid=grid,
        in_specs=[wide(RET_WIDTH), wide(MOBA_WIDTH), wide(D_MODEL)] + mod_specs + [const(a) for a in weights],
        out_specs=wide(D_MODEL),
        out_shape=jax.ShapeDtypeStruct((t, D_MODEL), F32),
        compiler_params=pltpu.CompilerParams(dimension_semantics=("arbitrary",) * len(grid),
                                             vmem_limit_bytes=56 * 1024 * 1024),
        name="out_ffn",
    )(a_ret, a_moba, x2d, *mod_args, *weights)


def _out_ffn_moba_body(pt_ref, ar_ref, am_ref, x_ref, ga_ref, shf_ref, scf_ref, gf_ref, wo_ref, g1_ref, b1_ref,
                       wu_ref, wd_ref, g2_ref, b2_ref, q_ref, kn_ref, vn_ref, kc_ref, vc_ref,
                       y_ref, o_ref, ring, sem, s_scr):
    n_seqs = pl.num_programs(0) * pl.num_programs(1)
    seq = pl.program_id(0) * pl.num_programs(1) + pl.program_id(1)
    sm = _SampleMoba(pt_ref, seq, n_seqs, q_ref, kn_ref, vn_ref, kc_ref, vc_ref, o_ref, ring, sem, s_scr)

    def up(c, h):
        u = jnp.maximum(jnp.dot(h, wu_ref[:, c * D_MODEL:(c + 1) * D_MODEL], preferred_element_type=F32), 0.0)
        return (u * u).astype(BF16)

    def down(c, u):
        return jnp.dot(u, wd_ref[c * D_MODEL:(c + 1) * D_MODEL, :], preferred_element_type=F32)

    half = CHUNK_PAGES // 2
    sm.prologue()

    sm.wait_k(0)
    sm.k_pages(0, 0, half)
    mixed = jnp.dot(ar_ref[...].astype(BF16), wo_ref[:RET_WIDTH, :], preferred_element_type=F32)
    sm.k_pages(0, half, CHUNK_PAGES)
    mixed = mixed + jnp.dot(am_ref[...].astype(BF16), wo_ref[RET_WIDTH:, :], preferred_element_type=F32)
    x1 = _layer_norm(ALPHA * x_ref[...] + ga_ref[...] * mixed, g1_ref[...], b1_ref[...])
    h = (x1 * (1.0 + scf_ref[...]) + shf_ref[...]).astype(BF16)
    sm.refill_after_k(0)

    sm.wait_k(1)
    sm.k_pages(1, 0, half)
    u = up(0, h)
    sm.k_pages(1, half, CHUNK_PAGES)
    sm.refill_after_k(1)

    sm.wait_k(2)
    sm.k_pages(2, 0, half)
    acc = down(0, u)
    sm.k_pages(2, half, CHUNK_PAGES)
    u = up(1, h)
    sm.refill_after_k(2)

    sm.wait_k(3)
    sm.k_pages(3, 0, half)
    acc = acc + down(1, u)
    sm.k_pages(3, half, CHUNK_PAGES)
    sm.refill_after_k(3)

    l, acc_s = sm.select()
    u = up(2, h)

    sm.wait_v(0)
    acc_s = sm.v_pages(0, 0, half, acc_s)
    acc = acc + down(2, u)
    acc_s = sm.v_pages(0, half, CHUNK_PAGES, acc_s)
    u = up(3, h)
    sm.refill_after_v(0)

    sm.wait_v(1)
    acc_s = sm.v_pages(1, 0, half, acc_s)
    acc = acc + down(3, u)
    acc_s = sm.v_pages(1, half, CHUNK_PAGES, acc_s)
    sm.refill_after_v(1)

    sm.wait_v(2)
    acc_s = sm.v_pages(2, 0, half, acc_s)
    y_ref[...] = _layer_norm(ALPHA * x1 + gf_ref[...] * acc, g2_ref[...], b2_ref[...])
    acc_s = sm.v_pages(2, half, CHUNK_PAGES, acc_s)
    sm.refill_after_v(2)

    sm.wait_v(3)
    acc_s = sm.v_pages(3, 0, CHUNK_PAGES, acc_s)
    sm.refill_after_v(3)
    sm.finish(l, acc_s)


def _out_ffn_moba(a_ret, a_moba, x2d, mod_rows, weights, page_table, mq_s, kn2d, vn2d, cache_k2d, cache_v2d, tm):
    t = x2d.shape[0]
    nt = SEQ // tm
    assert BATCH * nt == DEC_BATCH
    n_rows = MOBA_HEADS * DEC_SEQ
    tile = lambda b, i, pt: (b * nt + i, 0)
    wide = lambda w: pl.BlockSpec((tm, w), tile)
    mod = lambda chunk: pl.BlockSpec((None, 1, D_MODEL), lambda b, i, pt: (b * 6 + chunk, 0, 0))
    const = lambda a: pl.BlockSpec(a.shape, lambda b, i, pt: (0,) * a.ndim, pipeline_mode=pl.Buffered(1))
    seq_rows = lambda w, n: pl.BlockSpec((n, w), tile)
    hbm = pl.BlockSpec(memory_space=pl.ANY)
    grid_spec = pltpu.PrefetchScalarGridSpec(
        num_scalar_prefetch=1,
        grid=(BATCH, nt),
        in_specs=[wide(RET_WIDTH), wide(MOBA_WIDTH), wide(D_MODEL)] + [mod(c) for c in (2, 3, 4, 5)]
        + [const(a) for a in weights]
        + [seq_rows(MOBA_WIDTH, DEC_SEQ), seq_rows(LANES, n_rows), seq_rows(LANES, n_rows), hbm, hbm],
        out_specs=[wide(D_MODEL), seq_rows(MOBA_WIDTH, DEC_SEQ)],
        scratch_shapes=[pltpu.VMEM((RING_PAGES, PAGE_ROWS, LANES), F32),
                        pltpu.SemaphoreType.DMA((RING_PAGES,)),
                        pltpu.VMEM((PAGES_PER_SEQ, n_rows, PAGE_ROWS), F32)],
    )
    return pl.pallas_call(
        _out_ffn_moba_body,
        grid_spec=grid_spec,
        out_shape=[jax.ShapeDtypeStruct((t, D_MODEL), F32),
                   jax.ShapeDtypeStruct((N_SAMPLE_TOK, MOBA_WIDTH), F32)],
        compiler_params=pltpu.CompilerParams(dimension_semantics=("arbitrary", "arbitrary"),
                                             vmem_limit_bytes=58 * 1024 * 1024),
        name="out_ffn_moba",
    )(page_table.reshape(-1), a_ret, a_moba, x2d, *([mod_rows] * 4), *weights,
      mq_s, kn2d, vn2d, cache_k2d, cache_v2d)


def kernel(x_prompt, x_sample, cache_k, cache_v, state_ret, page_table, c_prompt, c_sample,
           w_ada, b_ada, w_in, w_o, ln1_g, ln1_b, w_up, w_down, ln2_g, ln2_b):
    n_prompt_tok = BATCH * SEQ
    past_len = page_table.shape[1] * PAGE_SIZE

    mod = _adaln(jnp.concatenate([c_prompt, c_sample], axis=0), w_ada[0], b_ada)
    mod_rows = mod.reshape((BATCH + DEC_BATCH) * 6, 1, D_MODEL)
    mod_s = jnp.repeat(mod[BATCH:], DEC_SEQ, axis=0)

    def prompt_mod(chunk):
        return pl.BlockSpec((None, 1, D_MODEL), lambda b, i: (b * 6 + chunk, 0, 0))

    def sample_mod(chunk):
        return pl.BlockSpec((N_SAMPLE_TOK, D_MODEL), lambda i: (0, chunk))

    w_in_b = w_in[0].astype(BF16)
    weights = (w_o[0].astype(BF16), ln1_g, ln1_b, w_up[0].astype(BF16), w_down[0].astype(BF16), ln2_g, ln2_b)

    tm = 512
    nt = SEQ // tm
    p_row = lambda b, i: (b * nt + i, 0)
    p_tab = pl.BlockSpec((tm, LANES), lambda b, i: (i, 0))
    s_row = lambda i: (0, 0)
    s_tab = pl.BlockSpec((N_SAMPLE_TOK, LANES), s_row)

    pos_p = np.arange(SEQ, dtype=np.int32)
    pos_s = np.tile(past_len + np.arange(DEC_SEQ, dtype=np.int32), DEC_BATCH)
    tabs_p = _rope_tables(pos_p, MOBA_HEAD_DIM) + _rope_tables(pos_p, RET_DK)
    tabs_s = _rope_tables(pos_s, MOBA_HEAD_DIM) + _rope_tables(pos_s, RET_DK)

    xp = x_prompt.reshape(n_prompt_tok, D_MODEL)
    rq, rk, rv, rg, mq, k_p, v_p = _inproj(
        xp, [prompt_mod(1), prompt_mod(0)], [mod_rows, mod_rows], [p_tab] * 4, tabs_p, w_in_b,
        (BATCH, nt), p_row, tm, BF16)
    a_ret, state_p = _ret_prompt(rq, rk, rv, rg)
    a_moba = _moba_prompt(mq, k_p, v_p)

    xs = x_sample.reshape(N_SAMPLE_TOK, D_MODEL)
    rq_s, rk_s, rv_s, rg_s, mq_s, k_s, v_s = _inproj(
        xs, [sample_mod(1), sample_mod(0)], [mod_s, mod_s], [s_tab] * 4, tabs_s, w_in_b,
        (1,), s_row, N_SAMPLE_TOK, F32)
    a_ret_s, state_s = _ret_sample(rq_s, rk_s, rv_s, rg_s, state_ret[0])

    cache_rows = cache_k.shape[1] * PAGE_ROWS
    y_p, a_moba_s = _out_ffn_moba(a_ret, a_moba, xp, mod_rows, weights, page_table, mq_s, k_s, v_s,
                                  cache_k.reshape(cache_rows, LANES), cache_v.reshape(cache_rows, LANES), tm)
    y_s = _out_ffn(a_ret_s, a_moba_s, xs, [sample_mod(c) for c in (2, 3, 4, 5)], [mod_s] * 4, weights,
                   (1,), s_row, N_SAMPLE_TOK)

    kv_p_shape = (DEPTH, BATCH, SEQ, MOBA_HEADS, MOBA_HEAD_DIM)
    kv_s_shape = (DEPTH, DEC_BATCH, DEC_SEQ, MOBA_HEADS, MOBA_HEAD_DIM)
    return (y_p.reshape(BATCH, SEQ, D_MODEL),
            y_s.reshape(DEC_BATCH, DEC_SEQ, D_MODEL),
            k_p.reshape(kv_p_shape), v_p.reshape(kv_p_shape), state_p[None],
            k_s.reshape(kv_s_shape), v_s.reshape(kv_s_shape), state_s[None])
```

```python
import functools

import numpy as np
import jax
import jax.numpy as jnp
from jax import lax
from jax.experimental import pallas as pl
from jax.experimental.pallas import tpu as pltpu

F32 = jnp.float32
BF16 = jnp.bfloat16

D_MODEL = 1024
BATCH = 8
SEQ = 2048
DEC_BATCH = 32
DEC_SEQ = 8
PAGE_SIZE = 128
RET_HEADS = 4
RET_DK = 64
RET_DV = 128
RET_CHUNK = 128
MOBA_HEADS = 4
MOBA_HEAD_DIM = 128
MOBA_BLOCK = 256
MOBA_TOPK = 3
D_FF = 4 * D_MODEL
ROPE_THETA = 10000.0
LN_EPS = 1e-5
GN_EPS = 1e-6
DEPTH = 1
ALPHA = (2 * DEPTH) ** 0.25
RET_QK = RET_HEADS * RET_DK
RET_WIDTH = RET_HEADS * RET_DV
MOBA_WIDTH = MOBA_HEADS * MOBA_HEAD_DIM
IN_WIDTH = 2 * RET_QK + 2 * RET_WIDTH + 3 * MOBA_WIDTH
OFF_RQ, OFF_RK, OFF_RV, OFF_RG = 0, 256, 512, 1024
OFF_MQ, OFF_MK, OFF_MV = 1536, 2048, 2560
LANES = 128
N_SAMPLE_TOK = DEC_BATCH * DEC_SEQ
PAGES_PER_SEQ = 64
CHUNK_PAGES = 16
RING_PAGES = 2 * CHUNK_PAGES
PAGE_ROWS = PAGE_SIZE * MOBA_HEADS
NEG_INF = float("-inf")
LOG2_E = 1.4426950408889634

_NT = (((1,), (1,)), ((), ()))
_TN = (((0,), (0,)), ((), ()))


def _log_decay():
    return np.log1p(-np.exp2(-5.0 - np.arange(RET_HEADS, dtype=np.float64)))


def _adaln_body(c_ref, w_ref, b_ref, o_ref):
    c = c_ref[...]
    a = (c * jax.nn.sigmoid(c)).astype(BF16)
    o_ref[...] = jnp.dot(a, w_ref[...].astype(BF16), preferred_element_type=F32) + b_ref[...]


def _adaln(c_all, w_ada, b_ada):
    n = c_all.shape[0]
    tn = 1024
    return pl.pallas_call(
        _adaln_body,
        grid=(6 * D_MODEL // tn,),
        in_specs=[pl.BlockSpec((n, D_MODEL), lambda j: (0, 0)),
                  pl.BlockSpec((D_MODEL, tn), lambda j: (0, j)),
                  pl.BlockSpec((1, tn), lambda j: (0, j))],
        out_specs=pl.BlockSpec((n, tn), lambda j: (0, j)),
        out_shape=jax.ShapeDtypeStruct((n, 6 * D_MODEL), F32),
        name="adaln",
    )(c_all, w_ada, b_ada)


def _rope_tables(pos, head_dim):
    half = head_dim // 2
    inv_freq = np.power(ROPE_THETA, -np.arange(half, dtype=np.float64) / half)
    ang = pos.astype(np.float64)[:, None] * inv_freq[None, :]
    cos, sin = np.cos(ang), np.sin(ang)
    reps = LANES // head_dim
    cos_t = np.tile(np.concatenate([cos, cos], axis=-1), (1, reps))
    sin_t = np.tile(np.concatenate([-sin, sin], axis=-1), (1, reps))
    return jnp.asarray(cos_t, dtype=F32), jnp.asarray(sin_t, dtype=F32)


def _inproj_body(x_ref, sc_ref, sh_ref, w_ref, cm_ref, sm_ref, cr_ref, sr_ref,
                 rq_ref, rk_ref, rv_ref, rg_ref, mq_ref, ko_ref, vo_ref):
    tm = x_ref.shape[0]
    h = (x_ref[...] * (1.0 + sc_ref[...]) + sh_ref[...]).astype(BF16)

    def proj(lo, width):
        return jnp.dot(h, w_ref[:, lo:lo + width], preferred_element_type=F32)

    lane = lax.broadcasted_iota(jnp.int32, (tm, LANES), 1)
    low_half = (lane & (RET_DK - 1)) < (RET_DK // 2)
    cr, sr = cr_ref[...], sr_ref[...]
    cm, sm = cm_ref[...], sm_ref[...]

    def rope_ret(z):
        rot = jnp.where(low_half, pltpu.roll(z, LANES - RET_DK // 2, 1), pltpu.roll(z, RET_DK // 2, 1))
        return z * cr + rot * sr

    def rope_moba(z):
        return z * cm + pltpu.roll(z, MOBA_HEAD_DIM // 2, 1) * sm

    zq = proj(OFF_RQ, RET_QK)
    zk = proj(OFF_RK, RET_QK)
    for s in range(RET_QK // LANES):
        sl = slice(s * LANES, (s + 1) * LANES)
        rq_ref[:, sl] = rope_ret(zq[:, sl])
        rk_ref[:, sl] = rope_ret(zk[:, sl]) * (RET_DK ** -0.5)
    rv_ref[...] = proj(OFF_RV, RET_WIDTH).astype(rv_ref.dtype)
    rg_ref[...] = proj(OFF_RG, RET_WIDTH)
    zq = proj(OFF_MQ, MOBA_WIDTH)
    zk = proj(OFF_MK, MOBA_WIDTH)
    zv = proj(OFF_MV, MOBA_WIDTH)
    for hd in range(MOBA_HEADS):
        sl = slice(hd * LANES, (hd + 1) * LANES)
        mq_ref[:, sl] = rope_moba(zq[:, sl]).astype(mq_ref.dtype)
        ko_ref[pl.ds(hd, tm, stride=MOBA_HEADS), :] = rope_moba(zk[:, sl])
        vo_ref[pl.ds(hd, tm, stride=MOBA_HEADS), :] = zv[:, sl]


def _inproj(x2d, mod_specs, mod_args, tab_specs, tabs, w_in, grid, row_map, tm, act_dtype):
    t = x2d.shape[0]
    wide = lambda w: pl.BlockSpec((tm, w), row_map)
    return pl.pallas_call(
        _inproj_body,
        grid=grid,
        in_specs=[wide(D_MODEL)] + mod_specs + [pl.BlockSpec((D_MODEL, IN_WIDTH), lambda *_: (0, 0))] + tab_specs,
        out_specs=[wide(RET_QK), wide(RET_QK), wide(RET_WIDTH), wide(RET_WIDTH), wide(MOBA_WIDTH),
                   pl.BlockSpec((tm * MOBA_HEADS, LANES), row_map),
                   pl.BlockSpec((tm * MOBA_HEADS, LANES), row_map)],
        out_shape=[jax.ShapeDtypeStruct((t, RET_QK), F32), jax.ShapeDtypeStruct((t, RET_QK), F32),
                   jax.ShapeDtypeStruct((t, RET_WIDTH), act_dtype), jax.ShapeDtypeStruct((t, RET_WIDTH), F32),
                   jax.ShapeDtypeStruct((t, MOBA_WIDTH), act_dtype),
                   jax.ShapeDtypeStruct((t * MOBA_HEADS, LANES), F32),
                   jax.ShapeDtypeStruct((t * MOBA_HEADS, LANES), F32)],
        compiler_params=pltpu.CompilerParams(dimension_semantics=("arbitrary",) * len(grid),
                                             vmem_limit_bytes=48 * 1024 * 1024),
        name="inproj",
    )(x2d, *mod_args, w_in, *tabs)


def _group_norm_gate(o, g):
    mu = jnp.mean(o, axis=-1, keepdims=True)
    d = o - mu
    var = jnp.mean(d * d, axis=-1, keepdims=True)
    return d * lax.rsqrt(var + GN_EPS) * (g * jax.nn.sigmoid(g))


def _ret_prompt_tables():
    lg = _log_decay()
    i = np.arange(RET_CHUNK, dtype=np.float64)
    diff = i[:, None] - i[None, :]
    dmat = np.where(diff >= 0, np.exp(np.maximum(diff, 0.0)[None] * lg[:, None, None]), 0.0)
    lane_head = np.arange(RET_QK) // RET_DK
    qd = np.exp((i[:, None] + 1.0) * lg[lane_head][None, :])
    kd = np.exp((RET_CHUNK - 1.0 - i)[:, None] * lg[lane_head][None, :])
    row_head = np.arange(RET_QK) // RET_DK
    col_head = np.arange(RET_WIDTH) // RET_DV
    same = row_head[:, None] == col_head[None, :]
    cdec = np.where(same, np.exp(RET_CHUNK * lg[row_head])[:, None], 0.0)
    return [jnp.asarray(a, dtype=F32) for a in (dmat, qd, kd, cdec, same.astype(np.float64))]


def _ret_prompt_body(q_ref, k_ref, v_ref, g_ref, dmat_ref, qd_ref, kd_ref, cdec_ref, bdm_ref,
                     o_ref, st_ref, state_scr):
    lane_head = lax.broadcasted_iota(jnp.int32, (RET_CHUNK, RET_QK), 1) >> 6
    chunk_rows = lambda c: slice(c * RET_CHUNK, (c + 1) * RET_CHUNK)

    def first_matmuls(c):
        rows = chunk_rows(c)
        q = q_ref[rows, :]
        k = k_ref[rows, :]
        v = v_ref[rows, :]
        kb = k.astype(BF16)
        state = state_scr[...]
        scores = [lax.dot_general(jnp.where(lane_head == hd, q, 0.0).astype(BF16), kb, _NT,
                                  preferred_element_type=F32) for hd in range(RET_HEADS)]
        cross = jnp.dot((q * qd_ref[...]).astype(BF16), state.astype(BF16), preferred_element_type=F32)
        kv = lax.dot_general((k * kd_ref[...]).astype(BF16), v, _TN, preferred_element_type=F32)
        state_scr[...] = cdec_ref[...] * state + bdm_ref[...] * kv
        return scores, cross, v

    def second_matmuls(c, scores, cross, v):
        rows = chunk_rows(c)
        g = g_ref[rows, :]
        decayed = [(scores[hd] * dmat_ref[hd]).astype(BF16) for hd in range(RET_HEADS)]
        for hd in range(RET_HEADS):
            sl = slice(hd * RET_DV, (hd + 1) * RET_DV)
            inner = jnp.dot(decayed[hd], v[:, sl], preferred_element_type=F32)
            o_ref[rows, sl] = _group_norm_gate(inner + cross[:, sl], g[:, sl]).astype(o_ref.dtype)

    state_scr[...] = jnp.zeros_like(state_scr)
    n_chunks = SEQ // RET_CHUNK
    ahead = first_matmuls(0)
    for c in range(n_chunks):
        current = ahead
        if c + 1 < n_chunks:
            ahead = first_matmuls(c + 1)
        second_matmuls(c, *current)
    for hd in range(RET_HEADS):
        st_ref[hd] = state_scr[hd * RET_DK:(hd + 1) * RET_DK, hd * RET_DV:(hd + 1) * RET_DV]


def _ret_prompt(rq, rk, rv, rg):
    tabs = _ret_prompt_tables()
    seq = lambda w: pl.BlockSpec((SEQ, w), lambda b: (b, 0))
    const = lambda a: pl.BlockSpec(a.shape, lambda b: (0,) * a.ndim)
    return pl.pallas_call(
        _ret_prompt_body,
        grid=(BATCH,),
        in_specs=[seq(RET_QK), seq(RET_QK), seq(RET_WIDTH), seq(RET_WIDTH)] + [const(a) for a in tabs],
        out_specs=[seq(RET_WIDTH), pl.BlockSpec((None, RET_HEADS, RET_DK, RET_DV), lambda b: (b, 0, 0, 0))],
        out_shape=[jax.ShapeDtypeStruct((BATCH * SEQ, RET_WIDTH), BF16),
                   jax.ShapeDtypeStruct((BATCH, RET_HEADS, RET_DK, RET_DV), F32)],
        scratch_shapes=[pltpu.VMEM((RET_QK, RET_WIDTH), F32)],
        compiler_params=pltpu.CompilerParams(dimension_semantics=("arbitrary",),
                                             vmem_limit_bytes=48 * 1024 * 1024),
        name="ret_prompt",
    )(rq, rk, rv, rg, *tabs)


def _ret_sample_tables():
    lg = _log_decay()
    t = np.arange(N_SAMPLE_TOK) % DEC_SEQ
    seq_id = np.arange(N_SAMPLE_TOK) // DEC_SEQ
    diff = (t[:, None] - t[None, :]).astype(np.float64)
    same_seq = seq_id[:, None] == seq_id[None, :]
    dmat = np.where(same_seq[None] & (diff >= 0)[None],
                    np.exp(np.maximum(diff, 0.0)[None] * lg[:, None, None]), 0.0)
    lane_head = np.arange(RET_QK) // RET_DK
    qd = np.exp((t[:, None] + 1.0) * lg[lane_head][None, :])
    kd = np.exp((DEC_SEQ - 1.0 - t)[:, None] * lg[lane_head][None, :])
    return [jnp.asarray(a, dtype=F32) for a in (dmat, qd, kd)]


def _ret_sample_body(q_ref, k_ref, v_ref, g_ref, st_ref, dmat_ref, qd_ref, kd_ref, o_ref, sto_ref):
    lg = _log_decay()
    q = q_ref[...]
    k = k_ref[...]
    kb = k.astype(BF16)
    qdec = q * qd_ref[...]
    kdec = k * kd_ref[...]
    vb = v_ref[...].astype(BF16)
    g = g_ref[...]
    lane = lax.broadcasted_iota(jnp.int32, (N_SAMPLE_TOK, LANES), 1)
    lane_head = lax.broadcasted_iota(jnp.int32, (N_SAMPLE_TOK, RET_QK), 1) >> 6
    n_state_rows = DEC_BATCH * RET_DK
    own_seq = ((lax.broadcasted_iota(jnp.int32, (N_SAMPLE_TOK, n_state_rows), 0) >> 3)
               == (lax.broadcasted_iota(jnp.int32, (N_SAMPLE_TOK, n_state_rows), 1) >> 6))
    for hd in range(RET_HEADS):
        sl = slice(hd * RET_DV, (hd + 1) * RET_DV)
        qm = jnp.where(lane_head == hd, q, 0.0).astype(BF16)
        s = lax.dot_general(qm, kb, _NT, preferred_element_type=F32) * dmat_ref[hd]
        inner = jnp.dot(s.astype(BF16), vb[:, sl], preferred_element_type=F32)

        def expand(z):
            slab = z[:, (hd // 2) * LANES:(hd // 2 + 1) * LANES]
            other = pltpu.roll(slab, RET_DK, 1)
            in_low = lane < RET_DK
            both = jnp.where(in_low, slab, other) if hd % 2 == 0 else jnp.where(in_low, other, slab)
            tiled = jnp.concatenate([both] * (n_state_rows // LANES), axis=1)
            return jnp.where(own_seq, tiled, 0.0).astype(BF16)

        st = st_ref[:, hd].reshape(n_state_rows, RET_DV)
        cross = jnp.dot(expand(qdec), st.astype(BF16), preferred_element_type=F32)
        o_ref[:, sl] = _group_norm_gate(inner + cross, g[:, sl])
        kv = lax.dot_general(expand(kdec), vb[:, sl], _TN, preferred_element_type=F32)
        new = float(np.exp(DEC_SEQ * lg[hd])) * st + kv
        sto_ref[:, hd] = new.reshape(DEC_BATCH, RET_DK, RET_DV)


def _ret_sample(rq, rk, rv, rg, state):
    tabs = _ret_sample_tables()
    full = lambda a: pl.BlockSpec(a.shape, lambda i: (0,) * a.ndim)
    args = (rq, rk, rv, rg, state, *tabs)
    return pl.pallas_call(
        _ret_sample_body,
        grid=(1,),
        in_specs=[full(a) for a in args],
        out_specs=[pl.BlockSpec((N_SAMPLE_TOK, RET_WIDTH), lambda i: (0, 0)),
                   pl.BlockSpec(state.shape, lambda i: (0, 0, 0, 0))],
        out_shape=[jax.ShapeDtypeStruct((N_SAMPLE_TOK, RET_WIDTH), F32),
                   jax.ShapeDtypeStruct(state.shape, F32)],
        compiler_params=pltpu.CompilerParams(dimension_semantics=("arbitrary",),
                                             vmem_limit_bytes=56 * 1024 * 1024),
        name="ret_sample",
    )(*args)


def _moba_prompt_body(q_ref, k_ref, v_ref, o_ref):
    hd = pl.program_id(1)
    n_blocks = SEQ // MOBA_BLOCK
    exp2_scale = MOBA_HEAD_DIM ** -0.5 * LOG2_E
    k32 = k_ref[pl.ds(hd, SEQ, stride=MOBA_HEADS), :]
    kb = k32.astype(BF16)
    vt = v_ref[pl.ds(hd, SEQ, stride=MOBA_HEADS), :].T.astype(BF16)
    kmean = jnp.sum(k32.reshape(n_blocks, MOBA_BLOCK, MOBA_HEAD_DIM), axis=1) * (1.0 / MOBA_BLOCK)
    kmb = kmean.astype(BF16)
    key_id = lax.broadcasted_iota(jnp.int32, (MOBA_BLOCK, MOBA_BLOCK), 0)
    qry_id = lax.broadcasted_iota(jnp.int32, (MOBA_BLOCK, MOBA_BLOCK), 1)
    causal = key_id <= qry_id

    blk = lambda n: slice(n * MOBA_BLOCK, (n + 1) * MOBA_BLOCK)

    def score_matmuls(i):
        qi = q_ref[blk(i), :]
        st = [lax.dot_general(kb[blk(n)], qi, _NT, preferred_element_type=F32) for n in range(i + 1)]
        gt = lax.dot_general(kmb, qi, _NT, preferred_element_type=F32) if i > MOBA_TOPK else None
        return st, gt

    ahead = score_matmuls(0)
    for i in range(n_blocks):
        st, gt = ahead
        if i + 1 < n_blocks:
            ahead = score_matmuls(i + 1)
        st[i] = jnp.where(causal, st[i], NEG_INF)
        if i > MOBA_TOPK:
            for n in range(i):
                beats = jnp.zeros((1, MOBA_BLOCK), F32)
                for mm in range(i):
                    if mm == n:
                        continue
                    win = (gt[mm:mm + 1] >= gt[n:n + 1]) if mm < n else (gt[mm:mm + 1] > gt[n:n + 1])
                    beats = beats + win.astype(F32)
                st[n] = st[n] + jnp.where(beats < MOBA_TOPK, 0.0, NEG_INF)
        m = functools.reduce(jnp.maximum, [jnp.max(s, axis=0, keepdims=True) for s in st])
        l = jnp.zeros((1, MOBA_BLOCK), F32)
        acc = jnp.zeros((MOBA_HEAD_DIM, MOBA_BLOCK), F32)
        for n in range(i + 1):
            e = jnp.exp2((st[n] - m) * exp2_scale)
            l = l + jnp.sum(e, axis=0, keepdims=True)
            acc = acc + jnp.dot(vt[:, blk(n)], e.astype(BF16), preferred_element_type=F32)
        o_ref[blk(i), :] = (acc / l).T.astype(o_ref.dtype)


def _moba_prompt(mq, k2d, v2d):
    kv_spec = pl.BlockSpec((SEQ * MOBA_HEADS, LANES), lambda b, h: (b, 0))
    return pl.pallas_call(
        _moba_prompt_body,
        grid=(BATCH, MOBA_HEADS),
        in_specs=[pl.BlockSpec((SEQ, MOBA_HEAD_DIM), lambda b, h: (b, h)), kv_spec, kv_spec],
        out_specs=pl.BlockSpec((SEQ, MOBA_HEAD_DIM), lambda b, h: (b, h)),
        out_shape=jax.ShapeDtypeStruct((BATCH * SEQ, MOBA_WIDTH), BF16),
        compiler_params=pltpu.CompilerParams(dimension_semantics=("arbitrary", "arbitrary"),
                                             vmem_limit_bytes=56 * 1024 * 1024),
        name="moba_prompt",
    )(mq, k2d, v2d)


class _SampleMoba:
    n_chunks = PAGES_PER_SEQ // CHUNK_PAGES
    n_rows = MOBA_HEADS * DEC_SEQ
    n_blocks = PAGES_PER_SEQ * PAGE_SIZE // MOBA_BLOCK
    pages_per_block = MOBA_BLOCK // PAGE_SIZE

    def __init__(self, pt_ref, seq, n_seqs, q_ref, kn_ref, vn_ref, kc_ref, vc_ref, o_ref, ring, sem, s_scr):
        self.pt_ref, self.seq, self.n_seqs = pt_ref, seq, n_seqs
        self.q_ref, self.kn_ref, self.vn_ref = q_ref, kn_ref, vn_ref
        self.kc_ref, self.vc_ref, self.o_ref = kc_ref, vc_ref, o_ref
        self.ring, self.sem, self.s_scr = ring, sem, s_scr

    def _page_copy(self, cache_ref, row0, slot):
        return pltpu.make_async_copy(cache_ref.at[pl.ds(row0, PAGE_ROWS)], self.ring.at[slot], self.sem.at[slot])

    def _start_chunk(self, cache_ref, seq, chunk):
        for r in range(CHUNK_PAGES):
            page = self.pt_ref[seq * PAGES_PER_SEQ + chunk * CHUNK_PAGES + r]
            self._page_copy(cache_ref, pl.multiple_of(page * PAGE_ROWS, PAGE_ROWS),
                            (chunk % 2) * CHUNK_PAGES + r).start()

    def _wait_chunk(self, cache_ref, chunk):
        for r in range(CHUNK_PAGES):
            self._page_copy(cache_ref, 0, (chunk % 2) * CHUNK_PAGES + r).wait()

    def prologue(self):
        @pl.when(self.seq == 0)
        def _():
            self._start_chunk(self.kc_ref, self.seq, 0)
            self._start_chunk(self.kc_ref, self.seq, 1)

        self.q = jnp.concatenate(
            [self.q_ref[:, hd * MOBA_HEAD_DIM:(hd + 1) * MOBA_HEAD_DIM] for hd in range(MOBA_HEADS)], axis=0
        ).astype(BF16)
        row_head = lax.broadcasted_iota(jnp.int32, (self.n_rows, LANES), 0) >> 3
        col_head = lax.broadcasted_iota(jnp.int32, (self.n_rows, LANES), 1) & (MOBA_HEADS - 1)
        self.same_head = row_head == col_head
        self.head_bias = jnp.where(self.same_head, 0.0, NEG_INF)
        self.block_sum, self.block_max = [], []

    @staticmethod
    def _slabs(x):
        return [x[:, j * LANES:(j + 1) * LANES] for j in range(PAGE_ROWS // LANES)]

    def wait_k(self, c):
        self._wait_chunk(self.kc_ref, c)

    def k_pages(self, c, first, last):
        assert first % self.pages_per_block == 0 and last % self.pages_per_block == 0
        for r0 in range(first, last, self.pages_per_block):
            tot = jnp.zeros((self.n_rows, LANES), F32)
            top = jnp.full((self.n_rows, LANES), NEG_INF, F32)
            for r in range(r0, r0 + self.pages_per_block):
                page = self.ring[(c % 2) * CHUNK_PAGES + r].astype(BF16)
                s = lax.dot_general(self.q, page, _NT, preferred_element_type=F32)
                self.s_scr[c * CHUNK_PAGES + r] = s
                for slab in self._slabs(s):
                    tot = tot + jnp.where(self.same_head, slab, 0.0)
                    top = jnp.maximum(top, slab + self.head_bias)
            self.block_sum.append(jnp.sum(tot, axis=-1, keepdims=True))
            self.block_max.append(jnp.max(top, axis=-1, keepdims=True))

    def refill_after_k(self, c):
        if c + 2 < self.n_chunks:
            self._start_chunk(self.kc_ref, self.seq, c + 2)
        else:
            self._start_chunk(self.vc_ref, self.seq, c + 2 - self.n_chunks)

    def select(self):
        n_rows, gs = self.n_rows, self.block_sum
        self.exp2_scale = MOBA_HEAD_DIM ** -0.5 * LOG2_E
        lane = lax.broadcasted_iota(jnp.int32, (n_rows, LANES), 1)
        g_all = jnp.full((n_rows, LANES), NEG_INF, F32)
        for n in range(self.n_blocks):
            g_all = jnp.where(lane == n, gs[n], g_all)
        self.keep_bias = []
        for n in range(self.n_blocks):
            wins = (g_all > gs[n]) | ((g_all == gs[n]) & (lane < n))
            beats = jnp.sum(wins.astype(F32), axis=-1, keepdims=True)
            self.keep_bias.append(jnp.where(beats < MOBA_TOPK, 0.0, NEG_INF))
        s_own = lax.dot_general(self.q, self.kn_ref[...].astype(BF16), _NT, preferred_element_type=F32)
        r_id = lax.broadcasted_iota(jnp.int32, (n_rows, n_rows), 0)
        c_id = lax.broadcasted_iota(jnp.int32, (n_rows, n_rows), 1)
        own_ok = ((c_id & (MOBA_HEADS - 1)) == (r_id >> 3)) & ((c_id >> 2) <= (r_id & (DEC_SEQ - 1)))
        s_own = jnp.where(own_ok, s_own, NEG_INF)
        m = jnp.max(s_own, axis=-1, keepdims=True)
        for n in range(self.n_blocks):
            m = jnp.maximum(m, self.block_max[n] + self.keep_bias[n])
        self.m = m
        self.lsum = jnp.zeros((n_rows, LANES), F32)
        e_own = jnp.exp2((s_own - m) * self.exp2_scale)
        return (jnp.sum(e_own, axis=-1, keepdims=True),
                jnp.dot(e_own.astype(BF16), self.vn_ref[...].astype(BF16), preferred_element_type=F32))

    def wait_v(self, c):
        self._wait_chunk(self.vc_ref, c)

    def v_pages(self, c, first, last, acc):
        assert first % self.pages_per_block == 0 and last % self.pages_per_block == 0
        for r0 in range(first, last, self.pages_per_block):
            shift = self.head_bias + (self.keep_bias[(c * CHUNK_PAGES + r0) // self.pages_per_block] - self.m)
            for r in range(r0, r0 + self.pages_per_block):
                e = [jnp.exp2((slab + shift) * self.exp2_scale) for slab in self._slabs(self.s_scr[c * CHUNK_PAGES + r])]
                self.lsum = self.lsum + functools.reduce(jnp.add, e)
                page = self.ring[(c % 2) * CHUNK_PAGES + r].astype(BF16)
                acc = acc + jnp.dot(jnp.concatenate(e, axis=1).astype(BF16), page, preferred_element_type=F32)
        return acc

    def refill_after_v(self, c):
        if c + 2 < self.n_chunks:
            self._start_chunk(self.vc_ref, self.seq, c + 2)
        else:
            @pl.when(self.seq + 1 < self.n_seqs)
            def _():
                self._start_chunk(self.kc_ref, self.seq + 1, c + 2 - self.n_chunks)

    def finish(self, l_own, acc):
        out = acc / (l_own + jnp.sum(self.lsum, axis=-1, keepdims=True))
        for hd in range(MOBA_HEADS):
            self.o_ref[:, hd * MOBA_HEAD_DIM:(hd + 1) * MOBA_HEAD_DIM] = out[hd * DEC_SEQ:(hd + 1) * DEC_SEQ]


def _layer_norm(x, g, b):
    mu = jnp.mean(x, axis=-1, keepdims=True)
    d = x - mu
    var = jnp.mean(d * d, axis=-1, keepdims=True)
    return d * lax.rsqrt(var + LN_EPS) * g + b


def _out_ffn_body(ar_ref, am_ref, x_ref, ga_ref, shf_ref, scf_ref, gf_ref, wo_ref, g1_ref, b1_ref,
                  wu_ref, wd_ref, g2_ref, b2_ref, y_ref):
    mixed = (jnp.dot(ar_ref[...].astype(BF16), wo_ref[:RET_WIDTH, :], preferred_element_type=F32)
             + jnp.dot(am_ref[...].astype(BF16), wo_ref[RET_WIDTH:, :], preferred_element_type=F32))
    x1 = _layer_norm(ALPHA * x_ref[...] + ga_ref[...] * mixed, g1_ref[...], b1_ref[...])
    h = (x1 * (1.0 + scf_ref[...]) + shf_ref[...]).astype(BF16)
    acc = jnp.zeros(x1.shape, F32)
    for c in range(D_FF // D_MODEL):
        cols = slice(c * D_MODEL, (c + 1) * D_MODEL)
        u = jnp.maximum(jnp.dot(h, wu_ref[:, cols], preferred_element_type=F32), 0.0)
        acc = acc + jnp.dot((u * u).astype(BF16), wd_ref[cols, :], preferred_element_type=F32)
    y_ref[...] = _layer_norm(ALPHA * x1 + gf_ref[...] * acc, g2_ref[...], b2_ref[...])


def _out_ffn(a_ret, a_moba, x2d, mod_specs, mod_args, weights, grid, row_map, tm):
    t = x2d.shape[0]
    wide = lambda w: pl.BlockSpec((tm, w), row_map)
    const = lambda a: pl.BlockSpec(a.shape, lambda *_: (0,) * a.ndim, pipeline_mode=pl.Buffered(1))
    return pl.pallas_call(
        _out_ffn_body,
        grid=grid,
        in_specs=[wide(RET_WIDTH), wide(MOBA_WIDTH), wide(D_MODEL)] + mod_specs + [const(a) for a in weights],
        out_specs=wide(D_MODEL),
        out_shape=jax.ShapeDtypeStruct((t, D_MODEL), F32),
        compiler_params=pltpu.CompilerParams(dimension_semantics=("arbitrary",) * len(grid),
                                             vmem_limit_bytes=56 * 1024 * 1024),
        name="out_ffn",
    )(a_ret, a_moba, x2d, *mod_args, *weights)


def _out_ffn_moba_body(pt_ref, ar_ref, am_ref, x_ref, ga_ref, shf_ref, scf_ref, gf_ref, wo_ref, g1_ref, b1_ref,
                       wu_ref, wd_ref, g2_ref, b2_ref, q_ref, kn_ref, vn_ref, kc_ref, vc_ref,
                       y_ref, o_ref, ring, sem, s_scr):
    n_seqs = pl.num_programs(0) * pl.num_programs(1)
    seq = pl.program_id(0) * pl.num_programs(1) + pl.program_id(1)
    sm = _SampleMoba(pt_ref, seq, n_seqs, q_ref, kn_ref, vn_ref, kc_ref, vc_ref, o_ref, ring, sem, s_scr)

    def up(c, h):
        u = jnp.maximum(jnp.dot(h, wu_ref[:, c * D_MODEL:(c + 1) * D_MODEL], preferred_element_type=F32), 0.0)
        return (u * u).astype(BF16)

    def down(c, u):
        return jnp.dot(u, wd_ref[c * D_MODEL:(c + 1) * D_MODEL, :], preferred_element_type=F32)

    half = CHUNK_PAGES // 2
    sm.prologue()

    sm.wait_k(0)
    sm.k_pages(0, 0, half)
    mixed = jnp.dot(ar_ref[...].astype(BF16), wo_ref[:RET_WIDTH, :], preferred_element_type=F32)
    sm.k_pages(0, half, CHUNK_PAGES)
    mixed = mixed + jnp.dot(am_ref[...].astype(BF16), wo_ref[RET_WIDTH:, :], preferred_element_type=F32)
    x1 = _layer_norm(ALPHA * x_ref[...] + ga_ref[...] * mixed, g1_ref[...], b1_ref[...])
    h = (x1 * (1.0 + scf_ref[...]) + shf_ref[...]).astype(BF16)
    sm.refill_after_k(0)

    sm.wait_k(1)
    sm.k_pages(1, 0, half)
    u = up(0, h)
    sm.k_pages(1, half, CHUNK_PAGES)
    sm.refill_after_k(1)

    sm.wait_k(2)
    sm.k_pages(2, 0, half)
    acc = down(0, u)
    sm.k_pages(2, half, CHUNK_PAGES)
    sm.refill_after_k(2)

    sm.wait_k(3)
    sm.k_pages(3, 0, half)
    u = up(1, h)
    sm.k_pages(3, half, CHUNK_PAGES)
    sm.refill_after_k(3)

    l, acc_s = sm.select()
    acc = acc + down(1, u)

    sm.wait_v(0)
    acc_s = sm.v_pages(0, 0, half, acc_s)
    u = up(2, h)
    acc_s = sm.v_pages(0, half, CHUNK_PAGES, acc_s)
    sm.refill_after_v(0)

    sm.wait_v(1)
    acc_s = sm.v_pages(1, 0, half, acc_s)
    acc = acc + down(2, u)
    acc_s = sm.v_pages(1, half, CHUNK_PAGES, acc_s)
    sm.refill_after_v(1)

    sm.wait_v(2)
    acc_s = sm.v_pages(2, 0, half, acc_s)
    u = up(3, h)
    acc_s = sm.v_pages(2, half, CHUNK_PAGES, acc_s)
    sm.refill_after_v(2)

    sm.wait_v(3)
    acc_s = sm.v_pages(3, 0, half, acc_s)
    acc = acc + down(3, u)
    acc_s = sm.v_pages(3, half, CHUNK_PAGES, acc_s)
    sm.refill_after_v(3)
    y_ref[...] = _layer_norm(ALPHA * x1 + gf_ref[...] * acc, g2_ref[...], b2_ref[...])
    sm.finish(l, acc_s)


def _out_ffn_moba(a_ret, a_moba, x2d, mod_rows, weights, page_table, mq_s, kn2d, vn2d, cache_k2d, cache_v2d, tm):
    t = x2d.shape[0]
    nt = SEQ // tm
    assert BATCH * nt == DEC_BATCH
    n_rows = MOBA_HEADS * DEC_SEQ
    tile = lambda b, i, pt: (b * nt + i, 0)
    wide = lambda w: pl.BlockSpec((tm, w), tile)
    mod = lambda chunk: pl.BlockSpec((None, 1, D_MODEL), lambda b, i, pt: (b * 6 + chunk, 0, 0))
    const = lambda a: pl.BlockSpec(a.shape, lambda b, i, pt: (0,) * a.ndim, pipeline_mode=pl.Buffered(1))
    seq_rows = lambda w, n: pl.BlockSpec((n, w), tile)
    hbm = pl.BlockSpec(memory_space=pl.ANY)
    grid_spec = pltpu.PrefetchScalarGridSpec(
        num_scalar_prefetch=1,
        grid=(BATCH, nt),
        in_specs=[wide(RET_WIDTH), wide(MOBA_WIDTH), wide(D_MODEL)] + [mod(c) for c in (2, 3, 4, 5)]
        + [const(a) for a in weights]
        + [seq_rows(MOBA_WIDTH, DEC_SEQ), seq_rows(LANES, n_rows), seq_rows(LANES, n_rows), hbm, hbm],
        out_specs=[wide(D_MODEL), seq_rows(MOBA_WIDTH, DEC_SEQ)],
        scratch_shapes=[pltpu.VMEM((RING_PAGES, PAGE_ROWS, LANES), F32),
                        pltpu.SemaphoreType.DMA((RING_PAGES,)),
                        pltpu.VMEM((PAGES_PER_SEQ, n_rows, PAGE_ROWS), F32)],
    )
    return pl.pallas_call(
        _out_ffn_moba_body,
        grid_spec=grid_spec,
        out_shape=[jax.ShapeDtypeStruct((t, D_MODEL), F32),
                   jax.ShapeDtypeStruct((N_SAMPLE_TOK, MOBA_WIDTH), F32)],
        compiler_params=pltpu.CompilerParams(dimension_semantics=("arbitrary", "arbitrary"),
                                             vmem_limit_bytes=58 * 1024 * 1024),
        name="out_ffn_moba",
    )(page_table.reshape(-1), a_ret, a_moba, x2d, *([mod_rows] * 4), *weights,
      mq_s, kn2d, vn2d, cache_k2d, cache_v2d)


def kernel(x_prompt, x_sample, cache_k, cache_v, state_ret, page_table, c_prompt, c_sample,
           w_ada, b_ada, w_in, w_o, ln1_g, ln1_b, w_up, w_down, ln2_g, ln2_b):
    n_prompt_tok = BATCH * SEQ
    past_len = page_table.shape[1] * PAGE_SIZE

    mod = _adaln(jnp.concatenate([c_prompt, c_sample], axis=0), w_ada[0], b_ada)
    mod_rows = mod.reshape((BATCH + DEC_BATCH) * 6, 1, D_MODEL)
    mod_s = jnp.repeat(mod[BATCH:], DEC_SEQ, axis=0)

    def prompt_mod(chunk):
        return pl.BlockSpec((None, 1, D_MODEL), lambda b, i: (b * 6 + chunk, 0, 0))

    def sample_mod(chunk):
        return pl.BlockSpec((N_SAMPLE_TOK, D_MODEL), lambda i: (0, chunk))

    w_in_b = w_in[0].astype(BF16)
    weights = (w_o[0].astype(BF16), ln1_g, ln1_b, w_up[0].astype(BF16), w_down[0].astype(BF16), ln2_g, ln2_b)

    tm = 512
    nt = SEQ // tm
    p_row = lambda b, i: (b * nt + i, 0)
    p_tab = pl.BlockSpec((tm, LANES), lambda b, i: (i, 0))
    s_row = lambda i: (0, 0)
    s_tab = pl.BlockSpec((N_SAMPLE_TOK, LANES), s_row)

    pos_p = np.arange(SEQ, dtype=np.int32)
    pos_s = np.tile(past_len + np.arange(DEC_SEQ, dtype=np.int32), DEC_BATCH)
    tabs_p = _rope_tables(pos_p, MOBA_HEAD_DIM) + _rope_tables(pos_p, RET_DK)
    tabs_s = _rope_tables(pos_s, MOBA_HEAD_DIM) + _rope_tables(pos_s, RET_DK)

    xp = x_prompt.reshape(n_prompt_tok, D_MODEL)
    rq, rk, rv, rg, mq, k_p, v_p = _inproj(
        xp, [prompt_mod(1), prompt_mod(0)], [mod_rows, mod_rows], [p_tab] * 4, tabs_p, w_in_b,
        (BATCH, nt), p_row, tm, BF16)
    a_ret, state_p = _ret_prompt(rq, rk, rv, rg)
    a_moba = _moba_prompt(mq, k_p, v_p)

    xs = x_sample.reshape(N_SAMPLE_TOK, D_MODEL)
    rq_s, rk_s, rv_s, rg_s, mq_s, k_s, v_s = _inproj(
        xs, [sample_mod(1), sample_mod(0)], [mod_s, mod_s], [s_tab] * 4, tabs_s, w_in_b,
        (1,), s_row, N_SAMPLE_TOK, F32)
    a_ret_s, state_s = _ret_sample(rq_s, rk_s, rv_s, rg_s, state_ret[0])

    cache_rows = cache_k.shape[1] * PAGE_ROWS
    y_p, a_moba_s = _out_ffn_moba(a_ret, a_moba, xp, mod_rows, weights, page_table, mq_s, k_s, v_s,
                                  cache_k.reshape(cache_rows, LANES), cache_v.reshape(cache_rows, LANES), tm)
    y_s = _out_ffn(a_ret_s, a_moba_s, xs, [sample_mod(c) for c in (2, 3, 4, 5)], [mod_s] * 4, weights,
                   (1,), s_row, N_SAMPLE_TOK)

    kv_p_shape = (DEPTH, BATCH, SEQ, MOBA_HEADS, MOBA_HEAD_DIM)
    kv_s_shape = (DEPTH, DEC_BATCH, DEC_SEQ, MOBA_HEADS, MOBA_HEAD_DIM)
    return (y_p.reshape(BATCH, SEQ, D_MODEL),
            y_s.reshape(DEC_BATCH, DEC_SEQ, D_MODEL),
            k_p.reshape(kv_p_shape), v_p.reshape(kv_p_shape), state_p[None],
            k_s.reshape(kv_s_shape), v_s.reshape(kv_s_shape), state_s[None])
```

```python
import functools

import numpy as np
import jax
import jax.numpy as jnp
from jax import lax
from jax.experimental import pallas as pl
from jax.experimental.pallas import tpu as pltpu

F32 = jnp.float32
BF16 = jnp.bfloat16

D_MODEL = 1024
BATCH = 8
SEQ = 2048
DEC_BATCH = 32
DEC_SEQ = 8
PAGE_SIZE = 128
RET_HEADS = 4
RET_DK = 64
RET_DV = 128
RET_CHUNK = 128
MOBA_HEADS = 4
MOBA_HEAD_DIM = 128
MOBA_BLOCK = 256
MOBA_TOPK = 3
D_FF = 4 * D_MODEL
ROPE_THETA = 10000.0
LN_EPS = 1e-5
GN_EPS = 1e-6
DEPTH = 1
ALPHA = (2 * DEPTH) ** 0.25
RET_QK = RET_HEADS * RET_DK
RET_WIDTH = RET_HEADS * RET_DV
MOBA_WIDTH = MOBA_HEADS * MOBA_HEAD_DIM
IN_WIDTH = 2 * RET_QK + 2 * RET_WIDTH + 3 * MOBA_WIDTH
OFF_RQ, OFF_RK, OFF_RV, OFF_RG = 0, 256, 512, 1024
OFF_MQ, OFF_MK, OFF_MV = 1536, 2048, 2560
LANES = 128
N_SAMPLE_TOK = DEC_BATCH * DEC_SEQ
PAGES_PER_SEQ = 64
CHUNK_PAGES = 16
RING_PAGES = 2 * CHUNK_PAGES
PAGE_ROWS = PAGE_SIZE * MOBA_HEADS
NEG_INF = float("-inf")
LOG2_E = 1.4426950408889634

_NT = (((1,), (1,)), ((), ()))
_TN = (((0,), (0,)), ((), ()))


def _log_decay():
    return np.log1p(-np.exp2(-5.0 - np.arange(RET_HEADS, dtype=np.float64)))


def _adaln_body(c_ref, w_ref, b_ref, o_ref):
    c = c_ref[...]
    a = (c * jax.nn.sigmoid(c)).astype(BF16)
    o_ref[...] = jnp.dot(a, w_ref[...].astype(BF16), preferred_element_type=F32) + b_ref[...]


def _adaln(c_all, w_ada, b_ada):
    n = c_all.shape[0]
    tn = 1024
    return pl.pallas_call(
        _adaln_body,
        grid=(6 * D_MODEL // tn,),
        in_specs=[pl.BlockSpec((n, D_MODEL), lambda j: (0, 0)),
                  pl.BlockSpec((D_MODEL, tn), lambda j: (0, j)),
                  pl.BlockSpec((1, tn), lambda j: (0, j))],
        out_specs=pl.BlockSpec((n, tn), lambda j: (0, j)),
        out_shape=jax.ShapeDtypeStruct((n, 6 * D_MODEL), F32),
        name="adaln",
    )(c_all, w_ada, b_ada)


def _rope_tables(pos, head_dim):
    half = head_dim // 2
    inv_freq = np.power(ROPE_THETA, -np.arange(half, dtype=np.float64) / half)
    ang = pos.astype(np.float64)[:, None] * inv_freq[None, :]
    cos, sin = np.cos(ang), np.sin(ang)
    reps = LANES // head_dim
    cos_t = np.tile(np.concatenate([cos, cos], axis=-1), (1, reps))
    sin_t = np.tile(np.concatenate([-sin, sin], axis=-1), (1, reps))
    return jnp.asarray(cos_t, dtype=F32), jnp.asarray(sin_t, dtype=F32)


def _inproj_body(n_casts, x_ref, sc_ref, sh_ref, w_ref, cm_ref, sm_ref, cr_ref, sr_ref, *refs):
    cast_in, refs = refs[:n_casts], refs[n_casts:]
    rq_ref, rk_ref, rv_ref, rg_ref, mq_ref, ko_ref, vo_ref = refs[:7]
    for src, dst in zip(cast_in, refs[7:]):
        dst[...] = src[...].astype(dst.dtype)
    tm = x_ref.shape[0]
    h = (x_ref[...] * (1.0 + sc_ref[...]) + sh_ref[...]).astype(BF16)

    def proj(lo, width):
        return jnp.dot(h, w_ref[:, lo:lo + width], preferred_element_type=F32)

    lane = lax.broadcasted_iota(jnp.int32, (tm, LANES), 1)
    low_half = (lane & (RET_DK - 1)) < (RET_DK // 2)
    cr, sr = cr_ref[...], sr_ref[...]
    cm, sm = cm_ref[...], sm_ref[...]

    def rope_ret(z):
        rot = jnp.where(low_half, pltpu.roll(z, LANES - RET_DK // 2, 1), pltpu.roll(z, RET_DK // 2, 1))
        return z * cr + rot * sr

    def rope_moba(z):
        return z * cm + pltpu.roll(z, MOBA_HEAD_DIM // 2, 1) * sm

    zq = proj(OFF_RQ, RET_QK)
    zk = proj(OFF_RK, RET_QK)
    for s in range(RET_QK // LANES):
        sl = slice(s * LANES, (s + 1) * LANES)
        rq_ref[:, sl] = rope_ret(zq[:, sl])
        rk_ref[:, sl] = rope_ret(zk[:, sl]) * (RET_DK ** -0.5)
    rv_ref[...] = proj(OFF_RV, RET_WIDTH).astype(rv_ref.dtype)
    rg_ref[...] = proj(OFF_RG, RET_WIDTH)
    zq = proj(OFF_MQ, MOBA_WIDTH)
    zk = proj(OFF_MK, MOBA_WIDTH)
    zv = proj(OFF_MV, MOBA_WIDTH)
    for hd in range(MOBA_HEADS):
        sl = slice(hd * LANES, (hd + 1) * LANES)
        mq_ref[:, sl] = rope_moba(zq[:, sl]).astype(mq_ref.dtype)
        ko_ref[pl.ds(hd, tm, stride=MOBA_HEADS), :] = rope_moba(zk[:, sl])
        vo_ref[pl.ds(hd, tm, stride=MOBA_HEADS), :] = zv[:, sl]


def _inproj(x2d, mod_specs, mod_args, tab_specs, tabs, w_in, grid, row_map, tm, act_dtype, casts=()):
    t = x2d.shape[0]
    n_steps = int(np.prod(grid))
    wide = lambda w: pl.BlockSpec((tm, w), row_map)
    cast_specs = [pl.BlockSpec((a.shape[0] // n_steps, a.shape[1]), row_map) for a in casts]
    return pl.pallas_call(
        functools.partial(_inproj_body, len(casts)),
        grid=grid,
        in_specs=[wide(D_MODEL)] + mod_specs + [pl.BlockSpec((D_MODEL, IN_WIDTH), lambda *_: (0, 0))] + tab_specs
        + cast_specs,
        out_specs=[wide(RET_QK), wide(RET_QK), wide(RET_WIDTH), wide(RET_WIDTH), wide(MOBA_WIDTH),
                   pl.BlockSpec((tm * MOBA_HEADS, LANES), row_map),
                   pl.BlockSpec((tm * MOBA_HEADS, LANES), row_map)] + cast_specs,
        out_shape=[jax.ShapeDtypeStruct((t, RET_QK), F32), jax.ShapeDtypeStruct((t, RET_QK), F32),
                   jax.ShapeDtypeStruct((t, RET_WIDTH), act_dtype), jax.ShapeDtypeStruct((t, RET_WIDTH), F32),
                   jax.ShapeDtypeStruct((t, MOBA_WIDTH), act_dtype),
                   jax.ShapeDtypeStruct((t * MOBA_HEADS, LANES), F32),
                   jax.ShapeDtypeStruct((t * MOBA_HEADS, LANES), F32)]
        + [jax.ShapeDtypeStruct(a.shape, BF16) for a in casts],
        compiler_params=pltpu.CompilerParams(dimension_semantics=("arbitrary",) * len(grid),
                                             vmem_limit_bytes=48 * 1024 * 1024),
        name="inproj",
    )(x2d, *mod_args, w_in, *tabs, *casts)


def _group_norm_gate(o, g):
    mu = jnp.mean(o, axis=-1, keepdims=True)
    d = o - mu
    var = jnp.mean(d * d, axis=-1, keepdims=True)
    return d * lax.rsqrt(var + GN_EPS) * (g * jax.nn.sigmoid(g))


def _ret_prompt_tables():
    lg = _log_decay()
    i = np.arange(RET_CHUNK, dtype=np.float64)
    diff = i[:, None] - i[None, :]
    dmat = np.where(diff >= 0, np.exp(np.maximum(diff, 0.0)[None] * lg[:, None, None]), 0.0)
    lane_head = np.arange(RET_QK) // RET_DK
    qd = np.exp((i[:, None] + 1.0) * lg[lane_head][None, :])
    kd = np.exp((RET_CHUNK - 1.0 - i)[:, None] * lg[lane_head][None, :])
    row_head = np.arange(RET_QK) // RET_DK
    col_head = np.arange(RET_WIDTH) // RET_DV
    same = row_head[:, None] == col_head[None, :]
    cdec = np.where(same, np.exp(RET_CHUNK * lg[row_head])[:, None], 0.0)
    return [jnp.asarray(a, dtype=F32) for a in (dmat, qd, kd, cdec, same.astype(np.float64))]


def _ret_prompt_body(q_ref, k_ref, v_ref, g_ref, dmat_ref, qd_ref, kd_ref, cdec_ref, bdm_ref,
                     o_ref, st_ref, state_scr):
    lane_head = lax.broadcasted_iota(jnp.int32, (RET_CHUNK, RET_QK), 1) >> 6
    chunk_rows = lambda c: slice(c * RET_CHUNK, (c + 1) * RET_CHUNK)

    def first_matmuls(c):
        rows = chunk_rows(c)
        q = q_ref[rows, :]
        k = k_ref[rows, :]
        v = v_ref[rows, :]
        kb = k.astype(BF16)
        state = state_scr[...]
        scores = [lax.dot_general(jnp.where(lane_head == hd, q, 0.0).astype(BF16), kb, _NT,
                                  preferred_element_type=F32) for hd in range(RET_HEADS)]
        cross = jnp.dot((q * qd_ref[...]).astype(BF16), state.astype(BF16), preferred_element_type=F32)
        kv = lax.dot_general((k * kd_ref[...]).astype(BF16), v, _TN, preferred_element_type=F32)
        state_scr[...] = cdec_ref[...] * state + bdm_ref[...] * kv
        return scores, cross, v

    def second_matmuls(c, scores, cross, v):
        rows = chunk_rows(c)
        g = g_ref[rows, :]
        decayed = [(scores[hd] * dmat_ref[hd]).astype(BF16) for hd in range(RET_HEADS)]
        for hd in range(RET_HEADS):
            sl = slice(hd * RET_DV, (hd + 1) * RET_DV)
            inner = jnp.dot(decayed[hd], v[:, sl], preferred_element_type=F32)
            o_ref[rows, sl] = _group_norm_gate(inner + cross[:, sl], g[:, sl]).astype(o_ref.dtype)

    state_scr[...] = jnp.zeros_like(state_scr)
    n_chunks = SEQ // RET_CHUNK
    ahead = first_matmuls(0)
    for c in range(n_chunks):
        current = ahead
        if c + 1 < n_chunks:
            ahead = first_matmuls(c + 1)
        second_matmuls(c, *current)
    for hd in range(RET_HEADS):
        st_ref[hd] = state_scr[hd * RET_DK:(hd + 1) * RET_DK, hd * RET_DV:(hd + 1) * RET_DV]


def _ret_prompt(rq, rk, rv, rg):
    tabs = _ret_prompt_tables()
    seq = lambda w: pl.BlockSpec((SEQ, w), lambda b: (b, 0))
    const = lambda a: pl.BlockSpec(a.shape, lambda b: (0,) * a.ndim)
    return pl.pallas_call(
        _ret_prompt_body,
        grid=(BATCH,),
        in_specs=[seq(RET_QK), seq(RET_QK), seq(RET_WIDTH), seq(RET_WIDTH)] + [const(a) for a in tabs],
        out_specs=[seq(RET_WIDTH), pl.BlockSpec((None, RET_HEADS, RET_DK, RET_DV), lambda b: (b, 0, 0, 0))],
        out_shape=[jax.ShapeDtypeStruct((BATCH * SEQ, RET_WIDTH), BF16),
                   jax.ShapeDtypeStruct((BATCH, RET_HEADS, RET_DK, RET_DV), F32)],
        scratch_shapes=[pltpu.VMEM((RET_QK, RET_WIDTH), F32)],
        compiler_params=pltpu.CompilerParams(dimension_semantics=("arbitrary",),
                                             vmem_limit_bytes=48 * 1024 * 1024),
        name="ret_prompt",
    )(rq, rk, rv, rg, *tabs)


def _ret_sample_tables():
    lg = _log_decay()
    t = np.arange(N_SAMPLE_TOK) % DEC_SEQ
    seq_id = np.arange(N_SAMPLE_TOK) // DEC_SEQ
    diff = (t[:, None] - t[None, :]).astype(np.float64)
    same_seq = seq_id[:, None] == seq_id[None, :]
    dmat = np.where(same_seq[None] & (diff >= 0)[None],
                    np.exp(np.maximum(diff, 0.0)[None] * lg[:, None, None]), 0.0)
    lane_head = np.arange(RET_QK) // RET_DK
    qd = np.exp((t[:, None] + 1.0) * lg[lane_head][None, :])
    kd = np.exp((DEC_SEQ - 1.0 - t)[:, None] * lg[lane_head][None, :])
    return [jnp.asarray(a, dtype=F32) for a in (dmat, qd, kd)]


def _ret_sample_body(q_ref, k_ref, v_ref, g_ref, st_ref, dmat_ref, qd_ref, kd_ref, o_ref, sto_ref):
    lg = _log_decay()
    q = q_ref[...]
    k = k_ref[...]
    kb = k.astype(BF16)
    qdec = q * qd_ref[...]
    kdec = k * kd_ref[...]
    vb = v_ref[...].astype(BF16)
    g = g_ref[...]
    lane = lax.broadcasted_iota(jnp.int32, (N_SAMPLE_TOK, LANES), 1)
    lane_head = lax.broadcasted_iota(jnp.int32, (N_SAMPLE_TOK, RET_QK), 1) >> 6
    n_state_rows = DEC_BATCH * RET_DK
    own_seq = ((lax.broadcasted_iota(jnp.int32, (N_SAMPLE_TOK, n_state_rows), 0) >> 3)
               == (lax.broadcasted_iota(jnp.int32, (N_SAMPLE_TOK, n_state_rows), 1) >> 6))
    for hd in range(RET_HEADS):
        sl = slice(hd * RET_DV, (hd + 1) * RET_DV)
        qm = jnp.where(lane_head == hd, q, 0.0).astype(BF16)
        s = lax.dot_general(qm, kb, _NT, preferred_element_type=F32) * dmat_ref[hd]
        inner = jnp.dot(s.astype(BF16), vb[:, sl], preferred_element_type=F32)

        def expand(z):
            slab = z[:, (hd // 2) * LANES:(hd // 2 + 1) * LANES]
            other = pltpu.roll(slab, RET_DK, 1)
            in_low = lane < RET_DK
            both = jnp.where(in_low, slab, other) if hd % 2 == 0 else jnp.where(in_low, other, slab)
            tiled = jnp.concatenate([both] * (n_state_rows // LANES), axis=1)
            return jnp.where(own_seq, tiled, 0.0).astype(BF16)

        st = st_ref[:, hd].reshape(n_state_rows, RET_DV)
        cross = jnp.dot(expand(qdec), st.astype(BF16), preferred_element_type=F32)
        o_ref[:, sl] = _group_norm_gate(inner + cross, g[:, sl])
        kv = lax.dot_general(expand(kdec), vb[:, sl], _TN, preferred_element_type=F32)
        new = float(np.exp(DEC_SEQ * lg[hd])) * st + kv
        sto_ref[:, hd] = new.reshape(DEC_BATCH, RET_DK, RET_DV)


def _ret_sample(rq, rk, rv, rg, state):
    tabs = _ret_sample_tables()
    full = lambda a: pl.BlockSpec(a.shape, lambda i: (0,) * a.ndim)
    args = (rq, rk, rv, rg, state, *tabs)
    return pl.pallas_call(
        _ret_sample_body,
        grid=(1,),
        in_specs=[full(a) for a in args],
        out_specs=[pl.BlockSpec((N_SAMPLE_TOK, RET_WIDTH), lambda i: (0, 0)),
                   pl.BlockSpec(state.shape, lambda i: (0, 0, 0, 0))],
        out_shape=[jax.ShapeDtypeStruct((N_SAMPLE_TOK, RET_WIDTH), F32),
                   jax.ShapeDtypeStruct(state.shape, F32)],
        compiler_params=pltpu.CompilerParams(dimension_semantics=("arbitrary",),
                                             vmem_limit_bytes=56 * 1024 * 1024),
        name="ret_sample",
    )(*args)


def _moba_prompt_body(q_ref, k_ref, v_ref, o_ref):
    hd = pl.program_id(1)
    n_blocks = SEQ // MOBA_BLOCK
    exp2_scale = MOBA_HEAD_DIM ** -0.5 * LOG2_E
    k32 = k_ref[pl.ds(hd, SEQ, stride=MOBA_HEADS), :]
    kb = k32.astype(BF16)
    vt = v_ref[pl.ds(hd, SEQ, stride=MOBA_HEADS), :].T.astype(BF16)
    kmean = jnp.sum(k32.reshape(n_blocks, MOBA_BLOCK, MOBA_HEAD_DIM), axis=1) * (1.0 / MOBA_BLOCK)
    kmb = kmean.astype(BF16)
    key_id = lax.broadcasted_iota(jnp.int32, (MOBA_BLOCK, MOBA_BLOCK), 0)
    qry_id = lax.broadcasted_iota(jnp.int32, (MOBA_BLOCK, MOBA_BLOCK), 1)
    causal = key_id <= qry_id

    blk = lambda n: slice(n * MOBA_BLOCK, (n + 1) * MOBA_BLOCK)

    def score_matmuls(i):
        qi = q_ref[blk(i), :]
        st = [lax.dot_general(kb[blk(n)], qi, _NT, preferred_element_type=F32) for n in range(i + 1)]
        gt = lax.dot_general(kmb, qi, _NT, preferred_element_type=F32) if i > MOBA_TOPK else None
        return st, gt

    ahead = score_matmuls(0)
    for i in range(n_blocks):
        st, gt = ahead
        if i + 1 < n_blocks:
            ahead = score_matmuls(i + 1)
        st[i] = jnp.where(causal, st[i], NEG_INF)
        if i > MOBA_TOPK:
            for n in range(i):
                beats = jnp.zeros((1, MOBA_BLOCK), F32)
                for mm in range(i):
                    if mm == n:
                        continue
                    win = (gt[mm:mm + 1] >= gt[n:n + 1]) if mm < n else (gt[mm:mm + 1] > gt[n:n + 1])
                    beats = beats + win.astype(F32)
                st[n] = st[n] + jnp.where(beats < MOBA_TOPK, 0.0, NEG_INF)
        m = functools.reduce(jnp.maximum, [jnp.max(s, axis=0, keepdims=True) for s in st])
        l = jnp.zeros((1, MOBA_BLOCK), F32)
        acc = jnp.zeros((MOBA_HEAD_DIM, MOBA_BLOCK), F32)
        for n in range(i + 1):
            e = jnp.exp2((st[n] - m) * exp2_scale)
            l = l + jnp.sum(e, axis=0, keepdims=True)
            acc = acc + jnp.dot(vt[:, blk(n)], e.astype(BF16), preferred_element_type=F32)
        o_ref[blk(i), :] = (acc / l).T.astype(o_ref.dtype)


def _moba_prompt(mq, k2d, v2d):
    kv_spec = pl.BlockSpec((SEQ * MOBA_HEADS, LANES), lambda b, h: (b, 0))
    return pl.pallas_call(
        _moba_prompt_body,
        grid=(BATCH, MOBA_HEADS),
        in_specs=[pl.BlockSpec((SEQ, MOBA_HEAD_DIM), lambda b, h: (b, h)), kv_spec, kv_spec],
        out_specs=pl.BlockSpec((SEQ, MOBA_HEAD_DIM), lambda b, h: (b, h)),
        out_shape=jax.ShapeDtypeStruct((BATCH * SEQ, MOBA_WIDTH), BF16),
        compiler_params=pltpu.CompilerParams(dimension_semantics=("arbitrary", "arbitrary"),
                                             vmem_limit_bytes=56 * 1024 * 1024),
        name="moba_prompt",
    )(mq, k2d, v2d)


class _SampleMoba:
    n_chunks = PAGES_PER_SEQ // CHUNK_PAGES
    n_rows = MOBA_HEADS * DEC_SEQ
    n_blocks = PAGES_PER_SEQ * PAGE_SIZE // MOBA_BLOCK
    pages_per_block = MOBA_BLOCK // PAGE_SIZE

    def __init__(self, pt_ref, seq, n_seqs, q_ref, kn_ref, vn_ref, kc_ref, vc_ref, o_ref, ring, sem, s_scr):
        self.pt_ref, self.seq, self.n_seqs = pt_ref, seq, n_seqs
        self.q_ref, self.kn_ref, self.vn_ref = q_ref, kn_ref, vn_ref
        self.kc_ref, self.vc_ref, self.o_ref = kc_ref, vc_ref, o_ref
        self.ring, self.sem, self.s_scr = ring, sem, s_scr

    def _page_copy(self, cache_ref, row0, slot):
        return pltpu.make_async_copy(cache_ref.at[pl.ds(row0, PAGE_ROWS)], self.ring.at[slot], self.sem.at[slot])

    def _start_chunk(self, cache_ref, seq, chunk):
        for r in range(CHUNK_PAGES):
            page = self.pt_ref[seq * PAGES_PER_SEQ + chunk * CHUNK_PAGES + r]
            self._page_copy(cache_ref, pl.multiple_of(page * PAGE_ROWS, PAGE_ROWS),
                            (chunk % 2) * CHUNK_PAGES + r).start()

    def _wait_chunk(self, cache_ref, chunk):
        for r in range(CHUNK_PAGES):
            self._page_copy(cache_ref, 0, (chunk % 2) * CHUNK_PAGES + r).wait()

    def prologue(self):
        @pl.when(self.seq == 0)
        def _():
            self._start_chunk(self.kc_ref, self.seq, 0)
            self._start_chunk(self.kc_ref, self.seq, 1)

        self.q = jnp.concatenate(
            [self.q_ref[:, hd * MOBA_HEAD_DIM:(hd + 1) * MOBA_HEAD_DIM] for hd in range(MOBA_HEADS)], axis=0
        ).astype(BF16)
        row_head = lax.broadcasted_iota(jnp.int32, (self.n_rows, LANES), 0) >> 3
        col_head = lax.broadcasted_iota(jnp.int32, (self.n_rows, LANES), 1) & (MOBA_HEADS - 1)
        self.same_head = row_head == col_head
        self.head_bias = jnp.where(self.same_head, 0.0, NEG_INF)
        self.block_sum, self.block_max = [], []

    @staticmethod
    def _slabs(x):
        return [x[:, j * LANES:(j + 1) * LANES] for j in range(PAGE_ROWS // LANES)]

    def wait_k(self, c):
        self._wait_chunk(self.kc_ref, c)

    def k_pages(self, c, first, last):
        assert first % self.pages_per_block == 0 and last % self.pages_per_block == 0
        for r0 in range(first, last, self.pages_per_block):
            tot = jnp.zeros((self.n_rows, LANES), F32)
            top = jnp.full((self.n_rows, LANES), NEG_INF, F32)
            for r in range(r0, r0 + self.pages_per_block):
                page = self.ring[(c % 2) * CHUNK_PAGES + r].astype(BF16)
                s = lax.dot_general(self.q, page, _NT, preferred_element_type=F32)
                self.s_scr[c * CHUNK_PAGES + r] = s
                for slab in self._slabs(s):
                    tot = tot + jnp.where(self.same_head, slab, 0.0)
                    top = jnp.maximum(top, slab + self.head_bias)
            self.block_sum.append(jnp.sum(tot, axis=-1, keepdims=True))
            self.block_max.append(jnp.max(top, axis=-1, keepdims=True))

    def refill_after_k(self, c):
        if c + 2 < self.n_chunks:
            self._start_chunk(self.kc_ref, self.seq, c + 2)
        else:
            self._start_chunk(self.vc_ref, self.seq, c + 2 - self.n_chunks)

    def select(self):
        n_rows, gs = self.n_rows, self.block_sum
        self.exp2_scale = MOBA_HEAD_DIM ** -0.5 * LOG2_E
        lane = lax.broadcasted_iota(jnp.int32, (n_rows, LANES), 1)
        g_all = jnp.full((n_rows, LANES), NEG_INF, F32)
        for n in range(self.n_blocks):
            g_all = jnp.where(lane == n, gs[n], g_all)
        self.keep_bias = []
        for n in range(self.n_blocks):
            wins = (g_all > gs[n]) | ((g_all == gs[n]) & (lane < n))
            beats = jnp.sum(wins.astype(F32), axis=-1, keepdims=True)
            self.keep_bias.append(jnp.where(beats < MOBA_TOPK, 0.0, NEG_INF))
        s_own = lax.dot_general(self.q, self.kn_ref[...].astype(BF16), _NT, preferred_element_type=F32)
        r_id = lax.broadcasted_iota(jnp.int32, (n_rows, n_rows), 0)
        c_id = lax.broadcasted_iota(jnp.int32, (n_rows, n_rows), 1)
        own_ok = ((c_id & (MOBA_HEADS - 1)) == (r_id >> 3)) & ((c_id >> 2) <= (r_id & (DEC_SEQ - 1)))
        s_own = jnp.where(own_ok, s_own, NEG_INF)
        m = jnp.max(s_own, axis=-1, keepdims=True)
        for n in range(self.n_blocks):
            m = jnp.maximum(m, self.block_max[n] + self.keep_bias[n])
        self.m = m
        self.lsum = jnp.zeros((n_rows, LANES), F32)
        e_own = jnp.exp2((s_own - m) * self.exp2_scale)
        return (jnp.sum(e_own, axis=-1, keepdims=True),
                jnp.dot(e_own.astype(BF16), self.vn_ref[...].astype(BF16), preferred_element_type=F32))

    def wait_v(self, c):
        self._wait_chunk(self.vc_ref, c)

    def v_pages(self, c, first, last, acc):
        assert first % self.pages_per_block == 0 and last % self.pages_per_block == 0
        for r0 in range(first, last, self.pages_per_block):
            shift = self.head_bias + (self.keep_bias[(c * CHUNK_PAGES + r0) // self.pages_per_block] - self.m)
            for r in range(r0, r0 + self.pages_per_block):
                e = [jnp.exp2((slab + shift) * self.exp2_scale) for slab in self._slabs(self.s_scr[c * CHUNK_PAGES + r])]
                self.lsum = self.lsum + functools.reduce(jnp.add, e)
                page = self.ring[(c % 2) * CHUNK_PAGES + r].astype(BF16)
                acc = acc + jnp.dot(jnp.concatenate(e, axis=1).astype(BF16), page, preferred_element_type=F32)
        return acc

    def refill_after_v(self, c):
        if c + 2 < self.n_chunks:
            self._start_chunk(self.vc_ref, self.seq, c + 2)
        else:
            @pl.when(self.seq + 1 < self.n_seqs)
            def _():
                self._start_chunk(self.kc_ref, self.seq + 1, c + 2 - self.n_chunks)

    def finish(self, l_own, acc):
        out = acc / (l_own + jnp.sum(self.lsum, axis=-1, keepdims=True))
        for hd in range(MOBA_HEADS):
            self.o_ref[:, hd * MOBA_HEAD_DIM:(hd + 1) * MOBA_HEAD_DIM] = out[hd * DEC_SEQ:(hd + 1) * DEC_SEQ]


def _layer_norm(x, g, b):
    mu = jnp.mean(x, axis=-1, keepdims=True)
    d = x - mu
    var = jnp.mean(d * d, axis=-1, keepdims=True)
    return d * lax.rsqrt(var + LN_EPS) * g + b


def _out_ffn_body(ar_ref, am_ref, x_ref, ga_ref, shf_ref, scf_ref, gf_ref, wo_ref, g1_ref, b1_ref,
                  wu_ref, wd_ref, g2_ref, b2_ref, y_ref, x1_scr, h_scr, acc_scr):
    c = pl.program_id(0)

    @pl.when(c == 0)
    def _():
        mixed = (jnp.dot(ar_ref[...].astype(BF16), wo_ref[:RET_WIDTH, :], preferred_element_type=F32)
                 + jnp.dot(am_ref[...].astype(BF16), wo_ref[RET_WIDTH:, :], preferred_element_type=F32))
        x1 = _layer_norm(ALPHA * x_ref[...] + ga_ref[...] * mixed, g1_ref[...], b1_ref[...])
        x1_scr[...] = x1
        h_scr[...] = (x1 * (1.0 + scf_ref[...]) + shf_ref[...]).astype(BF16)
        acc_scr[...] = jnp.zeros_like(acc_scr)

    u = jnp.maximum(jnp.dot(h_scr[...], wu_ref[...], preferred_element_type=F32), 0.0)
    acc_scr[...] += jnp.dot((u * u).astype(BF16), wd_ref[...], preferred_element_type=F32)

    @pl.when(c == pl.num_programs(0) - 1)
    def _():
        y_ref[...] = _layer_norm(ALPHA * x1_scr[...] + gf_ref[...] * acc_scr[...], g2_ref[...], b2_ref[...])


def _out_ffn(a_ret, a_moba, x2d, mod_specs, mod_args, weights):
    w_o, ln1_g, ln1_b, w_up, w_down, ln2_g, ln2_b = weights
    t = x2d.shape[0]
    whole = lambda a: pl.BlockSpec(a.shape, lambda c: (0,) * a.ndim)
    return pl.pallas_call(
        _out_ffn_body,
        grid=(D_FF // D_MODEL,),
        in_specs=[whole(a_ret), whole(a_moba), whole(x2d)] + mod_specs
        + [whole(w_o), whole(ln1_g), whole(ln1_b),
           pl.BlockSpec((D_MODEL, D_MODEL), lambda c: (0, c)), pl.BlockSpec((D_MODEL, D_MODEL), lambda c: (c, 0)),
           whole(ln2_g), whole(ln2_b)],
        out_specs=pl.BlockSpec((t, D_MODEL), lambda c: (0, 0)),
        out_shape=jax.ShapeDtypeStruct((t, D_MODEL), F32),
        scratch_shapes=[pltpu.VMEM((t, D_MODEL), F32), pltpu.VMEM((t, D_MODEL), BF16), pltpu.VMEM((t, D_MODEL), F32)],
        compiler_params=pltpu.CompilerParams(dimension_semantics=("arbitrary",),
                                             vmem_limit_bytes=48 * 1024 * 1024),
        name="out_ffn",
    )(a_ret, a_moba, x2d, *mod_args, *weights)


def _out_ffn_moba_body(pt_ref, ar_ref, am_ref, x_ref, ga_ref, shf_ref, scf_ref, gf_ref, wo_ref, g1_ref, b1_ref,
                       wu_ref, wd_ref, g2_ref, b2_ref, q_ref, kn_ref, vn_ref, kc_ref, vc_ref,
                       y_ref, o_ref, ring, sem, s_scr):
    n_seqs = pl.num_programs(0) * pl.num_programs(1)
    seq = pl.program_id(0) * pl.num_programs(1) + pl.program_id(1)
    sm = _SampleMoba(pt_ref, seq, n_seqs, q_ref, kn_ref, vn_ref, kc_ref, vc_ref, o_ref, ring, sem, s_scr)

    def up(c, h):
        u = jnp.maximum(jnp.dot(h, wu_ref[:, c * D_MODEL:(c + 1) * D_MODEL], preferred_element_type=F32), 0.0)
        return (u * u).astype(BF16)

    def down(c, u):
        return jnp.dot(u, wd_ref[c * D_MODEL:(c + 1) * D_MODEL, :], preferred_element_type=F32)

    half = CHUNK_PAGES // 2
    sm.prologue()

    sm.wait_k(0)
    sm.k_pages(0, 0, half)
    mixed = jnp.dot(ar_ref[...].astype(BF16), wo_ref[:RET_WIDTH, :], preferred_element_type=F32)
    sm.k_pages(0, half, CHUNK_PAGES)
    mixed = mixed + jnp.dot(am_ref[...].astype(BF16), wo_ref[RET_WIDTH:, :], preferred_element_type=F32)
    x1 = _layer_norm(ALPHA * x_ref[...] + ga_ref[...] * mixed, g1_ref[...], b1_ref[...])
    h = (x1 * (1.0 + scf_ref[...]) + shf_ref[...]).astype(BF16)
    sm.refill_after_k(0)

    sm.wait_k(1)
    sm.k_pages(1, 0, half)
    u = up(0, h)
    sm.k_pages(1, half, CHUNK_PAGES)
    sm.refill_after_k(1)

    sm.wait_k(2)
    sm.k_pages(2, 0, half)
    acc = down(0, u)
    sm.k_pages(2, half, CHUNK_PAGES)
    sm.refill_after_k(2)

    sm.wait_k(3)
    sm.k_pages(3, 0, half)
    u = up(1, h)
    sm.k_pages(3, half, CHUNK_PAGES)
    sm.refill_after_k(3)

    l, acc_s = sm.select()
    acc = acc + down(1, u)

    sm.wait_v(0)
    acc_s = sm.v_pages(0, 0, half, acc_s)
    u = up(2, h)
    acc_s = sm.v_pages(0, half, CHUNK_PAGES, acc_s)
    sm.refill_after_v(0)

    sm.wait_v(1)
    acc_s = sm.v_pages(1, 0, half, acc_s)
    acc = acc + down(2, u)
    acc_s = sm.v_pages(1, half, CHUNK_PAGES, acc_s)
    sm.refill_after_v(1)

    sm.wait_v(2)
    acc_s = sm.v_pages(2, 0, half, acc_s)
    u = up(3, h)
    acc_s = sm.v_pages(2, half, CHUNK_PAGES, acc_s)
    sm.refill_after_v(2)

    sm.wait_v(3)
    acc_s = sm.v_pages(3, 0, half, acc_s)
    acc = acc + down(3, u)
    acc_s = sm.v_pages(3, half, CHUNK_PAGES, acc_s)
    sm.refill_after_v(3)
    y_ref[...] = _layer_norm(ALPHA * x1 + gf_ref[...] * acc, g2_ref[...], b2_ref[...])
    sm.finish(l, acc_s)


def _out_ffn_moba(a_ret, a_moba, x2d, mod_rows, weights, page_table, mq_s, kn2d, vn2d, cache_k2d, cache_v2d, tm):
    t = x2d.shape[0]
    nt = SEQ // tm
    assert BATCH * nt == DEC_BATCH
    n_rows = MOBA_HEADS * DEC_SEQ
    tile = lambda b, i, pt: (b * nt + i, 0)
    wide = lambda w: pl.BlockSpec((tm, w), tile)
    mod = lambda chunk: pl.BlockSpec((None, 1, D_MODEL), lambda b, i, pt: (b * 6 + chunk, 0, 0))
    const = lambda a: pl.BlockSpec(a.shape, lambda b, i, pt: (0,) * a.ndim, pipeline_mode=pl.Buffered(1))
    seq_rows = lambda w, n: pl.BlockSpec((n, w), tile)
    hbm = pl.BlockSpec(memory_space=pl.ANY)
    grid_spec = pltpu.PrefetchScalarGridSpec(
        num_scalar_prefetch=1,
        grid=(BATCH, nt),
        in_specs=[wide(RET_WIDTH), wide(MOBA_WIDTH), wide(D_MODEL)] + [mod(c) for c in (2, 3, 4, 5)]
        + [const(a) for a in weights]
        + [seq_rows(MOBA_WIDTH, DEC_SEQ), seq_rows(LANES, n_rows), seq_rows(LANES, n_rows), hbm, hbm],
        out_specs=[wide(D_MODEL), seq_rows(MOBA_WIDTH, DEC_SEQ)],
        scratch_shapes=[pltpu.VMEM((RING_PAGES, PAGE_ROWS, LANES), F32),
                        pltpu.SemaphoreType.DMA((RING_PAGES,)),
                        pltpu.VMEM((PAGES_PER_SEQ, n_rows, PAGE_ROWS), F32)],
    )
    return pl.pallas_call(
        _out_ffn_moba_body,
        grid_spec=grid_spec,
        out_shape=[jax.ShapeDtypeStruct((t, D_MODEL), F32),
                   jax.ShapeDtypeStruct((N_SAMPLE_TOK, MOBA_WIDTH), F32)],
        compiler_params=pltpu.CompilerParams(dimension_semantics=("arbitrary", "arbitrary"),
                                             vmem_limit_bytes=58 * 1024 * 1024),
        name="out_ffn_moba",
    )(page_table.reshape(-1), a_ret, a_moba, x2d, *([mod_rows] * 4), *weights,
      mq_s, kn2d, vn2d, cache_k2d, cache_v2d)


def kernel(x_prompt, x_sample, cache_k, cache_v, state_ret, page_table, c_prompt, c_sample,
           w_ada, b_ada, w_in, w_o, ln1_g, ln1_b, w_up, w_down, ln2_g, ln2_b):
    n_prompt_tok = BATCH * SEQ
    past_len = page_table.shape[1] * PAGE_SIZE

    mod = _adaln(jnp.concatenate([c_prompt, c_sample], axis=0), w_ada[0], b_ada)
    mod_rows = mod.reshape((BATCH + DEC_BATCH) * 6, 1, D_MODEL)
    mod_s = jnp.repeat(mod[BATCH:], DEC_SEQ, axis=0)

    def prompt_mod(chunk):
        return pl.BlockSpec((None, 1, D_MODEL), lambda b, i: (b * 6 + chunk, 0, 0))

    def sample_mod(chunk):
        return pl.BlockSpec((N_SAMPLE_TOK, D_MODEL), lambda i: (0, chunk))

    w_in_b = w_in[0].astype(BF16)

    tm = 512
    nt = SEQ // tm
    p_row = lambda b, i: (b * nt + i, 0)
    p_tab = pl.BlockSpec((tm, LANES), lambda b, i: (i, 0))
    s_row = lambda i: (0, 0)
    s_tab = pl.BlockSpec((N_SAMPLE_TOK, LANES), s_row)

    pos_p = np.arange(SEQ, dtype=np.int32)
    pos_s = np.tile(past_len + np.arange(DEC_SEQ, dtype=np.int32), DEC_BATCH)
    tabs_p = _rope_tables(pos_p, MOBA_HEAD_DIM) + _rope_tables(pos_p, RET_DK)
    tabs_s = _rope_tables(pos_s, MOBA_HEAD_DIM) + _rope_tables(pos_s, RET_DK)

    xp = x_prompt.reshape(n_prompt_tok, D_MODEL)
    rq, rk, rv, rg, mq, k_p, v_p, w_o_b, w_up_b, w_down_b = _inproj(
        xp, [prompt_mod(1), prompt_mod(0)], [mod_rows, mod_rows], [p_tab] * 4, tabs_p, w_in_b,
        (BATCH, nt), p_row, tm, BF16, casts=(w_o[0], w_up[0], w_down[0]))
    weights = (w_o_b, ln1_g, ln1_b, w_up_b, w_down_b, ln2_g, ln2_b)
    a_ret, state_p = _ret_prompt(rq, rk, rv, rg)
    a_moba = _moba_prompt(mq, k_p, v_p)

    xs = x_sample.reshape(N_SAMPLE_TOK, D_MODEL)
    rq_s, rk_s, rv_s, rg_s, mq_s, k_s, v_s = _inproj(
        xs, [sample_mod(1), sample_mod(0)], [mod_s, mod_s], [s_tab] * 4, tabs_s, w_in_b,
        (1,), s_row, N_SAMPLE_TOK, F32)
    a_ret_s, state_s = _ret_sample(rq_s, rk_s, rv_s, rg_s, state_ret[0])

    cache_rows = cache_k.shape[1] * PAGE_ROWS
    y_p, a_moba_s = _out_ffn_moba(a_ret, a_moba, xp, mod_rows, weights, page_table, mq_s, k_s, v_s,
                                  cache_k.reshape(cache_rows, LANES), cache_v.reshape(cache_rows, LANES), tm)
    y_s = _out_ffn(a_ret_s, a_moba_s, xs, [sample_mod(c) for c in (2, 3, 4, 5)], [mod_s] * 4, weights)

    kv_p_shape = (DEPTH, BATCH, SEQ, MOBA_HEADS, MOBA_HEAD_DIM)
    kv_s_shape = (DEPTH, DEC_BATCH, DEC_SEQ, MOBA_HEADS, MOBA_HEAD_DIM)
    return (y_p.reshape(BATCH, SEQ, D_MODEL),
            y_s.reshape(DEC_BATCH, DEC_SEQ, D_MODEL),
            k_p.reshape(kv_p_shape), v_p.reshape(kv_p_shape), state_p[None],
            k_s.reshape(kv_s_shape), v_s.reshape(kv_s_shape), state_s[None])
```

```python
import functools

import numpy as np
import jax
import jax.numpy as jnp
from jax import lax
from jax.experimental import pallas as pl
from jax.experimental.pallas import tpu as pltpu

F32 = jnp.float32
BF16 = jnp.bfloat16

D_MODEL = 1024
BATCH = 8
SEQ = 2048
DEC_BATCH = 32
DEC_SEQ = 8
PAGE_SIZE = 128
RET_HEADS = 4
RET_DK = 64
RET_DV = 128
RET_CHUNK = 128
MOBA_HEADS = 4
MOBA_HEAD_DIM = 128
MOBA_BLOCK = 256
MOBA_TOPK = 3
D_FF = 4 * D_MODEL
ROPE_THETA = 10000.0
LN_EPS = 1e-5
GN_EPS = 1e-6
DEPTH = 1
ALPHA = (2 * DEPTH) ** 0.25
RET_QK = RET_HEADS * RET_DK
RET_WIDTH = RET_HEADS * RET_DV
MOBA_WIDTH = MOBA_HEADS * MOBA_HEAD_DIM
IN_WIDTH = 2 * RET_QK + 2 * RET_WIDTH + 3 * MOBA_WIDTH
OFF_RQ, OFF_RK, OFF_RV, OFF_RG = 0, 256, 512, 1024
OFF_MQ, OFF_MK, OFF_MV = 1536, 2048, 2560
LANES = 128
N_SAMPLE_TOK = DEC_BATCH * DEC_SEQ
PAGES_PER_SEQ = 64
CHUNK_PAGES = 16
RING_PAGES = 2 * CHUNK_PAGES
PAGE_ROWS = PAGE_SIZE * MOBA_HEADS
NEG_INF = float("-inf")
LOG2_E = 1.4426950408889634

_NT = (((1,), (1,)), ((), ()))
_TN = (((0,), (0,)), ((), ()))


def _log_decay():
    return np.log1p(-np.exp2(-5.0 - np.arange(RET_HEADS, dtype=np.float64)))


def _adaln_body(c_ref, w_ref, b_ref, o_ref):
    c = c_ref[...]
    a = (c * jax.nn.sigmoid(c)).astype(BF16)
    o_ref[...] = jnp.dot(a, w_ref[...].astype(BF16), preferred_element_type=F32) + b_ref[...]


def _adaln(c_all, w_ada, b_ada):
    n = c_all.shape[0]
    tn = 1024
    return pl.pallas_call(
        _adaln_body,
        grid=(6 * D_MODEL // tn,),
        in_specs=[pl.BlockSpec((n, D_MODEL), lambda j: (0, 0)),
                  pl.BlockSpec((D_MODEL, tn), lambda j: (0, j)),
                  pl.BlockSpec((1, tn), lambda j: (0, j))],
        out_specs=pl.BlockSpec((n, tn), lambda j: (0, j)),
        out_shape=jax.ShapeDtypeStruct((n, 6 * D_MODEL), F32),
        name="adaln",
    )(c_all, w_ada, b_ada)


def _rope_tables(pos, head_dim):
    half = head_dim // 2
    inv_freq = np.power(ROPE_THETA, -np.arange(half, dtype=np.float64) / half)
    ang = pos.astype(np.float64)[:, None] * inv_freq[None, :]
    cos, sin = np.cos(ang), np.sin(ang)
    reps = LANES // head_dim
    cos_t = np.tile(np.concatenate([cos, cos], axis=-1), (1, reps))
    sin_t = np.tile(np.concatenate([-sin, sin], axis=-1), (1, reps))
    return jnp.asarray(cos_t, dtype=F32), jnp.asarray(sin_t, dtype=F32)


def _inproj_body(n_casts, x_ref, sc_ref, sh_ref, w_ref, cm_ref, sm_ref, cr_ref, sr_ref, *refs):
    cast_in, refs = refs[:n_casts], refs[n_casts:]
    rq_ref, rk_ref, rv_ref, rg_ref, mq_ref, ko_ref, vo_ref = refs[:7]
    for src, dst in zip(cast_in, refs[7:]):
        dst[...] = src[...].astype(dst.dtype)
    tm = x_ref.shape[0]
    h = (x_ref[...] * (1.0 + sc_ref[...]) + sh_ref[...]).astype(BF16)

    def proj(lo, width):
        return jnp.dot(h, w_ref[:, lo:lo + width], preferred_element_type=F32)

    lane = lax.broadcasted_iota(jnp.int32, (tm, LANES), 1)
    low_half = (lane & (RET_DK - 1)) < (RET_DK // 2)
    cr, sr = cr_ref[...], sr_ref[...]
    cm, sm = cm_ref[...], sm_ref[...]

    def rope_ret(z):
        rot = jnp.where(low_half, pltpu.roll(z, LANES - RET_DK // 2, 1), pltpu.roll(z, RET_DK // 2, 1))
        return z * cr + rot * sr

    def rope_moba(z):
        return z * cm + pltpu.roll(z, MOBA_HEAD_DIM // 2, 1) * sm

    zq = proj(OFF_RQ, RET_QK)
    zk = proj(OFF_RK, RET_QK)
    for s in range(RET_QK // LANES):
        sl = slice(s * LANES, (s + 1) * LANES)
        rq_ref[:, sl] = rope_ret(zq[:, sl])
        rk_ref[:, sl] = rope_ret(zk[:, sl]) * (RET_DK ** -0.5)
    rv_ref[...] = proj(OFF_RV, RET_WIDTH).astype(rv_ref.dtype)
    rg_ref[...] = proj(OFF_RG, RET_WIDTH)
    zq = proj(OFF_MQ, MOBA_WIDTH)
    zk = proj(OFF_MK, MOBA_WIDTH)
    zv = proj(OFF_MV, MOBA_WIDTH)
    for hd in range(MOBA_HEADS):
        sl = slice(hd * LANES, (hd + 1) * LANES)
        mq_ref[:, sl] = rope_moba(zq[:, sl]).astype(mq_ref.dtype)
        ko_ref[pl.ds(hd, tm, stride=MOBA_HEADS), :] = rope_moba(zk[:, sl])
        vo_ref[pl.ds(hd, tm, stride=MOBA_HEADS), :] = zv[:, sl]


def _inproj(x2d, mod_specs, mod_args, tab_specs, tabs, w_in, grid, row_map, tm, act_dtype, casts=()):
    t = x2d.shape[0]
    n_steps = int(np.prod(grid))
    wide = lambda w: pl.BlockSpec((tm, w), row_map)
    cast_specs = [pl.BlockSpec((a.shape[0] // n_steps, a.shape[1]), row_map) for a in casts]
    return pl.pallas_call(
        functools.partial(_inproj_body, len(casts)),
        grid=grid,
        in_specs=[wide(D_MODEL)] + mod_specs + [pl.BlockSpec((D_MODEL, IN_WIDTH), lambda *_: (0, 0))] + tab_specs
        + cast_specs,
        out_specs=[wide(RET_QK), wide(RET_QK), wide(RET_WIDTH), wide(RET_WIDTH), wide(MOBA_WIDTH),
                   pl.BlockSpec((tm * MOBA_HEADS, LANES), row_map),
                   pl.BlockSpec((tm * MOBA_HEADS, LANES), row_map)] + cast_specs,
        out_shape=[jax.ShapeDtypeStruct((t, RET_QK), F32), jax.ShapeDtypeStruct((t, RET_QK), F32),
                   jax.ShapeDtypeStruct((t, RET_WIDTH), act_dtype), jax.ShapeDtypeStruct((t, RET_WIDTH), F32),
                   jax.ShapeDtypeStruct((t, MOBA_WIDTH), act_dtype),
                   jax.ShapeDtypeStruct((t * MOBA_HEADS, LANES), F32),
                   jax.ShapeDtypeStruct((t * MOBA_HEADS, LANES), F32)]
        + [jax.ShapeDtypeStruct(a.shape, BF16) for a in casts],
        compiler_params=pltpu.CompilerParams(dimension_semantics=("arbitrary",) * len(grid),
                                             vmem_limit_bytes=48 * 1024 * 1024),
        name="inproj",
    )(x2d, *mod_args, w_in, *tabs, *casts)


def _group_norm_gate(o, g):
    mu = jnp.mean(o, axis=-1, keepdims=True)
    d = o - mu
    var = jnp.mean(d * d, axis=-1, keepdims=True)
    return d * lax.rsqrt(var + GN_EPS) * (g * jax.nn.sigmoid(g))


def _ret_prompt_tables():
    lg = _log_decay()
    i = np.arange(RET_CHUNK, dtype=np.float64)
    diff = i[:, None] - i[None, :]
    dmat = np.where(diff >= 0, np.exp(np.maximum(diff, 0.0)[None] * lg[:, None, None]), 0.0)
    lane_head = np.arange(RET_QK) // RET_DK
    qd = np.exp((i[:, None] + 1.0) * lg[lane_head][None, :])
    kd = np.exp((RET_CHUNK - 1.0 - i)[:, None] * lg[lane_head][None, :])
    row_head = np.arange(RET_QK) // RET_DK
    col_head = np.arange(RET_WIDTH) // RET_DV
    same = row_head[:, None] == col_head[None, :]
    cdec = np.where(same, np.exp(RET_CHUNK * lg[row_head])[:, None], 0.0)
    return [jnp.asarray(a, dtype=F32) for a in (dmat, qd, kd, cdec, same.astype(np.float64))]


def _ret_prompt_body(q_ref, k_ref, v_ref, g_ref, dmat_ref, qd_ref, kd_ref, cdec_ref, bdm_ref,
                     o_ref, st_ref, state_scr):
    lane_head = lax.broadcasted_iota(jnp.int32, (RET_CHUNK, RET_QK), 1) >> 6
    chunk_rows = lambda c: slice(c * RET_CHUNK, (c + 1) * RET_CHUNK)

    def first_matmuls(c):
        rows = chunk_rows(c)
        q = q_ref[rows, :]
        k = k_ref[rows, :]
        v = v_ref[rows, :]
        kb = k.astype(BF16)
        state = state_scr[...]
        scores = [lax.dot_general(jnp.where(lane_head == hd, q, 0.0).astype(BF16), kb, _NT,
                                  preferred_element_type=F32) for hd in range(RET_HEADS)]
        cross = jnp.dot((q * qd_ref[...]).astype(BF16), state.astype(BF16), preferred_element_type=F32)
        kv = lax.dot_general((k * kd_ref[...]).astype(BF16), v, _TN, preferred_element_type=F32)
        state_scr[...] = cdec_ref[...] * state + bdm_ref[...] * kv
        return scores, cross, v

    def second_matmuls(c, scores, cross, v):
        rows = chunk_rows(c)
        g = g_ref[rows, :]
        decayed = [(scores[hd] * dmat_ref[hd]).astype(BF16) for hd in range(RET_HEADS)]
        for hd in range(RET_HEADS):
            sl = slice(hd * RET_DV, (hd + 1) * RET_DV)
            inner = jnp.dot(decayed[hd], v[:, sl], preferred_element_type=F32)
            o_ref[rows, sl] = _group_norm_gate(inner + cross[:, sl], g[:, sl]).astype(o_ref.dtype)

    state_scr[...] = jnp.zeros_like(state_scr)
    n_chunks = SEQ // RET_CHUNK
    ahead = first_matmuls(0)
    for c in range(n_chunks):
        current = ahead
        if c + 1 < n_chunks:
            ahead = first_matmuls(c + 1)
        second_matmuls(c, *current)
    for hd in range(RET_HEADS):
        st_ref[hd] = state_scr[hd * RET_DK:(hd + 1) * RET_DK, hd * RET_DV:(hd + 1) * RET_DV]


def _ret_prompt(rq, rk, rv, rg):
    tabs = _ret_prompt_tables()
    seq = lambda w: pl.BlockSpec((SEQ, w), lambda b: (b, 0))
    const = lambda a: pl.BlockSpec(a.shape, lambda b: (0,) * a.ndim)
    return pl.pallas_call(
        _ret_prompt_body,
        grid=(BATCH,),
        in_specs=[seq(RET_QK), seq(RET_QK), seq(RET_WIDTH), seq(RET_WIDTH)] + [const(a) for a in tabs],
        out_specs=[seq(RET_WIDTH), pl.BlockSpec((None, RET_HEADS, RET_DK, RET_DV), lambda b: (b, 0, 0, 0))],
        out_shape=[jax.ShapeDtypeStruct((BATCH * SEQ, RET_WIDTH), BF16),
                   jax.ShapeDtypeStruct((BATCH, RET_HEADS, RET_DK, RET_DV), F32)],
        scratch_shapes=[pltpu.VMEM((RET_QK, RET_WIDTH), F32)],
        compiler_params=pltpu.CompilerParams(dimension_semantics=("arbitrary",),
                                             vmem_limit_bytes=48 * 1024 * 1024),
        name="ret_prompt",
    )(rq, rk, rv, rg, *tabs)


def _ret_sample_tables():
    lg = _log_decay()
    t = np.arange(N_SAMPLE_TOK) % DEC_SEQ
    seq_id = np.arange(N_SAMPLE_TOK) // DEC_SEQ
    diff = (t[:, None] - t[None, :]).astype(np.float64)
    same_seq = seq_id[:, None] == seq_id[None, :]
    dmat = np.where(same_seq[None] & (diff >= 0)[None],
                    np.exp(np.maximum(diff, 0.0)[None] * lg[:, None, None]), 0.0)
    lane_head = np.arange(RET_QK) // RET_DK
    qd = np.exp((t[:, None] + 1.0) * lg[lane_head][None, :])
    kd = np.exp((DEC_SEQ - 1.0 - t)[:, None] * lg[lane_head][None, :])
    return [jnp.asarray(a, dtype=F32) for a in (dmat, qd, kd)]


def _ret_sample_body(q_ref, k_ref, v_ref, g_ref, st_ref, dmat_ref, qd_ref, kd_ref, o_ref, sto_ref):
    lg = _log_decay()
    q = q_ref[...]
    k = k_ref[...]
    kb = k.astype(BF16)
    qdec = q * qd_ref[...]
    kdec = k * kd_ref[...]
    vb = v_ref[...].astype(BF16)
    g = g_ref[...]
    lane = lax.broadcasted_iota(jnp.int32, (N_SAMPLE_TOK, LANES), 1)
    lane_head = lax.broadcasted_iota(jnp.int32, (N_SAMPLE_TOK, RET_QK), 1) >> 6
    n_state_rows = DEC_BATCH * RET_DK
    own_seq = ((lax.broadcasted_iota(jnp.int32, (N_SAMPLE_TOK, n_state_rows), 0) >> 3)
               == (lax.broadcasted_iota(jnp.int32, (N_SAMPLE_TOK, n_state_rows), 1) >> 6))
    for hd in range(RET_HEADS):
        sl = slice(hd * RET_DV, (hd + 1) * RET_DV)
        qm = jnp.where(lane_head == hd, q, 0.0).astype(BF16)
        s = lax.dot_general(qm, kb, _NT, preferred_element_type=F32) * dmat_ref[hd]
        inner = jnp.dot(s.astype(BF16), vb[:, sl], preferred_element_type=F32)

        def expand(z):
            slab = z[:, (hd // 2) * LANES:(hd // 2 + 1) * LANES]
            other = pltpu.roll(slab, RET_DK, 1)
            in_low = lane < RET_DK
            both = jnp.where(in_low, slab, other) if hd % 2 == 0 else jnp.where(in_low, other, slab)
            tiled = jnp.concatenate([both] * (n_state_rows // LANES), axis=1)
            return jnp.where(own_seq, tiled, 0.0).astype(BF16)

        st = st_ref[:, hd].reshape(n_state_rows, RET_DV)
        cross = jnp.dot(expand(qdec), st.astype(BF16), preferred_element_type=F32)
        o_ref[:, sl] = _group_norm_gate(inner + cross, g[:, sl])
        kv = lax.dot_general(expand(kdec), vb[:, sl], _TN, preferred_element_type=F32)
        new = float(np.exp(DEC_SEQ * lg[hd])) * st + kv
        sto_ref[:, hd] = new.reshape(DEC_BATCH, RET_DK, RET_DV)


def _ret_sample(rq, rk, rv, rg, state):
    tabs = _ret_sample_tables()
    full = lambda a: pl.BlockSpec(a.shape, lambda i: (0,) * a.ndim)
    args = (rq, rk, rv, rg, state, *tabs)
    return pl.pallas_call(
        _ret_sample_body,
        grid=(1,),
        in_specs=[full(a) for a in args],
        out_specs=[pl.BlockSpec((N_SAMPLE_TOK, RET_WIDTH), lambda i: (0, 0)),
                   pl.BlockSpec(state.shape, lambda i: (0, 0, 0, 0))],
        out_shape=[jax.ShapeDtypeStruct((N_SAMPLE_TOK, RET_WIDTH), F32),
                   jax.ShapeDtypeStruct(state.shape, F32)],
        compiler_params=pltpu.CompilerParams(dimension_semantics=("arbitrary",),
                                             vmem_limit_bytes=56 * 1024 * 1024),
        name="ret_sample",
    )(*args)


def _moba_prompt_body(q_ref, k_ref, v_ref, o_ref):
    hd = pl.program_id(1)
    n_blocks = SEQ // MOBA_BLOCK
    exp2_scale = MOBA_HEAD_DIM ** -0.5 * LOG2_E
    k32 = k_ref[pl.ds(hd, SEQ, stride=MOBA_HEADS), :]
    kb = k32.astype(BF16)
    vt = v_ref[pl.ds(hd, SEQ, stride=MOBA_HEADS), :].T.astype(BF16)
    kmean = jnp.sum(k32.reshape(n_blocks, MOBA_BLOCK, MOBA_HEAD_DIM), axis=1) * (1.0 / MOBA_BLOCK)
    kmb = kmean.astype(BF16)
    key_id = lax.broadcasted_iota(jnp.int32, (MOBA_BLOCK, MOBA_BLOCK), 0)
    qry_id = lax.broadcasted_iota(jnp.int32, (MOBA_BLOCK, MOBA_BLOCK), 1)
    causal = key_id <= qry_id

    blk = lambda n: slice(n * MOBA_BLOCK, (n + 1) * MOBA_BLOCK)

    def score_matmuls(i):
        qi = q_ref[blk(i), :]
        st = [lax.dot_general(kb[blk(n)], qi, _NT, preferred_element_type=F32) for n in range(i + 1)]
        gt = lax.dot_general(kmb, qi, _NT, preferred_element_type=F32) if i > MOBA_TOPK else None
        return st, gt

    ahead = score_matmuls(0)
    for i in range(n_blocks):
        st, gt = ahead
        if i + 1 < n_blocks:
            ahead = score_matmuls(i + 1)
        st[i] = jnp.where(causal, st[i], NEG_INF)
        if i > MOBA_TOPK:
            for n in range(i):
                beats = jnp.zeros((1, MOBA_BLOCK), F32)
                for mm in range(i):
                    if mm == n:
                        continue
                    win = (gt[mm:mm + 1] >= gt[n:n + 1]) if mm < n else (gt[mm:mm + 1] > gt[n:n + 1])
                    beats = beats + win.astype(F32)
                st[n] = st[n] + jnp.where(beats < MOBA_TOPK, 0.0, NEG_INF)
        m = functools.reduce(jnp.maximum, [jnp.max(s, axis=0, keepdims=True) for s in st])
        l = jnp.zeros((1, MOBA_BLOCK), F32)
        acc = jnp.zeros((MOBA_HEAD_DIM, MOBA_BLOCK), F32)
        for n in range(i + 1):
            e = jnp.exp2((st[n] - m) * exp2_scale)
            l = l + jnp.sum(e, axis=0, keepdims=True)
            acc = acc + jnp.dot(vt[:, blk(n)], e.astype(BF16), preferred_element_type=F32)
        o_ref[blk(i), :] = (acc / l).T.astype(o_ref.dtype)


def _moba_prompt(mq, k2d, v2d):
    kv_spec = pl.BlockSpec((SEQ * MOBA_HEADS, LANES), lambda b, h: (b, 0))
    return pl.pallas_call(
        _moba_prompt_body,
        grid=(BATCH, MOBA_HEADS),
        in_specs=[pl.BlockSpec((SEQ, MOBA_HEAD_DIM), lambda b, h: (b, h)), kv_spec, kv_spec],
        out_specs=pl.BlockSpec((SEQ, MOBA_HEAD_DIM), lambda b, h: (b, h)),
        out_shape=jax.ShapeDtypeStruct((BATCH * SEQ, MOBA_WIDTH), BF16),
        compiler_params=pltpu.CompilerParams(dimension_semantics=("arbitrary", "arbitrary"),
                                             vmem_limit_bytes=56 * 1024 * 1024),
        name="moba_prompt",
    )(mq, k2d, v2d)


class _SampleMoba:
    n_chunks = PAGES_PER_SEQ // CHUNK_PAGES
    n_rows = MOBA_HEADS * DEC_SEQ
    n_blocks = PAGES_PER_SEQ * PAGE_SIZE // MOBA_BLOCK
    pages_per_block = MOBA_BLOCK // PAGE_SIZE

    def __init__(self, pt_ref, seq, n_seqs, q_ref, kn_ref, vn_ref, kc_ref, vc_ref, o_ref, ring, sem, s_scr):
        self.pt_ref, self.seq, self.n_seqs = pt_ref, seq, n_seqs
        self.q_ref, self.kn_ref, self.vn_ref = q_ref, kn_ref, vn_ref
        self.kc_ref, self.vc_ref, self.o_ref = kc_ref, vc_ref, o_ref
        self.ring, self.sem, self.s_scr = ring, sem, s_scr

    def _page_copy(self, cache_ref, row0, slot):
        return pltpu.make_async_copy(cache_ref.at[pl.ds(row0, PAGE_ROWS)], self.ring.at[slot],
                                     self.sem.at[slot // CHUNK_PAGES])

    def _start_chunk(self, cache_ref, seq, chunk):
        for r in range(CHUNK_PAGES):
            page = self.pt_ref[seq * PAGES_PER_SEQ + chunk * CHUNK_PAGES + r]
            self._page_copy(cache_ref, pl.multiple_of(page * PAGE_ROWS, PAGE_ROWS),
                            (chunk % 2) * CHUNK_PAGES + r).start()

    def _wait_chunk(self, cache_ref, chunk):
        for r in range(CHUNK_PAGES):
            self._page_copy(cache_ref, 0, (chunk % 2) * CHUNK_PAGES + r).wait()

    def prologue(self):
        @pl.when(self.seq == 0)
        def _():
            self._start_chunk(self.kc_ref, self.seq, 0)
            self._start_chunk(self.kc_ref, self.seq, 1)

        self.q = jnp.concatenate(
            [self.q_ref[:, hd * MOBA_HEAD_DIM:(hd + 1) * MOBA_HEAD_DIM] for hd in range(MOBA_HEADS)], axis=0
        ).astype(BF16)
        row_head = lax.broadcasted_iota(jnp.int32, (self.n_rows, LANES), 0) >> 3
        col_head = lax.broadcasted_iota(jnp.int32, (self.n_rows, LANES), 1) & (MOBA_HEADS - 1)
        self.same_head = row_head == col_head
        self.head_bias = jnp.where(self.same_head, 0.0, NEG_INF)
        self.block_sum, self.block_max = [], []

    @staticmethod
    def _slabs(x):
        return [x[:, j * LANES:(j + 1) * LANES] for j in range(PAGE_ROWS // LANES)]

    def wait_k(self, c):
        self._wait_chunk(self.kc_ref, c)

    def k_pages(self, c, first, last):
        assert first % self.pages_per_block == 0 and last % self.pages_per_block == 0
        for r0 in range(first, last, self.pages_per_block):
            tot = jnp.zeros((self.n_rows, LANES), F32)
            top = jnp.full((self.n_rows, LANES), NEG_INF, F32)
            for r in range(r0, r0 + self.pages_per_block):
                page = self.ring[(c % 2) * CHUNK_PAGES + r].astype(BF16)
                s = lax.dot_general(self.q, page, _NT, preferred_element_type=F32)
                self.s_scr[c * CHUNK_PAGES + r] = s
                for slab in self._slabs(s):
                    tot = tot + jnp.where(self.same_head, slab, 0.0)
                    top = jnp.maximum(top, slab + self.head_bias)
            self.block_sum.append(jnp.sum(tot, axis=-1, keepdims=True))
            self.block_max.append(jnp.max(top, axis=-1, keepdims=True))

    def refill_after_k(self, c):
        if c + 2 < self.n_chunks:
            self._start_chunk(self.kc_ref, self.seq, c + 2)
        else:
            self._start_chunk(self.vc_ref, self.seq, c + 2 - self.n_chunks)

    def select(self):
        n_rows, gs = self.n_rows, self.block_sum
        self.exp2_scale = MOBA_HEAD_DIM ** -0.5 * LOG2_E
        lane = lax.broadcasted_iota(jnp.int32, (n_rows, LANES), 1)
        g_all = jnp.full((n_rows, LANES), NEG_INF, F32)
        for n in range(self.n_blocks):
            g_all = jnp.where(lane == n, gs[n], g_all)
        self.keep_bias = []
        for n in range(self.n_blocks):
            wins = (g_all > gs[n]) | ((g_all == gs[n]) & (lane < n))
            beats = jnp.sum(wins.astype(F32), axis=-1, keepdims=True)
            self.keep_bias.append(jnp.where(beats < MOBA_TOPK, 0.0, NEG_INF))
        s_own = lax.dot_general(self.q, self.kn_ref[...].astype(BF16), _NT, preferred_element_type=F32)
        r_id = lax.broadcasted_iota(jnp.int32, (n_rows, n_rows), 0)
        c_id = lax.broadcasted_iota(jnp.int32, (n_rows, n_rows), 1)
        own_ok = ((c_id & (MOBA_HEADS - 1)) == (r_id >> 3)) & ((c_id >> 2) <= (r_id & (DEC_SEQ - 1)))
        s_own = jnp.where(own_ok, s_own, NEG_INF)
        m = jnp.max(s_own, axis=-1, keepdims=True)
        for n in range(self.n_blocks):
            m = jnp.maximum(m, self.block_max[n] + self.keep_bias[n])
        self.m = m
        self.lsum = jnp.zeros((n_rows, LANES), F32)
        e_own = jnp.exp2((s_own - m) * self.exp2_scale)
        return (jnp.sum(e_own, axis=-1, keepdims=True),
                jnp.dot(e_own.astype(BF16), self.vn_ref[...].astype(BF16), preferred_element_type=F32))

    def wait_v(self, c):
        self._wait_chunk(self.vc_ref, c)

    def v_pages(self, c, first, last, acc):
        assert first % self.pages_per_block == 0 and last % self.pages_per_block == 0
        for r0 in range(first, last, self.pages_per_block):
            shift = self.head_bias + (self.keep_bias[(c * CHUNK_PAGES + r0) // self.pages_per_block] - self.m)
            for r in range(r0, r0 + self.pages_per_block):
                e = [jnp.exp2((slab + shift) * self.exp2_scale) for slab in self._slabs(self.s_scr[c * CHUNK_PAGES + r])]
                self.lsum = self.lsum + functools.reduce(jnp.add, e)
                page = self.ring[(c % 2) * CHUNK_PAGES + r].astype(BF16)
                acc = acc + jnp.dot(jnp.concatenate(e, axis=1).astype(BF16), page, preferred_element_type=F32)
        return acc

    def refill_after_v(self, c):
        if c + 2 < self.n_chunks:
            self._start_chunk(self.vc_ref, self.seq, c + 2)
        else:
            @pl.when(self.seq + 1 < self.n_seqs)
            def _():
                self._start_chunk(self.kc_ref, self.seq + 1, c + 2 - self.n_chunks)

    def finish(self, l_own, acc):
        out = acc / (l_own + jnp.sum(self.lsum, axis=-1, keepdims=True))
        for hd in range(MOBA_HEADS):
            self.o_ref[:, hd * MOBA_HEAD_DIM:(hd + 1) * MOBA_HEAD_DIM] = out[hd * DEC_SEQ:(hd + 1) * DEC_SEQ]


def _layer_norm(x, g, b):
    mu = jnp.mean(x, axis=-1, keepdims=True)
    d = x - mu
    var = jnp.mean(d * d, axis=-1, keepdims=True)
    return d * lax.rsqrt(var + LN_EPS) * g + b


def _out_ffn_body(ar_ref, am_ref, x_ref, ga_ref, shf_ref, scf_ref, gf_ref, wo_ref, g1_ref, b1_ref,
                  wu_ref, wd_ref, g2_ref, b2_ref, y_ref, x1_scr, h_scr, acc_scr):
    c = pl.program_id(0)

    @pl.when(c == 0)
    def _():
        mixed = (jnp.dot(ar_ref[...].astype(BF16), wo_ref[:RET_WIDTH, :], preferred_element_type=F32)
                 + jnp.dot(am_ref[...].astype(BF16), wo_ref[RET_WIDTH:, :], preferred_element_type=F32))
        x1 = _layer_norm(ALPHA * x_ref[...] + ga_ref[...] * mixed, g1_ref[...], b1_ref[...])
        x1_scr[...] = x1
        h_scr[...] = (x1 * (1.0 + scf_ref[...]) + shf_ref[...]).astype(BF16)
        acc_scr[...] = jnp.zeros_like(acc_scr)

    u = jnp.maximum(jnp.dot(h_scr[...], wu_ref[...], preferred_element_type=F32), 0.0)
    acc_scr[...] += jnp.dot((u * u).astype(BF16), wd_ref[...], preferred_element_type=F32)

    @pl.when(c == pl.num_programs(0) - 1)
    def _():
        y_ref[...] = _layer_norm(ALPHA * x1_scr[...] + gf_ref[...] * acc_scr[...], g2_ref[...], b2_ref[...])


def _out_ffn(a_ret, a_moba, x2d, mod_specs, mod_args, weights):
    w_o, ln1_g, ln1_b, w_up, w_down, ln2_g, ln2_b = weights
    t = x2d.shape[0]
    whole = lambda a: pl.BlockSpec(a.shape, lambda c: (0,) * a.ndim)
    return pl.pallas_call(
        _out_ffn_body,
        grid=(D_FF // D_MODEL,),
        in_specs=[whole(a_ret), whole(a_moba), whole(x2d)] + mod_specs
        + [whole(w_o), whole(ln1_g), whole(ln1_b),
           pl.BlockSpec((D_MODEL, D_MODEL), lambda c: (0, c)), pl.BlockSpec((D_MODEL, D_MODEL), lambda c: (c, 0)),
           whole(ln2_g), whole(ln2_b)],
        out_specs=pl.BlockSpec((t, D_MODEL), lambda c: (0, 0)),
        out_shape=jax.ShapeDtypeStruct((t, D_MODEL), F32),
        scratch_shapes=[pltpu.VMEM((t, D_MODEL), F32), pltpu.VMEM((t, D_MODEL), BF16), pltpu.VMEM((t, D_MODEL), F32)],
        compiler_params=pltpu.CompilerParams(dimension_semantics=("arbitrary",),
                                             vmem_limit_bytes=48 * 1024 * 1024),
        name="out_ffn",
    )(a_ret, a_moba, x2d, *mod_args, *weights)


def _out_ffn_moba_body(pt_ref, ar_ref, am_ref, x_ref, ga_ref, shf_ref, scf_ref, gf_ref, wo_ref, g1_ref, b1_ref,
                       wu_ref, wd_ref, g2_ref, b2_ref, q_ref, kn_ref, vn_ref, kc_ref, vc_ref,
                       y_ref, o_ref, ring, sem, s_scr):
    n_seqs = pl.num_programs(0) * pl.num_programs(1)
    seq = pl.program_id(0) * pl.num_programs(1) + pl.program_id(1)
    sm = _SampleMoba(pt_ref, seq, n_seqs, q_ref, kn_ref, vn_ref, kc_ref, vc_ref, o_ref, ring, sem, s_scr)

    def up(c, h):
        u = jnp.maximum(jnp.dot(h, wu_ref[:, c * D_MODEL:(c + 1) * D_MODEL], preferred_element_type=F32), 0.0)
        return (u * u).astype(BF16)

    def down(c, u):
        return jnp.dot(u, wd_ref[c * D_MODEL:(c + 1) * D_MODEL, :], preferred_element_type=F32)

    half = CHUNK_PAGES // 2
    sm.prologue()

    sm.wait_k(0)
    sm.k_pages(0, 0, half)
    mixed = jnp.dot(ar_ref[...].astype(BF16), wo_ref[:RET_WIDTH, :], preferred_element_type=F32)
    sm.k_pages(0, half, CHUNK_PAGES)
    mixed = mixed + jnp.dot(am_ref[...].astype(BF16), wo_ref[RET_WIDTH:, :], preferred_element_type=F32)
    x1 = _layer_norm(ALPHA * x_ref[...] + ga_ref[...] * mixed, g1_ref[...], b1_ref[...])
    h = (x1 * (1.0 + scf_ref[...]) + shf_ref[...]).astype(BF16)
    sm.refill_after_k(0)

    sm.wait_k(1)
    sm.k_pages(1, 0, half)
    u = up(0, h)
    sm.k_pages(1, half, CHUNK_PAGES)
    sm.refill_after_k(1)

    sm.wait_k(2)
    sm.k_pages(2, 0, half)
    acc = down(0, u)
    sm.k_pages(2, half, CHUNK_PAGES)
    sm.refill_after_k(2)

    sm.wait_k(3)
    sm.k_pages(3, 0, half)
    u = up(1, h)
    sm.k_pages(3, half, CHUNK_PAGES)
    sm.refill_after_k(3)

    l, acc_s = sm.select()
    acc = acc + down(1, u)

    sm.wait_v(0)
    acc_s = sm.v_pages(0, 0, half, acc_s)
    u = up(2, h)
    acc_s = sm.v_pages(0, half, CHUNK_PAGES, acc_s)
    sm.refill_after_v(0)

    sm.wait_v(1)
    acc_s = sm.v_pages(1, 0, half, acc_s)
    acc = acc + down(2, u)
    acc_s = sm.v_pages(1, half, CHUNK_PAGES, acc_s)
    sm.refill_after_v(1)

    sm.wait_v(2)
    acc_s = sm.v_pages(2, 0, half, acc_s)
    u = up(3, h)
    acc_s = sm.v_pages(2, half, CHUNK_PAGES, acc_s)
    sm.refill_after_v(2)

    sm.wait_v(3)
    acc_s = sm.v_pages(3, 0, half, acc_s)
    acc = acc + down(3, u)
    acc_s = sm.v_pages(3, half, CHUNK_PAGES, acc_s)
    sm.refill_after_v(3)
    y_ref[...] = _layer_norm(ALPHA * x1 + gf_ref[...] * acc, g2_ref[...], b2_ref[...])
    sm.finish(l, acc_s)


def _out_ffn_moba(a_ret, a_moba, x2d, mod_rows, weights, page_table, mq_s, kn2d, vn2d, cache_k2d, cache_v2d, tm):
    t = x2d.shape[0]
    nt = SEQ // tm
    assert BATCH * nt == DEC_BATCH
    n_rows = MOBA_HEADS * DEC_SEQ
    tile = lambda b, i, pt: (b * nt + i, 0)
    wide = lambda w: pl.BlockSpec((tm, w), tile)
    mod = lambda chunk: pl.BlockSpec((None, 1, D_MODEL), lambda b, i, pt: (b * 6 + chunk, 0, 0))
    const = lambda a: pl.BlockSpec(a.shape, lambda b, i, pt: (0,) * a.ndim, pipeline_mode=pl.Buffered(1))
    seq_rows = lambda w, n: pl.BlockSpec((n, w), tile)
    hbm = pl.BlockSpec(memory_space=pl.ANY)
    grid_spec = pltpu.PrefetchScalarGridSpec(
        num_scalar_prefetch=1,
        grid=(BATCH, nt),
        in_specs=[wide(RET_WIDTH), wide(MOBA_WIDTH), wide(D_MODEL)] + [mod(c) for c in (2, 3, 4, 5)]
        + [const(a) for a in weights]
        + [seq_rows(MOBA_WIDTH, DEC_SEQ), seq_rows(LANES, n_rows), seq_rows(LANES, n_rows), hbm, hbm],
        out_specs=[wide(D_MODEL), seq_rows(MOBA_WIDTH, DEC_SEQ)],
        scratch_shapes=[pltpu.VMEM((RING_PAGES, PAGE_ROWS, LANES), F32),
                        pltpu.SemaphoreType.DMA((RING_PAGES // CHUNK_PAGES,)),
                        pltpu.VMEM((PAGES_PER_SEQ, n_rows, PAGE_ROWS), F32)],
    )
    return pl.pallas_call(
        _out_ffn_moba_body,
        grid_spec=grid_spec,
        out_shape=[jax.ShapeDtypeStruct((t, D_MODEL), F32),
                   jax.ShapeDtypeStruct((N_SAMPLE_TOK, MOBA_WIDTH), F32)],
        compiler_params=pltpu.CompilerParams(dimension_semantics=("arbitrary", "arbitrary"),
                                             vmem_limit_bytes=58 * 1024 * 1024),
        name="out_ffn_moba",
    )(page_table.reshape(-1), a_ret, a_moba, x2d, *([mod_rows] * 4), *weights,
      mq_s, kn2d, vn2d, cache_k2d, cache_v2d)


def kernel(x_prompt, x_sample, cache_k, cache_v, state_ret, page_table, c_prompt, c_sample,
           w_ada, b_ada, w_in, w_o, ln1_g, ln1_b, w_up, w_down, ln2_g, ln2_b):
    n_prompt_tok = BATCH * SEQ
    past_len = page_table.shape[1] * PAGE_SIZE

    mod = _adaln(jnp.concatenate([c_prompt, c_sample], axis=0), w_ada[0], b_ada)
    mod_rows = mod.reshape((BATCH + DEC_BATCH) * 6, 1, D_MODEL)
    mod_s = jnp.repeat(mod[BATCH:], DEC_SEQ, axis=0)

    def prompt_mod(chunk):
        return pl.BlockSpec((None, 1, D_MODEL), lambda b, i: (b * 6 + chunk, 0, 0))

    def sample_mod(chunk):
        return pl.BlockSpec((N_SAMPLE_TOK, D_MODEL), lambda i: (0, chunk))

    w_in_b = w_in[0].astype(BF16)

    tm = 512
    nt = SEQ // tm
    p_row = lambda b, i: (b * nt + i, 0)
    p_tab = pl.BlockSpec((tm, LANES), lambda b, i: (i, 0))
    s_row = lambda i: (0, 0)
    s_tab = pl.BlockSpec((N_SAMPLE_TOK, LANES), s_row)

    pos_p = np.arange(SEQ, dtype=np.int32)
    pos_s = np.tile(past_len + np.arange(DEC_SEQ, dtype=np.int32), DEC_BATCH)
    tabs_p = _rope_tables(pos_p, MOBA_HEAD_DIM) + _rope_tables(pos_p, RET_DK)
    tabs_s = _rope_tables(pos_s, MOBA_HEAD_DIM) + _rope_tables(pos_s, RET_DK)

    xp = x_prompt.reshape(n_prompt_tok, D_MODEL)
    rq, rk, rv, rg, mq, k_p, v_p, w_o_b, w_up_b, w_down_b = _inproj(
        xp, [prompt_mod(1), prompt_mod(0)], [mod_rows, mod_rows], [p_tab] * 4, tabs_p, w_in_b,
        (BATCH, nt), p_row, tm, BF16, casts=(w_o[0], w_up[0], w_down[0]))
    weights = (w_o_b, ln1_g, ln1_b, w_up_b, w_down_b, ln2_g, ln2_b)
    a_ret, state_p = _ret_prompt(rq, rk, rv, rg)
    a_moba = _moba_prompt(mq, k_p, v_p)

    xs = x_sample.reshape(N_SAMPLE_TOK, D_MODEL)
    rq_s, rk_s, rv_s, rg_s, mq_s, k_s, v_s = _inproj(
        xs, [sample_mod(1), sample_mod(0)], [mod_s, mod_s], [s_tab] * 4, tabs_s, w_in_b,
        (1,), s_row, N_SAMPLE_TOK, F32)
    a_ret_s, state_s = _ret_sample(rq_s, rk_s, rv_s, rg_s, state_ret[0])

    cache_rows = cache_k.shape[1] * PAGE_ROWS
    y_p, a_moba_s = _out_ffn_moba(a_ret, a_moba, xp, mod_rows, weights, page_table, mq_s, k_s, v_s,
                                  cache_k.reshape(cache_rows, LANES), cache_v.reshape(cache_rows, LANES), tm)
    y_s = _out_ffn(a_ret_s, a_moba_s, xs, [sample_mod(c) for c in (2, 3, 4, 5)], [mod_s] * 4, weights)

    kv_p_shape = (DEPTH, BATCH, SEQ, MOBA_HEADS, MOBA_HEAD_DIM)
    kv_s_shape = (DEPTH, DEC_BATCH, DEC_SEQ, MOBA_HEADS, MOBA_HEAD_DIM)
    return (y_p.reshape(BATCH, SEQ, D_MODEL),
            y_s.reshape(DEC_BATCH, DEC_SEQ, D_MODEL),
            k_p.reshape(kv_p_shape), v_p.reshape(kv_p_shape), state_p[None],
            k_s.reshape(kv_s_shape), v_s.reshape(kv_s_shape), state_s[None])
```

```python
import functools

import numpy as np
import jax
import jax.numpy as jnp
from jax import lax
from jax.experimental import pallas as pl
from jax.experimental.pallas import tpu as pltpu

F32 = jnp.float32
BF16 = jnp.bfloat16

D_MODEL = 1024
BATCH = 8
SEQ = 2048
DEC_BATCH = 32
DEC_SEQ = 8
PAGE_SIZE = 128
RET_HEADS = 4
RET_DK = 64
RET_DV = 128
RET_CHUNK = 128
MOBA_HEADS = 4
MOBA_HEAD_DIM = 128
MOBA_BLOCK = 256
MOBA_TOPK = 3
D_FF = 4 * D_MODEL
ROPE_THETA = 10000.0
LN_EPS = 1e-5
GN_EPS = 1e-6
DEPTH = 1
ALPHA = (2 * DEPTH) ** 0.25
RET_QK = RET_HEADS * RET_DK
RET_WIDTH = RET_HEADS * RET_DV
MOBA_WIDTH = MOBA_HEADS * MOBA_HEAD_DIM
IN_WIDTH = 2 * RET_QK + 2 * RET_WIDTH + 3 * MOBA_WIDTH
OFF_RQ, OFF_RK, OFF_RV, OFF_RG = 0, 256, 512, 1024
OFF_MQ, OFF_MK, OFF_MV = 1536, 2048, 2560
LANES = 128
N_SAMPLE_TOK = DEC_BATCH * DEC_SEQ
PAGES_PER_SEQ = 64
CHUNK_PAGES = 16
RING_PAGES = 2 * CHUNK_PAGES
PAGE_ROWS = PAGE_SIZE * MOBA_HEADS
NEG_INF = float("-inf")
LOG2_E = 1.4426950408889634

_NT = (((1,), (1,)), ((), ()))
_TN = (((0,), (0,)), ((), ()))


def _log_decay():
    return np.log1p(-np.exp2(-5.0 - np.arange(RET_HEADS, dtype=np.float64)))


N_MOD = 6
MOD_ROWS = BATCH + DEC_BATCH


def _adaln_body(cp_ref, cs_ref, w_ref, b_ref, o_ref):
    w = w_ref[...].astype(BF16)
    for c_ref, lo in ((cp_ref, 0), (cs_ref, BATCH)):
        c = c_ref[...]
        a = (c * jax.nn.sigmoid(c)).astype(BF16)
        o_ref[lo:lo + c.shape[0], :] = jnp.dot(a, w, preferred_element_type=F32) + b_ref[...]


def _adaln(c_prompt, c_sample, w_ada, b_ada):
    return pl.pallas_call(
        _adaln_body,
        grid=(N_MOD,),
        in_specs=[pl.BlockSpec(c_prompt.shape, lambda j: (0, 0)),
                  pl.BlockSpec(c_sample.shape, lambda j: (0, 0)),
                  pl.BlockSpec((D_MODEL, D_MODEL), lambda j: (0, j)),
                  pl.BlockSpec((1, D_MODEL), lambda j: (0, j))],
        out_specs=pl.BlockSpec((None, MOD_ROWS, D_MODEL), lambda j: (j, 0, 0)),
        out_shape=jax.ShapeDtypeStruct((N_MOD, MOD_ROWS, D_MODEL), F32),
        name="adaln",
    )(c_prompt, c_sample, w_ada, b_ada)


def _modulation(ref, sample_group):
    if sample_group:
        rows = ref[BATCH:BATCH + DEC_BATCH, :]
        return jnp.broadcast_to(rows[:, None, :], (DEC_BATCH, DEC_SEQ, D_MODEL)).reshape(N_SAMPLE_TOK, D_MODEL)
    return ref[pl.ds(pl.program_id(0), 1), :]


def _mod_spec(term):
    return pl.BlockSpec((None, MOD_ROWS, D_MODEL), lambda *_: (term, 0, 0))


def _rope_tables(pos, head_dim):
    half = head_dim // 2
    inv_freq = np.power(ROPE_THETA, -np.arange(half, dtype=np.float64) / half)
    ang = pos.astype(np.float64)[:, None] * inv_freq[None, :]
    cos, sin = np.cos(ang), np.sin(ang)
    reps = LANES // head_dim
    cos_t = np.tile(np.concatenate([cos, cos], axis=-1), (1, reps))
    sin_t = np.tile(np.concatenate([-sin, sin], axis=-1), (1, reps))
    return jnp.asarray(cos_t, dtype=F32), jnp.asarray(sin_t, dtype=F32)


def _inproj_body(sample_group, n_casts, x_ref, sc_ref, sh_ref, w_ref, cm_ref, sm_ref, cr_ref, sr_ref, *refs):
    cast_in, refs = refs[:n_casts], refs[n_casts:]
    rq_ref, rk_ref, rv_ref, rg_ref, mq_ref, ko_ref, vo_ref = refs[:7]
    for src, dst in zip(cast_in, refs[7:]):
        dst[...] = src[...].astype(dst.dtype)
    tm = x_ref.shape[0]
    h = (x_ref[...] * (1.0 + _modulation(sc_ref, sample_group)) + _modulation(sh_ref, sample_group)).astype(BF16)

    def proj(lo, width):
        return jnp.dot(h, w_ref[:, lo:lo + width].astype(BF16), preferred_element_type=F32)

    lane = lax.broadcasted_iota(jnp.int32, (tm, LANES), 1)
    low_half = (lane & (RET_DK - 1)) < (RET_DK // 2)
    cr, sr = cr_ref[...], sr_ref[...]
    cm, sm = cm_ref[...], sm_ref[...]

    def rope_ret(z):
        rot = jnp.where(low_half, pltpu.roll(z, LANES - RET_DK // 2, 1), pltpu.roll(z, RET_DK // 2, 1))
        return z * cr + rot * sr

    def rope_moba(z):
        return z * cm + pltpu.roll(z, MOBA_HEAD_DIM // 2, 1) * sm

    zq = proj(OFF_RQ, RET_QK)
    zk = proj(OFF_RK, RET_QK)
    for s in range(RET_QK // LANES):
        sl = slice(s * LANES, (s + 1) * LANES)
        rq_ref[:, sl] = rope_ret(zq[:, sl])
        rk_ref[:, sl] = rope_ret(zk[:, sl]) * (RET_DK ** -0.5)
    rv_ref[...] = proj(OFF_RV, RET_WIDTH).astype(rv_ref.dtype)
    rg_ref[...] = proj(OFF_RG, RET_WIDTH)
    zq = proj(OFF_MQ, MOBA_WIDTH)
    zk = proj(OFF_MK, MOBA_WIDTH)
    zv = proj(OFF_MV, MOBA_WIDTH)
    for hd in range(MOBA_HEADS):
        sl = slice(hd * LANES, (hd + 1) * LANES)
        mq_ref[:, sl] = rope_moba(zq[:, sl]).astype(mq_ref.dtype)
        ko_ref[pl.ds(hd, tm, stride=MOBA_HEADS), :] = rope_moba(zk[:, sl])
        vo_ref[pl.ds(hd, tm, stride=MOBA_HEADS), :] = zv[:, sl]


def _inproj(x2d, mod, sample_group, tab_specs, tabs, w_in, grid, row_map, tm, act_dtype, casts=()):
    t = x2d.shape[0]
    n_steps = int(np.prod(grid))
    wide = lambda w: pl.BlockSpec((tm, w), row_map)
    cast_specs = [pl.BlockSpec((a.shape[0] // n_steps, a.shape[1]), row_map) for a in casts]
    w_spec = pl.BlockSpec((D_MODEL, IN_WIDTH), lambda *_: (0, 0), pipeline_mode=pl.Buffered(1))
    return pl.pallas_call(
        functools.partial(_inproj_body, sample_group, len(casts)),
        grid=grid,
        in_specs=[wide(D_MODEL), _mod_spec(1), _mod_spec(0), w_spec] + tab_specs + cast_specs,
        out_specs=[wide(RET_QK), wide(RET_QK), wide(RET_WIDTH), wide(RET_WIDTH), wide(MOBA_WIDTH),
                   pl.BlockSpec((tm * MOBA_HEADS, LANES), row_map),
                   pl.BlockSpec((tm * MOBA_HEADS, LANES), row_map)] + cast_specs,
        out_shape=[jax.ShapeDtypeStruct((t, RET_QK), F32), jax.ShapeDtypeStruct((t, RET_QK), F32),
                   jax.ShapeDtypeStruct((t, RET_WIDTH), act_dtype), jax.ShapeDtypeStruct((t, RET_WIDTH), F32),
                   jax.ShapeDtypeStruct((t, MOBA_WIDTH), act_dtype),
                   jax.ShapeDtypeStruct((t * MOBA_HEADS, LANES), F32),
                   jax.ShapeDtypeStruct((t * MOBA_HEADS, LANES), F32)]
        + [jax.ShapeDtypeStruct(a.shape, BF16) for a in casts],
        compiler_params=pltpu.CompilerParams(dimension_semantics=("arbitrary",) * len(grid),
                                             vmem_limit_bytes=48 * 1024 * 1024),
        name="inproj",
    )(x2d, mod, mod, w_in, *tabs, *casts)


def _group_norm_gate(o, g):
    mu = jnp.mean(o, axis=-1, keepdims=True)
    d = o - mu
    var = jnp.mean(d * d, axis=-1, keepdims=True)
    return d * lax.rsqrt(var + GN_EPS) * (g * jax.nn.sigmoid(g))


def _ret_prompt_tables():
    lg = _log_decay()
    i = np.arange(RET_CHUNK, dtype=np.float64)
    diff = i[:, None] - i[None, :]
    dmat = np.where(diff >= 0, np.exp(np.maximum(diff, 0.0)[None] * lg[:, None, None]), 0.0)
    lane_head = np.arange(RET_QK) // RET_DK
    qd = np.exp((i[:, None] + 1.0) * lg[lane_head][None, :])
    kd = np.exp((RET_CHUNK - 1.0 - i)[:, None] * lg[lane_head][None, :])
    row_head = np.arange(RET_QK) // RET_DK
    col_head = np.arange(RET_WIDTH) // RET_DV
    same = row_head[:, None] == col_head[None, :]
    cdec = np.where(same, np.exp(RET_CHUNK * lg[row_head])[:, None], 0.0)
    return [jnp.asarray(a, dtype=F32) for a in (dmat, qd, kd, cdec, same.astype(np.float64))]


def _ret_prompt_body(q_ref, k_ref, v_ref, g_ref, dmat_ref, qd_ref, kd_ref, cdec_ref, bdm_ref,
                     o_ref, st_ref, state_scr):
    lane_head = lax.broadcasted_iota(jnp.int32, (RET_CHUNK, RET_QK), 1) >> 6
    chunk_rows = lambda c: slice(c * RET_CHUNK, (c + 1) * RET_CHUNK)

    def first_matmuls(c):
        rows = chunk_rows(c)
        q = q_ref[rows, :]
        k = k_ref[rows, :]
        v = v_ref[rows, :]
        kb = k.astype(BF16)
        state = state_scr[...]
        scores = [lax.dot_general(jnp.where(lane_head == hd, q, 0.0).astype(BF16), kb, _NT,
                                  preferred_element_type=F32) for hd in range(RET_HEADS)]
        cross = jnp.dot((q * qd_ref[...]).astype(BF16), state.astype(BF16), preferred_element_type=F32)
        kv = lax.dot_general((k * kd_ref[...]).astype(BF16), v, _TN, preferred_element_type=F32)
        state_scr[...] = cdec_ref[...] * state + bdm_ref[...] * kv
        return scores, cross, v

    def second_matmuls(c, scores, cross, v):
        rows = chunk_rows(c)
        g = g_ref[rows, :]
        decayed = [(scores[hd] * dmat_ref[hd]).astype(BF16) for hd in range(RET_HEADS)]
        for hd in range(RET_HEADS):
            sl = slice(hd * RET_DV, (hd + 1) * RET_DV)
            inner = jnp.dot(decayed[hd], v[:, sl], preferred_element_type=F32)
            o_ref[rows, sl] = _group_norm_gate(inner + cross[:, sl], g[:, sl]).astype(o_ref.dtype)

    state_scr[...] = jnp.zeros_like(state_scr)
    n_chunks = SEQ // RET_CHUNK
    ahead = first_matmuls(0)
    for c in range(n_chunks):
        current = ahead
        if c + 1 < n_chunks:
            ahead = first_matmuls(c + 1)
        second_matmuls(c, *current)
    for hd in range(RET_HEADS):
        st_ref[hd] = state_scr[hd * RET_DK:(hd + 1) * RET_DK, hd * RET_DV:(hd + 1) * RET_DV]


def _ret_prompt(rq, rk, rv, rg):
    tabs = _ret_prompt_tables()
    seq = lambda w: pl.BlockSpec((SEQ, w), lambda b: (b, 0))
    const = lambda a: pl.BlockSpec(a.shape, lambda b: (0,) * a.ndim)
    return pl.pallas_call(
        _ret_prompt_body,
        grid=(BATCH,),
        in_specs=[seq(RET_QK), seq(RET_QK), seq(RET_WIDTH), seq(RET_WIDTH)] + [const(a) for a in tabs],
        out_specs=[seq(RET_WIDTH), pl.BlockSpec((None, RET_HEADS, RET_DK, RET_DV), lambda b: (b, 0, 0, 0))],
        out_shape=[jax.ShapeDtypeStruct((BATCH * SEQ, RET_WIDTH), BF16),
                   jax.ShapeDtypeStruct((BATCH, RET_HEADS, RET_DK, RET_DV), F32)],
        scratch_shapes=[pltpu.VMEM((RET_QK, RET_WIDTH), F32)],
        compiler_params=pltpu.CompilerParams(dimension_semantics=("arbitrary",),
                                             vmem_limit_bytes=48 * 1024 * 1024),
        name="ret_prompt",
    )(rq, rk, rv, rg, *tabs)


def _ret_sample_tables():
    lg = _log_decay()
    t = np.arange(N_SAMPLE_TOK) % DEC_SEQ
    seq_id = np.arange(N_SAMPLE_TOK) // DEC_SEQ
    diff = (t[:, None] - t[None, :]).astype(np.float64)
    same_seq = seq_id[:, None] == seq_id[None, :]
    dmat = np.where(same_seq[None] & (diff >= 0)[None],
                    np.exp(np.maximum(diff, 0.0)[None] * lg[:, None, None]), 0.0)
    lane_head = np.arange(RET_QK) // RET_DK
    qd = np.exp((t[:, None] + 1.0) * lg[lane_head][None, :])
    kd = np.exp((DEC_SEQ - 1.0 - t)[:, None] * lg[lane_head][None, :])
    return [jnp.asarray(a, dtype=F32) for a in (dmat, qd, kd)]


def _ret_sample_body(q_ref, k_ref, v_ref, g_ref, st_ref, dmat_ref, qd_ref, kd_ref, o_ref, sto_ref):
    lg = _log_decay()
    q = q_ref[...]
    k = k_ref[...]
    kb = k.astype(BF16)
    qdec = q * qd_ref[...]
    kdec = k * kd_ref[...]
    vb = v_ref[...].astype(BF16)
    g = g_ref[...]
    lane = lax.broadcasted_iota(jnp.int32, (N_SAMPLE_TOK, LANES), 1)
    lane_head = lax.broadcasted_iota(jnp.int32, (N_SAMPLE_TOK, RET_QK), 1) >> 6
    n_state_rows = DEC_BATCH * RET_DK
    own_seq = ((lax.broadcasted_iota(jnp.int32, (N_SAMPLE_TOK, n_state_rows), 0) >> 3)
               == (lax.broadcasted_iota(jnp.int32, (N_SAMPLE_TOK, n_state_rows), 1) >> 6))
    for hd in range(RET_HEADS):
        sl = slice(hd * RET_DV, (hd + 1) * RET_DV)
        qm = jnp.where(lane_head == hd, q, 0.0).astype(BF16)
        s = lax.dot_general(qm, kb, _NT, preferred_element_type=F32) * dmat_ref[hd]
        inner = jnp.dot(s.astype(BF16), vb[:, sl], preferred_element_type=F32)

        def expand(z):
            slab = z[:, (hd // 2) * LANES:(hd // 2 + 1) * LANES]
            other = pltpu.roll(slab, RET_DK, 1)
            in_low = lane < RET_DK
            both = jnp.where(in_low, slab, other) if hd % 2 == 0 else jnp.where(in_low, other, slab)
            tiled = jnp.concatenate([both] * (n_state_rows // LANES), axis=1)
            return jnp.where(own_seq, tiled, 0.0).astype(BF16)

        st = st_ref[:, hd].reshape(n_state_rows, RET_DV)
        cross = jnp.dot(expand(qdec), st.astype(BF16), preferred_element_type=F32)
        o_ref[:, sl] = _group_norm_gate(inner + cross, g[:, sl])
        kv = lax.dot_general(expand(kdec), vb[:, sl], _TN, preferred_element_type=F32)
        new = float(np.exp(DEC_SEQ * lg[hd])) * st + kv
        sto_ref[:, hd] = new.reshape(DEC_BATCH, RET_DK, RET_DV)


def _ret_sample(rq, rk, rv, rg, state):
    tabs = _ret_sample_tables()
    full = lambda a: pl.BlockSpec(a.shape, lambda i: (0,) * a.ndim)
    args = (rq, rk, rv, rg, state, *tabs)
    return pl.pallas_call(
        _ret_sample_body,
        grid=(1,),
        in_specs=[full(a) for a in args],
        out_specs=[pl.BlockSpec((N_SAMPLE_TOK, RET_WIDTH), lambda i: (0, 0)),
                   pl.BlockSpec(state.shape, lambda i: (0, 0, 0, 0))],
        out_shape=[jax.ShapeDtypeStruct((N_SAMPLE_TOK, RET_WIDTH), F32),
                   jax.ShapeDtypeStruct(state.shape, F32)],
        compiler_params=pltpu.CompilerParams(dimension_semantics=("arbitrary",),
                                             vmem_limit_bytes=56 * 1024 * 1024),
        name="ret_sample",
    )(*args)


def _moba_prompt_body(q_ref, k_ref, v_ref, o_ref):
    hd = pl.program_id(1)
    n_blocks = SEQ // MOBA_BLOCK
    exp2_scale = MOBA_HEAD_DIM ** -0.5 * LOG2_E
    k32 = k_ref[pl.ds(hd, SEQ, stride=MOBA_HEADS), :]
    kb = k32.astype(BF16)
    vt = v_ref[pl.ds(hd, SEQ, stride=MOBA_HEADS), :].T.astype(BF16)
    kmean = jnp.sum(k32.reshape(n_blocks, MOBA_BLOCK, MOBA_HEAD_DIM), axis=1) * (1.0 / MOBA_BLOCK)
    kmb = kmean.astype(BF16)
    key_id = lax.broadcasted_iota(jnp.int32, (MOBA_BLOCK, MOBA_BLOCK), 0)
    qry_id = lax.broadcasted_iota(jnp.int32, (MOBA_BLOCK, MOBA_BLOCK), 1)
    causal = key_id <= qry_id

    blk = lambda n: slice(n * MOBA_BLOCK, (n + 1) * MOBA_BLOCK)

    def score_matmuls(i):
        qi = q_ref[blk(i), :]
        st = [lax.dot_general(kb[blk(n)], qi, _NT, preferred_element_type=F32) for n in range(i + 1)]
        gt = lax.dot_general(kmb, qi, _NT, preferred_element_type=F32) if i > MOBA_TOPK else None
        return st, gt

    ahead = score_matmuls(0)
    for i in range(n_blocks):
        st, gt = ahead
        if i + 1 < n_blocks:
            ahead = score_matmuls(i + 1)
        st[i] = jnp.where(causal, st[i], NEG_INF)
        if i > MOBA_TOPK:
            for n in range(i):
                beats = jnp.zeros((1, MOBA_BLOCK), F32)
                for mm in range(i):
                    if mm == n:
                        continue
                    win = (gt[mm:mm + 1] >= gt[n:n + 1]) if mm < n else (gt[mm:mm + 1] > gt[n:n + 1])
                    beats = beats + win.astype(F32)
                st[n] = st[n] + jnp.where(beats < MOBA_TOPK, 0.0, NEG_INF)
        m = functools.reduce(jnp.maximum, [jnp.max(s, axis=0, keepdims=True) for s in st])
        l = jnp.zeros((1, MOBA_BLOCK), F32)
        acc = jnp.zeros((MOBA_HEAD_DIM, MOBA_BLOCK), F32)
        for n in range(i + 1):
            e = jnp.exp2((st[n] - m) * exp2_scale)
            l = l + jnp.sum(e, axis=0, keepdims=True)
            acc = acc + jnp.dot(vt[:, blk(n)], e.astype(BF16), preferred_element_type=F32)
        o_ref[blk(i), :] = (acc / l).T.astype(o_ref.dtype)


def _moba_prompt(mq, k2d, v2d):
    kv_spec = pl.BlockSpec((SEQ * MOBA_HEADS, LANES), lambda b, h: (b, 0))
    return pl.pallas_call(
        _moba_prompt_body,
        grid=(BATCH, MOBA_HEADS),
        in_specs=[pl.BlockSpec((SEQ, MOBA_HEAD_DIM), lambda b, h: (b, h)), kv_spec, kv_spec],
        out_specs=pl.BlockSpec((SEQ, MOBA_HEAD_DIM), lambda b, h: (b, h)),
        out_shape=jax.ShapeDtypeStruct((BATCH * SEQ, MOBA_WIDTH), BF16),
        compiler_params=pltpu.CompilerParams(dimension_semantics=("arbitrary", "arbitrary"),
                                             vmem_limit_bytes=56 * 1024 * 1024),
        name="moba_prompt",
    )(mq, k2d, v2d)


class _SampleMoba:
    n_chunks = PAGES_PER_SEQ // CHUNK_PAGES
    n_rows = MOBA_HEADS * DEC_SEQ
    n_blocks = PAGES_PER_SEQ * PAGE_SIZE // MOBA_BLOCK
    pages_per_block = MOBA_BLOCK // PAGE_SIZE

    def __init__(self, pt_ref, seq, n_seqs, q_ref, kn_ref, vn_ref, kc_ref, vc_ref, o_ref, ring, sem, s_scr):
        self.pt_ref, self.seq, self.n_seqs = pt_ref, seq, n_seqs
        self.q_ref, self.kn_ref, self.vn_ref = q_ref, kn_ref, vn_ref
        self.kc_ref, self.vc_ref, self.o_ref = kc_ref, vc_ref, o_ref
        self.ring, self.sem, self.s_scr = ring, sem, s_scr

    def _page_copy(self, cache_ref, row0, slot):
        return pltpu.make_async_copy(cache_ref.at[pl.ds(row0, PAGE_ROWS)], self.ring.at[slot],
                                     self.sem.at[slot // CHUNK_PAGES])

    def _start_chunk(self, cache_ref, seq, chunk):
        for r in range(CHUNK_PAGES):
            page = self.pt_ref[seq * PAGES_PER_SEQ + chunk * CHUNK_PAGES + r]
            self._page_copy(cache_ref, pl.multiple_of(page * PAGE_ROWS, PAGE_ROWS),
                            (chunk % 2) * CHUNK_PAGES + r).start()

    def _wait_chunk(self, cache_ref, chunk):
        for r in range(CHUNK_PAGES):
            self._page_copy(cache_ref, 0, (chunk % 2) * CHUNK_PAGES + r).wait()

    def prologue(self):
        @pl.when(self.seq == 0)
        def _():
            self._start_chunk(self.kc_ref, self.seq, 0)
            self._start_chunk(self.kc_ref, self.seq, 1)

        self.q = jnp.concatenate(
            [self.q_ref[:, hd * MOBA_HEAD_DIM:(hd + 1) * MOBA_HEAD_DIM] for hd in range(MOBA_HEADS)], axis=0
        ).astype(BF16)
        row_head = lax.broadcasted_iota(jnp.int32, (self.n_rows, LANES), 0) >> 3
        col_head = lax.broadcasted_iota(jnp.int32, (self.n_rows, LANES), 1) & (MOBA_HEADS - 1)
        self.same_head = row_head == col_head
        self.head_bias = jnp.where(self.same_head, 0.0, NEG_INF)
        self.block_sum, self.block_max = [], []

    @staticmethod
    def _slabs(x):
        return [x[:, j * LANES:(j + 1) * LANES] for j in range(PAGE_ROWS // LANES)]

    def wait_k(self, c):
        self._wait_chunk(self.kc_ref, c)

    def k_pages(self, c, first, last):
        assert first % self.pages_per_block == 0 and last % self.pages_per_block == 0
        for r0 in range(first, last, self.pages_per_block):
            tot = jnp.zeros((self.n_rows, LANES), F32)
            top = jnp.full((self.n_rows, LANES), NEG_INF, F32)
            for r in range(r0, r0 + self.pages_per_block):
                page = self.ring[(c % 2) * CHUNK_PAGES + r].astype(BF16)
                s = lax.dot_general(self.q, page, _NT, preferred_element_type=F32)
                self.s_scr[c * CHUNK_PAGES + r] = s
                for slab in self._slabs(s):
                    tot = tot + jnp.where(self.same_head, slab, 0.0)
                    top = jnp.maximum(top, slab + self.head_bias)
            self.block_sum.append(jnp.sum(tot, axis=-1, keepdims=True))
            self.block_max.append(jnp.max(top, axis=-1, keepdims=True))

    def refill_after_k(self, c):
        if c + 2 < self.n_chunks:
            self._start_chunk(self.kc_ref, self.seq, c + 2)
        else:
            self._start_chunk(self.vc_ref, self.seq, c + 2 - self.n_chunks)

    def select(self):
        n_rows, gs = self.n_rows, self.block_sum
        self.exp2_scale = MOBA_HEAD_DIM ** -0.5 * LOG2_E
        lane = lax.broadcasted_iota(jnp.int32, (n_rows, LANES), 1)
        g_all = jnp.full((n_rows, LANES), NEG_INF, F32)
        for n in range(self.n_blocks):
            g_all = jnp.where(lane == n, gs[n], g_all)
        self.keep_bias = []
        for n in range(self.n_blocks):
            wins = (g_all > gs[n]) | ((g_all == gs[n]) & (lane < n))
            beats = jnp.sum(wins.astype(F32), axis=-1, keepdims=True)
            self.keep_bias.append(jnp.where(beats < MOBA_TOPK, 0.0, NEG_INF))
        s_own = lax.dot_general(self.q, self.kn_ref[...].astype(BF16), _NT, preferred_element_type=F32)
        r_id = lax.broadcasted_iota(jnp.int32, (n_rows, n_rows), 0)
        c_id = lax.broadcasted_iota(jnp.int32, (n_rows, n_rows), 1)
        own_ok = ((c_id & (MOBA_HEADS - 1)) == (r_id >> 3)) & ((c_id >> 2) <= (r_id & (DEC_SEQ - 1)))
        s_own = jnp.where(own_ok, s_own, NEG_INF)
        m = jnp.max(s_own, axis=-1, keepdims=True)
        for n in range(self.n_blocks):
            m = jnp.maximum(m, self.block_max[n] + self.keep_bias[n])
        self.m = m
        self.lsum = jnp.zeros((n_rows, LANES), F32)
        e_own = jnp.exp2((s_own - m) * self.exp2_scale)
        return (jnp.sum(e_own, axis=-1, keepdims=True),
                jnp.dot(e_own.astype(BF16), self.vn_ref[...].astype(BF16), preferred_element_type=F32))

    def wait_v(self, c):
        self._wait_chunk(self.vc_ref, c)

    def v_pages(self, c, first, last, acc):
        assert first % self.pages_per_block == 0 and last % self.pages_per_block == 0
        for r0 in range(first, last, self.pages_per_block):
            shift = self.head_bias + (self.keep_bias[(c * CHUNK_PAGES + r0) // self.pages_per_block] - self.m)
            for r in range(r0, r0 + self.pages_per_block):
                e = [jnp.exp2((slab + shift) * self.exp2_scale) for slab in self._slabs(self.s_scr[c * CHUNK_PAGES + r])]
                self.lsum = self.lsum + functools.reduce(jnp.add, e)
                page = self.ring[(c % 2) * CHUNK_PAGES + r].astype(BF16)
                acc = acc + jnp.dot(jnp.concatenate(e, axis=1).astype(BF16), page, preferred_element_type=F32)
        return acc

    def refill_after_v(self, c):
        if c + 2 < self.n_chunks:
            self._start_chunk(self.vc_ref, self.seq, c + 2)
        else:
            @pl.when(self.seq + 1 < self.n_seqs)
            def _():
                self._start_chunk(self.kc_ref, self.seq + 1, c + 2 - self.n_chunks)

    def finish(self, l_own, acc):
        out = acc / (l_own + jnp.sum(self.lsum, axis=-1, keepdims=True))
        for hd in range(MOBA_HEADS):
            self.o_ref[:, hd * MOBA_HEAD_DIM:(hd + 1) * MOBA_HEAD_DIM] = out[hd * DEC_SEQ:(hd + 1) * DEC_SEQ]


def _layer_norm(x, g, b):
    mu = jnp.mean(x, axis=-1, keepdims=True)
    d = x - mu
    var = jnp.mean(d * d, axis=-1, keepdims=True)
    return d * lax.rsqrt(var + LN_EPS) * g + b


def _out_ffn_body(ar_ref, am_ref, x_ref, ga_ref, shf_ref, scf_ref, gf_ref, wo_ref, g1_ref, b1_ref,
                  wu_ref, wd_ref, g2_ref, b2_ref, y_ref, x1_scr, h_scr, acc_scr):
    c = pl.program_id(0)

    @pl.when(c == 0)
    def _():
        mixed = (jnp.dot(ar_ref[...].astype(BF16), wo_ref[:RET_WIDTH, :], preferred_element_type=F32)
                 + jnp.dot(am_ref[...].astype(BF16), wo_ref[RET_WIDTH:, :], preferred_element_type=F32))
        x1 = _layer_norm(ALPHA * x_ref[...] + _modulation(ga_ref, True) * mixed, g1_ref[...], b1_ref[...])
        x1_scr[...] = x1
        h_scr[...] = (x1 * (1.0 + _modulation(scf_ref, True)) + _modulation(shf_ref, True)).astype(BF16)
        acc_scr[...] = jnp.zeros_like(acc_scr)

    u = jnp.maximum(jnp.dot(h_scr[...], wu_ref[...], preferred_element_type=F32), 0.0)
    acc_scr[...] += jnp.dot((u * u).astype(BF16), wd_ref[...], preferred_element_type=F32)

    @pl.when(c == pl.num_programs(0) - 1)
    def _():
        y_ref[...] = _layer_norm(ALPHA * x1_scr[...] + _modulation(gf_ref, True) * acc_scr[...],
                                 g2_ref[...], b2_ref[...])


def _out_ffn(a_ret, a_moba, x2d, mod, weights):
    w_o, ln1_g, ln1_b, w_up, w_down, ln2_g, ln2_b = weights
    t = x2d.shape[0]
    whole = lambda a: pl.BlockSpec(a.shape, lambda c: (0,) * a.ndim)
    return pl.pallas_call(
        _out_ffn_body,
        grid=(D_FF // D_MODEL,),
        in_specs=[whole(a_ret), whole(a_moba), whole(x2d)] + [_mod_spec(term) for term in (2, 3, 4, 5)]
        + [whole(w_o), whole(ln1_g), whole(ln1_b),
           pl.BlockSpec((D_MODEL, D_MODEL), lambda c: (0, c)), pl.BlockSpec((D_MODEL, D_MODEL), lambda c: (c, 0)),
           whole(ln2_g), whole(ln2_b)],
        out_specs=pl.BlockSpec((t, D_MODEL), lambda c: (0, 0)),
        out_shape=jax.ShapeDtypeStruct((t, D_MODEL), F32),
        scratch_shapes=[pltpu.VMEM((t, D_MODEL), F32), pltpu.VMEM((t, D_MODEL), BF16), pltpu.VMEM((t, D_MODEL), F32)],
        compiler_params=pltpu.CompilerParams(dimension_semantics=("arbitrary",),
                                             vmem_limit_bytes=48 * 1024 * 1024),
        name="out_ffn",
    )(a_ret, a_moba, x2d, *([mod] * 4), *weights)


def _out_ffn_moba_body(pt_ref, ar_ref, am_ref, x_ref, ga_ref, shf_ref, scf_ref, gf_ref, wo_ref, g1_ref, b1_ref,
                       wu_ref, wd_ref, g2_ref, b2_ref, q_ref, kn_ref, vn_ref, kc_ref, vc_ref,
                       y_ref, o_ref, ring, sem, s_scr):
    n_seqs = pl.num_programs(0) * pl.num_programs(1)
    seq = pl.program_id(0) * pl.num_programs(1) + pl.program_id(1)
    sm = _SampleMoba(pt_ref, seq, n_seqs, q_ref, kn_ref, vn_ref, kc_ref, vc_ref, o_ref, ring, sem, s_scr)

    def up(c, h):
        u = jnp.maximum(jnp.dot(h, wu_ref[:, c * D_MODEL:(c + 1) * D_MODEL], preferred_element_type=F32), 0.0)
        return (u * u).astype(BF16)

    def down(c, u):
        return jnp.dot(u, wd_ref[c * D_MODEL:(c + 1) * D_MODEL, :], preferred_element_type=F32)

    half = CHUNK_PAGES // 2
    sm.prologue()

    sm.wait_k(0)
    sm.k_pages(0, 0, half)
    mixed = jnp.dot(ar_ref[...].astype(BF16), wo_ref[:RET_WIDTH, :], preferred_element_type=F32)
    sm.k_pages(0, half, CHUNK_PAGES)
    mixed = mixed + jnp.dot(am_ref[...].astype(BF16), wo_ref[RET_WIDTH:, :], preferred_element_type=F32)
    x1 = _layer_norm(ALPHA * x_ref[...] + _modulation(ga_ref, False) * mixed, g1_ref[...], b1_ref[...])
    h = (x1 * (1.0 + _modulation(scf_ref, False)) + _modulation(shf_ref, False)).astype(BF16)
    sm.refill_after_k(0)

    sm.wait_k(1)
    sm.k_pages(1, 0, half)
    u = up(0, h)
    sm.k_pages(1, half, CHUNK_PAGES)
    sm.refill_after_k(1)

    sm.wait_k(2)
    sm.k_pages(2, 0, half)
    acc = down(0, u)
    sm.k_pages(2, half, CHUNK_PAGES)
    sm.refill_after_k(2)

    sm.wait_k(3)
    sm.k_pages(3, 0, half)
    u = up(1, h)
    sm.k_pages(3, half, CHUNK_PAGES)
    sm.refill_after_k(3)

    l, acc_s = sm.select()
    acc = acc + down(1, u)

    sm.wait_v(0)
    acc_s = sm.v_pages(0, 0, half, acc_s)
    u = up(2, h)
    acc_s = sm.v_pages(0, half, CHUNK_PAGES, acc_s)
    sm.refill_after_v(0)

    sm.wait_v(1)
    acc_s = sm.v_pages(1, 0, half, acc_s)
    acc = acc + down(2, u)
    acc_s = sm.v_pages(1, half, CHUNK_PAGES, acc_s)
    sm.refill_after_v(1)

    sm.wait_v(2)
    acc_s = sm.v_pages(2, 0, half, acc_s)
    u = up(3, h)
    acc_s = sm.v_pages(2, half, CHUNK_PAGES, acc_s)
    sm.refill_after_v(2)

    sm.wait_v(3)
    acc_s = sm.v_pages(3, 0, half, acc_s)
    acc = acc + down(3, u)
    acc_s = sm.v_pages(3, half, CHUNK_PAGES, acc_s)
    sm.refill_after_v(3)
    y_ref[...] = _layer_norm(ALPHA * x1 + _modulation(gf_ref, False) * acc, g2_ref[...], b2_ref[...])
    sm.finish(l, acc_s)


def _out_ffn_moba(a_ret, a_moba, x2d, mod, weights, page_table, mq_s, kn2d, vn2d, cache_k2d, cache_v2d, tm):
    t = x2d.shape[0]
    nt = SEQ // tm
    assert BATCH * nt == DEC_BATCH
    n_rows = MOBA_HEADS * DEC_SEQ
    tile = lambda b, i, pt: (b * nt + i, 0)
    wide = lambda w: pl.BlockSpec((tm, w), tile)
    const = lambda a: pl.BlockSpec(a.shape, lambda b, i, pt: (0,) * a.ndim, pipeline_mode=pl.Buffered(1))
    seq_rows = lambda w, n: pl.BlockSpec((n, w), tile)
    hbm = pl.BlockSpec(memory_space=pl.ANY)
    grid_spec = pltpu.PrefetchScalarGridSpec(
        num_scalar_prefetch=1,
        grid=(BATCH, nt),
        in_specs=[wide(RET_WIDTH), wide(MOBA_WIDTH), wide(D_MODEL)] + [_mod_spec(term) for term in (2, 3, 4, 5)]
        + [const(a) for a in weights]
        + [seq_rows(MOBA_WIDTH, DEC_SEQ), seq_rows(LANES, n_rows), seq_rows(LANES, n_rows), hbm, hbm],
        out_specs=[wide(D_MODEL), seq_rows(MOBA_WIDTH, DEC_SEQ)],
        scratch_shapes=[pltpu.VMEM((RING_PAGES, PAGE_ROWS, LANES), F32),
                        pltpu.SemaphoreType.DMA((RING_PAGES // CHUNK_PAGES,)),
                        pltpu.VMEM((PAGES_PER_SEQ, n_rows, PAGE_ROWS), F32)],
    )
    return pl.pallas_call(
        _out_ffn_moba_body,
        grid_spec=grid_spec,
        out_shape=[jax.ShapeDtypeStruct((t, D_MODEL), F32),
                   jax.ShapeDtypeStruct((N_SAMPLE_TOK, MOBA_WIDTH), F32)],
        compiler_params=pltpu.CompilerParams(dimension_semantics=("arbitrary", "arbitrary"),
                                             vmem_limit_bytes=58 * 1024 * 1024),
        name="out_ffn_moba",
    )(page_table.reshape(-1), a_ret, a_moba, x2d, *([mod] * 4), *weights,
      mq_s, kn2d, vn2d, cache_k2d, cache_v2d)


def kernel(x_prompt, x_sample, cache_k, cache_v, state_ret, page_table, c_prompt, c_sample,
           w_ada, b_ada, w_in, w_o, ln1_g, ln1_b, w_up, w_down, ln2_g, ln2_b):
    n_prompt_tok = BATCH * SEQ
    past_len = page_table.shape[1] * PAGE_SIZE

    mod = _adaln(c_prompt, c_sample, w_ada[0], b_ada)

    tm = 512
    nt = SEQ // tm
    p_row = lambda b, i: (b * nt + i, 0)
    p_tab = pl.BlockSpec((tm, LANES), lambda b, i: (i, 0))
    s_row = lambda i: (0, 0)
    s_tab = pl.BlockSpec((N_SAMPLE_TOK, LANES), s_row)

    pos_p = np.arange(SEQ, dtype=np.int32)
    pos_s = np.tile(past_len + np.arange(DEC_SEQ, dtype=np.int32), DEC_BATCH)
    tabs_p = _rope_tables(pos_p, MOBA_HEAD_DIM) + _rope_tables(pos_p, RET_DK)
    tabs_s = _rope_tables(pos_s, MOBA_HEAD_DIM) + _rope_tables(pos_s, RET_DK)

    xp = x_prompt.reshape(n_prompt_tok, D_MODEL)
    rq, rk, rv, rg, mq, k_p, v_p, w_o_b, w_up_b, w_down_b = _inproj(
        xp, mod, False, [p_tab] * 4, tabs_p, w_in[0],
        (BATCH, nt), p_row, tm, BF16, casts=(w_o[0], w_up[0], w_down[0]))
    weights = (w_o_b, ln1_g, ln1_b, w_up_b, w_down_b, ln2_g, ln2_b)
    a_ret, state_p = _ret_prompt(rq, rk, rv, rg)
    a_moba = _moba_prompt(mq, k_p, v_p)

    xs = x_sample.reshape(N_SAMPLE_TOK, D_MODEL)
    rq_s, rk_s, rv_s, rg_s, mq_s, k_s, v_s = _inproj(
        xs, mod, True, [s_tab] * 4, tabs_s, w_in[0],
        (1,), s_row, N_SAMPLE_TOK, F32)
    a_ret_s, state_s = _ret_sample(rq_s, rk_s, rv_s, rg_s, state_ret[0])

    cache_rows = cache_k.shape[1] * PAGE_ROWS
    y_p, a_moba_s = _out_ffn_moba(a_ret, a_moba, xp, mod, weights, page_table, mq_s, k_s, v_s,
                                  cache_k.reshape(cache_rows, LANES), cache_v.reshape(cache_rows, LANES), tm)
    y_s = _out_ffn(a_ret_s, a_moba_s, xs, mod, weights)

    kv_p_shape = (DEPTH, BATCH, SEQ, MOBA_HEADS, MOBA_HEAD_DIM)
    kv_s_shape = (DEPTH, DEC_BATCH, DEC_SEQ, MOBA_HEADS, MOBA_HEAD_DIM)
    return (y_p.reshape(BATCH, SEQ, D_MODEL),
            y_s.reshape(DEC_BATCH, DEC_SEQ, D_MODEL),
            k_p.reshape(kv_p_shape), v_p.reshape(kv_p_shape), state_p[None],
            k_s.reshape(kv_s_shape), v_s.reshape(kv_s_shape), state_s[None])
```

```python
import functools

import numpy as np
import jax
import jax.numpy as jnp
from jax import lax
from jax.experimental import pallas as pl
from jax.experimental.pallas import tpu as pltpu

F32 = jnp.float32
BF16 = jnp.bfloat16

D_MODEL = 1024
BATCH = 8
SEQ = 2048
DEC_BATCH = 32
DEC_SEQ = 8
PAGE_SIZE = 128
RET_HEADS = 4
RET_DK = 64
RET_DV = 128
RET_CHUNK = 128
MOBA_HEADS = 4
MOBA_HEAD_DIM = 128
MOBA_BLOCK = 256
MOBA_TOPK = 3
D_FF = 4 * D_MODEL
ROPE_THETA = 10000.0
LN_EPS = 1e-5
GN_EPS = 1e-6
DEPTH = 1
ALPHA = (2 * DEPTH) ** 0.25
RET_QK = RET_HEADS * RET_DK
RET_WIDTH = RET_HEADS * RET_DV
MOBA_WIDTH = MOBA_HEADS * MOBA_HEAD_DIM
IN_WIDTH = 2 * RET_QK + 2 * RET_WIDTH + 3 * MOBA_WIDTH
OFF_RQ, OFF_RK, OFF_RV, OFF_RG = 0, 256, 512, 1024
OFF_MQ, OFF_MK, OFF_MV = 1536, 2048, 2560
LANES = 128
N_SAMPLE_TOK = DEC_BATCH * DEC_SEQ
PAGES_PER_SEQ = 64
CHUNK_PAGES = 16
RING_PAGES = 2 * CHUNK_PAGES
PAGE_ROWS = PAGE_SIZE * MOBA_HEADS
NEG_INF = float("-inf")
LOG2_E = 1.4426950408889634

_NT = (((1,), (1,)), ((), ()))
_TN = (((0,), (0,)), ((), ()))


def _log_decay():
    return np.log1p(-np.exp2(-5.0 - np.arange(RET_HEADS, dtype=np.float64)))


N_MOD = 6
MOD_ROWS = BATCH + DEC_BATCH


def _adaln_body(cp_ref, cs_ref, w_ref, b_ref, o_ref):
    w = w_ref[...].astype(BF16)
    for c_ref, lo in ((cp_ref, 0), (cs_ref, BATCH)):
        c = c_ref[...]
        a = (c * jax.nn.sigmoid(c)).astype(BF16)
        o_ref[lo:lo + c.shape[0], :] = jnp.dot(a, w, preferred_element_type=F32) + b_ref[...]


def _adaln(c_prompt, c_sample, w_ada, b_ada):
    return pl.pallas_call(
        _adaln_body,
        grid=(N_MOD,),
        in_specs=[pl.BlockSpec(c_prompt.shape, lambda j: (0, 0)),
                  pl.BlockSpec(c_sample.shape, lambda j: (0, 0)),
                  pl.BlockSpec((D_MODEL, D_MODEL), lambda j: (0, j)),
                  pl.BlockSpec((1, D_MODEL), lambda j: (0, j))],
        out_specs=pl.BlockSpec((None, MOD_ROWS, D_MODEL), lambda j: (j, 0, 0)),
        out_shape=jax.ShapeDtypeStruct((N_MOD, MOD_ROWS, D_MODEL), F32),
        name="adaln",
    )(c_prompt, c_sample, w_ada, b_ada)


def _modulation(ref, sample_group):
    if sample_group:
        rows = ref[BATCH:BATCH + DEC_BATCH, :]
        return jnp.broadcast_to(rows[:, None, :], (DEC_BATCH, DEC_SEQ, D_MODEL)).reshape(N_SAMPLE_TOK, D_MODEL)
    return ref[pl.ds(pl.program_id(0), 1), :]


def _mod_spec(term):
    return pl.BlockSpec((None, MOD_ROWS, D_MODEL), lambda *_: (term, 0, 0))


def _rope_table(pos):
    slabs = []
    for head_dim in (MOBA_HEAD_DIM, RET_DK):
        half = head_dim // 2
        inv_freq = np.power(ROPE_THETA, -np.arange(half, dtype=np.float64) / half)
        ang = pos.astype(np.float64)[:, None] * inv_freq[None, :]
        cos, sin = np.cos(ang), np.sin(ang)
        reps = LANES // head_dim
        slabs += [np.tile(np.concatenate([cos, cos], axis=-1), (1, reps)),
                  np.tile(np.concatenate([-sin, sin], axis=-1), (1, reps))]
    return jnp.asarray(np.concatenate(slabs, axis=-1), dtype=F32)


def _inproj_body(sample_group, n_casts, x_ref, sc_ref, sh_ref, w_ref, rope_ref, *refs):
    cast_in, refs = refs[:n_casts], refs[n_casts:]
    rq_ref, rk_ref, rv_ref, rg_ref, mq_ref, ko_ref, vo_ref = refs[:7]
    for src, dst in zip(cast_in, refs[7:]):
        dst[...] = src[...].astype(dst.dtype)
    tm = x_ref.shape[0]
    h = (x_ref[...] * (1.0 + _modulation(sc_ref, sample_group)) + _modulation(sh_ref, sample_group)).astype(BF16)

    def proj(lo, width):
        return jnp.dot(h, w_ref[:, lo:lo + width].astype(BF16), preferred_element_type=F32)

    lane = lax.broadcasted_iota(jnp.int32, (tm, LANES), 1)
    low_half = (lane & (RET_DK - 1)) < (RET_DK // 2)
    cm, sm, cr, sr = (rope_ref[:, j * LANES:(j + 1) * LANES] for j in range(4))

    def rope_ret(z):
        rot = jnp.where(low_half, pltpu.roll(z, LANES - RET_DK // 2, 1), pltpu.roll(z, RET_DK // 2, 1))
        return z * cr + rot * sr

    def rope_moba(z):
        return z * cm + pltpu.roll(z, MOBA_HEAD_DIM // 2, 1) * sm

    zq = proj(OFF_RQ, RET_QK)
    zk = proj(OFF_RK, RET_QK)
    for s in range(RET_QK // LANES):
        sl = slice(s * LANES, (s + 1) * LANES)
        rq_ref[:, sl] = rope_ret(zq[:, sl])
        rk_ref[:, sl] = rope_ret(zk[:, sl]) * (RET_DK ** -0.5)
    rv_ref[...] = proj(OFF_RV, RET_WIDTH).astype(rv_ref.dtype)
    rg_ref[...] = proj(OFF_RG, RET_WIDTH)
    zq = proj(OFF_MQ, MOBA_WIDTH)
    zk = proj(OFF_MK, MOBA_WIDTH)
    zv = proj(OFF_MV, MOBA_WIDTH)
    for hd in range(MOBA_HEADS):
        sl = slice(hd * LANES, (hd + 1) * LANES)
        mq_ref[:, sl] = rope_moba(zq[:, sl]).astype(mq_ref.dtype)
        ko_ref[pl.ds(hd, tm, stride=MOBA_HEADS), :] = rope_moba(zk[:, sl])
        vo_ref[pl.ds(hd, tm, stride=MOBA_HEADS), :] = zv[:, sl]


def _inproj(x2d, mod, sample_group, rope_spec, rope, w_in, grid, row_map, tm, act_dtype, casts=()):
    t = x2d.shape[0]
    n_steps = int(np.prod(grid))
    wide = lambda w: pl.BlockSpec((tm, w), row_map)
    cast_specs = [pl.BlockSpec((a.shape[0] // n_steps, a.shape[1]), row_map) for a in casts]
    w_spec = pl.BlockSpec((D_MODEL, IN_WIDTH), lambda *_: (0, 0), pipeline_mode=pl.Buffered(1))
    return pl.pallas_call(
        functools.partial(_inproj_body, sample_group, len(casts)),
        grid=grid,
        in_specs=[wide(D_MODEL), _mod_spec(1), _mod_spec(0), w_spec, rope_spec] + cast_specs,
        out_specs=[wide(RET_QK), wide(RET_QK), wide(RET_WIDTH), wide(RET_WIDTH), wide(MOBA_WIDTH),
                   pl.BlockSpec((tm * MOBA_HEADS, LANES), row_map),
                   pl.BlockSpec((tm * MOBA_HEADS, LANES), row_map)] + cast_specs,
        out_shape=[jax.ShapeDtypeStruct((t, RET_QK), F32), jax.ShapeDtypeStruct((t, RET_QK), F32),
                   jax.ShapeDtypeStruct((t, RET_WIDTH), act_dtype), jax.ShapeDtypeStruct((t, RET_WIDTH), F32),
                   jax.ShapeDtypeStruct((t, MOBA_WIDTH), act_dtype),
                   jax.ShapeDtypeStruct((t * MOBA_HEADS, LANES), F32),
                   jax.ShapeDtypeStruct((t * MOBA_HEADS, LANES), F32)]
        + [jax.ShapeDtypeStruct(a.shape, BF16) for a in casts],
        compiler_params=pltpu.CompilerParams(dimension_semantics=("arbitrary",) * len(grid),
                                             vmem_limit_bytes=48 * 1024 * 1024),
        name="inproj",
    )(x2d, mod, mod, w_in, rope, *casts)


def _group_norm_gate(o, g):
    mu = jnp.mean(o, axis=-1, keepdims=True)
    d = o - mu
    var = jnp.mean(d * d, axis=-1, keepdims=True)
    return d * lax.rsqrt(var + GN_EPS) * (g * jax.nn.sigmoid(g))


def _ret_prompt_tables():
    lg = _log_decay()
    i = np.arange(RET_CHUNK, dtype=np.float64)
    diff = i[:, None] - i[None, :]
    dmat = np.where(diff >= 0, np.exp(np.maximum(diff, 0.0)[None] * lg[:, None, None]), 0.0)
    lane_head = np.arange(RET_QK) // RET_DK
    qd = np.exp((i[:, None] + 1.0) * lg[lane_head][None, :])
    kd = np.exp((RET_CHUNK - 1.0 - i)[:, None] * lg[lane_head][None, :])
    row_head = np.arange(RET_QK) // RET_DK
    col_head = np.arange(RET_WIDTH) // RET_DV
    same = row_head[:, None] == col_head[None, :]
    cdec = np.where(same, np.exp(RET_CHUNK * lg[row_head])[:, None], 0.0)
    return [jnp.asarray(a, dtype=F32) for a in
            (dmat, np.concatenate([qd, kd], axis=1), np.concatenate([cdec, same.astype(np.float64)], axis=1))]


def _ret_prompt_body(q_ref, k_ref, v_ref, g_ref, dmat_ref, qkd_ref, sdm_ref, o_ref, st_ref, state_scr):
    lane_head = lax.broadcasted_iota(jnp.int32, (RET_CHUNK, RET_QK), 1) >> 6
    chunk_rows = lambda c: slice(c * RET_CHUNK, (c + 1) * RET_CHUNK)

    def first_matmuls(c):
        rows = chunk_rows(c)
        q = q_ref[rows, :]
        k = k_ref[rows, :]
        v = v_ref[rows, :]
        kb = k.astype(BF16)
        state = state_scr[...]
        scores = [lax.dot_general(jnp.where(lane_head == hd, q, 0.0).astype(BF16), kb, _NT,
                                  preferred_element_type=F32) for hd in range(RET_HEADS)]
        cross = jnp.dot((q * qkd_ref[:, :RET_QK]).astype(BF16), state.astype(BF16), preferred_element_type=F32)
        kv = lax.dot_general((k * qkd_ref[:, RET_QK:]).astype(BF16), v, _TN, preferred_element_type=F32)
        state_scr[...] = sdm_ref[:, :RET_WIDTH] * state + sdm_ref[:, RET_WIDTH:] * kv
        return scores, cross, v

    def second_matmuls(c, scores, cross, v):
        rows = chunk_rows(c)
        g = g_ref[rows, :]
        decayed = [(scores[hd] * dmat_ref[hd]).astype(BF16) for hd in range(RET_HEADS)]
        for hd in range(RET_HEADS):
            sl = slice(hd * RET_DV, (hd + 1) * RET_DV)
            inner = jnp.dot(decayed[hd], v[:, sl], preferred_element_type=F32)
            o_ref[rows, sl] = _group_norm_gate(inner + cross[:, sl], g[:, sl]).astype(o_ref.dtype)

    state_scr[...] = jnp.zeros_like(state_scr)
    n_chunks = SEQ // RET_CHUNK
    ahead = first_matmuls(0)
    for c in range(n_chunks):
        current = ahead
        if c + 1 < n_chunks:
            ahead = first_matmuls(c + 1)
        second_matmuls(c, *current)
    for hd in range(RET_HEADS):
        st_ref[hd] = state_scr[hd * RET_DK:(hd + 1) * RET_DK, hd * RET_DV:(hd + 1) * RET_DV]


def _ret_prompt(rq, rk, rv, rg):
    tabs = _ret_prompt_tables()
    seq = lambda w: pl.BlockSpec((SEQ, w), lambda b: (b, 0))
    const = lambda a: pl.BlockSpec(a.shape, lambda b: (0,) * a.ndim)
    return pl.pallas_call(
        _ret_prompt_body,
        grid=(BATCH,),
        in_specs=[seq(RET_QK), seq(RET_QK), seq(RET_WIDTH), seq(RET_WIDTH)] + [const(a) for a in tabs],
        out_specs=[seq(RET_WIDTH), pl.BlockSpec((None, RET_HEADS, RET_DK, RET_DV), lambda b: (b, 0, 0, 0))],
        out_shape=[jax.ShapeDtypeStruct((BATCH * SEQ, RET_WIDTH), BF16),
                   jax.ShapeDtypeStruct((BATCH, RET_HEADS, RET_DK, RET_DV), F32)],
        scratch_shapes=[pltpu.VMEM((RET_QK, RET_WIDTH), F32)],
        compiler_params=pltpu.CompilerParams(dimension_semantics=("arbitrary",),
                                             vmem_limit_bytes=48 * 1024 * 1024),
        name="ret_prompt",
    )(rq, rk, rv, rg, *tabs)


def _ret_sample_tables():
    lg = _log_decay()
    t = np.arange(N_SAMPLE_TOK) % DEC_SEQ
    seq_id = np.arange(N_SAMPLE_TOK) // DEC_SEQ
    diff = (t[:, None] - t[None, :]).astype(np.float64)
    same_seq = seq_id[:, None] == seq_id[None, :]
    dmat = np.where(same_seq[None] & (diff >= 0)[None],
                    np.exp(np.maximum(diff, 0.0)[None] * lg[:, None, None]), 0.0)
    lane_head = np.arange(RET_QK) // RET_DK
    qd = np.exp((t[:, None] + 1.0) * lg[lane_head][None, :])
    kd = np.exp((DEC_SEQ - 1.0 - t)[:, None] * lg[lane_head][None, :])
    return [jnp.asarray(a, dtype=F32) for a in (dmat, np.concatenate([qd, kd], axis=1))]


def _ret_sample_body(q_ref, k_ref, v_ref, g_ref, st_ref, dmat_ref, qkd_ref, o_ref, sto_ref):
    lg = _log_decay()
    q = q_ref[...]
    k = k_ref[...]
    kb = k.astype(BF16)
    qdec = q * qkd_ref[:, :RET_QK]
    kdec = k * qkd_ref[:, RET_QK:]
    vb = v_ref[...].astype(BF16)
    g = g_ref[...]
    lane = lax.broadcasted_iota(jnp.int32, (N_SAMPLE_TOK, LANES), 1)
    lane_head = lax.broadcasted_iota(jnp.int32, (N_SAMPLE_TOK, RET_QK), 1) >> 6
    n_state_rows = DEC_BATCH * RET_DK
    own_seq = ((lax.broadcasted_iota(jnp.int32, (N_SAMPLE_TOK, n_state_rows), 0) >> 3)
               == (lax.broadcasted_iota(jnp.int32, (N_SAMPLE_TOK, n_state_rows), 1) >> 6))
    for hd in range(RET_HEADS):
        sl = slice(hd * RET_DV, (hd + 1) * RET_DV)
        qm = jnp.where(lane_head == hd, q, 0.0).astype(BF16)
        s = lax.dot_general(qm, kb, _NT, preferred_element_type=F32) * dmat_ref[hd]
        inner = jnp.dot(s.astype(BF16), vb[:, sl], preferred_element_type=F32)

        def expand(z):
            slab = z[:, (hd // 2) * LANES:(hd // 2 + 1) * LANES]
            other = pltpu.roll(slab, RET_DK, 1)
            in_low = lane < RET_DK
            both = jnp.where(in_low, slab, other) if hd % 2 == 0 else jnp.where(in_low, other, slab)
            tiled = jnp.concatenate([both] * (n_state_rows // LANES), axis=1)
            return jnp.where(own_seq, tiled, 0.0).astype(BF16)

        st = st_ref[:, hd].reshape(n_state_rows, RET_DV)
        cross = jnp.dot(expand(qdec), st.astype(BF16), preferred_element_type=F32)
        o_ref[:, sl] = _group_norm_gate(inner + cross, g[:, sl])
        kv = lax.dot_general(expand(kdec), vb[:, sl], _TN, preferred_element_type=F32)
        new = float(np.exp(DEC_SEQ * lg[hd])) * st + kv
        sto_ref[:, hd] = new.reshape(DEC_BATCH, RET_DK, RET_DV)


def _ret_sample(rq, rk, rv, rg, state):
    tabs = _ret_sample_tables()
    full = lambda a: pl.BlockSpec(a.shape, lambda i: (0,) * a.ndim)
    args = (rq, rk, rv, rg, state, *tabs)
    return pl.pallas_call(
        _ret_sample_body,
        grid=(1,),
        in_specs=[full(a) for a in args],
        out_specs=[pl.BlockSpec((N_SAMPLE_TOK, RET_WIDTH), lambda i: (0, 0)),
                   pl.BlockSpec(state.shape, lambda i: (0, 0, 0, 0))],
        out_shape=[jax.ShapeDtypeStruct((N_SAMPLE_TOK, RET_WIDTH), F32),
                   jax.ShapeDtypeStruct(state.shape, F32)],
        compiler_params=pltpu.CompilerParams(dimension_semantics=("arbitrary",),
                                             vmem_limit_bytes=56 * 1024 * 1024),
        name="ret_sample",
    )(*args)


def _moba_prompt_body(q_ref, k_ref, v_ref, o_ref):
    hd = pl.program_id(1)
    n_blocks = SEQ // MOBA_BLOCK
    exp2_scale = MOBA_HEAD_DIM ** -0.5 * LOG2_E
    k32 = k_ref[pl.ds(hd, SEQ, stride=MOBA_HEADS), :]
    kb = k32.astype(BF16)
    vt = v_ref[pl.ds(hd, SEQ, stride=MOBA_HEADS), :].T.astype(BF16)
    kmean = jnp.sum(k32.reshape(n_blocks, MOBA_BLOCK, MOBA_HEAD_DIM), axis=1) * (1.0 / MOBA_BLOCK)
    kmb = kmean.astype(BF16)
    key_id = lax.broadcasted_iota(jnp.int32, (MOBA_BLOCK, MOBA_BLOCK), 0)
    qry_id = lax.broadcasted_iota(jnp.int32, (MOBA_BLOCK, MOBA_BLOCK), 1)
    causal = key_id <= qry_id

    blk = lambda n: slice(n * MOBA_BLOCK, (n + 1) * MOBA_BLOCK)

    def score_matmuls(i):
        qi = q_ref[blk(i), :]
        st = [lax.dot_general(kb[blk(n)], qi, _NT, preferred_element_type=F32) for n in range(i + 1)]
        gt = lax.dot_general(kmb, qi, _NT, preferred_element_type=F32) if i > MOBA_TOPK else None
        return st, gt

    ahead = score_matmuls(0)
    for i in range(n_blocks):
        st, gt = ahead
        if i + 1 < n_blocks:
            ahead = score_matmuls(i + 1)
        st[i] = jnp.where(causal, st[i], NEG_INF)
        if i > MOBA_TOPK:
            for n in range(i):
                beats = jnp.zeros((1, MOBA_BLOCK), F32)
                for mm in range(i):
                    if mm == n:
                        continue
                    win = (gt[mm:mm + 1] >= gt[n:n + 1]) if mm < n else (gt[mm:mm + 1] > gt[n:n + 1])
                    beats = beats + win.astype(F32)
                st[n] = st[n] + jnp.where(beats < MOBA_TOPK, 0.0, NEG_INF)
        m = functools.reduce(jnp.maximum, [jnp.max(s, axis=0, keepdims=True) for s in st])
        l = jnp.zeros((1, MOBA_BLOCK), F32)
        acc = jnp.zeros((MOBA_HEAD_DIM, MOBA_BLOCK), F32)
        for n in range(i + 1):
            e = jnp.exp2((st[n] - m) * exp2_scale)
            l = l + jnp.sum(e, axis=0, keepdims=True)
            acc = acc + jnp.dot(vt[:, blk(n)], e.astype(BF16), preferred_element_type=F32)
        o_ref[blk(i), :] = (acc / l).T.astype(o_ref.dtype)


def _moba_prompt(mq, k2d, v2d):
    kv_spec = pl.BlockSpec((SEQ * MOBA_HEADS, LANES), lambda b, h: (b, 0))
    return pl.pallas_call(
        _moba_prompt_body,
        grid=(BATCH, MOBA_HEADS),
        in_specs=[pl.BlockSpec((SEQ, MOBA_HEAD_DIM), lambda b, h: (b, h)), kv_spec, kv_spec],
        out_specs=pl.BlockSpec((SEQ, MOBA_HEAD_DIM), lambda b, h: (b, h)),
        out_shape=jax.ShapeDtypeStruct((BATCH * SEQ, MOBA_WIDTH), BF16),
        compiler_params=pltpu.CompilerParams(dimension_semantics=("arbitrary", "arbitrary"),
                                             vmem_limit_bytes=56 * 1024 * 1024),
        name="moba_prompt",
    )(mq, k2d, v2d)


class _SampleMoba:
    n_chunks = PAGES_PER_SEQ // CHUNK_PAGES
    n_rows = MOBA_HEADS * DEC_SEQ
    n_blocks = PAGES_PER_SEQ * PAGE_SIZE // MOBA_BLOCK
    pages_per_block = MOBA_BLOCK // PAGE_SIZE

    def __init__(self, pt_ref, seq, n_seqs, q_ref, kn_ref, vn_ref, kc_ref, vc_ref, o_ref, ring, sem, s_scr):
        self.pt_ref, self.seq, self.n_seqs = pt_ref, seq, n_seqs
        self.q_ref, self.kn_ref, self.vn_ref = q_ref, kn_ref, vn_ref
        self.kc_ref, self.vc_ref, self.o_ref = kc_ref, vc_ref, o_ref
        self.ring, self.sem, self.s_scr = ring, sem, s_scr

    def _page_copy(self, cache_ref, row0, slot):
        return pltpu.make_async_copy(cache_ref.at[pl.ds(row0, PAGE_ROWS)], self.ring.at[slot],
                                     self.sem.at[slot // CHUNK_PAGES])

    def _start_chunk(self, cache_ref, seq, chunk):
        for r in range(CHUNK_PAGES):
            page = self.pt_ref[seq * PAGES_PER_SEQ + chunk * CHUNK_PAGES + r]
            self._page_copy(cache_ref, pl.multiple_of(page * PAGE_ROWS, PAGE_ROWS),
                            (chunk % 2) * CHUNK_PAGES + r).start()

    def _wait_chunk(self, cache_ref, chunk):
        for r in range(CHUNK_PAGES):
            self._page_copy(cache_ref, 0, (chunk % 2) * CHUNK_PAGES + r).wait()

    def prologue(self):
        @pl.when(self.seq == 0)
        def _():
            self._start_chunk(self.kc_ref, self.seq, 0)
            self._start_chunk(self.kc_ref, self.seq, 1)

        self.q = jnp.concatenate(
            [self.q_ref[:, hd * MOBA_HEAD_DIM:(hd + 1) * MOBA_HEAD_DIM] for hd in range(MOBA_HEADS)], axis=0
        ).astype(BF16)
        row_head = lax.broadcasted_iota(jnp.int32, (self.n_rows, LANES), 0) >> 3
        col_head = lax.broadcasted_iota(jnp.int32, (self.n_rows, LANES), 1) & (MOBA_HEADS - 1)
        self.same_head = row_head == col_head
        self.head_bias = jnp.where(self.same_head, 0.0, NEG_INF)
        self.block_sum, self.block_max = [], []

    @staticmethod
    def _slabs(x):
        return [x[:, j * LANES:(j + 1) * LANES] for j in range(PAGE_ROWS // LANES)]

    def wait_k(self, c):
        self._wait_chunk(self.kc_ref, c)

    def k_pages(self, c, first, last):
        assert first % self.pages_per_block == 0 and last % self.pages_per_block == 0
        for r0 in range(first, last, self.pages_per_block):
            tot = jnp.zeros((self.n_rows, LANES), F32)
            top = jnp.full((self.n_rows, LANES), NEG_INF, F32)
            for r in range(r0, r0 + self.pages_per_block):
                page = self.ring[(c % 2) * CHUNK_PAGES + r].astype(BF16)
                s = lax.dot_general(self.q, page, _NT, preferred_element_type=F32)
                self.s_scr[c * CHUNK_PAGES + r] = s
                for slab in self._slabs(s):
                    tot = tot + jnp.where(self.same_head, slab, 0.0)
                    top = jnp.maximum(top, slab + self.head_bias)
            self.block_sum.append(jnp.sum(tot, axis=-1, keepdims=True))
            self.block_max.append(jnp.max(top, axis=-1, keepdims=True))

    def refill_after_k(self, c):
        if c + 2 < self.n_chunks:
            self._start_chunk(self.kc_ref, self.seq, c + 2)
        else:
            self._start_chunk(self.vc_ref, self.seq, c + 2 - self.n_chunks)

    def select(self):
        n_rows, gs = self.n_rows, self.block_sum
        self.exp2_scale = MOBA_HEAD_DIM ** -0.5 * LOG2_E
        lane = lax.broadcasted_iota(jnp.int32, (n_rows, LANES), 1)
        g_all = jnp.full((n_rows, LANES), NEG_INF, F32)
        for n in range(self.n_blocks):
            g_all = jnp.where(lane == n, gs[n], g_all)
        self.keep_bias = []
        for n in range(self.n_blocks):
            wins = (g_all > gs[n]) | ((g_all == gs[n]) & (lane < n))
            beats = jnp.sum(wins.astype(F32), axis=-1, keepdims=True)
            self.keep_bias.append(jnp.where(beats < MOBA_TOPK, 0.0, NEG_INF))
        s_own = lax.dot_general(self.q, self.kn_ref[...].astype(BF16), _NT, preferred_element_type=F32)
        r_id = lax.broadcasted_iota(jnp.int32, (n_rows, n_rows), 0)
        c_id = lax.broadcasted_iota(jnp.int32, (n_rows, n_rows), 1)
        own_ok = ((c_id & (MOBA_HEADS - 1)) == (r_id >> 3)) & ((c_id >> 2) <= (r_id & (DEC_SEQ - 1)))
        s_own = jnp.where(own_ok, s_own, NEG_INF)
        m = jnp.max(s_own, axis=-1, keepdims=True)
        for n in range(self.n_blocks):
            m = jnp.maximum(m, self.block_max[n] + self.keep_bias[n])
        self.m = m
        self.lsum = jnp.zeros((n_rows, LANES), F32)
        e_own = jnp.exp2((s_own - m) * self.exp2_scale)
        return (jnp.sum(e_own, axis=-1, keepdims=True),
                jnp.dot(e_own.astype(BF16), self.vn_ref[...].astype(BF16), preferred_element_type=F32))

    def wait_v(self, c):
        self._wait_chunk(self.vc_ref, c)

    def v_pages(self, c, first, last, acc):
        assert first % self.pages_per_block == 0 and last % self.pages_per_block == 0
        for r0 in range(first, last, self.pages_per_block):
            shift = self.head_bias + (self.keep_bias[(c * CHUNK_PAGES + r0) // self.pages_per_block] - self.m)
            for r in range(r0, r0 + self.pages_per_block):
                e = [jnp.exp2((slab + shift) * self.exp2_scale) for slab in self._slabs(self.s_scr[c * CHUNK_PAGES + r])]
                self.lsum = self.lsum + functools.reduce(jnp.add, e)
                page = self.ring[(c % 2) * CHUNK_PAGES + r].astype(BF16)
                acc = acc + jnp.dot(jnp.concatenate(e, axis=1).astype(BF16), page, preferred_element_type=F32)
        return acc

    def refill_after_v(self, c):
        if c + 2 < self.n_chunks:
            self._start_chunk(self.vc_ref, self.seq, c + 2)
        else:
            @pl.when(self.seq + 1 < self.n_seqs)
            def _():
                self._start_chunk(self.kc_ref, self.seq + 1, c + 2 - self.n_chunks)

    def finish(self, l_own, acc):
        out = acc / (l_own + jnp.sum(self.lsum, axis=-1, keepdims=True))
        for hd in range(MOBA_HEADS):
            self.o_ref[:, hd * MOBA_HEAD_DIM:(hd + 1) * MOBA_HEAD_DIM] = out[hd * DEC_SEQ:(hd + 1) * DEC_SEQ]


def _layer_norm(x, g, b):
    mu = jnp.mean(x, axis=-1, keepdims=True)
    d = x - mu
    var = jnp.mean(d * d, axis=-1, keepdims=True)
    return d * lax.rsqrt(var + LN_EPS) * g + b


def _out_ffn_body(ar_ref, am_ref, x_ref, ga_ref, shf_ref, scf_ref, gf_ref, wo_ref, g1_ref, b1_ref,
                  wu_ref, wd_ref, g2_ref, b2_ref, y_ref, x1_scr, h_scr, acc_scr):
    c = pl.program_id(0)

    @pl.when(c == 0)
    def _():
        mixed = (jnp.dot(ar_ref[...].astype(BF16), wo_ref[:RET_WIDTH, :], preferred_element_type=F32)
                 + jnp.dot(am_ref[...].astype(BF16), wo_ref[RET_WIDTH:, :], preferred_element_type=F32))
        x1 = _layer_norm(ALPHA * x_ref[...] + _modulation(ga_ref, True) * mixed, g1_ref[...], b1_ref[...])
        x1_scr[...] = x1
        h_scr[...] = (x1 * (1.0 + _modulation(scf_ref, True)) + _modulation(shf_ref, True)).astype(BF16)
        acc_scr[...] = jnp.zeros_like(acc_scr)

    u = jnp.maximum(jnp.dot(h_scr[...], wu_ref[...], preferred_element_type=F32), 0.0)
    acc_scr[...] += jnp.dot((u * u).astype(BF16), wd_ref[...], preferred_element_type=F32)

    @pl.when(c == pl.num_programs(0) - 1)
    def _():
        y_ref[...] = _layer_norm(ALPHA * x1_scr[...] + _modulation(gf_ref, True) * acc_scr[...],
                                 g2_ref[...], b2_ref[...])


def _out_ffn(a_ret, a_moba, x2d, mod, weights):
    w_o, ln1_g, ln1_b, w_up, w_down, ln2_g, ln2_b = weights
    t = x2d.shape[0]
    whole = lambda a: pl.BlockSpec(a.shape, lambda c: (0,) * a.ndim)
    return pl.pallas_call(
        _out_ffn_body,
        grid=(D_FF // D_MODEL,),
        in_specs=[whole(a_ret), whole(a_moba), whole(x2d)] + [_mod_spec(term) for term in (2, 3, 4, 5)]
        + [whole(w_o), whole(ln1_g), whole(ln1_b),
           pl.BlockSpec((D_MODEL, D_MODEL), lambda c: (0, c)), pl.BlockSpec((D_MODEL, D_MODEL), lambda c: (c, 0)),
           whole(ln2_g), whole(ln2_b)],
        out_specs=pl.BlockSpec((t, D_MODEL), lambda c: (0, 0)),
        out_shape=jax.ShapeDtypeStruct((t, D_MODEL), F32),
        scratch_shapes=[pltpu.VMEM((t, D_MODEL), F32), pltpu.VMEM((t, D_MODEL), BF16), pltpu.VMEM((t, D_MODEL), F32)],
        compiler_params=pltpu.CompilerParams(dimension_semantics=("arbitrary",),
                                             vmem_limit_bytes=48 * 1024 * 1024),
        name="out_ffn",
    )(a_ret, a_moba, x2d, *([mod] * 4), *weights)


def _out_ffn_moba_body(pt_ref, ar_ref, am_ref, x_ref, ga_ref, shf_ref, scf_ref, gf_ref, wo_ref, g1_ref, b1_ref,
                       wu_ref, wd_ref, g2_ref, b2_ref, q_ref, kn_ref, vn_ref, kc_ref, vc_ref,
                       y_ref, o_ref, ring, sem, s_scr):
    n_seqs = pl.num_programs(0) * pl.num_programs(1)
    seq = pl.program_id(0) * pl.num_programs(1) + pl.program_id(1)
    sm = _SampleMoba(pt_ref, seq, n_seqs, q_ref, kn_ref, vn_ref, kc_ref, vc_ref, o_ref, ring, sem, s_scr)

    def up(c, h):
        u = jnp.maximum(jnp.dot(h, wu_ref[:, c * D_MODEL:(c + 1) * D_MODEL], preferred_element_type=F32), 0.0)
        return (u * u).astype(BF16)

    def down(c, u):
        return jnp.dot(u, wd_ref[c * D_MODEL:(c + 1) * D_MODEL, :], preferred_element_type=F32)

    half = CHUNK_PAGES // 2
    sm.prologue()

    sm.wait_k(0)
    sm.k_pages(0, 0, half)
    mixed = jnp.dot(ar_ref[...].astype(BF16), wo_ref[:RET_WIDTH, :], preferred_element_type=F32)
    sm.k_pages(0, half, CHUNK_PAGES)
    mixed = mixed + jnp.dot(am_ref[...].astype(BF16), wo_ref[RET_WIDTH:, :], preferred_element_type=F32)
    x1 = _layer_norm(ALPHA * x_ref[...] + _modulation(ga_ref, False) * mixed, g1_ref[...], b1_ref[...])
    h = (x1 * (1.0 + _modulation(scf_ref, False)) + _modulation(shf_ref, False)).astype(BF16)
    sm.refill_after_k(0)

    sm.wait_k(1)
    sm.k_pages(1, 0, half)
    u = up(0, h)
    sm.k_pages(1, half, CHUNK_PAGES)
    sm.refill_after_k(1)

    sm.wait_k(2)
    sm.k_pages(2, 0, half)
    acc = down(0, u)
    sm.k_pages(2, half, CHUNK_PAGES)
    sm.refill_after_k(2)

    sm.wait_k(3)
    sm.k_pages(3, 0, half)
    u = up(1, h)
    sm.k_pages(3, half, CHUNK_PAGES)
    sm.refill_after_k(3)

    l, acc_s = sm.select()
    acc = acc + down(1, u)

    sm.wait_v(0)
    acc_s = sm.v_pages(0, 0, half, acc_s)
    u = up(2, h)
    acc_s = sm.v_pages(0, half, CHUNK_PAGES, acc_s)
    sm.refill_after_v(0)

    sm.wait_v(1)
    acc_s = sm.v_pages(1, 0, half, acc_s)
    acc = acc + down(2, u)
    acc_s = sm.v_pages(1, half, CHUNK_PAGES, acc_s)
    sm.refill_after_v(1)

    sm.wait_v(2)
    acc_s = sm.v_pages(2, 0, half, acc_s)
    u = up(3, h)
    acc_s = sm.v_pages(2, half, CHUNK_PAGES, acc_s)
    sm.refill_after_v(2)

    sm.wait_v(3)
    acc_s = sm.v_pages(3, 0, half, acc_s)
    acc = acc + down(3, u)
    acc_s = sm.v_pages(3, half, CHUNK_PAGES, acc_s)
    sm.refill_after_v(3)
    y_ref[...] = _layer_norm(ALPHA * x1 + _modulation(gf_ref, False) * acc, g2_ref[...], b2_ref[...])
    sm.finish(l, acc_s)


def _out_ffn_moba(a_ret, a_moba, x2d, mod, weights, page_table, mq_s, kn2d, vn2d, cache_k2d, cache_v2d, tm):
    t = x2d.shape[0]
    nt = SEQ // tm
    assert BATCH * nt == DEC_BATCH
    n_rows = MOBA_HEADS * DEC_SEQ
    tile = lambda b, i, pt: (b * nt + i, 0)
    wide = lambda w: pl.BlockSpec((tm, w), tile)
    const = lambda a: pl.BlockSpec(a.shape, lambda b, i, pt: (0,) * a.ndim, pipeline_mode=pl.Buffered(1))
    seq_rows = lambda w, n: pl.BlockSpec((n, w), tile)
    hbm = pl.BlockSpec(memory_space=pl.ANY)
    grid_spec = pltpu.PrefetchScalarGridSpec(
        num_scalar_prefetch=1,
        grid=(BATCH, nt),
        in_specs=[wide(RET_WIDTH), wide(MOBA_WIDTH), wide(D_MODEL)] + [_mod_spec(term) for term in (2, 3, 4, 5)]
        + [const(a) for a in weights]
        + [seq_rows(MOBA_WIDTH, DEC_SEQ), seq_rows(LANES, n_rows), seq_rows(LANES, n_rows), hbm, hbm],
        out_specs=[wide(D_MODEL), seq_rows(MOBA_WIDTH, DEC_SEQ)],
        scratch_shapes=[pltpu.VMEM((RING_PAGES, PAGE_ROWS, LANES), F32),
                        pltpu.SemaphoreType.DMA((RING_PAGES // CHUNK_PAGES,)),
                        pltpu.VMEM((PAGES_PER_SEQ, n_rows, PAGE_ROWS), F32)],
    )
    return pl.pallas_call(
        _out_ffn_moba_body,
        grid_spec=grid_spec,
        out_shape=[jax.ShapeDtypeStruct((t, D_MODEL), F32),
                   jax.ShapeDtypeStruct((N_SAMPLE_TOK, MOBA_WIDTH), F32)],
        compiler_params=pltpu.CompilerParams(dimension_semantics=("arbitrary", "arbitrary"),
                                             vmem_limit_bytes=58 * 1024 * 1024),
        name="out_ffn_moba",
    )(page_table.reshape(-1), a_ret, a_moba, x2d, *([mod] * 4), *weights,
      mq_s, kn2d, vn2d, cache_k2d, cache_v2d)


def kernel(x_prompt, x_sample, cache_k, cache_v, state_ret, page_table, c_prompt, c_sample,
           w_ada, b_ada, w_in, w_o, ln1_g, ln1_b, w_up, w_down, ln2_g, ln2_b):
    n_prompt_tok = BATCH * SEQ
    past_len = page_table.shape[1] * PAGE_SIZE

    mod = _adaln(c_prompt, c_sample, w_ada[0], b_ada)

    tm = 512
    nt = SEQ // tm
    p_row = lambda b, i: (b * nt + i, 0)
    p_tab = pl.BlockSpec((tm, 4 * LANES), lambda b, i: (i, 0))
    s_row = lambda i: (0, 0)
    s_tab = pl.BlockSpec((N_SAMPLE_TOK, 4 * LANES), s_row)

    rope_p = _rope_table(np.arange(SEQ, dtype=np.int32))
    rope_s = _rope_table(np.tile(past_len + np.arange(DEC_SEQ, dtype=np.int32), DEC_BATCH))

    xp = x_prompt.reshape(n_prompt_tok, D_MODEL)
    rq, rk, rv, rg, mq, k_p, v_p, w_o_b, w_up_b, w_down_b = _inproj(
        xp, mod, False, p_tab, rope_p, w_in[0],
        (BATCH, nt), p_row, tm, BF16, casts=(w_o[0], w_up[0], w_down[0]))
    weights = (w_o_b, ln1_g, ln1_b, w_up_b, w_down_b, ln2_g, ln2_b)
    a_ret, state_p = _ret_prompt(rq, rk, rv, rg)
    a_moba = _moba_prompt(mq, k_p, v_p)

    xs = x_sample.reshape(N_SAMPLE_TOK, D_MODEL)
    rq_s, rk_s, rv_s, rg_s, mq_s, k_s, v_s = _inproj(
        xs, mod, True, s_tab, rope_s, w_in[0],
        (1,), s_row, N_SAMPLE_TOK, F32)
    a_ret_s, state_s = _ret_sample(rq_s, rk_s, rv_s, rg_s, state_ret[0])

    cache_rows = cache_k.shape[1] * PAGE_ROWS
    y_p, a_moba_s = _out_ffn_moba(a_ret, a_moba, xp, mod, weights, page_table, mq_s, k_s, v_s,
                                  cache_k.reshape(cache_rows, LANES), cache_v.reshape(cache_rows, LANES), tm)
    y_s = _out_ffn(a_ret_s, a_moba_s, xs, mod, weights)

    kv_p_shape = (DEPTH, BATCH, SEQ, MOBA_HEADS, MOBA_HEAD_DIM)
    kv_s_shape = (DEPTH, DEC_BATCH, DEC_SEQ, MOBA_HEADS, MOBA_HEAD_DIM)
    return (y_p.reshape(BATCH, SEQ, D_MODEL),
            y_s.reshape(DEC_BATCH, DEC_SEQ, D_MODEL),
            k_p.reshape(kv_p_shape), v_p.reshape(kv_p_shape), state_p[None],
            k_s.reshape(kv_s_shape), v_s.reshape(kv_s_shape), state_s[None])
```

```python
import functools

import numpy as np
import jax
import jax.numpy as jnp
from jax import lax
from jax.experimental import pallas as pl
from jax.experimental.pallas import tpu as pltpu

F32 = jnp.float32
BF16 = jnp.bfloat16

D_MODEL = 1024
BATCH = 8
SEQ = 2048
DEC_BATCH = 32
DEC_SEQ = 8
PAGE_SIZE = 128
RET_HEADS = 4
RET_DK = 64
RET_DV = 128
RET_CHUNK = 128
MOBA_HEADS = 4
MOBA_HEAD_DIM = 128
MOBA_BLOCK = 256
MOBA_TOPK = 3
D_FF = 4 * D_MODEL
ROPE_THETA = 10000.0
LN_EPS = 1e-5
GN_EPS = 1e-6
DEPTH = 1
ALPHA = (2 * DEPTH) ** 0.25
RET_QK = RET_HEADS * RET_DK
RET_WIDTH = RET_HEADS * RET_DV
MOBA_WIDTH = MOBA_HEADS * MOBA_HEAD_DIM
IN_WIDTH = 2 * RET_QK + 2 * RET_WIDTH + 3 * MOBA_WIDTH
OFF_RQ, OFF_RK, OFF_RV, OFF_RG = 0, 256, 512, 1024
OFF_MQ, OFF_MK, OFF_MV = 1536, 2048, 2560
LANES = 128
N_SAMPLE_TOK = DEC_BATCH * DEC_SEQ
PAGES_PER_SEQ = 64
CHUNK_PAGES = 16
RING_PAGES = 2 * CHUNK_PAGES
PAGE_ROWS = PAGE_SIZE * MOBA_HEADS
NEG_INF = float("-inf")
LOG2_E = 1.4426950408889634

_NT = (((1,), (1,)), ((), ()))
_TN = (((0,), (0,)), ((), ()))


def _log_decay():
    return np.log1p(-np.exp2(-5.0 - np.arange(RET_HEADS, dtype=np.float64)))


N_MOD = 6
MOD_ROWS = BATCH + DEC_BATCH


def _adaln_body(cp_ref, cs_ref, w_ref, b_ref, o_ref):
    w = w_ref[...].astype(BF16)
    for c_ref, lo in ((cp_ref, 0), (cs_ref, BATCH)):
        c = c_ref[...]
        a = (c * jax.nn.sigmoid(c)).astype(BF16)
        o_ref[lo:lo + c.shape[0], :] = jnp.dot(a, w, preferred_element_type=F32) + b_ref[...]


def _adaln(c_prompt, c_sample, w_ada, b_ada):
    return pl.pallas_call(
        _adaln_body,
        grid=(N_MOD,),
        in_specs=[pl.BlockSpec(c_prompt.shape, lambda j: (0, 0)),
                  pl.BlockSpec(c_sample.shape, lambda j: (0, 0)),
                  pl.BlockSpec((D_MODEL, D_MODEL), lambda j: (0, j)),
                  pl.BlockSpec((1, D_MODEL), lambda j: (0, j))],
        out_specs=pl.BlockSpec((None, MOD_ROWS, D_MODEL), lambda j: (j, 0, 0)),
        out_shape=jax.ShapeDtypeStruct((N_MOD, MOD_ROWS, D_MODEL), F32),
        name="adaln",
    )(c_prompt, c_sample, w_ada, b_ada)


def _modulation(ref, sample_group):
    if sample_group:
        rows = ref[BATCH:BATCH + DEC_BATCH, :]
        return jnp.broadcast_to(rows[:, None, :], (DEC_BATCH, DEC_SEQ, D_MODEL)).reshape(N_SAMPLE_TOK, D_MODEL)
    return ref[pl.ds(pl.program_id(0), 1), :]


def _mod_spec(term):
    return pl.BlockSpec((None, MOD_ROWS, D_MODEL), lambda *_: (term, 0, 0))


def _rope_table(pos):
    slabs = []
    for head_dim in (MOBA_HEAD_DIM, RET_DK):
        half = head_dim // 2
        inv_freq = np.power(ROPE_THETA, -np.arange(half, dtype=np.float64) / half)
        ang = pos.astype(np.float64)[:, None] * inv_freq[None, :]
        cos, sin = np.cos(ang), np.sin(ang)
        reps = LANES // head_dim
        slabs += [np.tile(np.concatenate([cos, cos], axis=-1), (1, reps)),
                  np.tile(np.concatenate([-sin, sin], axis=-1), (1, reps))]
    return jnp.asarray(np.concatenate(slabs, axis=-1), dtype=F32)


def _inproj_body(sample_group, n_casts, x_ref, sc_ref, sh_ref, w_ref, rope_ref, *refs):
    cast_in, refs = refs[:n_casts], refs[n_casts:]
    rq_ref, rk_ref, rv_ref, rg_ref, mq_ref, ko_ref, vo_ref = refs[:7]
    for src, dst in zip(cast_in, refs[7:]):
        dst[...] = src[...].astype(dst.dtype)
    tm = x_ref.shape[0]
    h = (x_ref[...] * (1.0 + _modulation(sc_ref, sample_group)) + _modulation(sh_ref, sample_group)).astype(BF16)

    def proj(lo, width):
        return jnp.dot(h, w_ref[:, lo:lo + width].astype(BF16), preferred_element_type=F32)

    lane = lax.broadcasted_iota(jnp.int32, (tm, LANES), 1)
    low_half = (lane & (RET_DK - 1)) < (RET_DK // 2)
    pos = slice(None) if sample_group else pl.ds(pl.multiple_of(pl.program_id(1) * tm, tm), tm)
    cm, sm, cr, sr = (rope_ref[pos, j * LANES:(j + 1) * LANES] for j in range(4))

    def rope_ret(z):
        rot = jnp.where(low_half, pltpu.roll(z, LANES - RET_DK // 2, 1), pltpu.roll(z, RET_DK // 2, 1))
        return z * cr + rot * sr

    def rope_moba(z):
        return z * cm + pltpu.roll(z, MOBA_HEAD_DIM // 2, 1) * sm

    zq = proj(OFF_RQ, RET_QK)
    zk = proj(OFF_RK, RET_QK)
    for s in range(RET_QK // LANES):
        sl = slice(s * LANES, (s + 1) * LANES)
        rq_ref[:, sl] = rope_ret(zq[:, sl])
        rk_ref[:, sl] = rope_ret(zk[:, sl]) * (RET_DK ** -0.5)
    rv_ref[...] = proj(OFF_RV, RET_WIDTH).astype(rv_ref.dtype)
    rg_ref[...] = proj(OFF_RG, RET_WIDTH)
    zq = proj(OFF_MQ, MOBA_WIDTH)
    zk = proj(OFF_MK, MOBA_WIDTH)
    zv = proj(OFF_MV, MOBA_WIDTH)
    for hd in range(MOBA_HEADS):
        sl = slice(hd * LANES, (hd + 1) * LANES)
        mq_ref[:, sl] = rope_moba(zq[:, sl]).astype(mq_ref.dtype)
        ko_ref[pl.ds(hd, tm, stride=MOBA_HEADS), :] = rope_moba(zk[:, sl])
        vo_ref[pl.ds(hd, tm, stride=MOBA_HEADS), :] = zv[:, sl]


def _inproj(x2d, mod, sample_group, rope, w_in, grid, row_map, tm, act_dtype, casts=()):
    t = x2d.shape[0]
    rope_spec = pl.BlockSpec(rope.shape, lambda *_: (0, 0), pipeline_mode=pl.Buffered(1))
    n_steps = int(np.prod(grid))
    wide = lambda w: pl.BlockSpec((tm, w), row_map)
    cast_specs = [pl.BlockSpec((a.shape[0] // n_steps, a.shape[1]), row_map) for a in casts]
    w_spec = pl.BlockSpec((D_MODEL, IN_WIDTH), lambda *_: (0, 0), pipeline_mode=pl.Buffered(1))
    return pl.pallas_call(
        functools.partial(_inproj_body, sample_group, len(casts)),
        grid=grid,
        in_specs=[wide(D_MODEL), _mod_spec(1), _mod_spec(0), w_spec, rope_spec] + cast_specs,
        out_specs=[wide(RET_QK), wide(RET_QK), wide(RET_WIDTH), wide(RET_WIDTH), wide(MOBA_WIDTH),
                   pl.BlockSpec((tm * MOBA_HEADS, LANES), row_map),
                   pl.BlockSpec((tm * MOBA_HEADS, LANES), row_map)] + cast_specs,
        out_shape=[jax.ShapeDtypeStruct((t, RET_QK), F32), jax.ShapeDtypeStruct((t, RET_QK), F32),
                   jax.ShapeDtypeStruct((t, RET_WIDTH), act_dtype), jax.ShapeDtypeStruct((t, RET_WIDTH), F32),
                   jax.ShapeDtypeStruct((t, MOBA_WIDTH), act_dtype),
                   jax.ShapeDtypeStruct((t * MOBA_HEADS, LANES), F32),
                   jax.ShapeDtypeStruct((t * MOBA_HEADS, LANES), F32)]
        + [jax.ShapeDtypeStruct(a.shape, BF16) for a in casts],
        compiler_params=pltpu.CompilerParams(dimension_semantics=("arbitrary",) * len(grid),
                                             vmem_limit_bytes=48 * 1024 * 1024),
        name="inproj",
    )(x2d, mod, mod, w_in, rope, *casts)


def _group_norm_gate(o, g):
    mu = jnp.mean(o, axis=-1, keepdims=True)
    d = o - mu
    var = jnp.mean(d * d, axis=-1, keepdims=True)
    return d * lax.rsqrt(var + GN_EPS) * (g * jax.nn.sigmoid(g))


def _ret_prompt_tables():
    lg = _log_decay()
    i = np.arange(RET_CHUNK, dtype=np.float64)
    diff = i[:, None] - i[None, :]
    dmat = np.where(diff >= 0, np.exp(np.maximum(diff, 0.0)[None] * lg[:, None, None]), 0.0)
    lane_head = np.arange(RET_QK) // RET_DK
    qd = np.exp((i[:, None] + 1.0) * lg[lane_head][None, :])
    kd = np.exp((RET_CHUNK - 1.0 - i)[:, None] * lg[lane_head][None, :])
    row_head = np.arange(RET_QK) // RET_DK
    col_head = np.arange(RET_WIDTH) // RET_DV
    same = row_head[:, None] == col_head[None, :]
    cdec = np.where(same, np.exp(RET_CHUNK * lg[row_head])[:, None], 0.0)
    return [jnp.asarray(a, dtype=F32) for a in
            (dmat, np.concatenate([qd, kd], axis=1), np.concatenate([cdec, same.astype(np.float64)], axis=1))]


def _ret_prompt_body(q_ref, k_ref, v_ref, g_ref, dmat_ref, qkd_ref, sdm_ref, o_ref, st_ref, state_scr):
    lane_head = lax.broadcasted_iota(jnp.int32, (RET_CHUNK, RET_QK), 1) >> 6
    chunk_rows = lambda c: slice(c * RET_CHUNK, (c + 1) * RET_CHUNK)

    def first_matmuls(c):
        rows = chunk_rows(c)
        q = q_ref[rows, :]
        k = k_ref[rows, :]
        v = v_ref[rows, :]
        kb = k.astype(BF16)
        state = state_scr[...]
        scores = [lax.dot_general(jnp.where(lane_head == hd, q, 0.0).astype(BF16), kb, _NT,
                                  preferred_element_type=F32) for hd in range(RET_HEADS)]
        cross = jnp.dot((q * qkd_ref[:, :RET_QK]).astype(BF16), state.astype(BF16), preferred_element_type=F32)
        kv = lax.dot_general((k * qkd_ref[:, RET_QK:]).astype(BF16), v, _TN, preferred_element_type=F32)
        state_scr[...] = sdm_ref[:, :RET_WIDTH] * state + sdm_ref[:, RET_WIDTH:] * kv
        return scores, cross, v

    def second_matmuls(c, scores, cross, v):
        rows = chunk_rows(c)
        g = g_ref[rows, :]
        decayed = [(scores[hd] * dmat_ref[hd]).astype(BF16) for hd in range(RET_HEADS)]
        for hd in range(RET_HEADS):
            sl = slice(hd * RET_DV, (hd + 1) * RET_DV)
            inner = jnp.dot(decayed[hd], v[:, sl], preferred_element_type=F32)
            o_ref[rows, sl] = _group_norm_gate(inner + cross[:, sl], g[:, sl]).astype(o_ref.dtype)

    state_scr[...] = jnp.zeros_like(state_scr)
    n_chunks = SEQ // RET_CHUNK
    ahead = first_matmuls(0)
    for c in range(n_chunks):
        current = ahead
        if c + 1 < n_chunks:
            ahead = first_matmuls(c + 1)
        second_matmuls(c, *current)
    for hd in range(RET_HEADS):
        st_ref[hd] = state_scr[hd * RET_DK:(hd + 1) * RET_DK, hd * RET_DV:(hd + 1) * RET_DV]


def _ret_prompt(rq, rk, rv, rg):
    tabs = _ret_prompt_tables()
    seq = lambda w: pl.BlockSpec((SEQ, w), lambda b: (b, 0))
    const = lambda a: pl.BlockSpec(a.shape, lambda b: (0,) * a.ndim)
    return pl.pallas_call(
        _ret_prompt_body,
        grid=(BATCH,),
        in_specs=[seq(RET_QK), seq(RET_QK), seq(RET_WIDTH), seq(RET_WIDTH)] + [const(a) for a in tabs],
        out_specs=[seq(RET_WIDTH), pl.BlockSpec((None, RET_HEADS, RET_DK, RET_DV), lambda b: (b, 0, 0, 0))],
        out_shape=[jax.ShapeDtypeStruct((BATCH * SEQ, RET_WIDTH), BF16),
                   jax.ShapeDtypeStruct((BATCH, RET_HEADS, RET_DK, RET_DV), F32)],
        scratch_shapes=[pltpu.VMEM((RET_QK, RET_WIDTH), F32)],
        compiler_params=pltpu.CompilerParams(dimension_semantics=("arbitrary",),
                                             vmem_limit_bytes=48 * 1024 * 1024),
        name="ret_prompt",
    )(rq, rk, rv, rg, *tabs)


def _ret_sample_tables():
    lg = _log_decay()
    t = np.arange(N_SAMPLE_TOK) % DEC_SEQ
    seq_id = np.arange(N_SAMPLE_TOK) // DEC_SEQ
    diff = (t[:, None] - t[None, :]).astype(np.float64)
    same_seq = seq_id[:, None] == seq_id[None, :]
    dmat = np.where(same_seq[None] & (diff >= 0)[None],
                    np.exp(np.maximum(diff, 0.0)[None] * lg[:, None, None]), 0.0)
    lane_head = np.arange(RET_QK) // RET_DK
    qd = np.exp((t[:, None] + 1.0) * lg[lane_head][None, :])
    kd = np.exp((DEC_SEQ - 1.0 - t)[:, None] * lg[lane_head][None, :])
    return [jnp.asarray(a, dtype=F32) for a in (dmat, np.concatenate([qd, kd], axis=1))]


def _ret_sample_body(q_ref, k_ref, v_ref, g_ref, st_ref, dmat_ref, qkd_ref, o_ref, sto_ref):
    lg = _log_decay()
    q = q_ref[...]
    k = k_ref[...]
    kb = k.astype(BF16)
    qdec = q * qkd_ref[:, :RET_QK]
    kdec = k * qkd_ref[:, RET_QK:]
    vb = v_ref[...].astype(BF16)
    g = g_ref[...]
    lane = lax.broadcasted_iota(jnp.int32, (N_SAMPLE_TOK, LANES), 1)
    lane_head = lax.broadcasted_iota(jnp.int32, (N_SAMPLE_TOK, RET_QK), 1) >> 6
    n_state_rows = DEC_BATCH * RET_DK
    own_seq = ((lax.broadcasted_iota(jnp.int32, (N_SAMPLE_TOK, n_state_rows), 0) >> 3)
               == (lax.broadcasted_iota(jnp.int32, (N_SAMPLE_TOK, n_state_rows), 1) >> 6))
    for hd in range(RET_HEADS):
        sl = slice(hd * RET_DV, (hd + 1) * RET_DV)
        qm = jnp.where(lane_head == hd, q, 0.0).astype(BF16)
        s = lax.dot_general(qm, kb, _NT, preferred_element_type=F32) * dmat_ref[hd]
        inner = jnp.dot(s.astype(BF16), vb[:, sl], preferred_element_type=F32)

        def expand(z):
            slab = z[:, (hd // 2) * LANES:(hd // 2 + 1) * LANES]
            other = pltpu.roll(slab, RET_DK, 1)
            in_low = lane < RET_DK
            both = jnp.where(in_low, slab, other) if hd % 2 == 0 else jnp.where(in_low, other, slab)
            tiled = jnp.concatenate([both] * (n_state_rows // LANES), axis=1)
            return jnp.where(own_seq, tiled, 0.0).astype(BF16)

        st = st_ref[:, hd].reshape(n_state_rows, RET_DV)
        cross = jnp.dot(expand(qdec), st.astype(BF16), preferred_element_type=F32)
        o_ref[:, sl] = _group_norm_gate(inner + cross, g[:, sl])
        kv = lax.dot_general(expand(kdec), vb[:, sl], _TN, preferred_element_type=F32)
        new = float(np.exp(DEC_SEQ * lg[hd])) * st + kv
        sto_ref[:, hd] = new.reshape(DEC_BATCH, RET_DK, RET_DV)


def _ret_sample(rq, rk, rv, rg, state):
    tabs = _ret_sample_tables()
    full = lambda a: pl.BlockSpec(a.shape, lambda i: (0,) * a.ndim)
    args = (rq, rk, rv, rg, state, *tabs)
    return pl.pallas_call(
        _ret_sample_body,
        grid=(1,),
        in_specs=[full(a) for a in args],
        out_specs=[pl.BlockSpec((N_SAMPLE_TOK, RET_WIDTH), lambda i: (0, 0)),
                   pl.BlockSpec(state.shape, lambda i: (0, 0, 0, 0))],
        out_shape=[jax.ShapeDtypeStruct((N_SAMPLE_TOK, RET_WIDTH), F32),
                   jax.ShapeDtypeStruct(state.shape, F32)],
        compiler_params=pltpu.CompilerParams(dimension_semantics=("arbitrary",),
                                             vmem_limit_bytes=56 * 1024 * 1024),
        name="ret_sample",
    )(*args)


def _moba_prompt_body(q_ref, k_ref, v_ref, o_ref):
    hd = pl.program_id(1)
    n_blocks = SEQ // MOBA_BLOCK
    exp2_scale = MOBA_HEAD_DIM ** -0.5 * LOG2_E
    k32 = k_ref[pl.ds(hd, SEQ, stride=MOBA_HEADS), :]
    kb = k32.astype(BF16)
    vt = v_ref[pl.ds(hd, SEQ, stride=MOBA_HEADS), :].T.astype(BF16)
    kmean = jnp.sum(k32.reshape(n_blocks, MOBA_BLOCK, MOBA_HEAD_DIM), axis=1) * (1.0 / MOBA_BLOCK)
    kmb = kmean.astype(BF16)
    key_id = lax.broadcasted_iota(jnp.int32, (MOBA_BLOCK, MOBA_BLOCK), 0)
    qry_id = lax.broadcasted_iota(jnp.int32, (MOBA_BLOCK, MOBA_BLOCK), 1)
    causal = key_id <= qry_id

    blk = lambda n: slice(n * MOBA_BLOCK, (n + 1) * MOBA_BLOCK)

    def score_matmuls(i):
        qi = q_ref[blk(i), :]
        st = [lax.dot_general(kb[blk(n)], qi, _NT, preferred_element_type=F32) for n in range(i + 1)]
        gt = lax.dot_general(kmb, qi, _NT, preferred_element_type=F32) if i > MOBA_TOPK else None
        return st, gt

    ahead = score_matmuls(0)
    for i in range(n_blocks):
        st, gt = ahead
        if i + 1 < n_blocks:
            ahead = score_matmuls(i + 1)
        st[i] = jnp.where(causal, st[i], NEG_INF)
        if i > MOBA_TOPK:
            for n in range(i):
                beats = jnp.zeros((1, MOBA_BLOCK), F32)
                for mm in range(i):
                    if mm == n:
                        continue
                    win = (gt[mm:mm + 1] >= gt[n:n + 1]) if mm < n else (gt[mm:mm + 1] > gt[n:n + 1])
                    beats = beats + win.astype(F32)
                st[n] = st[n] + jnp.where(beats < MOBA_TOPK, 0.0, NEG_INF)
        m = functools.reduce(jnp.maximum, [jnp.max(s, axis=0, keepdims=True) for s in st])
        l = jnp.zeros((1, MOBA_BLOCK), F32)
        acc = jnp.zeros((MOBA_HEAD_DIM, MOBA_BLOCK), F32)
        for n in range(i + 1):
            e = jnp.exp2((st[n] - m) * exp2_scale)
            l = l + jnp.sum(e, axis=0, keepdims=True)
            acc = acc + jnp.dot(vt[:, blk(n)], e.astype(BF16), preferred_element_type=F32)
        o_ref[blk(i), :] = (acc / l).T.astype(o_ref.dtype)


def _moba_prompt(mq, k2d, v2d):
    kv_spec = pl.BlockSpec((SEQ * MOBA_HEADS, LANES), lambda b, h: (b, 0))
    return pl.pallas_call(
        _moba_prompt_body,
        grid=(BATCH, MOBA_HEADS),
        in_specs=[pl.BlockSpec((SEQ, MOBA_HEAD_DIM), lambda b, h: (b, h)), kv_spec, kv_spec],
        out_specs=pl.BlockSpec((SEQ, MOBA_HEAD_DIM), lambda b, h: (b, h)),
        out_shape=jax.ShapeDtypeStruct((BATCH * SEQ, MOBA_WIDTH), BF16),
        compiler_params=pltpu.CompilerParams(dimension_semantics=("arbitrary", "arbitrary"),
                                             vmem_limit_bytes=56 * 1024 * 1024),
        name="moba_prompt",
    )(mq, k2d, v2d)


class _SampleMoba:
    n_chunks = PAGES_PER_SEQ // CHUNK_PAGES
    n_rows = MOBA_HEADS * DEC_SEQ
    n_blocks = PAGES_PER_SEQ * PAGE_SIZE // MOBA_BLOCK
    pages_per_block = MOBA_BLOCK // PAGE_SIZE

    def __init__(self, pt_ref, seq, n_seqs, q_ref, kn_ref, vn_ref, kc_ref, vc_ref, o_ref, ring, sem, s_scr):
        self.pt_ref, self.seq, self.n_seqs = pt_ref, seq, n_seqs
        self.q_ref, self.kn_ref, self.vn_ref = q_ref, kn_ref, vn_ref
        self.kc_ref, self.vc_ref, self.o_ref = kc_ref, vc_ref, o_ref
        self.ring, self.sem, self.s_scr = ring, sem, s_scr

    def _page_copy(self, cache_ref, row0, slot):
        return pltpu.make_async_copy(cache_ref.at[pl.ds(row0, PAGE_ROWS)], self.ring.at[slot],
                                     self.sem.at[slot // CHUNK_PAGES])

    def _start_chunk(self, cache_ref, seq, chunk):
        for r in range(CHUNK_PAGES):
            page = self.pt_ref[seq * PAGES_PER_SEQ + chunk * CHUNK_PAGES + r]
            self._page_copy(cache_ref, pl.multiple_of(page * PAGE_ROWS, PAGE_ROWS),
                            (chunk % 2) * CHUNK_PAGES + r).start()

    def _wait_chunk(self, cache_ref, chunk):
        for r in range(CHUNK_PAGES):
            self._page_copy(cache_ref, 0, (chunk % 2) * CHUNK_PAGES + r).wait()

    def prologue(self):
        @pl.when(self.seq == 0)
        def _():
            self._start_chunk(self.kc_ref, self.seq, 0)
            self._start_chunk(self.kc_ref, self.seq, 1)

        self.q = jnp.concatenate(
            [self.q_ref[:, hd * MOBA_HEAD_DIM:(hd + 1) * MOBA_HEAD_DIM] for hd in range(MOBA_HEADS)], axis=0
        ).astype(BF16)
        row_head = lax.broadcasted_iota(jnp.int32, (self.n_rows, LANES), 0) >> 3
        col_head = lax.broadcasted_iota(jnp.int32, (self.n_rows, LANES), 1) & (MOBA_HEADS - 1)
        self.same_head = row_head == col_head
        self.head_bias = jnp.where(self.same_head, 0.0, NEG_INF)
        self.block_sum, self.block_max = [], []

    @staticmethod
    def _slabs(x):
        return [x[:, j * LANES:(j + 1) * LANES] for j in range(PAGE_ROWS // LANES)]

    def wait_k(self, c):
        self._wait_chunk(self.kc_ref, c)

    def k_pages(self, c, first, last):
        assert first % self.pages_per_block == 0 and last % self.pages_per_block == 0
        for r0 in range(first, last, self.pages_per_block):
            tot = jnp.zeros((self.n_rows, LANES), F32)
            top = jnp.full((self.n_rows, LANES), NEG_INF, F32)
            for r in range(r0, r0 + self.pages_per_block):
                page = self.ring[(c % 2) * CHUNK_PAGES + r].astype(BF16)
                s = lax.dot_general(self.q, page, _NT, preferred_element_type=F32)
                self.s_scr[c * CHUNK_PAGES + r] = s
                for slab in self._slabs(s):
                    tot = tot + jnp.where(self.same_head, slab, 0.0)
                    top = jnp.maximum(top, slab + self.head_bias)
            self.block_sum.append(jnp.sum(tot, axis=-1, keepdims=True))
            self.block_max.append(jnp.max(top, axis=-1, keepdims=True))

    def refill_after_k(self, c):
        if c + 2 < self.n_chunks:
            self._start_chunk(self.kc_ref, self.seq, c + 2)
        else:
            self._start_chunk(self.vc_ref, self.seq, c + 2 - self.n_chunks)

    def select(self):
        n_rows, gs = self.n_rows, self.block_sum
        self.exp2_scale = MOBA_HEAD_DIM ** -0.5 * LOG2_E
        lane = lax.broadcasted_iota(jnp.int32, (n_rows, LANES), 1)
        g_all = jnp.full((n_rows, LANES), NEG_INF, F32)
        for n in range(self.n_blocks):
            g_all = jnp.where(lane == n, gs[n], g_all)
        self.keep_bias = []
        for n in range(self.n_blocks):
            wins = (g_all > gs[n]) | ((g_all == gs[n]) & (lane < n))
            beats = jnp.sum(wins.astype(F32), axis=-1, keepdims=True)
            self.keep_bias.append(jnp.where(beats < MOBA_TOPK, 0.0, NEG_INF))
        s_own = lax.dot_general(self.q, self.kn_ref[...].astype(BF16), _NT, preferred_element_type=F32)
        r_id = lax.broadcasted_iota(jnp.int32, (n_rows, n_rows), 0)
        c_id = lax.broadcasted_iota(jnp.int32, (n_rows, n_rows), 1)
        own_ok = ((c_id & (MOBA_HEADS - 1)) == (r_id >> 3)) & ((c_id >> 2) <= (r_id & (DEC_SEQ - 1)))
        s_own = jnp.where(own_ok, s_own, NEG_INF)
        m = jnp.max(s_own, axis=-1, keepdims=True)
        for n in range(self.n_blocks):
            m = jnp.maximum(m, self.block_max[n] + self.keep_bias[n])
        self.m = m
        self.lsum = jnp.zeros((n_rows, LANES), F32)
        e_own = jnp.exp2((s_own - m) * self.exp2_scale)
        return (jnp.sum(e_own, axis=-1, keepdims=True),
                jnp.dot(e_own.astype(BF16), self.vn_ref[...].astype(BF16), preferred_element_type=F32))

    def wait_v(self, c):
        self._wait_chunk(self.vc_ref, c)

    def v_pages(self, c, first, last, acc):
        assert first % self.pages_per_block == 0 and last % self.pages_per_block == 0
        for r0 in range(first, last, self.pages_per_block):
            shift = self.head_bias + (self.keep_bias[(c * CHUNK_PAGES + r0) // self.pages_per_block] - self.m)
            for r in range(r0, r0 + self.pages_per_block):
                e = [jnp.exp2((slab + shift) * self.exp2_scale) for slab in self._slabs(self.s_scr[c * CHUNK_PAGES + r])]
                self.lsum = self.lsum + functools.reduce(jnp.add, e)
                page = self.ring[(c % 2) * CHUNK_PAGES + r].astype(BF16)
                acc = acc + jnp.dot(jnp.concatenate(e, axis=1).astype(BF16), page, preferred_element_type=F32)
        return acc

    def refill_after_v(self, c):
        if c + 2 < self.n_chunks:
            self._start_chunk(self.vc_ref, self.seq, c + 2)
        else:
            @pl.when(self.seq + 1 < self.n_seqs)
            def _():
                self._start_chunk(self.kc_ref, self.seq + 1, c + 2 - self.n_chunks)

    def finish(self, l_own, acc):
        out = acc / (l_own + jnp.sum(self.lsum, axis=-1, keepdims=True))
        for hd in range(MOBA_HEADS):
            self.o_ref[:, hd * MOBA_HEAD_DIM:(hd + 1) * MOBA_HEAD_DIM] = out[hd * DEC_SEQ:(hd + 1) * DEC_SEQ]


def _layer_norm(x, g, b):
    mu = jnp.mean(x, axis=-1, keepdims=True)
    d = x - mu
    var = jnp.mean(d * d, axis=-1, keepdims=True)
    return d * lax.rsqrt(var + LN_EPS) * g + b


def _out_ffn_body(ar_ref, am_ref, x_ref, ga_ref, shf_ref, scf_ref, gf_ref, wo_ref, g1_ref, b1_ref,
                  wu_ref, wd_ref, g2_ref, b2_ref, y_ref, x1_scr, h_scr, acc_scr):
    c = pl.program_id(0)

    @pl.when(c == 0)
    def _():
        mixed = (jnp.dot(ar_ref[...].astype(BF16), wo_ref[:RET_WIDTH, :], preferred_element_type=F32)
                 + jnp.dot(am_ref[...].astype(BF16), wo_ref[RET_WIDTH:, :], preferred_element_type=F32))
        x1 = _layer_norm(ALPHA * x_ref[...] + _modulation(ga_ref, True) * mixed, g1_ref[...], b1_ref[...])
        x1_scr[...] = x1
        h_scr[...] = (x1 * (1.0 + _modulation(scf_ref, True)) + _modulation(shf_ref, True)).astype(BF16)
        acc_scr[...] = jnp.zeros_like(acc_scr)

    u = jnp.maximum(jnp.dot(h_scr[...], wu_ref[...], preferred_element_type=F32), 0.0)
    acc_scr[...] += jnp.dot((u * u).astype(BF16), wd_ref[...], preferred_element_type=F32)

    @pl.when(c == pl.num_programs(0) - 1)
    def _():
        y_ref[...] = _layer_norm(ALPHA * x1_scr[...] + _modulation(gf_ref, True) * acc_scr[...],
                                 g2_ref[...], b2_ref[...])


def _out_ffn(a_ret, a_moba, x2d, mod, weights):
    w_o, ln1_g, ln1_b, w_up, w_down, ln2_g, ln2_b = weights
    t = x2d.shape[0]
    whole = lambda a: pl.BlockSpec(a.shape, lambda c: (0,) * a.ndim)
    return pl.pallas_call(
        _out_ffn_body,
        grid=(D_FF // D_MODEL,),
        in_specs=[whole(a_ret), whole(a_moba), whole(x2d)] + [_mod_spec(term) for term in (2, 3, 4, 5)]
        + [whole(w_o), whole(ln1_g), whole(ln1_b),
           pl.BlockSpec((D_MODEL, D_MODEL), lambda c: (0, c)), pl.BlockSpec((D_MODEL, D_MODEL), lambda c: (c, 0)),
           whole(ln2_g), whole(ln2_b)],
        out_specs=pl.BlockSpec((t, D_MODEL), lambda c: (0, 0)),
        out_shape=jax.ShapeDtypeStruct((t, D_MODEL), F32),
        scratch_shapes=[pltpu.VMEM((t, D_MODEL), F32), pltpu.VMEM((t, D_MODEL), BF16), pltpu.VMEM((t, D_MODEL), F32)],
        compiler_params=pltpu.CompilerParams(dimension_semantics=("arbitrary",),
                                             vmem_limit_bytes=48 * 1024 * 1024),
        name="out_ffn",
    )(a_ret, a_moba, x2d, *([mod] * 4), *weights)


def _out_ffn_moba_body(pt_ref, ar_ref, am_ref, x_ref, ga_ref, shf_ref, scf_ref, gf_ref, wo_ref, g1_ref, b1_ref,
                       wu_ref, wd_ref, g2_ref, b2_ref, q_ref, kn_ref, vn_ref, kc_ref, vc_ref,
                       y_ref, o_ref, ring, sem, s_scr):
    n_seqs = pl.num_programs(0) * pl.num_programs(1)
    seq = pl.program_id(0) * pl.num_programs(1) + pl.program_id(1)
    sm = _SampleMoba(pt_ref, seq, n_seqs, q_ref, kn_ref, vn_ref, kc_ref, vc_ref, o_ref, ring, sem, s_scr)

    def up(c, h):
        u = jnp.maximum(jnp.dot(h, wu_ref[:, c * D_MODEL:(c + 1) * D_MODEL], preferred_element_type=F32), 0.0)
        return (u * u).astype(BF16)

    def down(c, u):
        return jnp.dot(u, wd_ref[c * D_MODEL:(c + 1) * D_MODEL, :], preferred_element_type=F32)

    half = CHUNK_PAGES // 2
    sm.prologue()

    sm.wait_k(0)
    sm.k_pages(0, 0, half)
    mixed = jnp.dot(ar_ref[...].astype(BF16), wo_ref[:RET_WIDTH, :], preferred_element_type=F32)
    sm.k_pages(0, half, CHUNK_PAGES)
    mixed = mixed + jnp.dot(am_ref[...].astype(BF16), wo_ref[RET_WIDTH:, :], preferred_element_type=F32)
    x1 = _layer_norm(ALPHA * x_ref[...] + _modulation(ga_ref, False) * mixed, g1_ref[...], b1_ref[...])
    h = (x1 * (1.0 + _modulation(scf_ref, False)) + _modulation(shf_ref, False)).astype(BF16)
    sm.refill_after_k(0)

    sm.wait_k(1)
    sm.k_pages(1, 0, half)
    u = up(0, h)
    sm.k_pages(1, half, CHUNK_PAGES)
    sm.refill_after_k(1)

    sm.wait_k(2)
    sm.k_pages(2, 0, half)
    acc = down(0, u)
    sm.k_pages(2, half, CHUNK_PAGES)
    sm.refill_after_k(2)

    sm.wait_k(3)
    sm.k_pages(3, 0, half)
    u = up(1, h)
    sm.k_pages(3, half, CHUNK_PAGES)
    sm.refill_after_k(3)

    l, acc_s = sm.select()
    acc = acc + down(1, u)

    sm.wait_v(0)
    acc_s = sm.v_pages(0, 0, half, acc_s)
    u = up(2, h)
    acc_s = sm.v_pages(0, half, CHUNK_PAGES, acc_s)
    sm.refill_after_v(0)

    sm.wait_v(1)
    acc_s = sm.v_pages(1, 0, half, acc_s)
    acc = acc + down(2, u)
    acc_s = sm.v_pages(1, half, CHUNK_PAGES, acc_s)
    sm.refill_after_v(1)

    sm.wait_v(2)
    acc_s = sm.v_pages(2, 0, half, acc_s)
    u = up(3, h)
    acc_s = sm.v_pages(2, half, CHUNK_PAGES, acc_s)
    sm.refill_after_v(2)

    sm.wait_v(3)
    acc_s = sm.v_pages(3, 0, half, acc_s)
    acc = acc + down(3, u)
    acc_s = sm.v_pages(3, half, CHUNK_PAGES, acc_s)
    sm.refill_after_v(3)
    y_ref[...] = _layer_norm(ALPHA * x1 + _modulation(gf_ref, False) * acc, g2_ref[...], b2_ref[...])
    sm.finish(l, acc_s)


def _out_ffn_moba(a_ret, a_moba, x2d, mod, weights, page_table, mq_s, kn2d, vn2d, cache_k2d, cache_v2d, tm):
    t = x2d.shape[0]
    nt = SEQ // tm
    assert BATCH * nt == DEC_BATCH
    n_rows = MOBA_HEADS * DEC_SEQ
    tile = lambda b, i, pt: (b * nt + i, 0)
    wide = lambda w: pl.BlockSpec((tm, w), tile)
    const = lambda a: pl.BlockSpec(a.shape, lambda b, i, pt: (0,) * a.ndim, pipeline_mode=pl.Buffered(1))
    seq_rows = lambda w, n: pl.BlockSpec((n, w), tile)
    hbm = pl.BlockSpec(memory_space=pl.ANY)
    grid_spec = pltpu.PrefetchScalarGridSpec(
        num_scalar_prefetch=1,
        grid=(BATCH, nt),
        in_specs=[wide(RET_WIDTH), wide(MOBA_WIDTH), wide(D_MODEL)] + [_mod_spec(term) for term in (2, 3, 4, 5)]
        + [const(a) for a in weights]
        + [seq_rows(MOBA_WIDTH, DEC_SEQ), seq_rows(LANES, n_rows), seq_rows(LANES, n_rows), hbm, hbm],
        out_specs=[wide(D_MODEL), seq_rows(MOBA_WIDTH, DEC_SEQ)],
        scratch_shapes=[pltpu.VMEM((RING_PAGES, PAGE_ROWS, LANES), F32),
                        pltpu.SemaphoreType.DMA((RING_PAGES // CHUNK_PAGES,)),
                        pltpu.VMEM((PAGES_PER_SEQ, n_rows, PAGE_ROWS), F32)],
    )
    return pl.pallas_call(
        _out_ffn_moba_body,
        grid_spec=grid_spec,
        out_shape=[jax.ShapeDtypeStruct((t, D_MODEL), F32),
                   jax.ShapeDtypeStruct((N_SAMPLE_TOK, MOBA_WIDTH), F32)],
        compiler_params=pltpu.CompilerParams(dimension_semantics=("arbitrary", "arbitrary"),
                                             vmem_limit_bytes=58 * 1024 * 1024),
        name="out_ffn_moba",
    )(page_table.reshape(-1), a_ret, a_moba, x2d, *([mod] * 4), *weights,
      mq_s, kn2d, vn2d, cache_k2d, cache_v2d)


def kernel(x_prompt, x_sample, cache_k, cache_v, state_ret, page_table, c_prompt, c_sample,
           w_ada, b_ada, w_in, w_o, ln1_g, ln1_b, w_up, w_down, ln2_g, ln2_b):
    n_prompt_tok = BATCH * SEQ
    past_len = page_table.shape[1] * PAGE_SIZE

    mod = _adaln(c_prompt, c_sample, w_ada[0], b_ada)

    tm = 512
    nt = SEQ // tm
    p_row = lambda b, i: (b * nt + i, 0)
    s_row = lambda i: (0, 0)

    rope_p = _rope_table(np.arange(SEQ, dtype=np.int32))
    rope_s = _rope_table(np.tile(past_len + np.arange(DEC_SEQ, dtype=np.int32), DEC_BATCH))

    xp = x_prompt.reshape(n_prompt_tok, D_MODEL)
    rq, rk, rv, rg, mq, k_p, v_p, w_o_b, w_up_b, w_down_b = _inproj(
        xp, mod, False, rope_p, w_in[0],
        (BATCH, nt), p_row, tm, BF16, casts=(w_o[0], w_up[0], w_down[0]))
    weights = (w_o_b, ln1_g, ln1_b, w_up_b, w_down_b, ln2_g, ln2_b)
    a_ret, state_p = _ret_prompt(rq, rk, rv, rg)
    a_moba = _moba_prompt(mq, k_p, v_p)

    xs = x_sample.reshape(N_SAMPLE_TOK, D_MODEL)
    rq_s, rk_s, rv_s, rg_s, mq_s, k_s, v_s = _inproj(
        xs, mod, True, rope_s, w_in[0],
        (1,), s_row, N_SAMPLE_TOK, F32)
    a_ret_s, state_s = _ret_sample(rq_s, rk_s, rv_s, rg_s, state_ret[0])

    cache_rows = cache_k.shape[1] * PAGE_ROWS
    y_p, a_moba_s = _out_ffn_moba(a_ret, a_moba, xp, mod, weights, page_table, mq_s, k_s, v_s,
                                  cache_k.reshape(cache_rows, LANES), cache_v.reshape(cache_rows, LANES), tm)
    y_s = _out_ffn(a_ret_s, a_moba_s, xs, mod, weights)

    kv_p_shape = (DEPTH, BATCH, SEQ, MOBA_HEADS, MOBA_HEAD_DIM)
    kv_s_shape = (DEPTH, DEC_BATCH, DEC_SEQ, MOBA_HEADS, MOBA_HEAD_DIM)
    return (y_p.reshape(BATCH, SEQ, D_MODEL),
            y_s.reshape(DEC_BATCH, DEC_SEQ, D_MODEL),
            k_p.reshape(kv_p_shape), v_p.reshape(kv_p_shape), state_p[None],
            k_s.reshape(kv_s_shape), v_s.reshape(kv_s_shape), state_s[None])
```

```python
import functools

import numpy as np
import jax
import jax.numpy as jnp
from jax import lax
from jax.experimental import pallas as pl
from jax.experimental.pallas import tpu as pltpu

F32 = jnp.float32
BF16 = jnp.bfloat16

D_MODEL = 1024
BATCH = 8
SEQ = 2048
DEC_BATCH = 32
DEC_SEQ = 8
PAGE_SIZE = 128
RET_HEADS = 4
RET_DK = 64
RET_DV = 128
RET_CHUNK = 128
MOBA_HEADS = 4
MOBA_HEAD_DIM = 128
MOBA_BLOCK = 256
MOBA_TOPK = 3
D_FF = 4 * D_MODEL
ROPE_THETA = 10000.0
LN_EPS = 1e-5
GN_EPS = 1e-6
DEPTH = 1
ALPHA = (2 * DEPTH) ** 0.25
RET_QK = RET_HEADS * RET_DK
RET_WIDTH = RET_HEADS * RET_DV
MOBA_WIDTH = MOBA_HEADS * MOBA_HEAD_DIM
IN_WIDTH = 2 * RET_QK + 2 * RET_WIDTH + 3 * MOBA_WIDTH
OFF_RQ, OFF_RK, OFF_RV, OFF_RG = 0, 256, 512, 1024
OFF_MQ, OFF_MK, OFF_MV = 1536, 2048, 2560
LANES = 128
N_SAMPLE_TOK = DEC_BATCH * DEC_SEQ
PAGES_PER_SEQ = 64
CHUNK_PAGES = 16
RING_PAGES = 2 * CHUNK_PAGES
PAGE_ROWS = PAGE_SIZE * MOBA_HEADS
NEG_INF = float("-inf")
LOG2_E = 1.4426950408889634

_NT = (((1,), (1,)), ((), ()))
_TN = (((0,), (0,)), ((), ()))


def _log_decay():
    return np.log1p(-np.exp2(-5.0 - np.arange(RET_HEADS, dtype=np.float64)))


N_MOD = 6
MOD_ROWS = BATCH + DEC_BATCH


def _adaln_body(cp_ref, cs_ref, w_ref, b_ref, o_ref):
    w = w_ref[...].astype(BF16)
    for c_ref, lo in ((cp_ref, 0), (cs_ref, BATCH)):
        c = c_ref[...]
        a = (c * jax.nn.sigmoid(c)).astype(BF16)
        o_ref[lo:lo + c.shape[0], :] = jnp.dot(a, w, preferred_element_type=F32) + b_ref[...]


def _adaln(c_prompt, c_sample, w_ada, b_ada):
    return pl.pallas_call(
        _adaln_body,
        grid=(N_MOD,),
        in_specs=[pl.BlockSpec(c_prompt.shape, lambda j: (0, 0)),
                  pl.BlockSpec(c_sample.shape, lambda j: (0, 0)),
                  pl.BlockSpec((D_MODEL, D_MODEL), lambda j: (0, j)),
                  pl.BlockSpec((1, D_MODEL), lambda j: (0, j))],
        out_specs=pl.BlockSpec((None, MOD_ROWS, D_MODEL), lambda j: (j, 0, 0)),
        out_shape=jax.ShapeDtypeStruct((N_MOD, MOD_ROWS, D_MODEL), F32),
        name="adaln",
    )(c_prompt, c_sample, w_ada, b_ada)


def _modulation(ref, sample_group):
    if sample_group:
        rows = ref[BATCH:BATCH + DEC_BATCH, :]
        return jnp.broadcast_to(rows[:, None, :], (DEC_BATCH, DEC_SEQ, D_MODEL)).reshape(N_SAMPLE_TOK, D_MODEL)
    return ref[pl.ds(pl.program_id(0), 1), :]


def _mod_spec(term):
    return pl.BlockSpec((None, MOD_ROWS, D_MODEL), lambda *_: (term, 0, 0))


def _rope_table(pos):
    slabs = []
    for head_dim in (MOBA_HEAD_DIM, RET_DK):
        half = head_dim // 2
        inv_freq = np.power(ROPE_THETA, -np.arange(half, dtype=np.float64) / half)
        ang = pos.astype(np.float64)[:, None] * inv_freq[None, :]
        cos, sin = np.cos(ang), np.sin(ang)
        reps = LANES // head_dim
        slabs += [np.tile(np.concatenate([cos, cos], axis=-1), (1, reps)),
                  np.tile(np.concatenate([-sin, sin], axis=-1), (1, reps))]
    return jnp.asarray(np.concatenate(slabs, axis=-1), dtype=F32)


def _inproj_body(sample_group, n_casts, x_ref, sc_ref, sh_ref, w_ref, rope_ref, *refs):
    cast_in, refs = refs[:n_casts], refs[n_casts:]
    rq_ref, rk_ref, rv_ref, rg_ref, mq_ref, ko_ref, vo_ref = refs[:7]
    for src, dst in zip(cast_in, refs[7:]):
        dst[...] = src[...].astype(dst.dtype)
    tm = x_ref.shape[0]
    h = (x_ref[...] * (1.0 + _modulation(sc_ref, sample_group)) + _modulation(sh_ref, sample_group)).astype(BF16)

    def proj(lo, width):
        return jnp.dot(h, w_ref[:, lo:lo + width].astype(BF16), preferred_element_type=F32)

    lane = lax.broadcasted_iota(jnp.int32, (tm, LANES), 1)
    low_half = (lane & (RET_DK - 1)) < (RET_DK // 2)
    pos = slice(None) if sample_group else pl.ds(pl.multiple_of(pl.program_id(1) * tm, tm), tm)
    cm, sm, cr, sr = (rope_ref[pos, j * LANES:(j + 1) * LANES] for j in range(4))

    def rope_ret(z):
        rot = jnp.where(low_half, pltpu.roll(z, LANES - RET_DK // 2, 1), pltpu.roll(z, RET_DK // 2, 1))
        return z * cr + rot * sr

    def rope_moba(z):
        return z * cm + pltpu.roll(z, MOBA_HEAD_DIM // 2, 1) * sm

    zq = proj(OFF_RQ, RET_QK)
    zk = proj(OFF_RK, RET_QK)
    for s in range(RET_QK // LANES):
        sl = slice(s * LANES, (s + 1) * LANES)
        rq_ref[:, sl] = rope_ret(zq[:, sl])
        rk_ref[:, sl] = rope_ret(zk[:, sl]) * (RET_DK ** -0.5)
    rv_ref[...] = proj(OFF_RV, RET_WIDTH).astype(rv_ref.dtype)
    rg_ref[...] = proj(OFF_RG, RET_WIDTH)
    zq = proj(OFF_MQ, MOBA_WIDTH)
    zk = proj(OFF_MK, MOBA_WIDTH)
    zv = proj(OFF_MV, MOBA_WIDTH)
    for hd in range(MOBA_HEADS):
        sl = slice(hd * LANES, (hd + 1) * LANES)
        mq_ref[:, sl] = rope_moba(zq[:, sl]).astype(mq_ref.dtype)
        ko_ref[pl.ds(hd, tm, stride=MOBA_HEADS), :] = rope_moba(zk[:, sl])
        vo_ref[pl.ds(hd, tm, stride=MOBA_HEADS), :] = zv[:, sl]


def _inproj(x2d, mod, sample_group, rope, w_in, grid, row_map, tm, act_dtype, casts=()):
    t = x2d.shape[0]
    rope_spec = pl.BlockSpec(rope.shape, lambda *_: (0, 0), pipeline_mode=pl.Buffered(1))
    n_steps = int(np.prod(grid))
    wide = lambda w: pl.BlockSpec((tm, w), row_map)
    cast_specs = [pl.BlockSpec((a.shape[0] // n_steps, a.shape[1]), row_map) for a in casts]
    w_spec = pl.BlockSpec((D_MODEL, IN_WIDTH), lambda *_: (0, 0), pipeline_mode=pl.Buffered(1))
    return pl.pallas_call(
        functools.partial(_inproj_body, sample_group, len(casts)),
        grid=grid,
        in_specs=[wide(D_MODEL), _mod_spec(1), _mod_spec(0), w_spec, rope_spec] + cast_specs,
        out_specs=[wide(RET_QK), wide(RET_QK), wide(RET_WIDTH), wide(RET_WIDTH), wide(MOBA_WIDTH),
                   pl.BlockSpec((tm * MOBA_HEADS, LANES), row_map),
                   pl.BlockSpec((tm * MOBA_HEADS, LANES), row_map)] + cast_specs,
        out_shape=[jax.ShapeDtypeStruct((t, RET_QK), F32), jax.ShapeDtypeStruct((t, RET_QK), F32),
                   jax.ShapeDtypeStruct((t, RET_WIDTH), act_dtype), jax.ShapeDtypeStruct((t, RET_WIDTH), F32),
                   jax.ShapeDtypeStruct((t, MOBA_WIDTH), act_dtype),
                   jax.ShapeDtypeStruct((t * MOBA_HEADS, LANES), F32),
                   jax.ShapeDtypeStruct((t * MOBA_HEADS, LANES), F32)]
        + [jax.ShapeDtypeStruct(a.shape, BF16) for a in casts],
        compiler_params=pltpu.CompilerParams(dimension_semantics=("arbitrary",) * len(grid),
                                             vmem_limit_bytes=48 * 1024 * 1024),
        name="inproj",
    )(x2d, mod, mod, w_in, rope, *casts)


def _group_norm_gate(o, g):
    mu = jnp.mean(o, axis=-1, keepdims=True)
    d = o - mu
    var = jnp.mean(d * d, axis=-1, keepdims=True)
    return d * lax.rsqrt(var + GN_EPS) * (g * jax.nn.sigmoid(g))


def _ret_prompt_tables():
    lg = _log_decay()
    i = np.arange(RET_CHUNK, dtype=np.float64)
    diff = i[:, None] - i[None, :]
    dmat = np.where(diff >= 0, np.exp(np.maximum(diff, 0.0)[None] * lg[:, None, None]), 0.0)
    lane_head = np.arange(RET_QK) // RET_DK
    qd = np.exp((i[:, None] + 1.0) * lg[lane_head][None, :])
    kd = np.exp((RET_CHUNK - 1.0 - i)[:, None] * lg[lane_head][None, :])
    row_head = np.arange(RET_QK) // RET_DK
    col_head = np.arange(RET_WIDTH) // RET_DV
    same = row_head[:, None] == col_head[None, :]
    cdec = np.where(same, np.exp(RET_CHUNK * lg[row_head])[:, None], 0.0)
    return [jnp.asarray(a, dtype=F32) for a in
            (dmat, np.concatenate([qd, kd], axis=1), np.concatenate([cdec, same.astype(np.float64)], axis=1))]


def _ret_prompt_body(q_ref, k_ref, v_ref, g_ref, dmat_ref, qkd_ref, sdm_ref, o_ref, st_ref, state_scr):
    lane_head = lax.broadcasted_iota(jnp.int32, (RET_CHUNK, RET_QK), 1) >> 6
    chunk_rows = lambda c: slice(c * RET_CHUNK, (c + 1) * RET_CHUNK)

    def first_matmuls(c):
        rows = chunk_rows(c)
        q = q_ref[rows, :]
        k = k_ref[rows, :]
        v = v_ref[rows, :]
        kb = k.astype(BF16)
        state = state_scr[...]
        scores = [lax.dot_general(jnp.where(lane_head == hd, q, 0.0).astype(BF16), kb, _NT,
                                  preferred_element_type=F32) for hd in range(RET_HEADS)]
        cross = jnp.dot((q * qkd_ref[:, :RET_QK]).astype(BF16), state.astype(BF16), preferred_element_type=F32)
        kv = lax.dot_general((k * qkd_ref[:, RET_QK:]).astype(BF16), v, _TN, preferred_element_type=F32)
        state_scr[...] = sdm_ref[:, :RET_WIDTH] * state + sdm_ref[:, RET_WIDTH:] * kv
        return scores, cross, v

    def second_matmuls(c, scores, cross, v):
        rows = chunk_rows(c)
        g = g_ref[rows, :]
        decayed = [(scores[hd] * dmat_ref[hd]).astype(BF16) for hd in range(RET_HEADS)]
        for hd in range(RET_HEADS):
            sl = slice(hd * RET_DV, (hd + 1) * RET_DV)
            inner = jnp.dot(decayed[hd], v[:, sl], preferred_element_type=F32)
            o_ref[rows, sl] = _group_norm_gate(inner + cross[:, sl], g[:, sl]).astype(o_ref.dtype)

    state_scr[...] = jnp.zeros_like(state_scr)
    n_chunks = SEQ // RET_CHUNK
    ahead = first_matmuls(0)
    for c in range(n_chunks):
        current = ahead
        if c + 1 < n_chunks:
            ahead = first_matmuls(c + 1)
        second_matmuls(c, *current)
    for hd in range(RET_HEADS):
        st_ref[hd] = state_scr[hd * RET_DK:(hd + 1) * RET_DK, hd * RET_DV:(hd + 1) * RET_DV]


def _ret_prompt(rq, rk, rv, rg):
    tabs = _ret_prompt_tables()
    seq = lambda w: pl.BlockSpec((SEQ, w), lambda b: (b, 0))
    const = lambda a: pl.BlockSpec(a.shape, lambda b: (0,) * a.ndim)
    return pl.pallas_call(
        _ret_prompt_body,
        grid=(BATCH,),
        in_specs=[seq(RET_QK), seq(RET_QK), seq(RET_WIDTH), seq(RET_WIDTH)] + [const(a) for a in tabs],
        out_specs=[seq(RET_WIDTH), pl.BlockSpec((None, RET_HEADS, RET_DK, RET_DV), lambda b: (b, 0, 0, 0))],
        out_shape=[jax.ShapeDtypeStruct((BATCH * SEQ, RET_WIDTH), BF16),
                   jax.ShapeDtypeStruct((BATCH, RET_HEADS, RET_DK, RET_DV), F32)],
        scratch_shapes=[pltpu.VMEM((RET_QK, RET_WIDTH), F32)],
        compiler_params=pltpu.CompilerParams(dimension_semantics=("arbitrary",),
                                             vmem_limit_bytes=48 * 1024 * 1024),
        name="ret_prompt",
    )(rq, rk, rv, rg, *tabs)


def _ret_sample_tables():
    lg = _log_decay()
    t = np.arange(N_SAMPLE_TOK) % DEC_SEQ
    seq_id = np.arange(N_SAMPLE_TOK) // DEC_SEQ
    diff = (t[:, None] - t[None, :]).astype(np.float64)
    same_seq = seq_id[:, None] == seq_id[None, :]
    dmat = np.where(same_seq[None] & (diff >= 0)[None],
                    np.exp(np.maximum(diff, 0.0)[None] * lg[:, None, None]), 0.0)
    lane_head = np.arange(RET_QK) // RET_DK
    qd = np.exp((t[:, None] + 1.0) * lg[lane_head][None, :])
    kd = np.exp((DEC_SEQ - 1.0 - t)[:, None] * lg[lane_head][None, :])
    return [jnp.asarray(a, dtype=F32) for a in (dmat, np.concatenate([qd, kd], axis=1))]


def _ret_sample_body(q_ref, k_ref, v_ref, g_ref, st_ref, dmat_ref, qkd_ref, o_ref, sto_ref):
    lg = _log_decay()
    q = q_ref[...]
    k = k_ref[...]
    kb = k.astype(BF16)
    qdec = q * qkd_ref[:, :RET_QK]
    kdec = k * qkd_ref[:, RET_QK:]
    vb = v_ref[...].astype(BF16)
    g = g_ref[...]
    lane = lax.broadcasted_iota(jnp.int32, (N_SAMPLE_TOK, LANES), 1)
    lane_head = lax.broadcasted_iota(jnp.int32, (N_SAMPLE_TOK, RET_QK), 1) >> 6
    n_state_rows = DEC_BATCH * RET_DK
    own_seq = ((lax.broadcasted_iota(jnp.int32, (N_SAMPLE_TOK, n_state_rows), 0) >> 3)
               == (lax.broadcasted_iota(jnp.int32, (N_SAMPLE_TOK, n_state_rows), 1) >> 6))
    for hd in range(RET_HEADS):
        sl = slice(hd * RET_DV, (hd + 1) * RET_DV)
        qm = jnp.where(lane_head == hd, q, 0.0).astype(BF16)
        s = lax.dot_general(qm, kb, _NT, preferred_element_type=F32) * dmat_ref[hd]
        inner = jnp.dot(s.astype(BF16), vb[:, sl], preferred_element_type=F32)

        def expand(z):
            slab = z[:, (hd // 2) * LANES:(hd // 2 + 1) * LANES]
            other = pltpu.roll(slab, RET_DK, 1)
            in_low = lane < RET_DK
            both = jnp.where(in_low, slab, other) if hd % 2 == 0 else jnp.where(in_low, other, slab)
            tiled = jnp.concatenate([both] * (n_state_rows // LANES), axis=1)
            return jnp.where(own_seq, tiled, 0.0).astype(BF16)

        st = st_ref[:, hd].reshape(n_state_rows, RET_DV)
        cross = jnp.dot(expand(qdec), st.astype(BF16), preferred_element_type=F32)
        o_ref[:, sl] = _group_norm_gate(inner + cross, g[:, sl])
        kv = lax.dot_general(expand(kdec), vb[:, sl], _TN, preferred_element_type=F32)
        new = float(np.exp(DEC_SEQ * lg[hd])) * st + kv
        sto_ref[:, hd] = new.reshape(DEC_BATCH, RET_DK, RET_DV)


def _ret_sample(rq, rk, rv, rg, state):
    tabs = _ret_sample_tables()
    full = lambda a: pl.BlockSpec(a.shape, lambda i: (0,) * a.ndim)
    args = (rq, rk, rv, rg, state, *tabs)
    return pl.pallas_call(
        _ret_sample_body,
        grid=(1,),
        in_specs=[full(a) for a in args],
        out_specs=[pl.BlockSpec((N_SAMPLE_TOK, RET_WIDTH), lambda i: (0, 0)),
                   pl.BlockSpec(state.shape, lambda i: (0, 0, 0, 0))],
        out_shape=[jax.ShapeDtypeStruct((N_SAMPLE_TOK, RET_WIDTH), F32),
                   jax.ShapeDtypeStruct(state.shape, F32)],
        compiler_params=pltpu.CompilerParams(dimension_semantics=("arbitrary",),
                                             vmem_limit_bytes=56 * 1024 * 1024),
        name="ret_sample",
    )(*args)


def _moba_prompt_body(q_ref, k_ref, v_ref, o_ref):
    hd = pl.program_id(1)
    n_blocks = SEQ // MOBA_BLOCK
    exp2_scale = MOBA_HEAD_DIM ** -0.5 * LOG2_E
    k32 = k_ref[pl.ds(hd, SEQ, stride=MOBA_HEADS), :]
    kb = k32.astype(BF16)
    vt = v_ref[pl.ds(hd, SEQ, stride=MOBA_HEADS), :].T.astype(BF16)
    kmean = jnp.sum(k32.reshape(n_blocks, MOBA_BLOCK, MOBA_HEAD_DIM), axis=1) * (1.0 / MOBA_BLOCK)
    kmb = kmean.astype(BF16)
    key_id = lax.broadcasted_iota(jnp.int32, (MOBA_BLOCK, MOBA_BLOCK), 0)
    qry_id = lax.broadcasted_iota(jnp.int32, (MOBA_BLOCK, MOBA_BLOCK), 1)
    causal = key_id <= qry_id

    blk = lambda n: slice(n * MOBA_BLOCK, (n + 1) * MOBA_BLOCK)

    def score_matmuls(i):
        qi = q_ref[blk(i), :]
        st = [lax.dot_general(kb[blk(n)], qi, _NT, preferred_element_type=F32) for n in range(i + 1)]
        gt = lax.dot_general(kmb, qi, _NT, preferred_element_type=F32) if i > MOBA_TOPK else None
        return st, gt

    ahead = score_matmuls(n_blocks - 1)
    for i in reversed(range(n_blocks)):
        st, gt = ahead
        if i > 0:
            ahead = score_matmuls(i - 1)
        st[i] = jnp.where(causal, st[i], NEG_INF)
        if i > MOBA_TOPK:
            for n in range(i):
                beats = jnp.zeros((1, MOBA_BLOCK), F32)
                for mm in range(i):
                    if mm == n:
                        continue
                    win = (gt[mm:mm + 1] >= gt[n:n + 1]) if mm < n else (gt[mm:mm + 1] > gt[n:n + 1])
                    beats = beats + win.astype(F32)
                st[n] = st[n] + jnp.where(beats < MOBA_TOPK, 0.0, NEG_INF)
        m = functools.reduce(jnp.maximum, [jnp.max(s, axis=0, keepdims=True) for s in st])
        l = jnp.zeros((1, MOBA_BLOCK), F32)
        acc = jnp.zeros((MOBA_HEAD_DIM, MOBA_BLOCK), F32)
        for n in range(i + 1):
            e = jnp.exp2((st[n] - m) * exp2_scale)
            l = l + jnp.sum(e, axis=0, keepdims=True)
            acc = acc + jnp.dot(vt[:, blk(n)], e.astype(BF16), preferred_element_type=F32)
        o_ref[blk(i), :] = (acc / l).T.astype(o_ref.dtype)


def _moba_prompt(mq, k2d, v2d):
    kv_spec = pl.BlockSpec((SEQ * MOBA_HEADS, LANES), lambda b, h: (b, 0))
    return pl.pallas_call(
        _moba_prompt_body,
        grid=(BATCH, MOBA_HEADS),
        in_specs=[pl.BlockSpec((SEQ, MOBA_HEAD_DIM), lambda b, h: (b, h)), kv_spec, kv_spec],
        out_specs=pl.BlockSpec((SEQ, MOBA_HEAD_DIM), lambda b, h: (b, h)),
        out_shape=jax.ShapeDtypeStruct((BATCH * SEQ, MOBA_WIDTH), BF16),
        compiler_params=pltpu.CompilerParams(dimension_semantics=("arbitrary", "arbitrary"),
                                             vmem_limit_bytes=56 * 1024 * 1024),
        name="moba_prompt",
    )(mq, k2d, v2d)


class _SampleMoba:
    n_chunks = PAGES_PER_SEQ // CHUNK_PAGES
    n_rows = MOBA_HEADS * DEC_SEQ
    n_blocks = PAGES_PER_SEQ * PAGE_SIZE // MOBA_BLOCK
    pages_per_block = MOBA_BLOCK // PAGE_SIZE

    def __init__(self, pt_ref, seq, n_seqs, q_ref, kn_ref, vn_ref, kc_ref, vc_ref, o_ref, ring, sem, s_scr):
        self.pt_ref, self.seq, self.n_seqs = pt_ref, seq, n_seqs
        self.q_ref, self.kn_ref, self.vn_ref = q_ref, kn_ref, vn_ref
        self.kc_ref, self.vc_ref, self.o_ref = kc_ref, vc_ref, o_ref
        self.ring, self.sem, self.s_scr = ring, sem, s_scr

    def _page_copy(self, cache_ref, row0, slot):
        return pltpu.make_async_copy(cache_ref.at[pl.ds(row0, PAGE_ROWS)], self.ring.at[slot],
                                     self.sem.at[slot // CHUNK_PAGES])

    def _start_chunk(self, cache_ref, seq, chunk):
        for r in range(CHUNK_PAGES):
            page = self.pt_ref[seq * PAGES_PER_SEQ + chunk * CHUNK_PAGES + r]
            self._page_copy(cache_ref, pl.multiple_of(page * PAGE_ROWS, PAGE_ROWS),
                            (chunk % 2) * CHUNK_PAGES + r).start()

    def _wait_chunk(self, cache_ref, chunk):
        for r in range(CHUNK_PAGES):
            self._page_copy(cache_ref, 0, (chunk % 2) * CHUNK_PAGES + r).wait()

    def prologue(self):
        @pl.when(self.seq == 0)
        def _():
            self._start_chunk(self.kc_ref, self.seq, 0)
            self._start_chunk(self.kc_ref, self.seq, 1)

        self.q = jnp.concatenate(
            [self.q_ref[:, hd * MOBA_HEAD_DIM:(hd + 1) * MOBA_HEAD_DIM] for hd in range(MOBA_HEADS)], axis=0
        ).astype(BF16)
        row_head = lax.broadcasted_iota(jnp.int32, (self.n_rows, LANES), 0) >> 3
        col_head = lax.broadcasted_iota(jnp.int32, (self.n_rows, LANES), 1) & (MOBA_HEADS - 1)
        self.same_head = row_head == col_head
        self.head_bias = jnp.where(self.same_head, 0.0, NEG_INF)
        self.block_sum, self.block_max = [], []

    @staticmethod
    def _slabs(x):
        return [x[:, j * LANES:(j + 1) * LANES] for j in range(PAGE_ROWS // LANES)]

    def wait_k(self, c):
        self._wait_chunk(self.kc_ref, c)

    def k_pages(self, c, first, last):
        assert first % self.pages_per_block == 0 and last % self.pages_per_block == 0
        for r0 in range(first, last, self.pages_per_block):
            tot = jnp.zeros((self.n_rows, LANES), F32)
            top = jnp.full((self.n_rows, LANES), NEG_INF, F32)
            for r in range(r0, r0 + self.pages_per_block):
                page = self.ring[(c % 2) * CHUNK_PAGES + r].astype(BF16)
                s = lax.dot_general(self.q, page, _NT, preferred_element_type=F32)
                self.s_scr[c * CHUNK_PAGES + r] = s
                for slab in self._slabs(s):
                    tot = tot + jnp.where(self.same_head, slab, 0.0)
                    top = jnp.maximum(top, slab + self.head_bias)
            self.block_sum.append(jnp.sum(tot, axis=-1, keepdims=True))
            self.block_max.append(jnp.max(top, axis=-1, keepdims=True))

    def refill_after_k(self, c):
        if c + 2 < self.n_chunks:
            self._start_chunk(self.kc_ref, self.seq, c + 2)
        else:
            self._start_chunk(self.vc_ref, self.seq, c + 2 - self.n_chunks)

    def select(self):
        n_rows, gs = self.n_rows, self.block_sum
        self.exp2_scale = MOBA_HEAD_DIM ** -0.5 * LOG2_E
        lane = lax.broadcasted_iota(jnp.int32, (n_rows, LANES), 1)
        g_all = jnp.full((n_rows, LANES), NEG_INF, F32)
        for n in range(self.n_blocks):
            g_all = jnp.where(lane == n, gs[n], g_all)
        self.keep_bias = []
        for n in range(self.n_blocks):
            wins = (g_all > gs[n]) | ((g_all == gs[n]) & (lane < n))
            beats = jnp.sum(wins.astype(F32), axis=-1, keepdims=True)
            self.keep_bias.append(jnp.where(beats < MOBA_TOPK, 0.0, NEG_INF))
        s_own = lax.dot_general(self.q, self.kn_ref[...].astype(BF16), _NT, preferred_element_type=F32)
        r_id = lax.broadcasted_iota(jnp.int32, (n_rows, n_rows), 0)
        c_id = lax.broadcasted_iota(jnp.int32, (n_rows, n_rows), 1)
        own_ok = ((c_id & (MOBA_HEADS - 1)) == (r_id >> 3)) & ((c_id >> 2) <= (r_id & (DEC_SEQ - 1)))
        s_own = jnp.where(own_ok, s_own, NEG_INF)
        m = jnp.max(s_own, axis=-1, keepdims=True)
        for n in range(self.n_blocks):
            m = jnp.maximum(m, self.block_max[n] + self.keep_bias[n])
        self.m = m
        self.lsum = jnp.zeros((n_rows, LANES), F32)
        e_own = jnp.exp2((s_own - m) * self.exp2_scale)
        return (jnp.sum(e_own, axis=-1, keepdims=True),
                jnp.dot(e_own.astype(BF16), self.vn_ref[...].astype(BF16), preferred_element_type=F32))

    def wait_v(self, c):
        self._wait_chunk(self.vc_ref, c)

    def v_pages(self, c, first, last, acc):
        assert first % self.pages_per_block == 0 and last % self.pages_per_block == 0
        for r0 in range(first, last, self.pages_per_block):
            shift = self.head_bias + (self.keep_bias[(c * CHUNK_PAGES + r0) // self.pages_per_block] - self.m)
            for r in range(r0, r0 + self.pages_per_block):
                e = [jnp.exp2((slab + shift) * self.exp2_scale) for slab in self._slabs(self.s_scr[c * CHUNK_PAGES + r])]
                self.lsum = self.lsum + functools.reduce(jnp.add, e)
                page = self.ring[(c % 2) * CHUNK_PAGES + r].astype(BF16)
                acc = acc + jnp.dot(jnp.concatenate(e, axis=1).astype(BF16), page, preferred_element_type=F32)
        return acc

    def refill_after_v(self, c):
        if c + 2 < self.n_chunks:
            self._start_chunk(self.vc_ref, self.seq, c + 2)
        else:
            @pl.when(self.seq + 1 < self.n_seqs)
            def _():
                self._start_chunk(self.kc_ref, self.seq + 1, c + 2 - self.n_chunks)

    def finish(self, l_own, acc):
        out = acc / (l_own + jnp.sum(self.lsum, axis=-1, keepdims=True))
        for hd in range(MOBA_HEADS):
            self.o_ref[:, hd * MOBA_HEAD_DIM:(hd + 1) * MOBA_HEAD_DIM] = out[hd * DEC_SEQ:(hd + 1) * DEC_SEQ]


def _layer_norm(x, g, b):
    mu = jnp.mean(x, axis=-1, keepdims=True)
    d = x - mu
    var = jnp.mean(d * d, axis=-1, keepdims=True)
    return d * lax.rsqrt(var + LN_EPS) * g + b


def _out_ffn_body(ar_ref, am_ref, x_ref, ga_ref, shf_ref, scf_ref, gf_ref, wo_ref, g1_ref, b1_ref,
                  wu_ref, wd_ref, g2_ref, b2_ref, y_ref, x1_scr, h_scr, acc_scr):
    c = pl.program_id(0)

    @pl.when(c == 0)
    def _():
        mixed = (jnp.dot(ar_ref[...].astype(BF16), wo_ref[:RET_WIDTH, :], preferred_element_type=F32)
                 + jnp.dot(am_ref[...].astype(BF16), wo_ref[RET_WIDTH:, :], preferred_element_type=F32))
        x1 = _layer_norm(ALPHA * x_ref[...] + _modulation(ga_ref, True) * mixed, g1_ref[...], b1_ref[...])
        x1_scr[...] = x1
        h_scr[...] = (x1 * (1.0 + _modulation(scf_ref, True)) + _modulation(shf_ref, True)).astype(BF16)
        acc_scr[...] = jnp.zeros_like(acc_scr)

    u = jnp.maximum(jnp.dot(h_scr[...], wu_ref[...], preferred_element_type=F32), 0.0)
    acc_scr[...] += jnp.dot((u * u).astype(BF16), wd_ref[...], preferred_element_type=F32)

    @pl.when(c == pl.num_programs(0) - 1)
    def _():
        y_ref[...] = _layer_norm(ALPHA * x1_scr[...] + _modulation(gf_ref, True) * acc_scr[...],
                                 g2_ref[...], b2_ref[...])


def _out_ffn(a_ret, a_moba, x2d, mod, weights):
    w_o, ln1_g, ln1_b, w_up, w_down, ln2_g, ln2_b = weights
    t = x2d.shape[0]
    whole = lambda a: pl.BlockSpec(a.shape, lambda c: (0,) * a.ndim)
    return pl.pallas_call(
        _out_ffn_body,
        grid=(D_FF // D_MODEL,),
        in_specs=[whole(a_ret), whole(a_moba), whole(x2d)] + [_mod_spec(term) for term in (2, 3, 4, 5)]
        + [whole(w_o), whole(ln1_g), whole(ln1_b),
           pl.BlockSpec((D_MODEL, D_MODEL), lambda c: (0, c)), pl.BlockSpec((D_MODEL, D_MODEL), lambda c: (c, 0)),
           whole(ln2_g), whole(ln2_b)],
        out_specs=pl.BlockSpec((t, D_MODEL), lambda c: (0, 0)),
        out_shape=jax.ShapeDtypeStruct((t, D_MODEL), F32),
        scratch_shapes=[pltpu.VMEM((t, D_MODEL), F32), pltpu.VMEM((t, D_MODEL), BF16), pltpu.VMEM((t, D_MODEL), F32)],
        compiler_params=pltpu.CompilerParams(dimension_semantics=("arbitrary",),
                                             vmem_limit_bytes=48 * 1024 * 1024),
        name="out_ffn",
    )(a_ret, a_moba, x2d, *([mod] * 4), *weights)


def _out_ffn_moba_body(pt_ref, ar_ref, am_ref, x_ref, ga_ref, shf_ref, scf_ref, gf_ref, wo_ref, g1_ref, b1_ref,
                       wu_ref, wd_ref, g2_ref, b2_ref, q_ref, kn_ref, vn_ref, kc_ref, vc_ref,
                       y_ref, o_ref, ring, sem, s_scr):
    n_seqs = pl.num_programs(0) * pl.num_programs(1)
    seq = pl.program_id(0) * pl.num_programs(1) + pl.program_id(1)
    sm = _SampleMoba(pt_ref, seq, n_seqs, q_ref, kn_ref, vn_ref, kc_ref, vc_ref, o_ref, ring, sem, s_scr)

    def up(c, h):
        u = jnp.maximum(jnp.dot(h, wu_ref[:, c * D_MODEL:(c + 1) * D_MODEL], preferred_element_type=F32), 0.0)
        return (u * u).astype(BF16)

    def down(c, u):
        return jnp.dot(u, wd_ref[c * D_MODEL:(c + 1) * D_MODEL, :], preferred_element_type=F32)

    half = CHUNK_PAGES // 2
    sm.prologue()

    sm.wait_k(0)
    sm.k_pages(0, 0, half)
    mixed = jnp.dot(ar_ref[...].astype(BF16), wo_ref[:RET_WIDTH, :], preferred_element_type=F32)
    sm.k_pages(0, half, CHUNK_PAGES)
    mixed = mixed + jnp.dot(am_ref[...].astype(BF16), wo_ref[RET_WIDTH:, :], preferred_element_type=F32)
    x1 = _layer_norm(ALPHA * x_ref[...] + _modulation(ga_ref, False) * mixed, g1_ref[...], b1_ref[...])
    h = (x1 * (1.0 + _modulation(scf_ref, False)) + _modulation(shf_ref, False)).astype(BF16)
    sm.refill_after_k(0)

    sm.wait_k(1)
    sm.k_pages(1, 0, half)
    u = up(0, h)
    sm.k_pages(1, half, CHUNK_PAGES)
    sm.refill_after_k(1)

    sm.wait_k(2)
    sm.k_pages(2, 0, half)
    acc = down(0, u)
    sm.k_pages(2, half, CHUNK_PAGES)
    sm.refill_after_k(2)

    sm.wait_k(3)
    sm.k_pages(3, 0, half)
    u = up(1, h)
    sm.k_pages(3, half, CHUNK_PAGES)
    sm.refill_after_k(3)

    l, acc_s = sm.select()
    acc = acc + down(1, u)

    sm.wait_v(0)
    acc_s = sm.v_pages(0, 0, half, acc_s)
    u = up(2, h)
    acc_s = sm.v_pages(0, half, CHUNK_PAGES, acc_s)
    sm.refill_after_v(0)

    sm.wait_v(1)
    acc_s = sm.v_pages(1, 0, half, acc_s)
    acc = acc + down(2, u)
    acc_s = sm.v_pages(1, half, CHUNK_PAGES, acc_s)
    sm.refill_after_v(1)

    sm.wait_v(2)
    acc_s = sm.v_pages(2, 0, half, acc_s)
    u = up(3, h)
    acc_s = sm.v_pages(2, half, CHUNK_PAGES, acc_s)
    sm.refill_after_v(2)

    sm.wait_v(3)
    acc_s = sm.v_pages(3, 0, half, acc_s)
    acc = acc + down(3, u)
    acc_s = sm.v_pages(3, half, CHUNK_PAGES, acc_s)
    sm.refill_after_v(3)
    y_ref[...] = _layer_norm(ALPHA * x1 + _modulation(gf_ref, False) * acc, g2_ref[...], b2_ref[...])
    sm.finish(l, acc_s)


def _out_ffn_moba(a_ret, a_moba, x2d, mod, weights, page_table, mq_s, kn2d, vn2d, cache_k2d, cache_v2d, tm):
    t = x2d.shape[0]
    nt = SEQ // tm
    assert BATCH * nt == DEC_BATCH
    n_rows = MOBA_HEADS * DEC_SEQ
    tile = lambda b, i, pt: (b * nt + i, 0)
    wide = lambda w: pl.BlockSpec((tm, w), tile)
    const = lambda a: pl.BlockSpec(a.shape, lambda b, i, pt: (0,) * a.ndim, pipeline_mode=pl.Buffered(1))
    seq_rows = lambda w, n: pl.BlockSpec((n, w), tile)
    hbm = pl.BlockSpec(memory_space=pl.ANY)
    grid_spec = pltpu.PrefetchScalarGridSpec(
        num_scalar_prefetch=1,
        grid=(BATCH, nt),
        in_specs=[wide(RET_WIDTH), wide(MOBA_WIDTH), wide(D_MODEL)] + [_mod_spec(term) for term in (2, 3, 4, 5)]
        + [const(a) for a in weights]
        + [seq_rows(MOBA_WIDTH, DEC_SEQ), seq_rows(LANES, n_rows), seq_rows(LANES, n_rows), hbm, hbm],
        out_specs=[wide(D_MODEL), seq_rows(MOBA_WIDTH, DEC_SEQ)],
        scratch_shapes=[pltpu.VMEM((RING_PAGES, PAGE_ROWS, LANES), F32),
                        pltpu.SemaphoreType.DMA((RING_PAGES // CHUNK_PAGES,)),
                        pltpu.VMEM((PAGES_PER_SEQ, n_rows, PAGE_ROWS), F32)],
    )
    return pl.pallas_call(
        _out_ffn_moba_body,
        grid_spec=grid_spec,
        out_shape=[jax.ShapeDtypeStruct((t, D_MODEL), F32),
                   jax.ShapeDtypeStruct((N_SAMPLE_TOK, MOBA_WIDTH), F32)],
        compiler_params=pltpu.CompilerParams(dimension_semantics=("arbitrary", "arbitrary"),
                                             vmem_limit_bytes=58 * 1024 * 1024),
        name="out_ffn_moba",
    )(page_table.reshape(-1), a_ret, a_moba, x2d, *([mod] * 4), *weights,
      mq_s, kn2d, vn2d, cache_k2d, cache_v2d)


def kernel(x_prompt, x_sample, cache_k, cache_v, state_ret, page_table, c_prompt, c_sample,
           w_ada, b_ada, w_in, w_o, ln1_g, ln1_b, w_up, w_down, ln2_g, ln2_b):
    n_prompt_tok = BATCH * SEQ
    past_len = page_table.shape[1] * PAGE_SIZE

    mod = _adaln(c_prompt, c_sample, w_ada[0], b_ada)

    tm = 512
    nt = SEQ // tm
    p_row = lambda b, i: (b * nt + i, 0)
    s_row = lambda i: (0, 0)

    rope_p = _rope_table(np.arange(SEQ, dtype=np.int32))
    rope_s = _rope_table(np.tile(past_len + np.arange(DEC_SEQ, dtype=np.int32), DEC_BATCH))

    xp = x_prompt.reshape(n_prompt_tok, D_MODEL)
    rq, rk, rv, rg, mq, k_p, v_p, w_o_b, w_up_b, w_down_b = _inproj(
        xp, mod, False, rope_p, w_in[0],
        (BATCH, nt), p_row, tm, BF16, casts=(w_o[0], w_up[0], w_down[0]))
    weights = (w_o_b, ln1_g, ln1_b, w_up_b, w_down_b, ln2_g, ln2_b)
    a_ret, state_p = _ret_prompt(rq, rk, rv, rg)
    a_moba = _moba_prompt(mq, k_p, v_p)

    xs = x_sample.reshape(N_SAMPLE_TOK, D_MODEL)
    rq_s, rk_s, rv_s, rg_s, mq_s, k_s, v_s = _inproj(
        xs, mod, True, rope_s, w_in[0],
        (1,), s_row, N_SAMPLE_TOK, F32)
    a_ret_s, state_s = _ret_sample(rq_s, rk_s, rv_s, rg_s, state_ret[0])

    cache_rows = cache_k.shape[1] * PAGE_ROWS
    y_p, a_moba_s = _out_ffn_moba(a_ret, a_moba, xp, mod, weights, page_table, mq_s, k_s, v_s,
                                  cache_k.reshape(cache_rows, LANES), cache_v.reshape(cache_rows, LANES), tm)
    y_s = _out_ffn(a_ret_s, a_moba_s, xs, mod, weights)

    kv_p_shape = (DEPTH, BATCH, SEQ, MOBA_HEADS, MOBA_HEAD_DIM)
    kv_s_shape = (DEPTH, DEC_BATCH, DEC_SEQ, MOBA_HEADS, MOBA_HEAD_DIM)
    return (y_p.reshape(BATCH, SEQ, D_MODEL),
            y_s.reshape(DEC_BATCH, DEC_SEQ, D_MODEL),
            k_p.reshape(kv_p_shape), v_p.reshape(kv_p_shape), state_p[None],
            k_s.reshape(kv_s_shape), v_s.reshape(kv_s_shape), state_s[None])
```

```python
import functools

import numpy as np
import jax
import jax.numpy as jnp
from jax import lax
from jax.experimental import pallas as pl
from jax.experimental.pallas import tpu as pltpu

F32 = jnp.float32
BF16 = jnp.bfloat16

D_MODEL = 1024
BATCH = 8
SEQ = 2048
DEC_BATCH = 32
DEC_SEQ = 8
PAGE_SIZE = 128
RET_HEADS = 4
RET_DK = 64
RET_DV = 128
RET_CHUNK = 128
MOBA_HEADS = 4
MOBA_HEAD_DIM = 128
MOBA_BLOCK = 256
MOBA_TOPK = 3
D_FF = 4 * D_MODEL
ROPE_THETA = 10000.0
LN_EPS = 1e-5
GN_EPS = 1e-6
DEPTH = 1
ALPHA = (2 * DEPTH) ** 0.25
RET_QK = RET_HEADS * RET_DK
RET_WIDTH = RET_HEADS * RET_DV
MOBA_WIDTH = MOBA_HEADS * MOBA_HEAD_DIM
IN_WIDTH = 2 * RET_QK + 2 * RET_WIDTH + 3 * MOBA_WIDTH
OFF_RQ, OFF_RK, OFF_RV, OFF_RG = 0, 256, 512, 1024
OFF_MQ, OFF_MK, OFF_MV = 1536, 2048, 2560
LANES = 128
VMEM_BYTES_V7X = 64 * 1024 * 1024
VMEM_LIMIT = VMEM_BYTES_V7X * 3 // 4
VMEM_LIMIT_FUSED = VMEM_BYTES_V7X * 29 // 32
N_SAMPLE_TOK = DEC_BATCH * DEC_SEQ
PAGES_PER_SEQ = 64
CHUNK_PAGES = 16
RING_PAGES = 2 * CHUNK_PAGES
PAGE_ROWS = PAGE_SIZE * MOBA_HEADS
LOG2_RET_DK = RET_DK.bit_length() - 1
LOG2_DEC_SEQ = DEC_SEQ.bit_length() - 1
LOG2_MOBA_HEADS = MOBA_HEADS.bit_length() - 1
NEG_INF = float("-inf")
LOG2_E = 1.4426950408889634

_NT = (((1,), (1,)), ((), ()))
_TN = (((0,), (0,)), ((), ()))


def _log_decay():
    return np.log1p(-np.exp2(-5.0 - np.arange(RET_HEADS, dtype=np.float64)))


N_MOD = 6
MOD_ROWS = BATCH + DEC_BATCH


def _adaln_body(cp_ref, cs_ref, w_ref, b_ref, o_ref):
    w = w_ref[...].astype(BF16)
    for c_ref, lo in ((cp_ref, 0), (cs_ref, BATCH)):
        c = c_ref[...]
        a = (c * jax.nn.sigmoid(c)).astype(BF16)
        o_ref[lo:lo + c.shape[0], :] = jnp.dot(a, w, preferred_element_type=F32) + b_ref[...]


def _adaln(c_prompt, c_sample, w_ada, b_ada):
    return pl.pallas_call(
        _adaln_body,
        grid=(N_MOD,),
        in_specs=[pl.BlockSpec(c_prompt.shape, lambda j: (0, 0)),
                  pl.BlockSpec(c_sample.shape, lambda j: (0, 0)),
                  pl.BlockSpec((D_MODEL, D_MODEL), lambda j: (0, j)),
                  pl.BlockSpec((1, D_MODEL), lambda j: (0, j))],
        out_specs=pl.BlockSpec((None, MOD_ROWS, D_MODEL), lambda j: (j, 0, 0)),
        out_shape=jax.ShapeDtypeStruct((N_MOD, MOD_ROWS, D_MODEL), F32),
        name="adaln",
    )(c_prompt, c_sample, w_ada, b_ada)


def _modulation(ref, sample_group):
    if sample_group:
        rows = ref[BATCH:BATCH + DEC_BATCH, :]
        return jnp.broadcast_to(rows[:, None, :], (DEC_BATCH, DEC_SEQ, D_MODEL)).reshape(N_SAMPLE_TOK, D_MODEL)
    return ref[pl.ds(pl.program_id(0), 1), :]


def _mod_spec(term):
    return pl.BlockSpec((None, MOD_ROWS, D_MODEL), lambda *_: (term, 0, 0))


def _rope_tables(pos, head_dim):
    half = head_dim // 2
    inv_freq = np.power(ROPE_THETA, -np.arange(half, dtype=np.float64) / half)
    ang = pos.astype(np.float64)[:, None] * inv_freq[None, :]
    cos, sin = np.cos(ang), np.sin(ang)
    reps = LANES // head_dim
    cos_t = np.tile(np.concatenate([cos, cos], axis=-1), (1, reps))
    sin_t = np.tile(np.concatenate([-sin, sin], axis=-1), (1, reps))
    return jnp.asarray(cos_t, dtype=F32), jnp.asarray(sin_t, dtype=F32)


def _inproj_body(sample_group, n_casts, x_ref, sc_ref, sh_ref, w_ref, cm_ref, sm_ref, cr_ref, sr_ref, *refs):
    cast_in, refs = refs[:n_casts], refs[n_casts:]
    rq_ref, rk_ref, rv_ref, rg_ref, mq_ref, ko_ref, vo_ref = refs[:7]
    for src, dst in zip(cast_in, refs[7:]):
        dst[...] = src[...].astype(dst.dtype)
    tm = x_ref.shape[0]
    h = (x_ref[...] * (1.0 + _modulation(sc_ref, sample_group)) + _modulation(sh_ref, sample_group)).astype(BF16)

    def proj(lo, width):
        return jnp.dot(h, w_ref[:, lo:lo + width].astype(BF16), preferred_element_type=F32)

    lane = lax.broadcasted_iota(jnp.int32, (tm, LANES), 1)
    low_half = (lane & (RET_DK - 1)) < (RET_DK // 2)
    cr, sr = cr_ref[...], sr_ref[...]
    cm, sm = cm_ref[...], sm_ref[...]

    def rope_ret(z):
        rot = jnp.where(low_half, pltpu.roll(z, LANES - RET_DK // 2, 1), pltpu.roll(z, RET_DK // 2, 1))
        return z * cr + rot * sr

    def rope_moba(z):
        return z * cm + pltpu.roll(z, MOBA_HEAD_DIM // 2, 1) * sm

    zq = proj(OFF_RQ, RET_QK)
    zk = proj(OFF_RK, RET_QK)
    for s in range(RET_QK // LANES):
        sl = slice(s * LANES, (s + 1) * LANES)
        rq_ref[:, sl] = rope_ret(zq[:, sl])
        rk_ref[:, sl] = rope_ret(zk[:, sl]) * (RET_DK ** -0.5)
    rv_ref[...] = proj(OFF_RV, RET_WIDTH).astype(rv_ref.dtype)
    rg_ref[...] = proj(OFF_RG, RET_WIDTH)
    zq = proj(OFF_MQ, MOBA_WIDTH)
    zk = proj(OFF_MK, MOBA_WIDTH)
    zv = proj(OFF_MV, MOBA_WIDTH)
    for hd in range(MOBA_HEADS):
        sl = slice(hd * LANES, (hd + 1) * LANES)
        mq_ref[:, sl] = rope_moba(zq[:, sl]).astype(mq_ref.dtype)
        ko_ref[pl.ds(hd, tm, stride=MOBA_HEADS), :] = rope_moba(zk[:, sl])
        vo_ref[pl.ds(hd, tm, stride=MOBA_HEADS), :] = zv[:, sl]


def _inproj(x2d, mod, sample_group, tab_specs, tabs, w_in, grid, row_map, tm, act_dtype, casts=()):
    t = x2d.shape[0]
    n_steps = int(np.prod(grid))
    wide = lambda w: pl.BlockSpec((tm, w), row_map)
    cast_specs = [pl.BlockSpec((a.shape[0] // n_steps, a.shape[1]), row_map) for a in casts]
    w_spec = pl.BlockSpec((D_MODEL, IN_WIDTH), lambda *_: (0, 0), pipeline_mode=pl.Buffered(1))
    return pl.pallas_call(
        functools.partial(_inproj_body, sample_group, len(casts)),
        grid=grid,
        in_specs=[wide(D_MODEL), _mod_spec(1), _mod_spec(0), w_spec] + tab_specs + cast_specs,
        out_specs=[wide(RET_QK), wide(RET_QK), wide(RET_WIDTH), wide(RET_WIDTH), wide(MOBA_WIDTH),
                   pl.BlockSpec((tm * MOBA_HEADS, LANES), row_map),
                   pl.BlockSpec((tm * MOBA_HEADS, LANES), row_map)] + cast_specs,
        out_shape=[jax.ShapeDtypeStruct((t, RET_QK), F32), jax.ShapeDtypeStruct((t, RET_QK), F32),
                   jax.ShapeDtypeStruct((t, RET_WIDTH), act_dtype), jax.ShapeDtypeStruct((t, RET_WIDTH), F32),
                   jax.ShapeDtypeStruct((t, MOBA_WIDTH), act_dtype),
                   jax.ShapeDtypeStruct((t * MOBA_HEADS, LANES), F32),
                   jax.ShapeDtypeStruct((t * MOBA_HEADS, LANES), F32)]
        + [jax.ShapeDtypeStruct(a.shape, BF16) for a in casts],
        compiler_params=pltpu.CompilerParams(dimension_semantics=("arbitrary",) * len(grid),
                                             vmem_limit_bytes=VMEM_LIMIT),
        name="inproj",
    )(x2d, mod, mod, w_in, *tabs, *casts)


def _group_norm_gate(o, g):
    mu = jnp.mean(o, axis=-1, keepdims=True)
    d = o - mu
    var = jnp.mean(d * d, axis=-1, keepdims=True)
    return d * lax.rsqrt(var + GN_EPS) * (g * jax.nn.sigmoid(g))


def _ret_prompt_tables():
    lg = _log_decay()
    i = np.arange(RET_CHUNK, dtype=np.float64)
    diff = i[:, None] - i[None, :]
    dmat = np.where(diff >= 0, np.exp(np.maximum(diff, 0.0)[None] * lg[:, None, None]), 0.0)
    lane_head = np.arange(RET_QK) // RET_DK
    qd = np.exp((i[:, None] + 1.0) * lg[lane_head][None, :])
    kd = np.exp((RET_CHUNK - 1.0 - i)[:, None] * lg[lane_head][None, :])
    row_head = np.arange(RET_QK) // RET_DK
    col_head = np.arange(RET_WIDTH) // RET_DV
    same = row_head[:, None] == col_head[None, :]
    cdec = np.where(same, np.exp(RET_CHUNK * lg[row_head])[:, None], 0.0)
    return [jnp.asarray(a, dtype=F32) for a in
            (dmat, np.concatenate([qd, kd], axis=1), np.concatenate([cdec, same.astype(np.float64)], axis=1))]


def _ret_prompt_body(q_ref, k_ref, v_ref, g_ref, dmat_ref, qkd_ref, sdm_ref, o_ref, st_ref, state_scr):
    lane_head = lax.broadcasted_iota(jnp.int32, (RET_CHUNK, RET_QK), 1) >> LOG2_RET_DK
    chunk_rows = lambda c: slice(c * RET_CHUNK, (c + 1) * RET_CHUNK)

    def first_matmuls(c):
        rows = chunk_rows(c)
        q = q_ref[rows, :]
        k = k_ref[rows, :]
        v = v_ref[rows, :]
        kb = k.astype(BF16)
        state = state_scr[...]
        scores = [lax.dot_general(jnp.where(lane_head == hd, q, 0.0).astype(BF16), kb, _NT,
                                  preferred_element_type=F32) for hd in range(RET_HEADS)]
        cross = jnp.dot((q * qkd_ref[:, :RET_QK]).astype(BF16), state.astype(BF16), preferred_element_type=F32)
        kv = lax.dot_general((k * qkd_ref[:, RET_QK:]).astype(BF16), v, _TN, preferred_element_type=F32)
        state_scr[...] = sdm_ref[:, :RET_WIDTH] * state + sdm_ref[:, RET_WIDTH:] * kv
        return scores, cross, v

    def second_matmuls(c, scores, cross, v):
        rows = chunk_rows(c)
        g = g_ref[rows, :]
        decayed = [(scores[hd] * dmat_ref[hd]).astype(BF16) for hd in range(RET_HEADS)]
        for hd in range(RET_HEADS):
            sl = slice(hd * RET_DV, (hd + 1) * RET_DV)
            inner = jnp.dot(decayed[hd], v[:, sl], preferred_element_type=F32)
            o_ref[rows, sl] = _group_norm_gate(inner + cross[:, sl], g[:, sl]).astype(o_ref.dtype)

    state_scr[...] = jnp.zeros_like(state_scr)
    n_chunks = SEQ // RET_CHUNK
    ahead = first_matmuls(0)
    for c in range(n_chunks):
        current = ahead
        if c + 1 < n_chunks:
            ahead = first_matmuls(c + 1)
        second_matmuls(c, *current)
    for hd in range(RET_HEADS):
        st_ref[hd] = state_scr[hd * RET_DK:(hd + 1) * RET_DK, hd * RET_DV:(hd + 1) * RET_DV]


def _ret_prompt(rq, rk, rv, rg):
    tabs = _ret_prompt_tables()
    seq = lambda w: pl.BlockSpec((SEQ, w), lambda b: (b, 0))
    const = lambda a: pl.BlockSpec(a.shape, lambda b: (0,) * a.ndim)
    return pl.pallas_call(
        _ret_prompt_body,
        grid=(BATCH,),
        in_specs=[seq(RET_QK), seq(RET_QK), seq(RET_WIDTH), seq(RET_WIDTH)] + [const(a) for a in tabs],
        out_specs=[seq(RET_WIDTH), pl.BlockSpec((None, RET_HEADS, RET_DK, RET_DV), lambda b: (b, 0, 0, 0))],
        out_shape=[jax.ShapeDtypeStruct((BATCH * SEQ, RET_WIDTH), BF16),
                   jax.ShapeDtypeStruct((BATCH, RET_HEADS, RET_DK, RET_DV), F32)],
        scratch_shapes=[pltpu.VMEM((RET_QK, RET_WIDTH), F32)],
        compiler_params=pltpu.CompilerParams(dimension_semantics=("arbitrary",),
                                             vmem_limit_bytes=VMEM_LIMIT),
        name="ret_prompt",
    )(rq, rk, rv, rg, *tabs)


def _ret_sample_tables():
    lg = _log_decay()
    t = np.arange(N_SAMPLE_TOK) % DEC_SEQ
    seq_id = np.arange(N_SAMPLE_TOK) // DEC_SEQ
    diff = (t[:, None] - t[None, :]).astype(np.float64)
    same_seq = seq_id[:, None] == seq_id[None, :]
    dmat = np.where(same_seq[None] & (diff >= 0)[None],
                    np.exp(np.maximum(diff, 0.0)[None] * lg[:, None, None]), 0.0)
    lane_head = np.arange(RET_QK) // RET_DK
    qd = np.exp((t[:, None] + 1.0) * lg[lane_head][None, :])
    kd = np.exp((DEC_SEQ - 1.0 - t)[:, None] * lg[lane_head][None, :])
    return [jnp.asarray(a, dtype=F32) for a in (dmat, np.concatenate([qd, kd], axis=1))]


def _ret_sample_body(q_ref, k_ref, v_ref, g_ref, st_ref, dmat_ref, qkd_ref, o_ref, sto_ref):
    lg = _log_decay()
    q = q_ref[...]
    k = k_ref[...]
    kb = k.astype(BF16)
    qdec = q * qkd_ref[:, :RET_QK]
    kdec = k * qkd_ref[:, RET_QK:]
    vb = v_ref[...].astype(BF16)
    g = g_ref[...]
    lane = lax.broadcasted_iota(jnp.int32, (N_SAMPLE_TOK, LANES), 1)
    lane_head = lax.broadcasted_iota(jnp.int32, (N_SAMPLE_TOK, RET_QK), 1) >> LOG2_RET_DK
    n_state_rows = DEC_BATCH * RET_DK
    own_seq = ((lax.broadcasted_iota(jnp.int32, (N_SAMPLE_TOK, n_state_rows), 0) >> LOG2_DEC_SEQ)
               == (lax.broadcasted_iota(jnp.int32, (N_SAMPLE_TOK, n_state_rows), 1) >> LOG2_RET_DK))
    for hd in range(RET_HEADS):
        sl = slice(hd * RET_DV, (hd + 1) * RET_DV)
        qm = jnp.where(lane_head == hd, q, 0.0).astype(BF16)
        s = lax.dot_general(qm, kb, _NT, preferred_element_type=F32) * dmat_ref[hd]
        inner = jnp.dot(s.astype(BF16), vb[:, sl], preferred_element_type=F32)

        def expand(z):
            slab = z[:, (hd // 2) * LANES:(hd // 2 + 1) * LANES]
            other = pltpu.roll(slab, RET_DK, 1)
            in_low = lane < RET_DK
            both = jnp.where(in_low, slab, other) if hd % 2 == 0 else jnp.where(in_low, other, slab)
            tiled = jnp.concatenate([both] * (n_state_rows // LANES), axis=1)
            return jnp.where(own_seq, tiled, 0.0).astype(BF16)

        st = st_ref[:, hd].reshape(n_state_rows, RET_DV)
        cross = jnp.dot(expand(qdec), st.astype(BF16), preferred_element_type=F32)
        o_ref[:, sl] = _group_norm_gate(inner + cross, g[:, sl])
        kv = lax.dot_general(expand(kdec), vb[:, sl], _TN, preferred_element_type=F32)
        new = float(np.exp(DEC_SEQ * lg[hd])) * st + kv
        sto_ref[:, hd] = new.reshape(DEC_BATCH, RET_DK, RET_DV)


def _ret_sample(rq, rk, rv, rg, state):
    tabs = _ret_sample_tables()
    full = lambda a: pl.BlockSpec(a.shape, lambda i: (0,) * a.ndim)
    args = (rq, rk, rv, rg, state, *tabs)
    return pl.pallas_call(
        _ret_sample_body,
        grid=(1,),
        in_specs=[full(a) for a in args],
        out_specs=[pl.BlockSpec((N_SAMPLE_TOK, RET_WIDTH), lambda i: (0, 0)),
                   pl.BlockSpec(state.shape, lambda i: (0, 0, 0, 0))],
        out_shape=[jax.ShapeDtypeStruct((N_SAMPLE_TOK, RET_WIDTH), F32),
                   jax.ShapeDtypeStruct(state.shape, F32)],
        compiler_params=pltpu.CompilerParams(dimension_semantics=("arbitrary",),
                                             vmem_limit_bytes=VMEM_LIMIT),
        name="ret_sample",
    )(*args)


def _moba_prompt_body(q_ref, k_ref, v_ref, o_ref):
    hd = pl.program_id(1)
    n_blocks = SEQ // MOBA_BLOCK
    exp2_scale = MOBA_HEAD_DIM ** -0.5 * LOG2_E
    k32 = k_ref[pl.ds(hd, SEQ, stride=MOBA_HEADS), :]
    kb = k32.astype(BF16)
    vt = v_ref[pl.ds(hd, SEQ, stride=MOBA_HEADS), :].T.astype(BF16)
    kmean = jnp.sum(k32.reshape(n_blocks, MOBA_BLOCK, MOBA_HEAD_DIM), axis=1) * (1.0 / MOBA_BLOCK)
    kmb = kmean.astype(BF16)
    key_id = lax.broadcasted_iota(jnp.int32, (MOBA_BLOCK, MOBA_BLOCK), 0)
    qry_id = lax.broadcasted_iota(jnp.int32, (MOBA_BLOCK, MOBA_BLOCK), 1)
    causal = key_id <= qry_id

    blk = lambda n: slice(n * MOBA_BLOCK, (n + 1) * MOBA_BLOCK)

    def score_matmuls(i):
        qi = q_ref[blk(i), :]
        st = [lax.dot_general(kb[blk(n)], qi, _NT, preferred_element_type=F32) for n in range(i + 1)]
        gt = lax.dot_general(kmb, qi, _NT, preferred_element_type=F32) if i > MOBA_TOPK else None
        return st, gt

    ahead = score_matmuls(n_blocks - 1)
    for i in reversed(range(n_blocks)):
        st, gt = ahead
        if i > 0:
            ahead = score_matmuls(i - 1)
        st[i] = jnp.where(causal, st[i], NEG_INF)
        if i > MOBA_TOPK:
            for n in range(i):
                beats = jnp.zeros((1, MOBA_BLOCK), F32)
                for mm in range(i):
                    if mm == n:
                        continue
                    win = (gt[mm:mm + 1] >= gt[n:n + 1]) if mm < n else (gt[mm:mm + 1] > gt[n:n + 1])
                    beats = beats + win.astype(F32)
                st[n] = st[n] + jnp.where(beats < MOBA_TOPK, 0.0, NEG_INF)
        m = functools.reduce(jnp.maximum, [jnp.max(s, axis=0, keepdims=True) for s in st])
        l = jnp.zeros((1, MOBA_BLOCK), F32)
        acc = jnp.zeros((MOBA_HEAD_DIM, MOBA_BLOCK), F32)
        for n in range(i + 1):
            e = jnp.exp2((st[n] - m) * exp2_scale)
            l = l + jnp.sum(e, axis=0, keepdims=True)
            acc = acc + jnp.dot(vt[:, blk(n)], e.astype(BF16), preferred_element_type=F32)
        o_ref[blk(i), :] = (acc / l).T.astype(o_ref.dtype)


def _moba_prompt(mq, k2d, v2d):
    kv_spec = pl.BlockSpec((SEQ * MOBA_HEADS, LANES), lambda b, h: (b, 0))
    return pl.pallas_call(
        _moba_prompt_body,
        grid=(BATCH, MOBA_HEADS),
        in_specs=[pl.BlockSpec((SEQ, MOBA_HEAD_DIM), lambda b, h: (b, h)), kv_spec, kv_spec],
        out_specs=pl.BlockSpec((SEQ, MOBA_HEAD_DIM), lambda b, h: (b, h)),
        out_shape=jax.ShapeDtypeStruct((BATCH * SEQ, MOBA_WIDTH), BF16),
        compiler_params=pltpu.CompilerParams(dimension_semantics=("arbitrary", "arbitrary"),
                                             vmem_limit_bytes=VMEM_LIMIT),
        name="moba_prompt",
    )(mq, k2d, v2d)


class _SampleMoba:
    n_chunks = PAGES_PER_SEQ // CHUNK_PAGES
    n_rows = MOBA_HEADS * DEC_SEQ
    n_blocks = PAGES_PER_SEQ * PAGE_SIZE // MOBA_BLOCK
    pages_per_block = MOBA_BLOCK // PAGE_SIZE

    def __init__(self, pt_ref, seq, n_seqs, q_ref, kn_ref, vn_ref, kc_ref, vc_ref, o_ref, ring, sem, s_scr):
        self.pt_ref, self.seq, self.n_seqs = pt_ref, seq, n_seqs
        self.q_ref, self.kn_ref, self.vn_ref = q_ref, kn_ref, vn_ref
        self.kc_ref, self.vc_ref, self.o_ref = kc_ref, vc_ref, o_ref
        self.ring, self.sem, self.s_scr = ring, sem, s_scr

    def _page_copy(self, cache_ref, row0, slot):
        return pltpu.make_async_copy(cache_ref.at[pl.ds(row0, PAGE_ROWS)], self.ring.at[slot],
                                     self.sem.at[slot // CHUNK_PAGES])

    def _start_chunk(self, cache_ref, seq, chunk):
        for r in range(CHUNK_PAGES):
            page = self.pt_ref[seq * PAGES_PER_SEQ + chunk * CHUNK_PAGES + r]
            self._page_copy(cache_ref, pl.multiple_of(page * PAGE_ROWS, PAGE_ROWS),
                            (chunk % 2) * CHUNK_PAGES + r).start()

    def _wait_chunk(self, cache_ref, chunk):
        for r in range(CHUNK_PAGES):
            self._page_copy(cache_ref, 0, (chunk % 2) * CHUNK_PAGES + r).wait()

    def prologue(self):
        @pl.when(self.seq == 0)
        def _():
            self._start_chunk(self.kc_ref, self.seq, 0)
            self._start_chunk(self.kc_ref, self.seq, 1)

        self.q = jnp.concatenate(
            [self.q_ref[:, hd * MOBA_HEAD_DIM:(hd + 1) * MOBA_HEAD_DIM] for hd in range(MOBA_HEADS)], axis=0
        ).astype(BF16)
        row_head = lax.broadcasted_iota(jnp.int32, (self.n_rows, LANES), 0) >> LOG2_DEC_SEQ
        col_head = lax.broadcasted_iota(jnp.int32, (self.n_rows, LANES), 1) & (MOBA_HEADS - 1)
        self.same_head = row_head == col_head
        self.head_bias = jnp.where(self.same_head, 0.0, NEG_INF)
        self.block_sum, self.block_max = [], []

    @staticmethod
    def _slabs(x):
        return [x[:, j * LANES:(j + 1) * LANES] for j in range(PAGE_ROWS // LANES)]

    def wait_k(self, c):
        self._wait_chunk(self.kc_ref, c)

    def k_pages(self, c, first, last):
        assert first % self.pages_per_block == 0 and last % self.pages_per_block == 0
        for r0 in range(first, last, self.pages_per_block):
            tot = jnp.zeros((self.n_rows, LANES), F32)
            top = jnp.full((self.n_rows, LANES), NEG_INF, F32)
            for r in range(r0, r0 + self.pages_per_block):
                page = self.ring[(c % 2) * CHUNK_PAGES + r].astype(BF16)
                s = lax.dot_general(self.q, page, _NT, preferred_element_type=F32)
                self.s_scr[c * CHUNK_PAGES + r] = s
                for slab in self._slabs(s):
                    tot = tot + jnp.where(self.same_head, slab, 0.0)
                    top = jnp.maximum(top, slab + self.head_bias)
            self.block_sum.append(jnp.sum(tot, axis=-1, keepdims=True))
            self.block_max.append(jnp.max(top, axis=-1, keepdims=True))

    def refill_after_k(self, c):
        if c + 2 < self.n_chunks:
            self._start_chunk(self.kc_ref, self.seq, c + 2)
        else:
            self._start_chunk(self.vc_ref, self.seq, c + 2 - self.n_chunks)

    def select(self):
        n_rows, gs = self.n_rows, self.block_sum
        self.exp2_scale = MOBA_HEAD_DIM ** -0.5 * LOG2_E
        lane = lax.broadcasted_iota(jnp.int32, (n_rows, LANES), 1)
        g_all = jnp.full((n_rows, LANES), NEG_INF, F32)
        for n in range(self.n_blocks):
            g_all = jnp.where(lane == n, gs[n], g_all)
        self.keep_bias = []
        for n in range(self.n_blocks):
            wins = (g_all > gs[n]) | ((g_all == gs[n]) & (lane < n))
            beats = jnp.sum(wins.astype(F32), axis=-1, keepdims=True)
            self.keep_bias.append(jnp.where(beats < MOBA_TOPK, 0.0, NEG_INF))
        s_own = lax.dot_general(self.q, self.kn_ref[...].astype(BF16), _NT, preferred_element_type=F32)
        r_id = lax.broadcasted_iota(jnp.int32, (n_rows, n_rows), 0)
        c_id = lax.broadcasted_iota(jnp.int32, (n_rows, n_rows), 1)
        own_ok = ((c_id & (MOBA_HEADS - 1)) == (r_id >> LOG2_DEC_SEQ)) & ((c_id >> LOG2_MOBA_HEADS) <= (r_id & (DEC_SEQ - 1)))
        s_own = jnp.where(own_ok, s_own, NEG_INF)
        m = jnp.max(s_own, axis=-1, keepdims=True)
        for n in range(self.n_blocks):
            m = jnp.maximum(m, self.block_max[n] + self.keep_bias[n])
        self.m = m
        self.lsum = jnp.zeros((n_rows, LANES), F32)
        e_own = jnp.exp2((s_own - m) * self.exp2_scale)
        return (jnp.sum(e_own, axis=-1, keepdims=True),
                jnp.dot(e_own.astype(BF16), self.vn_ref[...].astype(BF16), preferred_element_type=F32))

    def wait_v(self, c):
        self._wait_chunk(self.vc_ref, c)

    def v_pages(self, c, first, last, acc):
        assert first % self.pages_per_block == 0 and last % self.pages_per_block == 0
        for r0 in range(first, last, self.pages_per_block):
            shift = self.head_bias + (self.keep_bias[(c * CHUNK_PAGES + r0) // self.pages_per_block] - self.m)
            for r in range(r0, r0 + self.pages_per_block):
                e = [jnp.exp2((slab + shift) * self.exp2_scale) for slab in self._slabs(self.s_scr[c * CHUNK_PAGES + r])]
                self.lsum = self.lsum + functools.reduce(jnp.add, e)
                page = self.ring[(c % 2) * CHUNK_PAGES + r].astype(BF16)
                acc = acc + jnp.dot(jnp.concatenate(e, axis=1).astype(BF16), page, preferred_element_type=F32)
        return acc

    def refill_after_v(self, c):
        if c + 2 < self.n_chunks:
            self._start_chunk(self.vc_ref, self.seq, c + 2)
        else:
            @pl.when(self.seq + 1 < self.n_seqs)
            def _():
                self._start_chunk(self.kc_ref, self.seq + 1, c + 2 - self.n_chunks)

    def finish(self, l_own, acc):
        out = acc / (l_own + jnp.sum(self.lsum, axis=-1, keepdims=True))
        for hd in range(MOBA_HEADS):
            self.o_ref[:, hd * MOBA_HEAD_DIM:(hd + 1) * MOBA_HEAD_DIM] = out[hd * DEC_SEQ:(hd + 1) * DEC_SEQ]


def _layer_norm(x, g, b):
    mu = jnp.mean(x, axis=-1, keepdims=True)
    d = x - mu
    var = jnp.mean(d * d, axis=-1, keepdims=True)
    return d * lax.rsqrt(var + LN_EPS) * g + b


def _out_ffn_body(ar_ref, am_ref, x_ref, ga_ref, shf_ref, scf_ref, gf_ref, wo_ref, g1_ref, b1_ref,
                  wu_ref, wd_ref, g2_ref, b2_ref, y_ref, x1_scr, h_scr, acc_scr):
    c = pl.program_id(0)

    @pl.when(c == 0)
    def _():
        mixed = (jnp.dot(ar_ref[...].astype(BF16), wo_ref[:RET_WIDTH, :], preferred_element_type=F32)
                 + jnp.dot(am_ref[...].astype(BF16), wo_ref[RET_WIDTH:, :], preferred_element_type=F32))
        x1 = _layer_norm(ALPHA * x_ref[...] + _modulation(ga_ref, True) * mixed, g1_ref[...], b1_ref[...])
        x1_scr[...] = x1
        h_scr[...] = (x1 * (1.0 + _modulation(scf_ref, True)) + _modulation(shf_ref, True)).astype(BF16)
        acc_scr[...] = jnp.zeros_like(acc_scr)

    u = jnp.maximum(jnp.dot(h_scr[...], wu_ref[...], preferred_element_type=F32), 0.0)
    acc_scr[...] += jnp.dot((u * u).astype(BF16), wd_ref[...], preferred_element_type=F32)

    @pl.when(c == pl.num_programs(0) - 1)
    def _():
        y_ref[...] = _layer_norm(ALPHA * x1_scr[...] + _modulation(gf_ref, True) * acc_scr[...],
                                 g2_ref[...], b2_ref[...])


def _out_ffn(a_ret, a_moba, x2d, mod, weights):
    w_o, ln1_g, ln1_b, w_up, w_down, ln2_g, ln2_b = weights
    t = x2d.shape[0]
    whole = lambda a: pl.BlockSpec(a.shape, lambda c: (0,) * a.ndim)
    return pl.pallas_call(
        _out_ffn_body,
        grid=(D_FF // D_MODEL,),
        in_specs=[whole(a_ret), whole(a_moba), whole(x2d)] + [_mod_spec(term) for term in (2, 3, 4, 5)]
        + [whole(w_o), whole(ln1_g), whole(ln1_b),
           pl.BlockSpec((D_MODEL, D_MODEL), lambda c: (0, c)), pl.BlockSpec((D_MODEL, D_MODEL), lambda c: (c, 0)),
           whole(ln2_g), whole(ln2_b)],
        out_specs=pl.BlockSpec((t, D_MODEL), lambda c: (0, 0)),
        out_shape=jax.ShapeDtypeStruct((t, D_MODEL), F32),
        scratch_shapes=[pltpu.VMEM((t, D_MODEL), F32), pltpu.VMEM((t, D_MODEL), BF16), pltpu.VMEM((t, D_MODEL), F32)],
        compiler_params=pltpu.CompilerParams(dimension_semantics=("arbitrary",),
                                             vmem_limit_bytes=VMEM_LIMIT),
        name="out_ffn",
    )(a_ret, a_moba, x2d, *([mod] * 4), *weights)


def _out_ffn_moba_body(pt_ref, ar_ref, am_ref, x_ref, ga_ref, shf_ref, scf_ref, gf_ref, wo_ref, g1_ref, b1_ref,
                       wu_ref, wd_ref, g2_ref, b2_ref, q_ref, kn_ref, vn_ref, kc_ref, vc_ref,
                       y_ref, o_ref, ring, sem, s_scr):
    n_seqs = pl.num_programs(0) * pl.num_programs(1)
    seq = pl.program_id(0) * pl.num_programs(1) + pl.program_id(1)
    sm = _SampleMoba(pt_ref, seq, n_seqs, q_ref, kn_ref, vn_ref, kc_ref, vc_ref, o_ref, ring, sem, s_scr)

    def up(c, h):
        u = jnp.maximum(jnp.dot(h, wu_ref[:, c * D_MODEL:(c + 1) * D_MODEL], preferred_element_type=F32), 0.0)
        return (u * u).astype(BF16)

    def down(c, u):
        return jnp.dot(u, wd_ref[c * D_MODEL:(c + 1) * D_MODEL, :], preferred_element_type=F32)

    half = CHUNK_PAGES // 2
    sm.prologue()

    sm.wait_k(0)
    sm.k_pages(0, 0, half)
    mixed = jnp.dot(ar_ref[...].astype(BF16), wo_ref[:RET_WIDTH, :], preferred_element_type=F32)
    sm.k_pages(0, half, CHUNK_PAGES)
    mixed = mixed + jnp.dot(am_ref[...].astype(BF16), wo_ref[RET_WIDTH:, :], preferred_element_type=F32)
    x1 = _layer_norm(ALPHA * x_ref[...] + _modulation(ga_ref, False) * mixed, g1_ref[...], b1_ref[...])
    h = (x1 * (1.0 + _modulation(scf_ref, False)) + _modulation(shf_ref, False)).astype(BF16)
    sm.refill_after_k(0)

    sm.wait_k(1)
    sm.k_pages(1, 0, half)
    u = up(0, h)
    sm.k_pages(1, half, CHUNK_PAGES)
    sm.refill_after_k(1)

    sm.wait_k(2)
    sm.k_pages(2, 0, half)
    acc = down(0, u)
    sm.k_pages(2, half, CHUNK_PAGES)
    sm.refill_after_k(2)

    sm.wait_k(3)
    sm.k_pages(3, 0, half)
    u = up(1, h)
    sm.k_pages(3, half, CHUNK_PAGES)
    sm.refill_after_k(3)

    l, acc_s = sm.select()
    acc = acc + down(1, u)

    sm.wait_v(0)
    acc_s = sm.v_pages(0, 0, half, acc_s)
    u = up(2, h)
    acc_s = sm.v_pages(0, half, CHUNK_PAGES, acc_s)
    sm.refill_after_v(0)

    sm.wait_v(1)
    acc_s = sm.v_pages(1, 0, half, acc_s)
    acc = acc + down(2, u)
    acc_s = sm.v_pages(1, half, CHUNK_PAGES, acc_s)
    sm.refill_after_v(1)

    sm.wait_v(2)
    acc_s = sm.v_pages(2, 0, half, acc_s)
    u = up(3, h)
    acc_s = sm.v_pages(2, half, CHUNK_PAGES, acc_s)
    sm.refill_after_v(2)

    sm.wait_v(3)
    acc_s = sm.v_pages(3, 0, half, acc_s)
    acc = acc + down(3, u)
    acc_s = sm.v_pages(3, half, CHUNK_PAGES, acc_s)
    sm.refill_after_v(3)
    y_ref[...] = _layer_norm(ALPHA * x1 + _modulation(gf_ref, False) * acc, g2_ref[...], b2_ref[...])
    sm.finish(l, acc_s)


def _out_ffn_moba(a_ret, a_moba, x2d, mod, weights, page_table, mq_s, kn2d, vn2d, cache_k2d, cache_v2d, tm):
    t = x2d.shape[0]
    nt = SEQ // tm
    assert BATCH * nt == DEC_BATCH
    n_rows = MOBA_HEADS * DEC_SEQ
    tile = lambda b, i, pt: (b * nt + i, 0)
    wide = lambda w: pl.BlockSpec((tm, w), tile)
    const = lambda a: pl.BlockSpec(a.shape, lambda b, i, pt: (0,) * a.ndim, pipeline_mode=pl.Buffered(1))
    seq_rows = lambda w, n: pl.BlockSpec((n, w), tile)
    hbm = pl.BlockSpec(memory_space=pl.ANY)
    grid_spec = pltpu.PrefetchScalarGridSpec(
        num_scalar_prefetch=1,
        grid=(BATCH, nt),
        in_specs=[wide(RET_WIDTH), wide(MOBA_WIDTH), wide(D_MODEL)] + [_mod_spec(term) for term in (2, 3, 4, 5)]
        + [const(a) for a in weights]
        + [seq_rows(MOBA_WIDTH, DEC_SEQ), seq_rows(LANES, n_rows), seq_rows(LANES, n_rows), hbm, hbm],
        out_specs=[wide(D_MODEL), seq_rows(MOBA_WIDTH, DEC_SEQ)],
        scratch_shapes=[pltpu.VMEM((RING_PAGES, PAGE_ROWS, LANES), F32),
                        pltpu.SemaphoreType.DMA((RING_PAGES // CHUNK_PAGES,)),
                        pltpu.VMEM((PAGES_PER_SEQ, n_rows, PAGE_ROWS), F32)],
    )
    return pl.pallas_call(
        _out_ffn_moba_body,
        grid_spec=grid_spec,
        out_shape=[jax.ShapeDtypeStruct((t, D_MODEL), F32),
                   jax.ShapeDtypeStruct((N_SAMPLE_TOK, MOBA_WIDTH), F32)],
        compiler_params=pltpu.CompilerParams(dimension_semantics=("arbitrary", "arbitrary"),
                                             vmem_limit_bytes=VMEM_LIMIT_FUSED),
        name="out_ffn_moba",
    )(page_table.reshape(-1), a_ret, a_moba, x2d, *([mod] * 4), *weights,
      mq_s, kn2d, vn2d, cache_k2d, cache_v2d)


def kernel(x_prompt, x_sample, cache_k, cache_v, state_ret, page_table, c_prompt, c_sample,
           w_ada, b_ada, w_in, w_o, ln1_g, ln1_b, w_up, w_down, ln2_g, ln2_b):
    n_prompt_tok = BATCH * SEQ
    past_len = page_table.shape[1] * PAGE_SIZE

    mod = _adaln(c_prompt, c_sample, w_ada[0], b_ada)

    tm = 512
    nt = SEQ // tm
    p_row = lambda b, i: (b * nt + i, 0)
    p_tab = pl.BlockSpec((tm, LANES), lambda b, i: (i, 0))
    s_row = lambda i: (0, 0)
    s_tab = pl.BlockSpec((N_SAMPLE_TOK, LANES), s_row)

    pos_p = np.arange(SEQ, dtype=np.int32)
    pos_s = np.tile(past_len + np.arange(DEC_SEQ, dtype=np.int32), DEC_BATCH)
    tabs_p = _rope_tables(pos_p, MOBA_HEAD_DIM) + _rope_tables(pos_p, RET_DK)
    tabs_s = _rope_tables(pos_s, MOBA_HEAD_DIM) + _rope_tables(pos_s, RET_DK)

    xp = x_prompt.reshape(n_prompt_tok, D_MODEL)
    rq, rk, rv, rg, mq, k_p, v_p, w_o_b, w_up_b, w_down_b = _inproj(
        xp, mod, False, [p_tab] * 4, tabs_p, w_in[0],
        (BATCH, nt), p_row, tm, BF16, casts=(w_o[0], w_up[0], w_down[0]))
    weights = (w_o_b, ln1_g, ln1_b, w_up_b, w_down_b, ln2_g, ln2_b)
    a_ret, state_p = _ret_prompt(rq, rk, rv, rg)
    a_moba = _moba_prompt(mq, k_p, v_p)

    xs = x_sample.reshape(N_SAMPLE_TOK, D_MODEL)
    rq_s, rk_s, rv_s, rg_s, mq_s, k_s, v_s = _inproj(
        xs, mod, True, [s_tab] * 4, tabs_s, w_in[0],
        (1,), s_row, N_SAMPLE_TOK, F32)
    a_ret_s, state_s = _ret_sample(rq_s, rk_s, rv_s, rg_s, state_ret[0])

    cache_rows = cache_k.shape[1] * PAGE_ROWS
    y_p, a_moba_s = _out_ffn_moba(a_ret, a_moba, xp, mod, weights, page_table, mq_s, k_s, v_s,
                                  cache_k.reshape(cache_rows, LANES), cache_v.reshape(cache_rows, LANES), tm)
    y_s = _out_ffn(a_ret_s, a_moba_s, xs, mod, weights)

    kv_p_shape = (DEPTH, BATCH, SEQ, MOBA_HEADS, MOBA_HEAD_DIM)
    kv_s_shape = (DEPTH, DEC_BATCH, DEC_SEQ, MOBA_HEADS, MOBA_HEAD_DIM)
    return (y_p.reshape(BATCH, SEQ, D_MODEL),
            y_s.reshape(DEC_BATCH, DEC_SEQ, D_MODEL),
            k_p.reshape(kv_p_shape), v_p.reshape(kv_p_shape), state_p[None],
            k_s.reshape(kv_s_shape), v_s.reshape(kv_s_shape), state_s[None])
```

```python
import functools

import numpy as np
import jax
import jax.numpy as jnp
from jax import lax
from jax.experimental import pallas as pl
from jax.experimental.pallas import tpu as pltpu

F32 = jnp.float32
BF16 = jnp.bfloat16

D_MODEL = 1024
BATCH = 8
SEQ = 2048
DEC_BATCH = 32
DEC_SEQ = 8
PAGE_SIZE = 128
RET_HEADS = 4
RET_DK = 64
RET_DV = 128
RET_CHUNK = 128
MOBA_HEADS = 4
MOBA_HEAD_DIM = 128
MOBA_BLOCK = 256
MOBA_TOPK = 3
D_FF = 4 * D_MODEL
ROPE_THETA = 10000.0
LN_EPS = 1e-5
GN_EPS = 1e-6
DEPTH = 1
ALPHA = (2 * DEPTH) ** 0.25
RET_QK = RET_HEADS * RET_DK
RET_WIDTH = RET_HEADS * RET_DV
MOBA_WIDTH = MOBA_HEADS * MOBA_HEAD_DIM
IN_WIDTH = 2 * RET_QK + 2 * RET_WIDTH + 3 * MOBA_WIDTH
OFF_RQ, OFF_RK, OFF_RV, OFF_RG = 0, 256, 512, 1024
OFF_MQ, OFF_MK, OFF_MV = 1536, 2048, 2560
LANES = 128
VMEM_BYTES_V7X = 64 * 1024 * 1024
VMEM_LIMIT = VMEM_BYTES_V7X * 3 // 4
VMEM_LIMIT_FUSED = VMEM_BYTES_V7X * 29 // 32
N_SAMPLE_TOK = DEC_BATCH * DEC_SEQ
PAGES_PER_SEQ = 64
CHUNK_PAGES = 16
RING_PAGES = 2 * CHUNK_PAGES
PAGE_ROWS = PAGE_SIZE * MOBA_HEADS
LOG2_RET_DK = RET_DK.bit_length() - 1
LOG2_DEC_SEQ = DEC_SEQ.bit_length() - 1
LOG2_MOBA_HEADS = MOBA_HEADS.bit_length() - 1
NEG_INF = float("-inf")
LOG2_E = 1.4426950408889634

_NT = (((1,), (1,)), ((), ()))
_TN = (((0,), (0,)), ((), ()))


def _log_decay():
    return np.log1p(-np.exp2(-5.0 - np.arange(RET_HEADS, dtype=np.float64)))


N_MOD = 6
MOD_ROWS = BATCH + DEC_BATCH


def _adaln_body(cp_ref, cs_ref, w_ref, b_ref, o_ref):
    w = w_ref[...].astype(BF16)
    for c_ref, lo in ((cp_ref, 0), (cs_ref, BATCH)):
        c = c_ref[...]
        a = (c * jax.nn.sigmoid(c)).astype(BF16)
        o_ref[lo:lo + c.shape[0], :] = jnp.dot(a, w, preferred_element_type=F32) + b_ref[...]


def _adaln(c_prompt, c_sample, w_ada, b_ada):
    return pl.pallas_call(
        _adaln_body,
        grid=(N_MOD,),
        in_specs=[pl.BlockSpec(c_prompt.shape, lambda j: (0, 0)),
                  pl.BlockSpec(c_sample.shape, lambda j: (0, 0)),
                  pl.BlockSpec((D_MODEL, D_MODEL), lambda j: (0, j)),
                  pl.BlockSpec((1, D_MODEL), lambda j: (0, j))],
        out_specs=pl.BlockSpec((None, MOD_ROWS, D_MODEL), lambda j: (j, 0, 0)),
        out_shape=jax.ShapeDtypeStruct((N_MOD, MOD_ROWS, D_MODEL), F32),
        name="adaln",
    )(c_prompt, c_sample, w_ada, b_ada)


def _modulation(ref, sample_group):
    if sample_group:
        rows = ref[BATCH:BATCH + DEC_BATCH, :]
        return jnp.broadcast_to(rows[:, None, :], (DEC_BATCH, DEC_SEQ, D_MODEL)).reshape(N_SAMPLE_TOK, D_MODEL)
    return ref[pl.ds(pl.program_id(0), 1), :]


def _mod_spec(term):
    return pl.BlockSpec((None, MOD_ROWS, D_MODEL), lambda *_: (term, 0, 0))


def _rope_tables(pos, head_dim):
    half = head_dim // 2
    inv_freq = np.power(ROPE_THETA, -np.arange(half, dtype=np.float64) / half)
    ang = pos.astype(np.float64)[:, None] * inv_freq[None, :]
    cos, sin = np.cos(ang), np.sin(ang)
    reps = LANES // head_dim
    cos_t = np.tile(np.concatenate([cos, cos], axis=-1), (1, reps))
    sin_t = np.tile(np.concatenate([-sin, sin], axis=-1), (1, reps))
    return jnp.asarray(cos_t, dtype=F32), jnp.asarray(sin_t, dtype=F32)


def _inproj_body(sample_group, n_casts, x_ref, sc_ref, sh_ref, w_ref, cm_ref, sm_ref, cr_ref, sr_ref, *refs):
    cast_in, refs = refs[:n_casts], refs[n_casts:]
    rq_ref, rk_ref, rv_ref, rg_ref, mq_ref, ko_ref, vo_ref = refs[:7]
    for src, dst in zip(cast_in, refs[7:]):
        dst[...] = src[...].astype(dst.dtype)
    tm = x_ref.shape[0]
    h = (x_ref[...] * (1.0 + _modulation(sc_ref, sample_group)) + _modulation(sh_ref, sample_group)).astype(BF16)

    def proj(lo, width):
        return jnp.dot(h, w_ref[:, lo:lo + width].astype(BF16), preferred_element_type=F32)

    lane = lax.broadcasted_iota(jnp.int32, (tm, LANES), 1)
    low_half = (lane & (RET_DK - 1)) < (RET_DK // 2)
    cr, sr = cr_ref[...], sr_ref[...]
    cm, sm = cm_ref[...], sm_ref[...]

    def rope_ret(z):
        rot = jnp.where(low_half, pltpu.roll(z, LANES - RET_DK // 2, 1), pltpu.roll(z, RET_DK // 2, 1))
        return z * cr + rot * sr

    def rope_moba(z):
        return z * cm + pltpu.roll(z, MOBA_HEAD_DIM // 2, 1) * sm

    zq = proj(OFF_RQ, RET_QK)
    zk = proj(OFF_RK, RET_QK)
    for s in range(RET_QK // LANES):
        sl = slice(s * LANES, (s + 1) * LANES)
        rq_ref[:, sl] = rope_ret(zq[:, sl])
        rk_ref[:, sl] = rope_ret(zk[:, sl]) * (RET_DK ** -0.5)
    rv_ref[...] = proj(OFF_RV, RET_WIDTH).astype(rv_ref.dtype)
    rg_ref[...] = proj(OFF_RG, RET_WIDTH)
    zq = proj(OFF_MQ, MOBA_WIDTH)
    zk = proj(OFF_MK, MOBA_WIDTH)
    zv = proj(OFF_MV, MOBA_WIDTH)
    for hd in range(MOBA_HEADS):
        sl = slice(hd * LANES, (hd + 1) * LANES)
        mq_ref[:, sl] = rope_moba(zq[:, sl]).astype(mq_ref.dtype)
        ko_ref[pl.ds(hd, tm, stride=MOBA_HEADS), :] = rope_moba(zk[:, sl])
        vo_ref[pl.ds(hd, tm, stride=MOBA_HEADS), :] = zv[:, sl]


def _inproj(x2d, mod, sample_group, tab_specs, tabs, w_in, grid, row_map, tm, act_dtype, casts=()):
    t = x2d.shape[0]
    n_steps = int(np.prod(grid))
    wide = lambda w: pl.BlockSpec((tm, w), row_map)
    cast_specs = [pl.BlockSpec((a.shape[0] // n_steps, a.shape[1]), row_map) for a in casts]
    w_spec = pl.BlockSpec((D_MODEL, IN_WIDTH), lambda *_: (0, 0), pipeline_mode=pl.Buffered(1))
    return pl.pallas_call(
        functools.partial(_inproj_body, sample_group, len(casts)),
        grid=grid,
        in_specs=[wide(D_MODEL), _mod_spec(1), _mod_spec(0), w_spec] + tab_specs + cast_specs,
        out_specs=[wide(RET_QK), wide(RET_QK), wide(RET_WIDTH), wide(RET_WIDTH), wide(MOBA_WIDTH),
                   pl.BlockSpec((tm * MOBA_HEADS, LANES), row_map),
                   pl.BlockSpec((tm * MOBA_HEADS, LANES), row_map)] + cast_specs,
        out_shape=[jax.ShapeDtypeStruct((t, RET_QK), F32), jax.ShapeDtypeStruct((t, RET_QK), F32),
                   jax.ShapeDtypeStruct((t, RET_WIDTH), act_dtype), jax.ShapeDtypeStruct((t, RET_WIDTH), F32),
                   jax.ShapeDtypeStruct((t, MOBA_WIDTH), act_dtype),
                   jax.ShapeDtypeStruct((t * MOBA_HEADS, LANES), F32),
                   jax.ShapeDtypeStruct((t * MOBA_HEADS, LANES), F32)]
        + [jax.ShapeDtypeStruct(a.shape, BF16) for a in casts],
        compiler_params=pltpu.CompilerParams(dimension_semantics=("arbitrary",) * len(grid),
                                             vmem_limit_bytes=VMEM_LIMIT),
        name="inproj",
    )(x2d, mod, mod, w_in, *tabs, *casts)


def _group_norm_gate(o, g):
    mu = jnp.mean(o, axis=-1, keepdims=True)
    d = o - mu
    var = jnp.mean(d * d, axis=-1, keepdims=True)
    return d * lax.rsqrt(var + GN_EPS) * (g * jax.nn.sigmoid(g))


def _ret_prompt_tables():
    lg = _log_decay()
    i = np.arange(RET_CHUNK, dtype=np.float64)
    diff = i[:, None] - i[None, :]
    dmat = np.where(diff >= 0, np.exp(np.maximum(diff, 0.0)[None] * lg[:, None, None]), 0.0)
    lane_head = np.arange(RET_QK) // RET_DK
    qd = np.exp((i[:, None] + 1.0) * lg[lane_head][None, :])
    kd = np.exp((RET_CHUNK - 1.0 - i)[:, None] * lg[lane_head][None, :])
    return [jnp.asarray(a, dtype=F32) for a in (dmat, np.concatenate([qd, kd], axis=1))]


def _ret_prompt_body(q_ref, k_ref, v_ref, g_ref, dmat_ref, qkd_ref, o_ref, st_ref):
    lane_head = lax.broadcasted_iota(jnp.int32, (RET_CHUNK, RET_QK), 1) >> LOG2_RET_DK
    chunk_rows = lambda c: slice(c * RET_CHUNK, (c + 1) * RET_CHUNK)
    chunk_decay = [float(np.exp(RET_CHUNK * lg)) for lg in _log_decay()]
    head_block = lambda h: (slice(h * RET_DK, (h + 1) * RET_DK), slice(h * RET_DV, (h + 1) * RET_DV))

    def block_diagonal(state):
        zero = jnp.zeros((RET_DK, RET_DV), BF16)
        return jnp.concatenate(
            [jnp.concatenate([state[h].astype(BF16) if c == h else zero for c in range(RET_HEADS)], axis=1)
             for h in range(RET_HEADS)], axis=0)

    def first_matmuls(c, state):
        rows = chunk_rows(c)
        q = q_ref[rows, :]
        k = k_ref[rows, :]
        v = v_ref[rows, :]
        kb = k.astype(BF16)
        scores = [lax.dot_general(jnp.where(lane_head == hd, q, 0.0).astype(BF16), kb, _NT,
                                  preferred_element_type=F32) for hd in range(RET_HEADS)]
        cross = jnp.dot((q * qkd_ref[:, :RET_QK]).astype(BF16), block_diagonal(state),
                        preferred_element_type=F32)
        kv = lax.dot_general((k * qkd_ref[:, RET_QK:]).astype(BF16), v, _TN, preferred_element_type=F32)
        new_state = [chunk_decay[h] * state[h] + kv[head_block(h)] for h in range(RET_HEADS)]
        return (scores, cross, v), new_state

    def second_matmuls(c, scores, cross, v):
        rows = chunk_rows(c)
        g = g_ref[rows, :]
        decayed = [(scores[hd] * dmat_ref[hd]).astype(BF16) for hd in range(RET_HEADS)]
        for hd in range(RET_HEADS):
            sl = slice(hd * RET_DV, (hd + 1) * RET_DV)
            inner = jnp.dot(decayed[hd], v[:, sl], preferred_element_type=F32)
            o_ref[rows, sl] = _group_norm_gate(inner + cross[:, sl], g[:, sl]).astype(o_ref.dtype)

    n_chunks = SEQ // RET_CHUNK
    ahead, state = first_matmuls(0, [jnp.zeros((RET_DK, RET_DV), F32)] * RET_HEADS)
    for c in range(n_chunks):
        current = ahead
        if c + 1 < n_chunks:
            ahead, state = first_matmuls(c + 1, state)
        second_matmuls(c, *current)
    for hd in range(RET_HEADS):
        st_ref[hd] = state[hd]


def _ret_prompt(rq, rk, rv, rg):
    tabs = _ret_prompt_tables()
    seq = lambda w: pl.BlockSpec((SEQ, w), lambda b: (b, 0))
    const = lambda a: pl.BlockSpec(a.shape, lambda b: (0,) * a.ndim)
    return pl.pallas_call(
        _ret_prompt_body,
        grid=(BATCH,),
        in_specs=[seq(RET_QK), seq(RET_QK), seq(RET_WIDTH), seq(RET_WIDTH)] + [const(a) for a in tabs],
        out_specs=[seq(RET_WIDTH), pl.BlockSpec((None, RET_HEADS, RET_DK, RET_DV), lambda b: (b, 0, 0, 0))],
        out_shape=[jax.ShapeDtypeStruct((BATCH * SEQ, RET_WIDTH), BF16),
                   jax.ShapeDtypeStruct((BATCH, RET_HEADS, RET_DK, RET_DV), F32)],
        compiler_params=pltpu.CompilerParams(dimension_semantics=("arbitrary",),
                                             vmem_limit_bytes=VMEM_LIMIT),
        name="ret_prompt",
    )(rq, rk, rv, rg, *tabs)


def _ret_sample_tables():
    lg = _log_decay()
    t = np.arange(N_SAMPLE_TOK) % DEC_SEQ
    seq_id = np.arange(N_SAMPLE_TOK) // DEC_SEQ
    diff = (t[:, None] - t[None, :]).astype(np.float64)
    same_seq = seq_id[:, None] == seq_id[None, :]
    dmat = np.where(same_seq[None] & (diff >= 0)[None],
                    np.exp(np.maximum(diff, 0.0)[None] * lg[:, None, None]), 0.0)
    lane_head = np.arange(RET_QK) // RET_DK
    qd = np.exp((t[:, None] + 1.0) * lg[lane_head][None, :])
    kd = np.exp((DEC_SEQ - 1.0 - t)[:, None] * lg[lane_head][None, :])
    return [jnp.asarray(a, dtype=F32) for a in (dmat, np.concatenate([qd, kd], axis=1))]


def _ret_sample_body(q_ref, k_ref, v_ref, g_ref, st_ref, dmat_ref, qkd_ref, o_ref, sto_ref):
    lg = _log_decay()
    q = q_ref[...]
    k = k_ref[...]
    kb = k.astype(BF16)
    qdec = q * qkd_ref[:, :RET_QK]
    kdec = k * qkd_ref[:, RET_QK:]
    vb = v_ref[...].astype(BF16)
    g = g_ref[...]
    lane = lax.broadcasted_iota(jnp.int32, (N_SAMPLE_TOK, LANES), 1)
    lane_head = lax.broadcasted_iota(jnp.int32, (N_SAMPLE_TOK, RET_QK), 1) >> LOG2_RET_DK
    n_state_rows = DEC_BATCH * RET_DK
    own_seq = ((lax.broadcasted_iota(jnp.int32, (N_SAMPLE_TOK, n_state_rows), 0) >> LOG2_DEC_SEQ)
               == (lax.broadcasted_iota(jnp.int32, (N_SAMPLE_TOK, n_state_rows), 1) >> LOG2_RET_DK))
    for hd in range(RET_HEADS):
        sl = slice(hd * RET_DV, (hd + 1) * RET_DV)
        qm = jnp.where(lane_head == hd, q, 0.0).astype(BF16)
        s = lax.dot_general(qm, kb, _NT, preferred_element_type=F32) * dmat_ref[hd]
        inner = jnp.dot(s.astype(BF16), vb[:, sl], preferred_element_type=F32)

        def expand(z):
            slab = z[:, (hd // 2) * LANES:(hd // 2 + 1) * LANES]
            other = pltpu.roll(slab, RET_DK, 1)
            in_low = lane < RET_DK
            both = jnp.where(in_low, slab, other) if hd % 2 == 0 else jnp.where(in_low, other, slab)
            tiled = jnp.concatenate([both] * (n_state_rows // LANES), axis=1)
            return jnp.where(own_seq, tiled, 0.0).astype(BF16)

        st = st_ref[:, hd].reshape(n_state_rows, RET_DV)
        cross = jnp.dot(expand(qdec), st.astype(BF16), preferred_element_type=F32)
        o_ref[:, sl] = _group_norm_gate(inner + cross, g[:, sl])
        kv = lax.dot_general(expand(kdec), vb[:, sl], _TN, preferred_element_type=F32)
        new = float(np.exp(DEC_SEQ * lg[hd])) * st + kv
        sto_ref[:, hd] = new.reshape(DEC_BATCH, RET_DK, RET_DV)


def _ret_sample(rq, rk, rv, rg, state):
    tabs = _ret_sample_tables()
    full = lambda a: pl.BlockSpec(a.shape, lambda i: (0,) * a.ndim)
    args = (rq, rk, rv, rg, state, *tabs)
    return pl.pallas_call(
        _ret_sample_body,
        grid=(1,),
        in_specs=[full(a) for a in args],
        out_specs=[pl.BlockSpec((N_SAMPLE_TOK, RET_WIDTH), lambda i: (0, 0)),
                   pl.BlockSpec(state.shape, lambda i: (0, 0, 0, 0))],
        out_shape=[jax.ShapeDtypeStruct((N_SAMPLE_TOK, RET_WIDTH), F32),
                   jax.ShapeDtypeStruct(state.shape, F32)],
        compiler_params=pltpu.CompilerParams(dimension_semantics=("arbitrary",),
                                             vmem_limit_bytes=VMEM_LIMIT),
        name="ret_sample",
    )(*args)


def _moba_prompt_body(q_ref, k_ref, v_ref, o_ref):
    hd = pl.program_id(1)
    n_blocks = SEQ // MOBA_BLOCK
    exp2_scale = MOBA_HEAD_DIM ** -0.5 * LOG2_E
    k32 = k_ref[pl.ds(hd, SEQ, stride=MOBA_HEADS), :]
    kb = k32.astype(BF16)
    vt = v_ref[pl.ds(hd, SEQ, stride=MOBA_HEADS), :].T.astype(BF16)
    kmean = jnp.sum(k32.reshape(n_blocks, MOBA_BLOCK, MOBA_HEAD_DIM), axis=1) * (1.0 / MOBA_BLOCK)
    kmb = kmean.astype(BF16)
    key_id = lax.broadcasted_iota(jnp.int32, (MOBA_BLOCK, MOBA_BLOCK), 0)
    qry_id = lax.broadcasted_iota(jnp.int32, (MOBA_BLOCK, MOBA_BLOCK), 1)
    causal = key_id <= qry_id

    blk = lambda n: slice(n * MOBA_BLOCK, (n + 1) * MOBA_BLOCK)

    def score_matmuls(i):
        qi = q_ref[blk(i), :]
        st = [lax.dot_general(kb[blk(n)], qi, _NT, preferred_element_type=F32) for n in range(i + 1)]
        gt = lax.dot_general(kmb, qi, _NT, preferred_element_type=F32) if i > MOBA_TOPK else None
        return st, gt

    ahead = score_matmuls(n_blocks - 1)
    for i in reversed(range(n_blocks)):
        st, gt = ahead
        if i > 0:
            ahead = score_matmuls(i - 1)
        st[i] = jnp.where(causal, st[i], NEG_INF)
        if i > MOBA_TOPK:
            for n in range(i):
                beats = jnp.zeros((1, MOBA_BLOCK), F32)
                for mm in range(i):
                    if mm == n:
                        continue
                    win = (gt[mm:mm + 1] >= gt[n:n + 1]) if mm < n else (gt[mm:mm + 1] > gt[n:n + 1])
                    beats = beats + win.astype(F32)
                st[n] = st[n] + jnp.where(beats < MOBA_TOPK, 0.0, NEG_INF)
        m = functools.reduce(jnp.maximum, [jnp.max(s, axis=0, keepdims=True) for s in st])
        l = jnp.zeros((1, MOBA_BLOCK), F32)
        acc = jnp.zeros((MOBA_HEAD_DIM, MOBA_BLOCK), F32)
        for n in range(i + 1):
            e = jnp.exp2((st[n] - m) * exp2_scale)
            l = l + jnp.sum(e, axis=0, keepdims=True)
            acc = acc + jnp.dot(vt[:, blk(n)], e.astype(BF16), preferred_element_type=F32)
        o_ref[blk(i), :] = (acc / l).T.astype(o_ref.dtype)


def _moba_prompt(mq, k2d, v2d):
    kv_spec = pl.BlockSpec((SEQ * MOBA_HEADS, LANES), lambda b, h: (b, 0))
    return pl.pallas_call(
        _moba_prompt_body,
        grid=(BATCH, MOBA_HEADS),
        in_specs=[pl.BlockSpec((SEQ, MOBA_HEAD_DIM), lambda b, h: (b, h)), kv_spec, kv_spec],
        out_specs=pl.BlockSpec((SEQ, MOBA_HEAD_DIM), lambda b, h: (b, h)),
        out_shape=jax.ShapeDtypeStruct((BATCH * SEQ, MOBA_WIDTH), BF16),
        compiler_params=pltpu.CompilerParams(dimension_semantics=("arbitrary", "arbitrary"),
                                             vmem_limit_bytes=VMEM_LIMIT),
        name="moba_prompt",
    )(mq, k2d, v2d)


class _SampleMoba:
    n_chunks = PAGES_PER_SEQ // CHUNK_PAGES
    n_rows = MOBA_HEADS * DEC_SEQ
    n_blocks = PAGES_PER_SEQ * PAGE_SIZE // MOBA_BLOCK
    pages_per_block = MOBA_BLOCK // PAGE_SIZE

    def __init__(self, pt_ref, seq, n_seqs, q_ref, kn_ref, vn_ref, kc_ref, vc_ref, o_ref, ring, sem, s_scr):
        self.pt_ref, self.seq, self.n_seqs = pt_ref, seq, n_seqs
        self.q_ref, self.kn_ref, self.vn_ref = q_ref, kn_ref, vn_ref
        self.kc_ref, self.vc_ref, self.o_ref = kc_ref, vc_ref, o_ref
        self.ring, self.sem, self.s_scr = ring, sem, s_scr

    def _page_copy(self, cache_ref, row0, slot):
        return pltpu.make_async_copy(cache_ref.at[pl.ds(row0, PAGE_ROWS)], self.ring.at[slot],
                                     self.sem.at[slot // CHUNK_PAGES])

    def _start_chunk(self, cache_ref, seq, chunk):
        for r in range(CHUNK_PAGES):
            page = self.pt_ref[seq * PAGES_PER_SEQ + chunk * CHUNK_PAGES + r]
            self._page_copy(cache_ref, pl.multiple_of(page * PAGE_ROWS, PAGE_ROWS),
                            (chunk % 2) * CHUNK_PAGES + r).start()

    def _wait_chunk(self, cache_ref, chunk):
        for r in range(CHUNK_PAGES):
            self._page_copy(cache_ref, 0, (chunk % 2) * CHUNK_PAGES + r).wait()

    def prologue(self):
        @pl.when(self.seq == 0)
        def _():
            self._start_chunk(self.kc_ref, self.seq, 0)
            self._start_chunk(self.kc_ref, self.seq, 1)

        self.q = jnp.concatenate(
            [self.q_ref[:, hd * MOBA_HEAD_DIM:(hd + 1) * MOBA_HEAD_DIM] for hd in range(MOBA_HEADS)], axis=0
        ).astype(BF16)
        row_head = lax.broadcasted_iota(jnp.int32, (self.n_rows, LANES), 0) >> LOG2_DEC_SEQ
        col_head = lax.broadcasted_iota(jnp.int32, (self.n_rows, LANES), 1) & (MOBA_HEADS - 1)
        self.same_head = row_head == col_head
        self.head_bias = jnp.where(self.same_head, 0.0, NEG_INF)
        self.block_sum, self.block_max = [], []

    @staticmethod
    def _slabs(x):
        return [x[:, j * LANES:(j + 1) * LANES] for j in range(PAGE_ROWS // LANES)]

    def wait_k(self, c):
        self._wait_chunk(self.kc_ref, c)

    def k_pages(self, c, first, last):
        assert first % self.pages_per_block == 0 and last % self.pages_per_block == 0
        for r0 in range(first, last, self.pages_per_block):
            tot = jnp.zeros((self.n_rows, LANES), F32)
            top = jnp.full((self.n_rows, LANES), NEG_INF, F32)
            for r in range(r0, r0 + self.pages_per_block):
                page = self.ring[(c % 2) * CHUNK_PAGES + r].astype(BF16)
                s = lax.dot_general(self.q, page, _NT, preferred_element_type=F32)
                self.s_scr[c * CHUNK_PAGES + r] = s
                for slab in self._slabs(s):
                    tot = tot + jnp.where(self.same_head, slab, 0.0)
                    top = jnp.maximum(top, slab + self.head_bias)
            self.block_sum.append(jnp.sum(tot, axis=-1, keepdims=True))
            self.block_max.append(jnp.max(top, axis=-1, keepdims=True))

    def refill_after_k(self, c):
        if c + 2 < self.n_chunks:
            self._start_chunk(self.kc_ref, self.seq, c + 2)
        else:
            self._start_chunk(self.vc_ref, self.seq, c + 2 - self.n_chunks)

    def select(self):
        n_rows, gs = self.n_rows, self.block_sum
        self.exp2_scale = MOBA_HEAD_DIM ** -0.5 * LOG2_E
        lane = lax.broadcasted_iota(jnp.int32, (n_rows, LANES), 1)
        g_all = jnp.full((n_rows, LANES), NEG_INF, F32)
        for n in range(self.n_blocks):
            g_all = jnp.where(lane == n, gs[n], g_all)
        self.keep_bias = []
        for n in range(self.n_blocks):
            wins = (g_all > gs[n]) | ((g_all == gs[n]) & (lane < n))
            beats = jnp.sum(wins.astype(F32), axis=-1, keepdims=True)
            self.keep_bias.append(jnp.where(beats < MOBA_TOPK, 0.0, NEG_INF))
        s_own = lax.dot_general(self.q, self.kn_ref[...].astype(BF16), _NT, preferred_element_type=F32)
        r_id = lax.broadcasted_iota(jnp.int32, (n_rows, n_rows), 0)
        c_id = lax.broadcasted_iota(jnp.int32, (n_rows, n_rows), 1)
        own_ok = ((c_id & (MOBA_HEADS - 1)) == (r_id >> LOG2_DEC_SEQ)) & ((c_id >> LOG2_MOBA_HEADS) <= (r_id & (DEC_SEQ - 1)))
        s_own = jnp.where(own_ok, s_own, NEG_INF)
        m = jnp.max(s_own, axis=-1, keepdims=True)
        for n in range(self.n_blocks):
            m = jnp.maximum(m, self.block_max[n] + self.keep_bias[n])
        self.m = m
        self.lsum = jnp.zeros((n_rows, LANES), F32)
        e_own = jnp.exp2((s_own - m) * self.exp2_scale)
        return (jnp.sum(e_own, axis=-1, keepdims=True),
                jnp.dot(e_own.astype(BF16), self.vn_ref[...].astype(BF16), preferred_element_type=F32))

    def wait_v(self, c):
        self._wait_chunk(self.vc_ref, c)

    def v_pages(self, c, first, last, acc):
        assert first % self.pages_per_block == 0 and last % self.pages_per_block == 0
        for r0 in range(first, last, self.pages_per_block):
            shift = self.head_bias + (self.keep_bias[(c * CHUNK_PAGES + r0) // self.pages_per_block] - self.m)
            for r in range(r0, r0 + self.pages_per_block):
                e = [jnp.exp2((slab + shift) * self.exp2_scale) for slab in self._slabs(self.s_scr[c * CHUNK_PAGES + r])]
                self.lsum = self.lsum + functools.reduce(jnp.add, e)
                page = self.ring[(c % 2) * CHUNK_PAGES + r].astype(BF16)
                acc = acc + jnp.dot(jnp.concatenate(e, axis=1).astype(BF16), page, preferred_element_type=F32)
        return acc

    def refill_after_v(self, c):
        if c + 2 < self.n_chunks:
            self._start_chunk(self.vc_ref, self.seq, c + 2)
        else:
            @pl.when(self.seq + 1 < self.n_seqs)
            def _():
                self._start_chunk(self.kc_ref, self.seq + 1, c + 2 - self.n_chunks)

    def finish(self, l_own, acc):
        out = acc / (l_own + jnp.sum(self.lsum, axis=-1, keepdims=True))
        for hd in range(MOBA_HEADS):
            self.o_ref[:, hd * MOBA_HEAD_DIM:(hd + 1) * MOBA_HEAD_DIM] = out[hd * DEC_SEQ:(hd + 1) * DEC_SEQ]


def _layer_norm(x, g, b):
    mu = jnp.mean(x, axis=-1, keepdims=True)
    d = x - mu
    var = jnp.mean(d * d, axis=-1, keepdims=True)
    return d * lax.rsqrt(var + LN_EPS) * g + b


def _out_ffn_body(ar_ref, am_ref, x_ref, ga_ref, shf_ref, scf_ref, gf_ref, wo_ref, g1_ref, b1_ref,
                  wu_ref, wd_ref, g2_ref, b2_ref, y_ref, x1_scr, h_scr, acc_scr):
    c = pl.program_id(0)

    @pl.when(c == 0)
    def _():
        mixed = (jnp.dot(ar_ref[...].astype(BF16), wo_ref[:RET_WIDTH, :], preferred_element_type=F32)
                 + jnp.dot(am_ref[...].astype(BF16), wo_ref[RET_WIDTH:, :], preferred_element_type=F32))
        x1 = _layer_norm(ALPHA * x_ref[...] + _modulation(ga_ref, True) * mixed, g1_ref[...], b1_ref[...])
        x1_scr[...] = x1
        h_scr[...] = (x1 * (1.0 + _modulation(scf_ref, True)) + _modulation(shf_ref, True)).astype(BF16)
        acc_scr[...] = jnp.zeros_like(acc_scr)

    u = jnp.maximum(jnp.dot(h_scr[...], wu_ref[...], preferred_element_type=F32), 0.0)
    acc_scr[...] += jnp.dot((u * u).astype(BF16), wd_ref[...], preferred_element_type=F32)

    @pl.when(c == pl.num_programs(0) - 1)
    def _():
        y_ref[...] = _layer_norm(ALPHA * x1_scr[...] + _modulation(gf_ref, True) * acc_scr[...],
                                 g2_ref[...], b2_ref[...])


def _out_ffn(a_ret, a_moba, x2d, mod, weights):
    w_o, ln1_g, ln1_b, w_up, w_down, ln2_g, ln2_b = weights
    t = x2d.shape[0]
    whole = lambda a: pl.BlockSpec(a.shape, lambda c: (0,) * a.ndim)
    return pl.pallas_call(
        _out_ffn_body,
        grid=(D_FF // D_MODEL,),
        in_specs=[whole(a_ret), whole(a_moba), whole(x2d)] + [_mod_spec(term) for term in (2, 3, 4, 5)]
        + [whole(w_o), whole(ln1_g), whole(ln1_b),
           pl.BlockSpec((D_MODEL, D_MODEL), lambda c: (0, c)), pl.BlockSpec((D_MODEL, D_MODEL), lambda c: (c, 0)),
           whole(ln2_g), whole(ln2_b)],
        out_specs=pl.BlockSpec((t, D_MODEL), lambda c: (0, 0)),
        out_shape=jax.ShapeDtypeStruct((t, D_MODEL), F32),
        scratch_shapes=[pltpu.VMEM((t, D_MODEL), F32), pltpu.VMEM((t, D_MODEL), BF16), pltpu.VMEM((t, D_MODEL), F32)],
        compiler_params=pltpu.CompilerParams(dimension_semantics=("arbitrary",),
                                             vmem_limit_bytes=VMEM_LIMIT),
        name="out_ffn",
    )(a_ret, a_moba, x2d, *([mod] * 4), *weights)


def _out_ffn_moba_body(pt_ref, ar_ref, am_ref, x_ref, ga_ref, shf_ref, scf_ref, gf_ref, wo_ref, g1_ref, b1_ref,
                       wu_ref, wd_ref, g2_ref, b2_ref, q_ref, kn_ref, vn_ref, kc_ref, vc_ref,
                       y_ref, o_ref, ring, sem, s_scr):
    n_seqs = pl.num_programs(0) * pl.num_programs(1)
    seq = pl.program_id(0) * pl.num_programs(1) + pl.program_id(1)
    sm = _SampleMoba(pt_ref, seq, n_seqs, q_ref, kn_ref, vn_ref, kc_ref, vc_ref, o_ref, ring, sem, s_scr)

    def up(c, h):
        u = jnp.maximum(jnp.dot(h, wu_ref[:, c * D_MODEL:(c + 1) * D_MODEL], preferred_element_type=F32), 0.0)
        return (u * u).astype(BF16)

    def down(c, u):
        return jnp.dot(u, wd_ref[c * D_MODEL:(c + 1) * D_MODEL, :], preferred_element_type=F32)

    half = CHUNK_PAGES // 2
    sm.prologue()

    sm.wait_k(0)
    sm.k_pages(0, 0, half)
    mixed = jnp.dot(ar_ref[...].astype(BF16), wo_ref[:RET_WIDTH, :], preferred_element_type=F32)
    sm.k_pages(0, half, CHUNK_PAGES)
    mixed = mixed + jnp.dot(am_ref[...].astype(BF16), wo_ref[RET_WIDTH:, :], preferred_element_type=F32)
    x1 = _layer_norm(ALPHA * x_ref[...] + _modulation(ga_ref, False) * mixed, g1_ref[...], b1_ref[...])
    h = (x1 * (1.0 + _modulation(scf_ref, False)) + _modulation(shf_ref, False)).astype(BF16)
    sm.refill_after_k(0)

    sm.wait_k(1)
    sm.k_pages(1, 0, half)
    u = up(0, h)
    sm.k_pages(1, half, CHUNK_PAGES)
    sm.refill_after_k(1)

    sm.wait_k(2)
    sm.k_pages(2, 0, half)
    acc = down(0, u)
    sm.k_pages(2, half, CHUNK_PAGES)
    sm.refill_after_k(2)

    sm.wait_k(3)
    sm.k_pages(3, 0, half)
    u = up(1, h)
    sm.k_pages(3, half, CHUNK_PAGES)
    sm.refill_after_k(3)

    l, acc_s = sm.select()
    acc = acc + down(1, u)

    sm.wait_v(0)
    acc_s = sm.v_pages(0, 0, half, acc_s)
    u = up(2, h)
    acc_s = sm.v_pages(0, half, CHUNK_PAGES, acc_s)
    sm.refill_after_v(0)

    sm.wait_v(1)
    acc_s = sm.v_pages(1, 0, half, acc_s)
    acc = acc + down(2, u)
    acc_s = sm.v_pages(1, half, CHUNK_PAGES, acc_s)
    sm.refill_after_v(1)

    sm.wait_v(2)
    acc_s = sm.v_pages(2, 0, half, acc_s)
    u = up(3, h)
    acc_s = sm.v_pages(2, half, CHUNK_PAGES, acc_s)
    sm.refill_after_v(2)

    sm.wait_v(3)
    acc_s = sm.v_pages(3, 0, half, acc_s)
    acc = acc + down(3, u)
    acc_s = sm.v_pages(3, half, CHUNK_PAGES, acc_s)
    sm.refill_after_v(3)
    y_ref[...] = _layer_norm(ALPHA * x1 + _modulation(gf_ref, False) * acc, g2_ref[...], b2_ref[...])
    sm.finish(l, acc_s)


def _out_ffn_moba(a_ret, a_moba, x2d, mod, weights, page_table, mq_s, kn2d, vn2d, cache_k2d, cache_v2d, tm):
    t = x2d.shape[0]
    nt = SEQ // tm
    assert BATCH * nt == DEC_BATCH
    n_rows = MOBA_HEADS * DEC_SEQ
    tile = lambda b, i, pt: (b * nt + i, 0)
    wide = lambda w: pl.BlockSpec((tm, w), tile)
    const = lambda a: pl.BlockSpec(a.shape, lambda b, i, pt: (0,) * a.ndim, pipeline_mode=pl.Buffered(1))
    seq_rows = lambda w, n: pl.BlockSpec((n, w), tile)
    hbm = pl.BlockSpec(memory_space=pl.ANY)
    grid_spec = pltpu.PrefetchScalarGridSpec(
        num_scalar_prefetch=1,
        grid=(BATCH, nt),
        in_specs=[wide(RET_WIDTH), wide(MOBA_WIDTH), wide(D_MODEL)] + [_mod_spec(term) for term in (2, 3, 4, 5)]
        + [const(a) for a in weights]
        + [seq_rows(MOBA_WIDTH, DEC_SEQ), seq_rows(LANES, n_rows), seq_rows(LANES, n_rows), hbm, hbm],
        out_specs=[wide(D_MODEL), seq_rows(MOBA_WIDTH, DEC_SEQ)],
        scratch_shapes=[pltpu.VMEM((RING_PAGES, PAGE_ROWS, LANES), F32),
                        pltpu.SemaphoreType.DMA((RING_PAGES // CHUNK_PAGES,)),
                        pltpu.VMEM((PAGES_PER_SEQ, n_rows, PAGE_ROWS), F32)],
    )
    return pl.pallas_call(
        _out_ffn_moba_body,
        grid_spec=grid_spec,
        out_shape=[jax.ShapeDtypeStruct((t, D_MODEL), F32),
                   jax.ShapeDtypeStruct((N_SAMPLE_TOK, MOBA_WIDTH), F32)],
        compiler_params=pltpu.CompilerParams(dimension_semantics=("arbitrary", "arbitrary"),
                                             vmem_limit_bytes=VMEM_LIMIT_FUSED),
        name="out_ffn_moba",
    )(page_table.reshape(-1), a_ret, a_moba, x2d, *([mod] * 4), *weights,
      mq_s, kn2d, vn2d, cache_k2d, cache_v2d)


def kernel(x_prompt, x_sample, cache_k, cache_v, state_ret, page_table, c_prompt, c_sample,
           w_ada, b_ada, w_in, w_o, ln1_g, ln1_b, w_up, w_down, ln2_g, ln2_b):
    n_prompt_tok = BATCH * SEQ
    past_len = page_table.shape[1] * PAGE_SIZE

    mod = _adaln(c_prompt, c_sample, w_ada[0], b_ada)

    tm = 512
    nt = SEQ // tm
    p_row = lambda b, i: (b * nt + i, 0)
    p_tab = pl.BlockSpec((tm, LANES), lambda b, i: (i, 0))
    s_row = lambda i: (0, 0)
    s_tab = pl.BlockSpec((N_SAMPLE_TOK, LANES), s_row)

    pos_p = np.arange(SEQ, dtype=np.int32)
    pos_s = np.tile(past_len + np.arange(DEC_SEQ, dtype=np.int32), DEC_BATCH)
    tabs_p = _rope_tables(pos_p, MOBA_HEAD_DIM) + _rope_tables(pos_p, RET_DK)
    tabs_s = _rope_tables(pos_s, MOBA_HEAD_DIM) + _rope_tables(pos_s, RET_DK)

    xp = x_prompt.reshape(n_prompt_tok, D_MODEL)
    rq, rk, rv, rg, mq, k_p, v_p, w_o_b, w_up_b, w_down_b = _inproj(
        xp, mod, False, [p_tab] * 4, tabs_p, w_in[0],
        (BATCH, nt), p_row, tm, BF16, casts=(w_o[0], w_up[0], w_down[0]))
    weights = (w_o_b, ln1_g, ln1_b, w_up_b, w_down_b, ln2_g, ln2_b)
    a_ret, state_p = _ret_prompt(rq, rk, rv, rg)
    a_moba = _moba_prompt(mq, k_p, v_p)

    xs = x_sample.reshape(N_SAMPLE_TOK, D_MODEL)
    rq_s, rk_s, rv_s, rg_s, mq_s, k_s, v_s = _inproj(
        xs, mod, True, [s_tab] * 4, tabs_s, w_in[0],
        (1,), s_row, N_SAMPLE_TOK, F32)
    a_ret_s, state_s = _ret_sample(rq_s, rk_s, rv_s, rg_s, state_ret[0])

    cache_rows = cache_k.shape[1] * PAGE_ROWS
    y_p, a_moba_s = _out_ffn_moba(a_ret, a_moba, xp, mod, weights, page_table, mq_s, k_s, v_s,
                                  cache_k.reshape(cache_rows, LANES), cache_v.reshape(cache_rows, LANES), tm)
    y_s = _out_ffn(a_ret_s, a_moba_s, xs, mod, weights)

    kv_p_shape = (DEPTH, BATCH, SEQ, MOBA_HEADS, MOBA_HEAD_DIM)
    kv_s_shape = (DEPTH, DEC_BATCH, DEC_SEQ, MOBA_HEADS, MOBA_HEAD_DIM)
    return (y_p.reshape(BATCH, SEQ, D_MODEL),
            y_s.reshape(DEC_BATCH, DEC_SEQ, D_MODEL),
            k_p.reshape(kv_p_shape), v_p.reshape(kv_p_shape), state_p[None],
            k_s.reshape(kv_s_shape), v_s.reshape(kv_s_shape), state_s[None])
```

```python
import functools

import numpy as np
import jax
import jax.numpy as jnp
from jax import lax
from jax.experimental import pallas as pl
from jax.experimental.pallas import tpu as pltpu

F32 = jnp.float32
BF16 = jnp.bfloat16

D_MODEL = 1024
BATCH = 8
SEQ = 2048
DEC_BATCH = 32
DEC_SEQ = 8
PAGE_SIZE = 128
RET_HEADS = 4
RET_DK = 64
RET_DV = 128
RET_CHUNK = 128
MOBA_HEADS = 4
MOBA_HEAD_DIM = 128
MOBA_BLOCK = 256
MOBA_TOPK = 3
D_FF = 4 * D_MODEL
ROPE_THETA = 10000.0
LN_EPS = 1e-5
GN_EPS = 1e-6
DEPTH = 1
ALPHA = (2 * DEPTH) ** 0.25
RET_QK = RET_HEADS * RET_DK
RET_WIDTH = RET_HEADS * RET_DV
MOBA_WIDTH = MOBA_HEADS * MOBA_HEAD_DIM
IN_WIDTH = 2 * RET_QK + 2 * RET_WIDTH + 3 * MOBA_WIDTH
OFF_RQ, OFF_RK, OFF_RV, OFF_RG = 0, 256, 512, 1024
OFF_MQ, OFF_MK, OFF_MV = 1536, 2048, 2560
LANES = 128
VMEM_BYTES_V7X = 64 * 1024 * 1024
VMEM_LIMIT = VMEM_BYTES_V7X * 3 // 4
VMEM_LIMIT_FUSED = VMEM_BYTES_V7X * 29 // 32
N_SAMPLE_TOK = DEC_BATCH * DEC_SEQ
PAGES_PER_SEQ = 64
CHUNK_PAGES = 16
RING_PAGES = 2 * CHUNK_PAGES
PAGE_ROWS = PAGE_SIZE * MOBA_HEADS
LOG2_RET_DK = RET_DK.bit_length() - 1
LOG2_DEC_SEQ = DEC_SEQ.bit_length() - 1
LOG2_MOBA_HEADS = MOBA_HEADS.bit_length() - 1
NEG_INF = float("-inf")
LOG2_E = 1.4426950408889634

_NT = (((1,), (1,)), ((), ()))
_TN = (((0,), (0,)), ((), ()))


def _log_decay():
    return np.log1p(-np.exp2(-5.0 - np.arange(RET_HEADS, dtype=np.float64)))


N_MOD = 6
MOD_ROWS = BATCH + DEC_BATCH


def _adaln_body(cp_ref, cs_ref, w_ref, b_ref, o_ref):
    w = w_ref[...].astype(BF16)
    for c_ref, lo in ((cp_ref, 0), (cs_ref, BATCH)):
        c = c_ref[...]
        a = (c * jax.nn.sigmoid(c)).astype(BF16)
        o_ref[lo:lo + c.shape[0], :] = jnp.dot(a, w, preferred_element_type=F32) + b_ref[...]


def _adaln(c_prompt, c_sample, w_ada, b_ada):
    return pl.pallas_call(
        _adaln_body,
        grid=(N_MOD,),
        in_specs=[pl.BlockSpec(c_prompt.shape, lambda j: (0, 0)),
                  pl.BlockSpec(c_sample.shape, lambda j: (0, 0)),
                  pl.BlockSpec((D_MODEL, D_MODEL), lambda j: (0, j)),
                  pl.BlockSpec((1, D_MODEL), lambda j: (0, j))],
        out_specs=pl.BlockSpec((None, MOD_ROWS, D_MODEL), lambda j: (j, 0, 0)),
        out_shape=jax.ShapeDtypeStruct((N_MOD, MOD_ROWS, D_MODEL), F32),
        name="adaln",
    )(c_prompt, c_sample, w_ada, b_ada)


def _modulation(ref, sample_group):
    if sample_group:
        rows = ref[BATCH:BATCH + DEC_BATCH, :]
        return jnp.broadcast_to(rows[:, None, :], (DEC_BATCH, DEC_SEQ, D_MODEL)).reshape(N_SAMPLE_TOK, D_MODEL)
    return ref[pl.ds(pl.program_id(0), 1), :]


def _mod_spec(term):
    return pl.BlockSpec((None, MOD_ROWS, D_MODEL), lambda *_: (term, 0, 0))


def _rope_tables(pos, head_dim):
    half = head_dim // 2
    inv_freq = np.power(ROPE_THETA, -np.arange(half, dtype=np.float64) / half)
    ang = pos.astype(np.float64)[:, None] * inv_freq[None, :]
    cos, sin = np.cos(ang), np.sin(ang)
    reps = LANES // head_dim
    cos_t = np.tile(np.concatenate([cos, cos], axis=-1), (1, reps))
    sin_t = np.tile(np.concatenate([-sin, sin], axis=-1), (1, reps))
    return jnp.asarray(cos_t, dtype=F32), jnp.asarray(sin_t, dtype=F32)


def _inproj_body(sample_group, n_casts, x_ref, sc_ref, sh_ref, w_ref, cm_ref, sm_ref, cr_ref, sr_ref, *refs):
    cast_in, refs = refs[:n_casts], refs[n_casts:]
    rq_ref, rk_ref, rv_ref, rg_ref, mq_ref, ko_ref, vo_ref = refs[:7]
    for src, dst in zip(cast_in, refs[7:]):
        dst[...] = src[...].astype(dst.dtype)
    tm = x_ref.shape[0]
    h = (x_ref[...] * (1.0 + _modulation(sc_ref, sample_group)) + _modulation(sh_ref, sample_group)).astype(BF16)

    def proj(lo, width):
        return jnp.dot(h, w_ref[:, lo:lo + width].astype(BF16), preferred_element_type=F32)

    lane = lax.broadcasted_iota(jnp.int32, (tm, LANES), 1)
    low_half = (lane & (RET_DK - 1)) < (RET_DK // 2)
    cr, sr = cr_ref[...], sr_ref[...]
    cm, sm = cm_ref[...], sm_ref[...]

    def rope_ret(z):
        rot = jnp.where(low_half, pltpu.roll(z, LANES - RET_DK // 2, 1), pltpu.roll(z, RET_DK // 2, 1))
        return z * cr + rot * sr

    def rope_moba(z):
        return z * cm + pltpu.roll(z, MOBA_HEAD_DIM // 2, 1) * sm

    zq = proj(OFF_RQ, RET_QK)
    zk = proj(OFF_RK, RET_QK)
    for s in range(RET_QK // LANES):
        sl = slice(s * LANES, (s + 1) * LANES)
        rq_ref[:, sl] = rope_ret(zq[:, sl])
        rk_ref[:, sl] = rope_ret(zk[:, sl]) * (RET_DK ** -0.5)
    rv_ref[...] = proj(OFF_RV, RET_WIDTH).astype(rv_ref.dtype)
    rg_ref[...] = proj(OFF_RG, RET_WIDTH)
    zq = proj(OFF_MQ, MOBA_WIDTH)
    zk = proj(OFF_MK, MOBA_WIDTH)
    zv = proj(OFF_MV, MOBA_WIDTH)
    for hd in range(MOBA_HEADS):
        sl = slice(hd * LANES, (hd + 1) * LANES)
        mq_ref[:, sl] = rope_moba(zq[:, sl]).astype(mq_ref.dtype)
        ko_ref[pl.ds(hd, tm, stride=MOBA_HEADS), :] = rope_moba(zk[:, sl])
        vo_ref[pl.ds(hd, tm, stride=MOBA_HEADS), :] = zv[:, sl]


def _inproj(x2d, mod, sample_group, tab_specs, tabs, w_in, grid, row_map, tm, act_dtype, casts=()):
    t = x2d.shape[0]
    n_steps = int(np.prod(grid))
    wide = lambda w: pl.BlockSpec((tm, w), row_map)
    cast_specs = [pl.BlockSpec((a.shape[0] // n_steps, a.shape[1]), row_map) for a in casts]
    w_spec = pl.BlockSpec((D_MODEL, IN_WIDTH), lambda *_: (0, 0), pipeline_mode=pl.Buffered(1))
    return pl.pallas_call(
        functools.partial(_inproj_body, sample_group, len(casts)),
        grid=grid,
        in_specs=[wide(D_MODEL), _mod_spec(1), _mod_spec(0), w_spec] + tab_specs + cast_specs,
        out_specs=[wide(RET_QK), wide(RET_QK), wide(RET_WIDTH), wide(RET_WIDTH), wide(MOBA_WIDTH),
                   pl.BlockSpec((tm * MOBA_HEADS, LANES), row_map),
                   pl.BlockSpec((tm * MOBA_HEADS, LANES), row_map)] + cast_specs,
        out_shape=[jax.ShapeDtypeStruct((t, RET_QK), F32), jax.ShapeDtypeStruct((t, RET_QK), F32),
                   jax.ShapeDtypeStruct((t, RET_WIDTH), act_dtype), jax.ShapeDtypeStruct((t, RET_WIDTH), F32),
                   jax.ShapeDtypeStruct((t, MOBA_WIDTH), act_dtype),
                   jax.ShapeDtypeStruct((t * MOBA_HEADS, LANES), F32),
                   jax.ShapeDtypeStruct((t * MOBA_HEADS, LANES), F32)]
        + [jax.ShapeDtypeStruct(a.shape, BF16) for a in casts],
        compiler_params=pltpu.CompilerParams(dimension_semantics=("arbitrary",) * len(grid),
                                             vmem_limit_bytes=VMEM_LIMIT),
        name="inproj",
    )(x2d, mod, mod, w_in, *tabs, *casts)


def _group_norm_gate(o, g):
    mu = jnp.mean(o, axis=-1, keepdims=True)
    d = o - mu
    var = jnp.mean(d * d, axis=-1, keepdims=True)
    return d * lax.rsqrt(var + GN_EPS) * (g * jax.nn.sigmoid(g))


def _ret_prompt_tables():
    lg = _log_decay()
    i = np.arange(RET_CHUNK, dtype=np.float64)
    diff = i[:, None] - i[None, :]
    dmat = np.where(diff >= 0, np.exp(np.maximum(diff, 0.0)[None] * lg[:, None, None]), 0.0)
    lane_head = np.arange(RET_QK) // RET_DK
    qd = np.exp((i[:, None] + 1.0) * lg[lane_head][None, :])
    kd = np.exp((RET_CHUNK - 1.0 - i)[:, None] * lg[lane_head][None, :])
    return [jnp.asarray(a, dtype=F32) for a in (dmat, np.concatenate([qd, kd], axis=1))]


def _ret_prompt_body(q_ref, k_ref, v_ref, g_ref, dmat_ref, qkd_ref, o_ref, st_ref):
    lane_head = lax.broadcasted_iota(jnp.int32, (RET_CHUNK, RET_QK), 1) >> LOG2_RET_DK
    chunk_rows = lambda c: slice(c * RET_CHUNK, (c + 1) * RET_CHUNK)
    chunk_decay = [float(np.exp(RET_CHUNK * lg)) for lg in _log_decay()]
    head_block = lambda h: (slice(h * RET_DK, (h + 1) * RET_DK), slice(h * RET_DV, (h + 1) * RET_DV))

    def block_diagonal(state):
        zero = jnp.zeros((RET_DK, RET_DV), BF16)
        return jnp.concatenate(
            [jnp.concatenate([state[h].astype(BF16) if c == h else zero for c in range(RET_HEADS)], axis=1)
             for h in range(RET_HEADS)], axis=0)

    def first_matmuls(c, state):
        rows = chunk_rows(c)
        q = q_ref[rows, :]
        k = k_ref[rows, :]
        v = v_ref[rows, :]
        kb = k.astype(BF16)
        scores = [lax.dot_general(jnp.where(lane_head == hd, q, 0.0).astype(BF16), kb, _NT,
                                  preferred_element_type=F32) for hd in range(RET_HEADS)]
        cross = jnp.dot((q * qkd_ref[:, :RET_QK]).astype(BF16), block_diagonal(state),
                        preferred_element_type=F32)
        kv = lax.dot_general((k * qkd_ref[:, RET_QK:]).astype(BF16), v, _TN, preferred_element_type=F32)
        new_state = [chunk_decay[h] * state[h] + kv[head_block(h)] for h in range(RET_HEADS)]
        return (scores, cross, v), new_state

    def second_matmuls(c, scores, cross, v):
        rows = chunk_rows(c)
        g = g_ref[rows, :]
        decayed = [(scores[hd] * dmat_ref[hd]).astype(BF16) for hd in range(RET_HEADS)]
        for hd in range(RET_HEADS):
            sl = slice(hd * RET_DV, (hd + 1) * RET_DV)
            inner = jnp.dot(decayed[hd], v[:, sl], preferred_element_type=F32)
            o_ref[rows, sl] = _group_norm_gate(inner + cross[:, sl], g[:, sl]).astype(o_ref.dtype)

    n_chunks = SEQ // RET_CHUNK
    ahead, state = first_matmuls(0, [jnp.zeros((RET_DK, RET_DV), F32)] * RET_HEADS)
    for c in range(n_chunks):
        current = ahead
        if c + 1 < n_chunks:
            ahead, state = first_matmuls(c + 1, state)
        second_matmuls(c, *current)
    for hd in range(RET_HEADS):
        st_ref[hd] = state[hd]


def _ret_prompt(rq, rk, rv, rg):
    tabs = _ret_prompt_tables()
    seq = lambda w: pl.BlockSpec((SEQ, w), lambda b: (b, 0))
    const = lambda a: pl.BlockSpec(a.shape, lambda b: (0,) * a.ndim)
    return pl.pallas_call(
        _ret_prompt_body,
        grid=(BATCH,),
        in_specs=[seq(RET_QK), seq(RET_QK), seq(RET_WIDTH), seq(RET_WIDTH)] + [const(a) for a in tabs],
        out_specs=[seq(RET_WIDTH), pl.BlockSpec((None, RET_HEADS, RET_DK, RET_DV), lambda b: (b, 0, 0, 0))],
        out_shape=[jax.ShapeDtypeStruct((BATCH * SEQ, RET_WIDTH), BF16),
                   jax.ShapeDtypeStruct((BATCH, RET_HEADS, RET_DK, RET_DV), F32)],
        compiler_params=pltpu.CompilerParams(dimension_semantics=("arbitrary",),
                                             vmem_limit_bytes=VMEM_LIMIT),
        name="ret_prompt",
    )(rq, rk, rv, rg, *tabs)


def _ret_sample_tables():
    lg = _log_decay()
    t = np.arange(N_SAMPLE_TOK) % DEC_SEQ
    seq_id = np.arange(N_SAMPLE_TOK) // DEC_SEQ
    diff = (t[:, None] - t[None, :]).astype(np.float64)
    same_seq = seq_id[:, None] == seq_id[None, :]
    dmat = np.where(same_seq[None] & (diff >= 0)[None],
                    np.exp(np.maximum(diff, 0.0)[None] * lg[:, None, None]), 0.0)
    lane_head = np.arange(RET_QK) // RET_DK
    qd = np.exp((t[:, None] + 1.0) * lg[lane_head][None, :])
    kd = np.exp((DEC_SEQ - 1.0 - t)[:, None] * lg[lane_head][None, :])
    return [jnp.asarray(a, dtype=F32) for a in (dmat, np.concatenate([qd, kd], axis=1))]


def _ret_sample_body(q_ref, k_ref, v_ref, g_ref, st_ref, dmat_ref, qkd_ref, o_ref, sto_ref):
    lg = _log_decay()
    q = q_ref[...]
    k = k_ref[...]
    kb = k.astype(BF16)
    qdec = q * qkd_ref[:, :RET_QK]
    kdec = k * qkd_ref[:, RET_QK:]
    vb = v_ref[...].astype(BF16)
    g = g_ref[...]
    lane = lax.broadcasted_iota(jnp.int32, (N_SAMPLE_TOK, LANES), 1)
    lane_head = lax.broadcasted_iota(jnp.int32, (N_SAMPLE_TOK, RET_QK), 1) >> LOG2_RET_DK
    n_state_rows = DEC_BATCH * RET_DK
    own_seq = ((lax.broadcasted_iota(jnp.int32, (N_SAMPLE_TOK, n_state_rows), 0) >> LOG2_DEC_SEQ)
               == (lax.broadcasted_iota(jnp.int32, (N_SAMPLE_TOK, n_state_rows), 1) >> LOG2_RET_DK))
    for hd in range(RET_HEADS):
        sl = slice(hd * RET_DV, (hd + 1) * RET_DV)
        qm = jnp.where(lane_head == hd, q, 0.0).astype(BF16)
        s = lax.dot_general(qm, kb, _NT, preferred_element_type=F32) * dmat_ref[hd]
        inner = jnp.dot(s.astype(BF16), vb[:, sl], preferred_element_type=F32)

        def expand(z):
            slab = z[:, (hd // 2) * LANES:(hd // 2 + 1) * LANES]
            other = pltpu.roll(slab, RET_DK, 1)
            in_low = lane < RET_DK
            both = jnp.where(in_low, slab, other) if hd % 2 == 0 else jnp.where(in_low, other, slab)
            tiled = jnp.concatenate([both] * (n_state_rows // LANES), axis=1)
            return jnp.where(own_seq, tiled, 0.0).astype(BF16)

        st = st_ref[:, hd].reshape(n_state_rows, RET_DV)
        cross = jnp.dot(expand(qdec), st.astype(BF16), preferred_element_type=F32)
        o_ref[:, sl] = _group_norm_gate(inner + cross, g[:, sl])
        kv = lax.dot_general(expand(kdec), vb[:, sl], _TN, preferred_element_type=F32)
        new = float(np.exp(DEC_SEQ * lg[hd])) * st + kv
        sto_ref[:, hd] = new.reshape(DEC_BATCH, RET_DK, RET_DV)


def _ret_sample(rq, rk, rv, rg, state):
    tabs = _ret_sample_tables()
    full = lambda a: pl.BlockSpec(a.shape, lambda i: (0,) * a.ndim)
    args = (rq, rk, rv, rg, state, *tabs)
    return pl.pallas_call(
        _ret_sample_body,
        grid=(1,),
        in_specs=[full(a) for a in args],
        out_specs=[pl.BlockSpec((N_SAMPLE_TOK, RET_WIDTH), lambda i: (0, 0)),
                   pl.BlockSpec(state.shape, lambda i: (0, 0, 0, 0))],
        out_shape=[jax.ShapeDtypeStruct((N_SAMPLE_TOK, RET_WIDTH), F32),
                   jax.ShapeDtypeStruct(state.shape, F32)],
        compiler_params=pltpu.CompilerParams(dimension_semantics=("arbitrary",),
                                             vmem_limit_bytes=VMEM_LIMIT),
        name="ret_sample",
    )(*args)


def _moba_prompt_body(q_ref, k_ref, v_ref, o_ref):
    hd = pl.program_id(1)
    n_blocks = SEQ // MOBA_BLOCK
    exp2_scale = MOBA_HEAD_DIM ** -0.5 * LOG2_E
    k32 = k_ref[pl.ds(hd, SEQ, stride=MOBA_HEADS), :]
    kb = k32.astype(BF16)
    vt = v_ref[pl.ds(hd, SEQ, stride=MOBA_HEADS), :].T.astype(BF16)
    kmean = jnp.sum(k32.reshape(n_blocks, MOBA_BLOCK, MOBA_HEAD_DIM), axis=1) * (1.0 / MOBA_BLOCK)
    kmb = kmean.astype(BF16)
    key_id = lax.broadcasted_iota(jnp.int32, (MOBA_BLOCK, MOBA_BLOCK), 0)
    qry_id = lax.broadcasted_iota(jnp.int32, (MOBA_BLOCK, MOBA_BLOCK), 1)
    causal = key_id <= qry_id

    blk = lambda n: slice(n * MOBA_BLOCK, (n + 1) * MOBA_BLOCK)

    def score_matmuls(i):
        qi = q_ref[blk(i), :]
        st = [lax.dot_general(kb[blk(n)], qi, _NT, preferred_element_type=F32) for n in range(i + 1)]
        st[i] = jnp.where(causal, st[i], NEG_INF)
        top = [jnp.max(s, axis=0, keepdims=True) for s in st]
        gt = lax.dot_general(kmb, qi, _NT, preferred_element_type=F32) if i > MOBA_TOPK else None
        return st, top, gt

    ahead = score_matmuls(n_blocks - 1)
    for i in reversed(range(n_blocks)):
        st, top, gt = ahead
        if i > 0:
            ahead = score_matmuls(i - 1)
        drop = [0.0] * (i + 1)
        if i > MOBA_TOPK:
            for n in range(i):
                beats = jnp.zeros((1, MOBA_BLOCK), F32)
                for mm in range(i):
                    if mm == n:
                        continue
                    win = (gt[mm:mm + 1] >= gt[n:n + 1]) if mm < n else (gt[mm:mm + 1] > gt[n:n + 1])
                    beats = beats + win.astype(F32)
                drop[n] = jnp.where(beats < MOBA_TOPK, 0.0, NEG_INF)
        m = functools.reduce(jnp.maximum, [top[n] + drop[n] for n in range(i + 1)])
        l = jnp.zeros((1, MOBA_BLOCK), F32)
        acc = jnp.zeros((MOBA_HEAD_DIM, MOBA_BLOCK), F32)
        for n in range(i + 1):
            e = jnp.exp2((st[n] - (m - drop[n])) * exp2_scale)
            l = l + jnp.sum(e, axis=0, keepdims=True)
            acc = acc + jnp.dot(vt[:, blk(n)], e.astype(BF16), preferred_element_type=F32)
        o_ref[blk(i), :] = (acc / l).T.astype(o_ref.dtype)


def _moba_prompt(mq, k2d, v2d):
    kv_spec = pl.BlockSpec((SEQ * MOBA_HEADS, LANES), lambda b, h: (b, 0))
    return pl.pallas_call(
        _moba_prompt_body,
        grid=(BATCH, MOBA_HEADS),
        in_specs=[pl.BlockSpec((SEQ, MOBA_HEAD_DIM), lambda b, h: (b, h)), kv_spec, kv_spec],
        out_specs=pl.BlockSpec((SEQ, MOBA_HEAD_DIM), lambda b, h: (b, h)),
        out_shape=jax.ShapeDtypeStruct((BATCH * SEQ, MOBA_WIDTH), BF16),
        compiler_params=pltpu.CompilerParams(dimension_semantics=("arbitrary", "arbitrary"),
                                             vmem_limit_bytes=VMEM_LIMIT),
        name="moba_prompt",
    )(mq, k2d, v2d)


class _SampleMoba:
    n_chunks = PAGES_PER_SEQ // CHUNK_PAGES
    n_rows = MOBA_HEADS * DEC_SEQ
    n_blocks = PAGES_PER_SEQ * PAGE_SIZE // MOBA_BLOCK
    pages_per_block = MOBA_BLOCK // PAGE_SIZE

    def __init__(self, pt_ref, seq, n_seqs, q_ref, kn_ref, vn_ref, kc_ref, vc_ref, o_ref, ring, sem, s_scr):
        self.pt_ref, self.seq, self.n_seqs = pt_ref, seq, n_seqs
        self.q_ref, self.kn_ref, self.vn_ref = q_ref, kn_ref, vn_ref
        self.kc_ref, self.vc_ref, self.o_ref = kc_ref, vc_ref, o_ref
        self.ring, self.sem, self.s_scr = ring, sem, s_scr

    def _page_copy(self, cache_ref, row0, slot):
        return pltpu.make_async_copy(cache_ref.at[pl.ds(row0, PAGE_ROWS)], self.ring.at[slot],
                                     self.sem.at[slot // CHUNK_PAGES])

    def _start_chunk(self, cache_ref, seq, chunk):
        for r in range(CHUNK_PAGES):
            page = self.pt_ref[seq * PAGES_PER_SEQ + chunk * CHUNK_PAGES + r]
            self._page_copy(cache_ref, pl.multiple_of(page * PAGE_ROWS, PAGE_ROWS),
                            (chunk % 2) * CHUNK_PAGES + r).start()

    def _wait_chunk(self, cache_ref, chunk):
        for r in range(CHUNK_PAGES):
            self._page_copy(cache_ref, 0, (chunk % 2) * CHUNK_PAGES + r).wait()

    def prologue(self):
        @pl.when(self.seq == 0)
        def _():
            self._start_chunk(self.kc_ref, self.seq, 0)
            self._start_chunk(self.kc_ref, self.seq, 1)

        self.q = jnp.concatenate(
            [self.q_ref[:, hd * MOBA_HEAD_DIM:(hd + 1) * MOBA_HEAD_DIM] for hd in range(MOBA_HEADS)], axis=0
        ).astype(BF16)
        row_head = lax.broadcasted_iota(jnp.int32, (self.n_rows, LANES), 0) >> LOG2_DEC_SEQ
        col_head = lax.broadcasted_iota(jnp.int32, (self.n_rows, LANES), 1) & (MOBA_HEADS - 1)
        self.same_head = row_head == col_head
        self.head_bias = jnp.where(self.same_head, 0.0, NEG_INF)
        self.block_sum, self.block_max = [], []

    @staticmethod
    def _slabs(x):
        return [x[:, j * LANES:(j + 1) * LANES] for j in range(PAGE_ROWS // LANES)]

    def wait_k(self, c):
        self._wait_chunk(self.kc_ref, c)

    def k_pages(self, c, first, last):
        assert first % self.pages_per_block == 0 and last % self.pages_per_block == 0
        for r0 in range(first, last, self.pages_per_block):
            tot = jnp.zeros((self.n_rows, LANES), F32)
            top = jnp.full((self.n_rows, LANES), NEG_INF, F32)
            for r in range(r0, r0 + self.pages_per_block):
                page = self.ring[(c % 2) * CHUNK_PAGES + r].astype(BF16)
                s = lax.dot_general(self.q, page, _NT, preferred_element_type=F32)
                self.s_scr[c * CHUNK_PAGES + r] = s
                for slab in self._slabs(s):
                    tot = tot + jnp.where(self.same_head, slab, 0.0)
                    top = jnp.maximum(top, slab + self.head_bias)
            self.block_sum.append(jnp.sum(tot, axis=-1, keepdims=True))
            self.block_max.append(jnp.max(top, axis=-1, keepdims=True))

    def refill_after_k(self, c):
        if c + 2 < self.n_chunks:
            self._start_chunk(self.kc_ref, self.seq, c + 2)
        else:
            self._start_chunk(self.vc_ref, self.seq, c + 2 - self.n_chunks)

    def select(self):
        n_rows, gs = self.n_rows, self.block_sum
        self.exp2_scale = MOBA_HEAD_DIM ** -0.5 * LOG2_E
        lane = lax.broadcasted_iota(jnp.int32, (n_rows, LANES), 1)
        g_all = jnp.full((n_rows, LANES), NEG_INF, F32)
        for n in range(self.n_blocks):
            g_all = jnp.where(lane == n, gs[n], g_all)
        self.keep_bias = []
        for n in range(self.n_blocks):
            wins = (g_all > gs[n]) | ((g_all == gs[n]) & (lane < n))
            beats = jnp.sum(wins.astype(F32), axis=-1, keepdims=True)
            self.keep_bias.append(jnp.where(beats < MOBA_TOPK, 0.0, NEG_INF))
        s_own = lax.dot_general(self.q, self.kn_ref[...].astype(BF16), _NT, preferred_element_type=F32)
        r_id = lax.broadcasted_iota(jnp.int32, (n_rows, n_rows), 0)
        c_id = lax.broadcasted_iota(jnp.int32, (n_rows, n_rows), 1)
        own_ok = ((c_id & (MOBA_HEADS - 1)) == (r_id >> LOG2_DEC_SEQ)) & ((c_id >> LOG2_MOBA_HEADS) <= (r_id & (DEC_SEQ - 1)))
        s_own = jnp.where(own_ok, s_own, NEG_INF)
        m = jnp.max(s_own, axis=-1, keepdims=True)
        for n in range(self.n_blocks):
            m = jnp.maximum(m, self.block_max[n] + self.keep_bias[n])
        self.m = m
        self.lsum = jnp.zeros((n_rows, LANES), F32)
        e_own = jnp.exp2((s_own - m) * self.exp2_scale)
        return (jnp.sum(e_own, axis=-1, keepdims=True),
                jnp.dot(e_own.astype(BF16), self.vn_ref[...].astype(BF16), preferred_element_type=F32))

    def wait_v(self, c):
        self._wait_chunk(self.vc_ref, c)

    def v_pages(self, c, first, last, acc):
        assert first % self.pages_per_block == 0 and last % self.pages_per_block == 0
        for r0 in range(first, last, self.pages_per_block):
            shift = self.head_bias + (self.keep_bias[(c * CHUNK_PAGES + r0) // self.pages_per_block] - self.m)
            for r in range(r0, r0 + self.pages_per_block):
                e = [jnp.exp2((slab + shift) * self.exp2_scale) for slab in self._slabs(self.s_scr[c * CHUNK_PAGES + r])]
                self.lsum = self.lsum + functools.reduce(jnp.add, e)
                page = self.ring[(c % 2) * CHUNK_PAGES + r].astype(BF16)
                acc = acc + jnp.dot(jnp.concatenate(e, axis=1).astype(BF16), page, preferred_element_type=F32)
        return acc

    def refill_after_v(self, c):
        if c + 2 < self.n_chunks:
            self._start_chunk(self.vc_ref, self.seq, c + 2)
        else:
            @pl.when(self.seq + 1 < self.n_seqs)
            def _():
                self._start_chunk(self.kc_ref, self.seq + 1, c + 2 - self.n_chunks)

    def finish(self, l_own, acc):
        out = acc / (l_own + jnp.sum(self.lsum, axis=-1, keepdims=True))
        for hd in range(MOBA_HEADS):
            self.o_ref[:, hd * MOBA_HEAD_DIM:(hd + 1) * MOBA_HEAD_DIM] = out[hd * DEC_SEQ:(hd + 1) * DEC_SEQ]


def _layer_norm(x, g, b):
    mu = jnp.mean(x, axis=-1, keepdims=True)
    d = x - mu
    var = jnp.mean(d * d, axis=-1, keepdims=True)
    return d * lax.rsqrt(var + LN_EPS) * g + b


def _out_ffn_body(ar_ref, am_ref, x_ref, ga_ref, shf_ref, scf_ref, gf_ref, wo_ref, g1_ref, b1_ref,
                  wu_ref, wd_ref, g2_ref, b2_ref, y_ref, x1_scr, h_scr, acc_scr):
    c = pl.program_id(0)

    @pl.when(c == 0)
    def _():
        mixed = (jnp.dot(ar_ref[...].astype(BF16), wo_ref[:RET_WIDTH, :], preferred_element_type=F32)
                 + jnp.dot(am_ref[...].astype(BF16), wo_ref[RET_WIDTH:, :], preferred_element_type=F32))
        x1 = _layer_norm(ALPHA * x_ref[...] + _modulation(ga_ref, True) * mixed, g1_ref[...], b1_ref[...])
        x1_scr[...] = x1
        h_scr[...] = (x1 * (1.0 + _modulation(scf_ref, True)) + _modulation(shf_ref, True)).astype(BF16)
        acc_scr[...] = jnp.zeros_like(acc_scr)

    u = jnp.maximum(jnp.dot(h_scr[...], wu_ref[...], preferred_element_type=F32), 0.0)
    acc_scr[...] += jnp.dot((u * u).astype(BF16), wd_ref[...], preferred_element_type=F32)

    @pl.when(c == pl.num_programs(0) - 1)
    def _():
        y_ref[...] = _layer_norm(ALPHA * x1_scr[...] + _modulation(gf_ref, True) * acc_scr[...],
                                 g2_ref[...], b2_ref[...])


def _out_ffn(a_ret, a_moba, x2d, mod, weights):
    w_o, ln1_g, ln1_b, w_up, w_down, ln2_g, ln2_b = weights
    t = x2d.shape[0]
    whole = lambda a: pl.BlockSpec(a.shape, lambda c: (0,) * a.ndim)
    return pl.pallas_call(
        _out_ffn_body,
        grid=(D_FF // D_MODEL,),
        in_specs=[whole(a_ret), whole(a_moba), whole(x2d)] + [_mod_spec(term) for term in (2, 3, 4, 5)]
        + [whole(w_o), whole(ln1_g), whole(ln1_b),
           pl.BlockSpec((D_MODEL, D_MODEL), lambda c: (0, c)), pl.BlockSpec((D_MODEL, D_MODEL), lambda c: (c, 0)),
           whole(ln2_g), whole(ln2_b)],
        out_specs=pl.BlockSpec((t, D_MODEL), lambda c: (0, 0)),
        out_shape=jax.ShapeDtypeStruct((t, D_MODEL), F32),
        scratch_shapes=[pltpu.VMEM((t, D_MODEL), F32), pltpu.VMEM((t, D_MODEL), BF16), pltpu.VMEM((t, D_MODEL), F32)],
        compiler_params=pltpu.CompilerParams(dimension_semantics=("arbitrary",),
                                             vmem_limit_bytes=VMEM_LIMIT),
        name="out_ffn",
    )(a_ret, a_moba, x2d, *([mod] * 4), *weights)


def _out_ffn_moba_body(pt_ref, ar_ref, am_ref, x_ref, ga_ref, shf_ref, scf_ref, gf_ref, wo_ref, g1_ref, b1_ref,
                       wu_ref, wd_ref, g2_ref, b2_ref, q_ref, kn_ref, vn_ref, kc_ref, vc_ref,
                       y_ref, o_ref, ring, sem, s_scr):
    n_seqs = pl.num_programs(0) * pl.num_programs(1)
    seq = pl.program_id(0) * pl.num_programs(1) + pl.program_id(1)
    sm = _SampleMoba(pt_ref, seq, n_seqs, q_ref, kn_ref, vn_ref, kc_ref, vc_ref, o_ref, ring, sem, s_scr)

    def up(c, h):
        u = jnp.maximum(jnp.dot(h, wu_ref[:, c * D_MODEL:(c + 1) * D_MODEL], preferred_element_type=F32), 0.0)
        return (u * u).astype(BF16)

    def down(c, u):
        return jnp.dot(u, wd_ref[c * D_MODEL:(c + 1) * D_MODEL, :], preferred_element_type=F32)

    half = CHUNK_PAGES // 2
    sm.prologue()

    sm.wait_k(0)
    sm.k_pages(0, 0, half)
    mixed = jnp.dot(ar_ref[...].astype(BF16), wo_ref[:RET_WIDTH, :], preferred_element_type=F32)
    sm.k_pages(0, half, CHUNK_PAGES)
    mixed = mixed + jnp.dot(am_ref[...].astype(BF16), wo_ref[RET_WIDTH:, :], preferred_element_type=F32)
    x1 = _layer_norm(ALPHA * x_ref[...] + _modulation(ga_ref, False) * mixed, g1_ref[...], b1_ref[...])
    h = (x1 * (1.0 + _modulation(scf_ref, False)) + _modulation(shf_ref, False)).astype(BF16)
    sm.refill_after_k(0)

    sm.wait_k(1)
    sm.k_pages(1, 0, half)
    u = up(0, h)
    sm.k_pages(1, half, CHUNK_PAGES)
    sm.refill_after_k(1)

    sm.wait_k(2)
    sm.k_pages(2, 0, half)
    acc = down(0, u)
    sm.k_pages(2, half, CHUNK_PAGES)
    sm.refill_after_k(2)

    sm.wait_k(3)
    sm.k_pages(3, 0, half)
    u = up(1, h)
    sm.k_pages(3, half, CHUNK_PAGES)
    sm.refill_after_k(3)

    l, acc_s = sm.select()
    acc = acc + down(1, u)

    sm.wait_v(0)
    acc_s = sm.v_pages(0, 0, half, acc_s)
    u = up(2, h)
    acc_s = sm.v_pages(0, half, CHUNK_PAGES, acc_s)
    sm.refill_after_v(0)

    sm.wait_v(1)
    acc_s = sm.v_pages(1, 0, half, acc_s)
    acc = acc + down(2, u)
    acc_s = sm.v_pages(1, half, CHUNK_PAGES, acc_s)
    sm.refill_after_v(1)

    sm.wait_v(2)
    acc_s = sm.v_pages(2, 0, half, acc_s)
    u = up(3, h)
    acc_s = sm.v_pages(2, half, CHUNK_PAGES, acc_s)
    sm.refill_after_v(2)

    sm.wait_v(3)
    acc_s = sm.v_pages(3, 0, half, acc_s)
    acc = acc + down(3, u)
    acc_s = sm.v_pages(3, half, CHUNK_PAGES, acc_s)
    sm.refill_after_v(3)
    y_ref[...] = _layer_norm(ALPHA * x1 + _modulation(gf_ref, False) * acc, g2_ref[...], b2_ref[...])
    sm.finish(l, acc_s)


def _out_ffn_moba(a_ret, a_moba, x2d, mod, weights, page_table, mq_s, kn2d, vn2d, cache_k2d, cache_v2d, tm):
    t = x2d.shape[0]
    nt = SEQ // tm
    assert BATCH * nt == DEC_BATCH
    n_rows = MOBA_HEADS * DEC_SEQ
    tile = lambda b, i, pt: (b * nt + i, 0)
    wide = lambda w: pl.BlockSpec((tm, w), tile)
    const = lambda a: pl.BlockSpec(a.shape, lambda b, i, pt: (0,) * a.ndim, pipeline_mode=pl.Buffered(1))
    seq_rows = lambda w, n: pl.BlockSpec((n, w), tile)
    hbm = pl.BlockSpec(memory_space=pl.ANY)
    grid_spec = pltpu.PrefetchScalarGridSpec(
        num_scalar_prefetch=1,
        grid=(BATCH, nt),
        in_specs=[wide(RET_WIDTH), wide(MOBA_WIDTH), wide(D_MODEL)] + [_mod_spec(term) for term in (2, 3, 4, 5)]
        + [const(a) for a in weights]
        + [seq_rows(MOBA_WIDTH, DEC_SEQ), seq_rows(LANES, n_rows), seq_rows(LANES, n_rows), hbm, hbm],
        out_specs=[wide(D_MODEL), seq_rows(MOBA_WIDTH, DEC_SEQ)],
        scratch_shapes=[pltpu.VMEM((RING_PAGES, PAGE_ROWS, LANES), F32),
                        pltpu.SemaphoreType.DMA((RING_PAGES // CHUNK_PAGES,)),
                        pltpu.VMEM((PAGES_PER_SEQ, n_rows, PAGE_ROWS), F32)],
    )
    return pl.pallas_call(
        _out_ffn_moba_body,
        grid_spec=grid_spec,
        out_shape=[jax.ShapeDtypeStruct((t, D_MODEL), F32),
                   jax.ShapeDtypeStruct((N_SAMPLE_TOK, MOBA_WIDTH), F32)],
        compiler_params=pltpu.CompilerParams(dimension_semantics=("arbitrary", "arbitrary"),
                                             vmem_limit_bytes=VMEM_LIMIT_FUSED),
        name="out_ffn_moba",
    )(page_table.reshape(-1), a_ret, a_moba, x2d, *([mod] * 4), *weights,
      mq_s, kn2d, vn2d, cache_k2d, cache_v2d)


def kernel(x_prompt, x_sample, cache_k, cache_v, state_ret, page_table, c_prompt, c_sample,
           w_ada, b_ada, w_in, w_o, ln1_g, ln1_b, w_up, w_down, ln2_g, ln2_b):
    n_prompt_tok = BATCH * SEQ
    past_len = page_table.shape[1] * PAGE_SIZE

    mod = _adaln(c_prompt, c_sample, w_ada[0], b_ada)

    tm = 512
    nt = SEQ // tm
    p_row = lambda b, i: (b * nt + i, 0)
    p_tab = pl.BlockSpec((tm, LANES), lambda b, i: (i, 0))
    s_row = lambda i: (0, 0)
    s_tab = pl.BlockSpec((N_SAMPLE_TOK, LANES), s_row)

    pos_p = np.arange(SEQ, dtype=np.int32)
    pos_s = np.tile(past_len + np.arange(DEC_SEQ, dtype=np.int32), DEC_BATCH)
    tabs_p = _rope_tables(pos_p, MOBA_HEAD_DIM) + _rope_tables(pos_p, RET_DK)
    tabs_s = _rope_tables(pos_s, MOBA_HEAD_DIM) + _rope_tables(pos_s, RET_DK)

    xp = x_prompt.reshape(n_prompt_tok, D_MODEL)
    rq, rk, rv, rg, mq, k_p, v_p, w_o_b, w_up_b, w_down_b = _inproj(
        xp, mod, False, [p_tab] * 4, tabs_p, w_in[0],
        (BATCH, nt), p_row, tm, BF16, casts=(w_o[0], w_up[0], w_down[0]))
    weights = (w_o_b, ln1_g, ln1_b, w_up_b, w_down_b, ln2_g, ln2_b)
    a_ret, state_p = _ret_prompt(rq, rk, rv, rg)
    a_moba = _moba_prompt(mq, k_p, v_p)

    xs = x_sample.reshape(N_SAMPLE_TOK, D_MODEL)
    rq_s, rk_s, rv_s, rg_s, mq_s, k_s, v_s = _inproj(
        xs, mod, True, [s_tab] * 4, tabs_s, w_in[0],
        (1,), s_row, N_SAMPLE_TOK, F32)
    a_ret_s, state_s = _ret_sample(rq_s, rk_s, rv_s, rg_s, state_ret[0])

    cache_rows = cache_k.shape[1] * PAGE_ROWS
    y_p, a_moba_s = _out_ffn_moba(a_ret, a_moba, xp, mod, weights, page_table, mq_s, k_s, v_s,
                                  cache_k.reshape(cache_rows, LANES), cache_v.reshape(cache_rows, LANES), tm)
    y_s = _out_ffn(a_ret_s, a_moba_s, xs, mod, weights)

    kv_p_shape = (DEPTH, BATCH, SEQ, MOBA_HEADS, MOBA_HEAD_DIM)
    kv_s_shape = (DEPTH, DEC_BATCH, DEC_SEQ, MOBA_HEADS, MOBA_HEAD_DIM)
    return (y_p.reshape(BATCH, SEQ, D_MODEL),
            y_s.reshape(DEC_BATCH, DEC_SEQ, D_MODEL),
            k_p.reshape(kv_p_shape), v_p.reshape(kv_p_shape), state_p[None],
            k_s.reshape(kv_s_shape), v_s.reshape(kv_s_shape), state_s[None])
```

```python
import functools

import numpy as np
import jax
import jax.numpy as jnp
from jax import lax
from jax.experimental import pallas as pl
from jax.experimental.pallas import tpu as pltpu

F32 = jnp.float32
BF16 = jnp.bfloat16

D_MODEL = 1024
BATCH = 8
SEQ = 2048
DEC_BATCH = 32
DEC_SEQ = 8
PAGE_SIZE = 128
RET_HEADS = 4
RET_DK = 64
RET_DV = 128
RET_CHUNK = 128
MOBA_HEADS = 4
MOBA_HEAD_DIM = 128
MOBA_BLOCK = 256
MOBA_TOPK = 3
D_FF = 4 * D_MODEL
ROPE_THETA = 10000.0
LN_EPS = 1e-5
GN_EPS = 1e-6
DEPTH = 1
ALPHA = (2 * DEPTH) ** 0.25
RET_QK = RET_HEADS * RET_DK
RET_WIDTH = RET_HEADS * RET_DV
MOBA_WIDTH = MOBA_HEADS * MOBA_HEAD_DIM
IN_WIDTH = 2 * RET_QK + 2 * RET_WIDTH + 3 * MOBA_WIDTH
OFF_RQ, OFF_RK, OFF_RV, OFF_RG = 0, 256, 512, 1024
OFF_MQ, OFF_MK, OFF_MV = 1536, 2048, 2560
LANES = 128
VMEM_BYTES_V7X = 64 * 1024 * 1024
VMEM_LIMIT = VMEM_BYTES_V7X * 3 // 4
VMEM_LIMIT_FUSED = VMEM_BYTES_V7X * 29 // 32
N_SAMPLE_TOK = DEC_BATCH * DEC_SEQ
PAGES_PER_SEQ = 64
CHUNK_PAGES = 16
RING_PAGES = 2 * CHUNK_PAGES
PAGE_ROWS = PAGE_SIZE * MOBA_HEADS
LOG2_RET_DK = RET_DK.bit_length() - 1
LOG2_DEC_SEQ = DEC_SEQ.bit_length() - 1
LOG2_MOBA_HEADS = MOBA_HEADS.bit_length() - 1
NEG_INF = float("-inf")
LOG2_E = 1.4426950408889634

_NT = (((1,), (1,)), ((), ()))
_TN = (((0,), (0,)), ((), ()))


def _log_decay():
    return np.log1p(-np.exp2(-5.0 - np.arange(RET_HEADS, dtype=np.float64)))


N_MOD = 6
MOD_ROWS = BATCH + DEC_BATCH


def _adaln_body(cp_ref, cs_ref, w_ref, b_ref, o_ref):
    w = w_ref[...].astype(BF16)
    for c_ref, lo in ((cp_ref, 0), (cs_ref, BATCH)):
        c = c_ref[...]
        a = (c * jax.nn.sigmoid(c)).astype(BF16)
        o_ref[lo:lo + c.shape[0], :] = jnp.dot(a, w, preferred_element_type=F32) + b_ref[...]


def _adaln(c_prompt, c_sample, w_ada, b_ada):
    return pl.pallas_call(
        _adaln_body,
        grid=(N_MOD,),
        in_specs=[pl.BlockSpec(c_prompt.shape, lambda j: (0, 0)),
                  pl.BlockSpec(c_sample.shape, lambda j: (0, 0)),
                  pl.BlockSpec((D_MODEL, D_MODEL), lambda j: (0, j)),
                  pl.BlockSpec((1, D_MODEL), lambda j: (0, j))],
        out_specs=pl.BlockSpec((None, MOD_ROWS, D_MODEL), lambda j: (j, 0, 0)),
        out_shape=jax.ShapeDtypeStruct((N_MOD, MOD_ROWS, D_MODEL), F32),
        name="adaln",
    )(c_prompt, c_sample, w_ada, b_ada)


def _modulation(ref, sample_group):
    if sample_group:
        rows = ref[BATCH:BATCH + DEC_BATCH, :]
        return jnp.broadcast_to(rows[:, None, :], (DEC_BATCH, DEC_SEQ, D_MODEL)).reshape(N_SAMPLE_TOK, D_MODEL)
    return ref[pl.ds(pl.program_id(0), 1), :]


def _mod_spec(term):
    return pl.BlockSpec((None, MOD_ROWS, D_MODEL), lambda *_: (term, 0, 0))


def _rope_tables(pos, head_dim):
    half = head_dim // 2
    inv_freq = np.power(ROPE_THETA, -np.arange(half, dtype=np.float64) / half)
    ang = pos.astype(np.float64)[:, None] * inv_freq[None, :]
    cos, sin = np.cos(ang), np.sin(ang)
    reps = LANES // head_dim
    cos_t = np.tile(np.concatenate([cos, cos], axis=-1), (1, reps))
    sin_t = np.tile(np.concatenate([-sin, sin], axis=-1), (1, reps))
    return jnp.asarray(cos_t, dtype=F32), jnp.asarray(sin_t, dtype=F32)


def _inproj_body(sample_group, n_casts, x_ref, sc_ref, sh_ref, w_ref, cm_ref, sm_ref, cr_ref, sr_ref, *refs):
    cast_in, refs = refs[:n_casts], refs[n_casts:]
    rq_ref, rk_ref, rv_ref, rg_ref, mq_ref, ko_ref, vo_ref = refs[:7]
    for src, dst in zip(cast_in, refs[7:]):
        dst[...] = src[...].astype(dst.dtype)
    tm = x_ref.shape[0]
    h = (x_ref[...] * (1.0 + _modulation(sc_ref, sample_group)) + _modulation(sh_ref, sample_group)).astype(BF16)

    def proj(lo, width):
        return jnp.dot(h, w_ref[:, lo:lo + width].astype(BF16), preferred_element_type=F32)

    lane = lax.broadcasted_iota(jnp.int32, (tm, LANES), 1)
    low_half = (lane & (RET_DK - 1)) < (RET_DK // 2)
    cr, sr = cr_ref[...], sr_ref[...]
    cm, sm = cm_ref[...], sm_ref[...]

    def rope_ret(z):
        rot = jnp.where(low_half, pltpu.roll(z, LANES - RET_DK // 2, 1), pltpu.roll(z, RET_DK // 2, 1))
        return z * cr + rot * sr

    def rope_moba(z):
        return z * cm + pltpu.roll(z, MOBA_HEAD_DIM // 2, 1) * sm

    zq = proj(OFF_RQ, RET_QK)
    zk = proj(OFF_RK, RET_QK)
    for s in range(RET_QK // LANES):
        sl = slice(s * LANES, (s + 1) * LANES)
        rq_ref[:, sl] = rope_ret(zq[:, sl])
        rk_ref[:, sl] = rope_ret(zk[:, sl]) * (RET_DK ** -0.5)
    rv_ref[...] = proj(OFF_RV, RET_WIDTH).astype(rv_ref.dtype)
    rg_ref[...] = proj(OFF_RG, RET_WIDTH)
    zq = proj(OFF_MQ, MOBA_WIDTH)
    zk = proj(OFF_MK, MOBA_WIDTH)
    zv = proj(OFF_MV, MOBA_WIDTH)
    for hd in range(MOBA_HEADS):
        sl = slice(hd * LANES, (hd + 1) * LANES)
        mq_ref[:, sl] = rope_moba(zq[:, sl]).astype(mq_ref.dtype)
        ko_ref[pl.ds(hd, tm, stride=MOBA_HEADS), :] = rope_moba(zk[:, sl])
        vo_ref[pl.ds(hd, tm, stride=MOBA_HEADS), :] = zv[:, sl]


def _inproj(x2d, mod, sample_group, tab_specs, tabs, w_in, grid, row_map, tm, act_dtype, casts=()):
    t = x2d.shape[0]
    n_steps = int(np.prod(grid))
    wide = lambda w: pl.BlockSpec((tm, w), row_map)
    cast_specs = [pl.BlockSpec((a.shape[0] // n_steps, a.shape[1]), row_map) for a in casts]
    w_spec = pl.BlockSpec((D_MODEL, IN_WIDTH), lambda *_: (0, 0), pipeline_mode=pl.Buffered(1))
    return pl.pallas_call(
        functools.partial(_inproj_body, sample_group, len(casts)),
        grid=grid,
        in_specs=[wide(D_MODEL), _mod_spec(1), _mod_spec(0), w_spec] + tab_specs + cast_specs,
        out_specs=[wide(RET_QK), wide(RET_QK), wide(RET_WIDTH), wide(RET_WIDTH), wide(MOBA_WIDTH),
                   pl.BlockSpec((tm * MOBA_HEADS, LANES), row_map),
                   pl.BlockSpec((tm * MOBA_HEADS, LANES), row_map)] + cast_specs,
        out_shape=[jax.ShapeDtypeStruct((t, RET_QK), F32), jax.ShapeDtypeStruct((t, RET_QK), F32),
                   jax.ShapeDtypeStruct((t, RET_WIDTH), act_dtype), jax.ShapeDtypeStruct((t, RET_WIDTH), F32),
                   jax.ShapeDtypeStruct((t, MOBA_WIDTH), act_dtype),
                   jax.ShapeDtypeStruct((t * MOBA_HEADS, LANES), F32),
                   jax.ShapeDtypeStruct((t * MOBA_HEADS, LANES), F32)]
        + [jax.ShapeDtypeStruct(a.shape, BF16) for a in casts],
        compiler_params=pltpu.CompilerParams(dimension_semantics=("arbitrary",) * len(grid),
                                             vmem_limit_bytes=VMEM_LIMIT),
        name="inproj",
    )(x2d, mod, mod, w_in, *tabs, *casts)


def _group_norm_gate(o, g):
    mu = jnp.mean(o, axis=-1, keepdims=True)
    d = o - mu
    var = jnp.mean(d * d, axis=-1, keepdims=True)
    return d * lax.rsqrt(var + GN_EPS) * (g * jax.nn.sigmoid(g))


def _ret_prompt_tables():
    lg = _log_decay()
    i = np.arange(RET_CHUNK, dtype=np.float64)
    diff = i[:, None] - i[None, :]
    dmat = np.where(diff >= 0, np.exp(np.maximum(diff, 0.0)[None] * lg[:, None, None]), 0.0)
    lane_head = np.arange(RET_QK) // RET_DK
    qd = np.exp((i[:, None] + 1.0) * lg[lane_head][None, :])
    kd = np.exp((RET_CHUNK - 1.0 - i)[:, None] * lg[lane_head][None, :])
    return [jnp.asarray(a, dtype=F32) for a in (dmat, np.concatenate([qd, kd], axis=1))]


def _ret_prompt_body(q_ref, k_ref, v_ref, g_ref, dmat_ref, qkd_ref, o_ref, st_ref):
    lane_head = lax.broadcasted_iota(jnp.int32, (RET_CHUNK, RET_QK), 1) >> LOG2_RET_DK
    chunk_rows = lambda c: slice(c * RET_CHUNK, (c + 1) * RET_CHUNK)
    chunk_decay = [float(np.exp(RET_CHUNK * lg)) for lg in _log_decay()]
    head_block = lambda h: (slice(h * RET_DK, (h + 1) * RET_DK), slice(h * RET_DV, (h + 1) * RET_DV))

    def block_diagonal(state):
        zero = jnp.zeros((RET_DK, RET_DV), BF16)
        return jnp.concatenate(
            [jnp.concatenate([state[h].astype(BF16) if c == h else zero for c in range(RET_HEADS)], axis=1)
             for h in range(RET_HEADS)], axis=0)

    def first_matmuls(c, state):
        rows = chunk_rows(c)
        q = q_ref[rows, :]
        k = k_ref[rows, :]
        v = v_ref[rows, :]
        kb = k.astype(BF16)
        scores = [lax.dot_general(jnp.where(lane_head == hd, q, 0.0).astype(BF16), kb, _NT,
                                  preferred_element_type=F32) for hd in range(RET_HEADS)]
        cross = jnp.dot((q * qkd_ref[:, :RET_QK]).astype(BF16), block_diagonal(state),
                        preferred_element_type=F32)
        kv = lax.dot_general((k * qkd_ref[:, RET_QK:]).astype(BF16), v, _TN, preferred_element_type=F32)
        new_state = [chunk_decay[h] * state[h] + kv[head_block(h)] for h in range(RET_HEADS)]
        return (scores, cross, v), new_state

    def second_matmuls(c, scores, cross, v):
        rows = chunk_rows(c)
        g = g_ref[rows, :]
        decayed = [(scores[hd] * dmat_ref[hd]).astype(BF16) for hd in range(RET_HEADS)]
        for hd in range(RET_HEADS):
            sl = slice(hd * RET_DV, (hd + 1) * RET_DV)
            inner = jnp.dot(decayed[hd], v[:, sl], preferred_element_type=F32)
            o_ref[rows, sl] = _group_norm_gate(inner + cross[:, sl], g[:, sl]).astype(o_ref.dtype)

    n_chunks = SEQ // RET_CHUNK
    ahead, state = first_matmuls(0, [jnp.zeros((RET_DK, RET_DV), F32)] * RET_HEADS)
    for c in range(n_chunks):
        current = ahead
        if c + 1 < n_chunks:
            ahead, state = first_matmuls(c + 1, state)
        second_matmuls(c, *current)
    for hd in range(RET_HEADS):
        st_ref[hd] = state[hd]


def _ret_prompt(rq, rk, rv, rg):
    tabs = _ret_prompt_tables()
    seq = lambda w: pl.BlockSpec((SEQ, w), lambda b: (b, 0))
    const = lambda a: pl.BlockSpec(a.shape, lambda b: (0,) * a.ndim)
    return pl.pallas_call(
        _ret_prompt_body,
        grid=(BATCH,),
        in_specs=[seq(RET_QK), seq(RET_QK), seq(RET_WIDTH), seq(RET_WIDTH)] + [const(a) for a in tabs],
        out_specs=[seq(RET_WIDTH), pl.BlockSpec((None, RET_HEADS, RET_DK, RET_DV), lambda b: (b, 0, 0, 0))],
        out_shape=[jax.ShapeDtypeStruct((BATCH * SEQ, RET_WIDTH), BF16),
                   jax.ShapeDtypeStruct((BATCH, RET_HEADS, RET_DK, RET_DV), F32)],
        compiler_params=pltpu.CompilerParams(dimension_semantics=("arbitrary",),
                                             vmem_limit_bytes=VMEM_LIMIT),
        name="ret_prompt",
    )(rq, rk, rv, rg, *tabs)


def _ret_sample_tables():
    lg = _log_decay()
    t = np.arange(N_SAMPLE_TOK) % DEC_SEQ
    seq_id = np.arange(N_SAMPLE_TOK) // DEC_SEQ
    diff = (t[:, None] - t[None, :]).astype(np.float64)
    same_seq = seq_id[:, None] == seq_id[None, :]
    dmat = np.where(same_seq[None] & (diff >= 0)[None],
                    np.exp(np.maximum(diff, 0.0)[None] * lg[:, None, None]), 0.0)
    lane_head = np.arange(RET_QK) // RET_DK
    qd = np.exp((t[:, None] + 1.0) * lg[lane_head][None, :])
    kd = np.exp((DEC_SEQ - 1.0 - t)[:, None] * lg[lane_head][None, :])
    return [jnp.asarray(a, dtype=F32) for a in (dmat, np.concatenate([qd, kd], axis=1))]


def _ret_sample_body(q_ref, k_ref, v_ref, g_ref, st_ref, dmat_ref, qkd_ref, o_ref, sto_ref):
    lg = _log_decay()
    q = q_ref[...]
    k = k_ref[...]
    kb = k.astype(BF16)
    qdec = q * qkd_ref[:, :RET_QK]
    kdec = k * qkd_ref[:, RET_QK:]
    vb = v_ref[...].astype(BF16)
    g = g_ref[...]
    lane = lax.broadcasted_iota(jnp.int32, (N_SAMPLE_TOK, LANES), 1)
    lane_head = lax.broadcasted_iota(jnp.int32, (N_SAMPLE_TOK, RET_QK), 1) >> LOG2_RET_DK
    n_state_rows = DEC_BATCH * RET_DK
    own_seq = ((lax.broadcasted_iota(jnp.int32, (N_SAMPLE_TOK, n_state_rows), 0) >> LOG2_DEC_SEQ)
               == (lax.broadcasted_iota(jnp.int32, (N_SAMPLE_TOK, n_state_rows), 1) >> LOG2_RET_DK))
    for hd in range(RET_HEADS):
        sl = slice(hd * RET_DV, (hd + 1) * RET_DV)
        qm = jnp.where(lane_head == hd, q, 0.0).astype(BF16)
        s = lax.dot_general(qm, kb, _NT, preferred_element_type=F32) * dmat_ref[hd]
        inner = jnp.dot(s.astype(BF16), vb[:, sl], preferred_element_type=F32)

        def expand(z):
            slab = z[:, (hd // 2) * LANES:(hd // 2 + 1) * LANES]
            other = pltpu.roll(slab, RET_DK, 1)
            in_low = lane < RET_DK
            both = jnp.where(in_low, slab, other) if hd % 2 == 0 else jnp.where(in_low, other, slab)
            tiled = jnp.concatenate([both] * (n_state_rows // LANES), axis=1)
            return jnp.where(own_seq, tiled, 0.0).astype(BF16)

        st = st_ref[:, hd].reshape(n_state_rows, RET_DV)
        cross = jnp.dot(expand(qdec), st.astype(BF16), preferred_element_type=F32)
        o_ref[:, sl] = _group_norm_gate(inner + cross, g[:, sl])
        kv = lax.dot_general(expand(kdec), vb[:, sl], _TN, preferred_element_type=F32)
        new = float(np.exp(DEC_SEQ * lg[hd])) * st + kv
        sto_ref[:, hd] = new.reshape(DEC_BATCH, RET_DK, RET_DV)


def _ret_sample(rq, rk, rv, rg, state):
    tabs = _ret_sample_tables()
    full = lambda a: pl.BlockSpec(a.shape, lambda i: (0,) * a.ndim)
    args = (rq, rk, rv, rg, state, *tabs)
    return pl.pallas_call(
        _ret_sample_body,
        grid=(1,),
        in_specs=[full(a) for a in args],
        out_specs=[pl.BlockSpec((N_SAMPLE_TOK, RET_WIDTH), lambda i: (0, 0)),
                   pl.BlockSpec(state.shape, lambda i: (0, 0, 0, 0))],
        out_shape=[jax.ShapeDtypeStruct((N_SAMPLE_TOK, RET_WIDTH), F32),
                   jax.ShapeDtypeStruct(state.shape, F32)],
        compiler_params=pltpu.CompilerParams(dimension_semantics=("arbitrary",),
                                             vmem_limit_bytes=VMEM_LIMIT),
        name="ret_sample",
    )(*args)


def _moba_prompt_body(q_ref, k_ref, v_ref, o_ref):
    hd = pl.program_id(1)
    n_blocks = SEQ // MOBA_BLOCK
    exp2_scale = MOBA_HEAD_DIM ** -0.5 * LOG2_E
    k32 = k_ref[pl.ds(hd, SEQ, stride=MOBA_HEADS), :]
    kb = (k32 * exp2_scale).astype(BF16)
    vt = v_ref[pl.ds(hd, SEQ, stride=MOBA_HEADS), :].T.astype(BF16)
    kmean = jnp.sum(k32.reshape(n_blocks, MOBA_BLOCK, MOBA_HEAD_DIM), axis=1) * (1.0 / MOBA_BLOCK)
    kmb = kmean.astype(BF16)
    key_id = lax.broadcasted_iota(jnp.int32, (MOBA_BLOCK, MOBA_BLOCK), 0)
    qry_id = lax.broadcasted_iota(jnp.int32, (MOBA_BLOCK, MOBA_BLOCK), 1)
    causal = key_id <= qry_id

    blk = lambda n: slice(n * MOBA_BLOCK, (n + 1) * MOBA_BLOCK)

    def score_matmuls(i):
        qi = q_ref[blk(i), :]
        st = [lax.dot_general(kb[blk(n)], qi, _NT, preferred_element_type=F32) for n in range(i + 1)]
        st[i] = jnp.where(causal, st[i], NEG_INF)
        top = [jnp.max(s, axis=0, keepdims=True) for s in st]
        gt = lax.dot_general(kmb, qi, _NT, preferred_element_type=F32) if i > MOBA_TOPK else None
        return st, top, gt

    ahead = score_matmuls(n_blocks - 1)
    for i in reversed(range(n_blocks)):
        st, top, gt = ahead
        if i > 0:
            ahead = score_matmuls(i - 1)
        drop = [0.0] * (i + 1)
        if i > MOBA_TOPK:
            for n in range(i):
                beats = jnp.zeros((1, MOBA_BLOCK), F32)
                for mm in range(i):
                    if mm == n:
                        continue
                    win = (gt[mm:mm + 1] >= gt[n:n + 1]) if mm < n else (gt[mm:mm + 1] > gt[n:n + 1])
                    beats = beats + win.astype(F32)
                drop[n] = jnp.where(beats < MOBA_TOPK, 0.0, NEG_INF)
        m = functools.reduce(jnp.maximum, [top[n] + drop[n] for n in range(i + 1)])
        l = jnp.zeros((1, MOBA_BLOCK), F32)
        acc = jnp.zeros((MOBA_HEAD_DIM, MOBA_BLOCK), F32)
        for n in range(i + 1):
            e = jnp.exp2(st[n] - (m - drop[n]))
            l = l + jnp.sum(e, axis=0, keepdims=True)
            acc = acc + jnp.dot(vt[:, blk(n)], e.astype(BF16), preferred_element_type=F32)
        o_ref[blk(i), :] = (acc / l).T.astype(o_ref.dtype)


def _moba_prompt(mq, k2d, v2d):
    kv_spec = pl.BlockSpec((SEQ * MOBA_HEADS, LANES), lambda b, h: (b, 0))
    return pl.pallas_call(
        _moba_prompt_body,
        grid=(BATCH, MOBA_HEADS),
        in_specs=[pl.BlockSpec((SEQ, MOBA_HEAD_DIM), lambda b, h: (b, h)), kv_spec, kv_spec],
        out_specs=pl.BlockSpec((SEQ, MOBA_HEAD_DIM), lambda b, h: (b, h)),
        out_shape=jax.ShapeDtypeStruct((BATCH * SEQ, MOBA_WIDTH), BF16),
        compiler_params=pltpu.CompilerParams(dimension_semantics=("arbitrary", "arbitrary"),
                                             vmem_limit_bytes=VMEM_LIMIT),
        name="moba_prompt",
    )(mq, k2d, v2d)


class _SampleMoba:
    n_chunks = PAGES_PER_SEQ // CHUNK_PAGES
    n_rows = MOBA_HEADS * DEC_SEQ
    n_blocks = PAGES_PER_SEQ * PAGE_SIZE // MOBA_BLOCK
    pages_per_block = MOBA_BLOCK // PAGE_SIZE

    def __init__(self, pt_ref, seq, n_seqs, q_ref, kn_ref, vn_ref, kc_ref, vc_ref, o_ref, ring, sem, s_scr):
        self.pt_ref, self.seq, self.n_seqs = pt_ref, seq, n_seqs
        self.q_ref, self.kn_ref, self.vn_ref = q_ref, kn_ref, vn_ref
        self.kc_ref, self.vc_ref, self.o_ref = kc_ref, vc_ref, o_ref
        self.ring, self.sem, self.s_scr = ring, sem, s_scr

    def _page_copy(self, cache_ref, row0, slot):
        return pltpu.make_async_copy(cache_ref.at[pl.ds(row0, PAGE_ROWS)], self.ring.at[slot],
                                     self.sem.at[slot // CHUNK_PAGES])

    def _start_chunk(self, cache_ref, seq, chunk):
        for r in range(CHUNK_PAGES):
            page = self.pt_ref[seq * PAGES_PER_SEQ + chunk * CHUNK_PAGES + r]
            self._page_copy(cache_ref, pl.multiple_of(page * PAGE_ROWS, PAGE_ROWS),
                            (chunk % 2) * CHUNK_PAGES + r).start()

    def _wait_chunk(self, cache_ref, chunk):
        for r in range(CHUNK_PAGES):
            self._page_copy(cache_ref, 0, (chunk % 2) * CHUNK_PAGES + r).wait()

    def prologue(self):
        @pl.when(self.seq == 0)
        def _():
            self._start_chunk(self.kc_ref, self.seq, 0)
            self._start_chunk(self.kc_ref, self.seq, 1)

        self.q = jnp.concatenate(
            [self.q_ref[:, hd * MOBA_HEAD_DIM:(hd + 1) * MOBA_HEAD_DIM] for hd in range(MOBA_HEADS)], axis=0
        ).astype(BF16)
        row_head = lax.broadcasted_iota(jnp.int32, (self.n_rows, LANES), 0) >> LOG2_DEC_SEQ
        col_head = lax.broadcasted_iota(jnp.int32, (self.n_rows, LANES), 1) & (MOBA_HEADS - 1)
        self.same_head = row_head == col_head
        self.head_bias = jnp.where(self.same_head, 0.0, NEG_INF)
        self.block_sum, self.block_max = [], []

    @staticmethod
    def _slabs(x):
        return [x[:, j * LANES:(j + 1) * LANES] for j in range(PAGE_ROWS // LANES)]

    def wait_k(self, c):
        self._wait_chunk(self.kc_ref, c)

    def k_pages(self, c, first, last):
        assert first % self.pages_per_block == 0 and last % self.pages_per_block == 0
        for r0 in range(first, last, self.pages_per_block):
            tot = jnp.zeros((self.n_rows, LANES), F32)
            top = jnp.full((self.n_rows, LANES), NEG_INF, F32)
            for r in range(r0, r0 + self.pages_per_block):
                page = self.ring[(c % 2) * CHUNK_PAGES + r].astype(BF16)
                s = lax.dot_general(self.q, page, _NT, preferred_element_type=F32)
                self.s_scr[c * CHUNK_PAGES + r] = s
                for slab in self._slabs(s):
                    tot = tot + jnp.where(self.same_head, slab, 0.0)
                    top = jnp.maximum(top, slab + self.head_bias)
            self.block_sum.append(jnp.sum(tot, axis=-1, keepdims=True))
            self.block_max.append(jnp.max(top, axis=-1, keepdims=True))

    def refill_after_k(self, c):
        if c + 2 < self.n_chunks:
            self._start_chunk(self.kc_ref, self.seq, c + 2)
        else:
            self._start_chunk(self.vc_ref, self.seq, c + 2 - self.n_chunks)

    def select(self):
        n_rows, gs = self.n_rows, self.block_sum
        self.exp2_scale = MOBA_HEAD_DIM ** -0.5 * LOG2_E
        lane = lax.broadcasted_iota(jnp.int32, (n_rows, LANES), 1)
        g_all = jnp.full((n_rows, LANES), NEG_INF, F32)
        for n in range(self.n_blocks):
            g_all = jnp.where(lane == n, gs[n], g_all)
        self.keep_bias = []
        for n in range(self.n_blocks):
            wins = (g_all > gs[n]) | ((g_all == gs[n]) & (lane < n))
            beats = jnp.sum(wins.astype(F32), axis=-1, keepdims=True)
            self.keep_bias.append(jnp.where(beats < MOBA_TOPK, 0.0, NEG_INF))
        s_own = lax.dot_general(self.q, self.kn_ref[...].astype(BF16), _NT, preferred_element_type=F32)
        r_id = lax.broadcasted_iota(jnp.int32, (n_rows, n_rows), 0)
        c_id = lax.broadcasted_iota(jnp.int32, (n_rows, n_rows), 1)
        own_ok = ((c_id & (MOBA_HEADS - 1)) == (r_id >> LOG2_DEC_SEQ)) & ((c_id >> LOG2_MOBA_HEADS) <= (r_id & (DEC_SEQ - 1)))
        s_own = jnp.where(own_ok, s_own, NEG_INF)
        m = jnp.max(s_own, axis=-1, keepdims=True)
        for n in range(self.n_blocks):
            m = jnp.maximum(m, self.block_max[n] + self.keep_bias[n])
        self.m = m
        self.lsum = jnp.zeros((n_rows, LANES), F32)
        e_own = jnp.exp2((s_own - m) * self.exp2_scale)
        return (jnp.sum(e_own, axis=-1, keepdims=True),
                jnp.dot(e_own.astype(BF16), self.vn_ref[...].astype(BF16), preferred_element_type=F32))

    def wait_v(self, c):
        self._wait_chunk(self.vc_ref, c)

    def v_pages(self, c, first, last, acc):
        assert first % self.pages_per_block == 0 and last % self.pages_per_block == 0
        for r0 in range(first, last, self.pages_per_block):
            shift = self.head_bias + (self.keep_bias[(c * CHUNK_PAGES + r0) // self.pages_per_block] - self.m)
            for r in range(r0, r0 + self.pages_per_block):
                e = [jnp.exp2((slab + shift) * self.exp2_scale) for slab in self._slabs(self.s_scr[c * CHUNK_PAGES + r])]
                self.lsum = self.lsum + functools.reduce(jnp.add, e)
                page = self.ring[(c % 2) * CHUNK_PAGES + r].astype(BF16)
                acc = acc + jnp.dot(jnp.concatenate(e, axis=1).astype(BF16), page, preferred_element_type=F32)
        return acc

    def refill_after_v(self, c):
        if c + 2 < self.n_chunks:
            self._start_chunk(self.vc_ref, self.seq, c + 2)
        else:
            @pl.when(self.seq + 1 < self.n_seqs)
            def _():
                self._start_chunk(self.kc_ref, self.seq + 1, c + 2 - self.n_chunks)

    def finish(self, l_own, acc):
        out = acc / (l_own + jnp.sum(self.lsum, axis=-1, keepdims=True))
        for hd in range(MOBA_HEADS):
            self.o_ref[:, hd * MOBA_HEAD_DIM:(hd + 1) * MOBA_HEAD_DIM] = out[hd * DEC_SEQ:(hd + 1) * DEC_SEQ]


def _layer_norm(x, g, b):
    mu = jnp.mean(x, axis=-1, keepdims=True)
    d = x - mu
    var = jnp.mean(d * d, axis=-1, keepdims=True)
    return d * lax.rsqrt(var + LN_EPS) * g + b


def _out_ffn_body(ar_ref, am_ref, x_ref, ga_ref, shf_ref, scf_ref, gf_ref, wo_ref, g1_ref, b1_ref,
                  wu_ref, wd_ref, g2_ref, b2_ref, y_ref, x1_scr, h_scr, acc_scr):
    c = pl.program_id(0)

    @pl.when(c == 0)
    def _():
        mixed = (jnp.dot(ar_ref[...].astype(BF16), wo_ref[:RET_WIDTH, :], preferred_element_type=F32)
                 + jnp.dot(am_ref[...].astype(BF16), wo_ref[RET_WIDTH:, :], preferred_element_type=F32))
        x1 = _layer_norm(ALPHA * x_ref[...] + _modulation(ga_ref, True) * mixed, g1_ref[...], b1_ref[...])
        x1_scr[...] = x1
        h_scr[...] = (x1 * (1.0 + _modulation(scf_ref, True)) + _modulation(shf_ref, True)).astype(BF16)
        acc_scr[...] = jnp.zeros_like(acc_scr)

    u = jnp.maximum(jnp.dot(h_scr[...], wu_ref[...], preferred_element_type=F32), 0.0)
    acc_scr[...] += jnp.dot((u * u).astype(BF16), wd_ref[...], preferred_element_type=F32)

    @pl.when(c == pl.num_programs(0) - 1)
    def _():
        y_ref[...] = _layer_norm(ALPHA * x1_scr[...] + _modulation(gf_ref, True) * acc_scr[...],
                                 g2_ref[...], b2_ref[...])


def _out_ffn(a_ret, a_moba, x2d, mod, weights):
    w_o, ln1_g, ln1_b, w_up, w_down, ln2_g, ln2_b = weights
    t = x2d.shape[0]
    whole = lambda a: pl.BlockSpec(a.shape, lambda c: (0,) * a.ndim)
    return pl.pallas_call(
        _out_ffn_body,
        grid=(D_FF // D_MODEL,),
        in_specs=[whole(a_ret), whole(a_moba), whole(x2d)] + [_mod_spec(term) for term in (2, 3, 4, 5)]
        + [whole(w_o), whole(ln1_g), whole(ln1_b),
           pl.BlockSpec((D_MODEL, D_MODEL), lambda c: (0, c)), pl.BlockSpec((D_MODEL, D_MODEL), lambda c: (c, 0)),
           whole(ln2_g), whole(ln2_b)],
        out_specs=pl.BlockSpec((t, D_MODEL), lambda c: (0, 0)),
        out_shape=jax.ShapeDtypeStruct((t, D_MODEL), F32),
        scratch_shapes=[pltpu.VMEM((t, D_MODEL), F32), pltpu.VMEM((t, D_MODEL), BF16), pltpu.VMEM((t, D_MODEL), F32)],
        compiler_params=pltpu.CompilerParams(dimension_semantics=("arbitrary",),
                                             vmem_limit_bytes=VMEM_LIMIT),
        name="out_ffn",
    )(a_ret, a_moba, x2d, *([mod] * 4), *weights)


def _out_ffn_moba_body(pt_ref, ar_ref, am_ref, x_ref, ga_ref, shf_ref, scf_ref, gf_ref, wo_ref, g1_ref, b1_ref,
                       wu_ref, wd_ref, g2_ref, b2_ref, q_ref, kn_ref, vn_ref, kc_ref, vc_ref,
                       y_ref, o_ref, ring, sem, s_scr):
    n_seqs = pl.num_programs(0) * pl.num_programs(1)
    seq = pl.program_id(0) * pl.num_programs(1) + pl.program_id(1)
    sm = _SampleMoba(pt_ref, seq, n_seqs, q_ref, kn_ref, vn_ref, kc_ref, vc_ref, o_ref, ring, sem, s_scr)

    def up(c, h):
        u = jnp.maximum(jnp.dot(h, wu_ref[:, c * D_MODEL:(c + 1) * D_MODEL], preferred_element_type=F32), 0.0)
        return (u * u).astype(BF16)

    def down(c, u):
        return jnp.dot(u, wd_ref[c * D_MODEL:(c + 1) * D_MODEL, :], preferred_element_type=F32)

    half = CHUNK_PAGES // 2
    sm.prologue()

    sm.wait_k(0)
    sm.k_pages(0, 0, half)
    mixed = jnp.dot(ar_ref[...].astype(BF16), wo_ref[:RET_WIDTH, :], preferred_element_type=F32)
    sm.k_pages(0, half, CHUNK_PAGES)
    mixed = mixed + jnp.dot(am_ref[...].astype(BF16), wo_ref[RET_WIDTH:, :], preferred_element_type=F32)
    x1 = _layer_norm(ALPHA * x_ref[...] + _modulation(ga_ref, False) * mixed, g1_ref[...], b1_ref[...])
    h = (x1 * (1.0 + _modulation(scf_ref, False)) + _modulation(shf_ref, False)).astype(BF16)
    sm.refill_after_k(0)

    sm.wait_k(1)
    sm.k_pages(1, 0, half)
    u = up(0, h)
    sm.k_pages(1, half, CHUNK_PAGES)
    sm.refill_after_k(1)

    sm.wait_k(2)
    sm.k_pages(2, 0, half)
    acc = down(0, u)
    sm.k_pages(2, half, CHUNK_PAGES)
    sm.refill_after_k(2)

    sm.wait_k(3)
    sm.k_pages(3, 0, half)
    u = up(1, h)
    sm.k_pages(3, half, CHUNK_PAGES)
    sm.refill_after_k(3)

    l, acc_s = sm.select()
    acc = acc + down(1, u)

    sm.wait_v(0)
    acc_s = sm.v_pages(0, 0, half, acc_s)
    u = up(2, h)
    acc_s = sm.v_pages(0, half, CHUNK_PAGES, acc_s)
    sm.refill_after_v(0)

    sm.wait_v(1)
    acc_s = sm.v_pages(1, 0, half, acc_s)
    acc = acc + down(2, u)
    acc_s = sm.v_pages(1, half, CHUNK_PAGES, acc_s)
    sm.refill_after_v(1)

    sm.wait_v(2)
    acc_s = sm.v_pages(2, 0, half, acc_s)
    u = up(3, h)
    acc_s = sm.v_pages(2, half, CHUNK_PAGES, acc_s)
    sm.refill_after_v(2)

    sm.wait_v(3)
    acc_s = sm.v_pages(3, 0, half, acc_s)
    acc = acc + down(3, u)
    acc_s = sm.v_pages(3, half, CHUNK_PAGES, acc_s)
    sm.refill_after_v(3)
    y_ref[...] = _layer_norm(ALPHA * x1 + _modulation(gf_ref, False) * acc, g2_ref[...], b2_ref[...])
    sm.finish(l, acc_s)


def _out_ffn_moba(a_ret, a_moba, x2d, mod, weights, page_table, mq_s, kn2d, vn2d, cache_k2d, cache_v2d, tm):
    t = x2d.shape[0]
    nt = SEQ // tm
    assert BATCH * nt == DEC_BATCH
    n_rows = MOBA_HEADS * DEC_SEQ
    tile = lambda b, i, pt: (b * nt + i, 0)
    wide = lambda w: pl.BlockSpec((tm, w), tile)
    const = lambda a: pl.BlockSpec(a.shape, lambda b, i, pt: (0,) * a.ndim, pipeline_mode=pl.Buffered(1))
    seq_rows = lambda w, n: pl.BlockSpec((n, w), tile)
    hbm = pl.BlockSpec(memory_space=pl.ANY)
    grid_spec = pltpu.PrefetchScalarGridSpec(
        num_scalar_prefetch=1,
        grid=(BATCH, nt),
        in_specs=[wide(RET_WIDTH), wide(MOBA_WIDTH), wide(D_MODEL)] + [_mod_spec(term) for term in (2, 3, 4, 5)]
        + [const(a) for a in weights]
        + [seq_rows(MOBA_WIDTH, DEC_SEQ), seq_rows(LANES, n_rows), seq_rows(LANES, n_rows), hbm, hbm],
        out_specs=[wide(D_MODEL), seq_rows(MOBA_WIDTH, DEC_SEQ)],
        scratch_shapes=[pltpu.VMEM((RING_PAGES, PAGE_ROWS, LANES), F32),
                        pltpu.SemaphoreType.DMA((RING_PAGES // CHUNK_PAGES,)),
                        pltpu.VMEM((PAGES_PER_SEQ, n_rows, PAGE_ROWS), F32)],
    )
    return pl.pallas_call(
        _out_ffn_moba_body,
        grid_spec=grid_spec,
        out_shape=[jax.ShapeDtypeStruct((t, D_MODEL), F32),
                   jax.ShapeDtypeStruct((N_SAMPLE_TOK, MOBA_WIDTH), F32)],
        compiler_params=pltpu.CompilerParams(dimension_semantics=("arbitrary", "arbitrary"),
                                             vmem_limit_bytes=VMEM_LIMIT_FUSED),
        name="out_ffn_moba",
    )(page_table.reshape(-1), a_ret, a_moba, x2d, *([mod] * 4), *weights,
      mq_s, kn2d, vn2d, cache_k2d, cache_v2d)


def kernel(x_prompt, x_sample, cache_k, cache_v, state_ret, page_table, c_prompt, c_sample,
           w_ada, b_ada, w_in, w_o, ln1_g, ln1_b, w_up, w_down, ln2_g, ln2_b):
    n_prompt_tok = BATCH * SEQ
    past_len = page_table.shape[1] * PAGE_SIZE

    mod = _adaln(c_prompt, c_sample, w_ada[0], b_ada)

    tm = 512
    nt = SEQ // tm
    p_row = lambda b, i: (b * nt + i, 0)
    p_tab = pl.BlockSpec((tm, LANES), lambda b, i: (i, 0))
    s_row = lambda i: (0, 0)
    s_tab = pl.BlockSpec((N_SAMPLE_TOK, LANES), s_row)

    pos_p = np.arange(SEQ, dtype=np.int32)
    pos_s = np.tile(past_len + np.arange(DEC_SEQ, dtype=np.int32), DEC_BATCH)
    tabs_p = _rope_tables(pos_p, MOBA_HEAD_DIM) + _rope_tables(pos_p, RET_DK)
    tabs_s = _rope_tables(pos_s, MOBA_HEAD_DIM) + _rope_tables(pos_s, RET_DK)

    xp = x_prompt.reshape(n_prompt_tok, D_MODEL)
    rq, rk, rv, rg, mq, k_p, v_p, w_o_b, w_up_b, w_down_b = _inproj(
        xp, mod, False, [p_tab] * 4, tabs_p, w_in[0],
        (BATCH, nt), p_row, tm, BF16, casts=(w_o[0], w_up[0], w_down[0]))
    weights = (w_o_b, ln1_g, ln1_b, w_up_b, w_down_b, ln2_g, ln2_b)
    a_ret, state_p = _ret_prompt(rq, rk, rv, rg)
    a_moba = _moba_prompt(mq, k_p, v_p)

    xs = x_sample.reshape(N_SAMPLE_TOK, D_MODEL)
    rq_s, rk_s, rv_s, rg_s, mq_s, k_s, v_s = _inproj(
        xs, mod, True, [s_tab] * 4, tabs_s, w_in[0],
        (1,), s_row, N_SAMPLE_TOK, F32)
    a_ret_s, state_s = _ret_sample(rq_s, rk_s, rv_s, rg_s, state_ret[0])

    cache_rows = cache_k.shape[1] * PAGE_ROWS
    y_p, a_moba_s = _out_ffn_moba(a_ret, a_moba, xp, mod, weights, page_table, mq_s, k_s, v_s,
                                  cache_k.reshape(cache_rows, LANES), cache_v.reshape(cache_rows, LANES), tm)
    y_s = _out_ffn(a_ret_s, a_moba_s, xs, mod, weights)

    kv_p_shape = (DEPTH, BATCH, SEQ, MOBA_HEADS, MOBA_HEAD_DIM)
    kv_s_shape = (DEPTH, DEC_BATCH, DEC_SEQ, MOBA_HEADS, MOBA_HEAD_DIM)
    return (y_p.reshape(BATCH, SEQ, D_MODEL),
            y_s.reshape(DEC_BATCH, DEC_SEQ, D_MODEL),
            k_p.reshape(kv_p_shape), v_p.reshape(kv_p_shape), state_p[None],
            k_s.reshape(kv_s_shape), v_s.reshape(kv_s_shape), state_s[None])
```

```python
import functools

import numpy as np
import jax
import jax.numpy as jnp
from jax import lax
from jax.experimental import pallas as pl
from jax.experimental.pallas import tpu as pltpu

F32 = jnp.float32
BF16 = jnp.bfloat16

D_MODEL = 1024
BATCH = 8
SEQ = 2048
DEC_BATCH = 32
DEC_SEQ = 8
PAGE_SIZE = 128
RET_HEADS = 4
RET_DK = 64
RET_DV = 128
RET_CHUNK = 128
MOBA_HEADS = 4
MOBA_HEAD_DIM = 128
MOBA_BLOCK = 256
MOBA_TOPK = 3
D_FF = 4 * D_MODEL
ROPE_THETA = 10000.0
LN_EPS = 1e-5
GN_EPS = 1e-6
DEPTH = 1
ALPHA = (2 * DEPTH) ** 0.25
RET_QK = RET_HEADS * RET_DK
RET_WIDTH = RET_HEADS * RET_DV
MOBA_WIDTH = MOBA_HEADS * MOBA_HEAD_DIM
IN_WIDTH = 2 * RET_QK + 2 * RET_WIDTH + 3 * MOBA_WIDTH
OFF_RQ, OFF_RK, OFF_RV, OFF_RG = 0, 256, 512, 1024
OFF_MQ, OFF_MK, OFF_MV = 1536, 2048, 2560
LANES = 128
VMEM_BYTES_V7X = 64 * 1024 * 1024
VMEM_LIMIT = VMEM_BYTES_V7X * 3 // 4
VMEM_LIMIT_FUSED = VMEM_BYTES_V7X * 29 // 32
N_SAMPLE_TOK = DEC_BATCH * DEC_SEQ
PAGES_PER_SEQ = 64
CHUNK_PAGES = 16
RING_PAGES = 2 * CHUNK_PAGES
PAGE_ROWS = PAGE_SIZE * MOBA_HEADS
LOG2_RET_DK = RET_DK.bit_length() - 1
LOG2_DEC_SEQ = DEC_SEQ.bit_length() - 1
LOG2_MOBA_HEADS = MOBA_HEADS.bit_length() - 1
NEG_INF = float("-inf")
LOG2_E = 1.4426950408889634

_NT = (((1,), (1,)), ((), ()))
_TN = (((0,), (0,)), ((), ()))


def _log_decay():
    return np.log1p(-np.exp2(-5.0 - np.arange(RET_HEADS, dtype=np.float64)))


N_MOD = 6
MOD_ROWS = BATCH + DEC_BATCH


def _adaln_body(cp_ref, cs_ref, w_ref, b_ref, o_ref):
    c = jnp.concatenate([cp_ref[...], cs_ref[...]], axis=0)
    a = (c * jax.nn.sigmoid(c)).astype(BF16)
    o_ref[...] = jnp.dot(a, w_ref[...].astype(BF16), preferred_element_type=F32) + b_ref[...]


def _adaln(c_prompt, c_sample, w_ada, b_ada):
    return pl.pallas_call(
        _adaln_body,
        grid=(N_MOD,),
        in_specs=[pl.BlockSpec(c_prompt.shape, lambda j: (0, 0)),
                  pl.BlockSpec(c_sample.shape, lambda j: (0, 0)),
                  pl.BlockSpec((D_MODEL, D_MODEL), lambda j: (0, j)),
                  pl.BlockSpec((1, D_MODEL), lambda j: (0, j))],
        out_specs=pl.BlockSpec((None, MOD_ROWS, D_MODEL), lambda j: (j, 0, 0)),
        out_shape=jax.ShapeDtypeStruct((N_MOD, MOD_ROWS, D_MODEL), F32),
        name="adaln",
    )(c_prompt, c_sample, w_ada, b_ada)


def _modulation(ref, sample_group):
    if sample_group:
        rows = ref[BATCH:BATCH + DEC_BATCH, :]
        return jnp.broadcast_to(rows[:, None, :], (DEC_BATCH, DEC_SEQ, D_MODEL)).reshape(N_SAMPLE_TOK, D_MODEL)
    return ref[pl.ds(pl.program_id(0), 1), :]


def _mod_spec(term):
    return pl.BlockSpec((None, MOD_ROWS, D_MODEL), lambda *_: (term, 0, 0))


def _rope_tables(pos, head_dim):
    half = head_dim // 2
    inv_freq = np.power(ROPE_THETA, -np.arange(half, dtype=np.float64) / half)
    ang = pos.astype(np.float64)[:, None] * inv_freq[None, :]
    cos, sin = np.cos(ang), np.sin(ang)
    reps = LANES // head_dim
    cos_t = np.tile(np.concatenate([cos, cos], axis=-1), (1, reps))
    sin_t = np.tile(np.concatenate([-sin, sin], axis=-1), (1, reps))
    return jnp.asarray(cos_t, dtype=F32), jnp.asarray(sin_t, dtype=F32)


def _inproj_body(sample_group, n_casts, x_ref, sc_ref, sh_ref, w_ref, cm_ref, sm_ref, cr_ref, sr_ref, *refs):
    cast_in, refs = refs[:n_casts], refs[n_casts:]
    rq_ref, rk_ref, rv_ref, rg_ref, mq_ref, ko_ref, vo_ref = refs[:7]
    for src, dst in zip(cast_in, refs[7:]):
        dst[...] = src[...].astype(dst.dtype)
    tm = x_ref.shape[0]
    h = (x_ref[...] * (1.0 + _modulation(sc_ref, sample_group)) + _modulation(sh_ref, sample_group)).astype(BF16)

    def proj(lo, width):
        return jnp.dot(h, w_ref[:, lo:lo + width].astype(BF16), preferred_element_type=F32)

    lane = lax.broadcasted_iota(jnp.int32, (tm, LANES), 1)
    low_half = (lane & (RET_DK - 1)) < (RET_DK // 2)
    cr, sr = cr_ref[...], sr_ref[...]
    cm, sm = cm_ref[...], sm_ref[...]

    def rope_ret(z):
        rot = jnp.where(low_half, pltpu.roll(z, LANES - RET_DK // 2, 1), pltpu.roll(z, RET_DK // 2, 1))
        return z * cr + rot * sr

    def rope_moba(z):
        return z * cm + pltpu.roll(z, MOBA_HEAD_DIM // 2, 1) * sm

    zq = proj(OFF_RQ, RET_QK)
    zk = proj(OFF_RK, RET_QK)
    for s in range(RET_QK // LANES):
        sl = slice(s * LANES, (s + 1) * LANES)
        rq_ref[:, sl] = rope_ret(zq[:, sl])
        rk_ref[:, sl] = rope_ret(zk[:, sl]) * (RET_DK ** -0.5)
    rv_ref[...] = proj(OFF_RV, RET_WIDTH).astype(rv_ref.dtype)
    rg_ref[...] = proj(OFF_RG, RET_WIDTH)
    zq = proj(OFF_MQ, MOBA_WIDTH)
    zk = proj(OFF_MK, MOBA_WIDTH)
    zv = proj(OFF_MV, MOBA_WIDTH)
    for hd in range(MOBA_HEADS):
        sl = slice(hd * LANES, (hd + 1) * LANES)
        mq_ref[:, sl] = rope_moba(zq[:, sl]).astype(mq_ref.dtype)
        ko_ref[pl.ds(hd, tm, stride=MOBA_HEADS), :] = rope_moba(zk[:, sl])
        vo_ref[pl.ds(hd, tm, stride=MOBA_HEADS), :] = zv[:, sl]


def _inproj(x2d, mod, sample_group, tab_specs, tabs, w_in, grid, row_map, tm, act_dtype, casts=()):
    t = x2d.shape[0]
    n_steps = int(np.prod(grid))
    wide = lambda w: pl.BlockSpec((tm, w), row_map)
    cast_specs = [pl.BlockSpec((a.shape[0] // n_steps, a.shape[1]), row_map) for a in casts]
    w_spec = pl.BlockSpec((D_MODEL, IN_WIDTH), lambda *_: (0, 0), pipeline_mode=pl.Buffered(1))
    return pl.pallas_call(
        functools.partial(_inproj_body, sample_group, len(casts)),
        grid=grid,
        in_specs=[wide(D_MODEL), _mod_spec(1), _mod_spec(0), w_spec] + tab_specs + cast_specs,
        out_specs=[wide(RET_QK), wide(RET_QK), wide(RET_WIDTH), wide(RET_WIDTH), wide(MOBA_WIDTH),
                   pl.BlockSpec((tm * MOBA_HEADS, LANES), row_map),
                   pl.BlockSpec((tm * MOBA_HEADS, LANES), row_map)] + cast_specs,
        out_shape=[jax.ShapeDtypeStruct((t, RET_QK), F32), jax.ShapeDtypeStruct((t, RET_QK), F32),
                   jax.ShapeDtypeStruct((t, RET_WIDTH), act_dtype), jax.ShapeDtypeStruct((t, RET_WIDTH), F32),
                   jax.ShapeDtypeStruct((t, MOBA_WIDTH), act_dtype),
                   jax.ShapeDtypeStruct((t * MOBA_HEADS, LANES), F32),
                   jax.ShapeDtypeStruct((t * MOBA_HEADS, LANES), F32)]
        + [jax.ShapeDtypeStruct(a.shape, BF16) for a in casts],
        compiler_params=pltpu.CompilerParams(dimension_semantics=("arbitrary",) * len(grid),
                                             vmem_limit_bytes=VMEM_LIMIT),
        name="inproj",
    )(x2d, mod, mod, w_in, *tabs, *casts)


def _group_norm_gate(o, g):
    mu = jnp.mean(o, axis=-1, keepdims=True)
    d = o - mu
    var = jnp.mean(d * d, axis=-1, keepdims=True)
    return d * lax.rsqrt(var + GN_EPS) * (g * jax.nn.sigmoid(g))


def _ret_prompt_tables():
    lg = _log_decay()
    i = np.arange(RET_CHUNK, dtype=np.float64)
    diff = i[:, None] - i[None, :]
    dmat = np.where(diff >= 0, np.exp(np.maximum(diff, 0.0)[None] * lg[:, None, None]), 0.0)
    lane_head = np.arange(RET_QK) // RET_DK
    qd = np.exp((i[:, None] + 1.0) * lg[lane_head][None, :])
    kd = np.exp((RET_CHUNK - 1.0 - i)[:, None] * lg[lane_head][None, :])
    return [jnp.asarray(a, dtype=F32) for a in (dmat, np.concatenate([qd, kd], axis=1))]


def _ret_prompt_body(q_ref, k_ref, v_ref, g_ref, dmat_ref, qkd_ref, o_ref, st_ref):
    lane_head = lax.broadcasted_iota(jnp.int32, (RET_CHUNK, RET_QK), 1) >> LOG2_RET_DK
    chunk_rows = lambda c: slice(c * RET_CHUNK, (c + 1) * RET_CHUNK)
    chunk_decay = [float(np.exp(RET_CHUNK * lg)) for lg in _log_decay()]
    head_block = lambda h: (slice(h * RET_DK, (h + 1) * RET_DK), slice(h * RET_DV, (h + 1) * RET_DV))

    def block_diagonal(state):
        zero = jnp.zeros((RET_DK, RET_DV), BF16)
        return jnp.concatenate(
            [jnp.concatenate([state[h].astype(BF16) if c == h else zero for c in range(RET_HEADS)], axis=1)
             for h in range(RET_HEADS)], axis=0)

    def first_matmuls(c, state):
        rows = chunk_rows(c)
        q = q_ref[rows, :]
        k = k_ref[rows, :]
        v = v_ref[rows, :]
        kb = k.astype(BF16)
        scores = [lax.dot_general(jnp.where(lane_head == hd, q, 0.0).astype(BF16), kb, _NT,
                                  preferred_element_type=F32) for hd in range(RET_HEADS)]
        cross = jnp.dot((q * qkd_ref[:, :RET_QK]).astype(BF16), block_diagonal(state),
                        preferred_element_type=F32)
        kv = lax.dot_general((k * qkd_ref[:, RET_QK:]).astype(BF16), v, _TN, preferred_element_type=F32)
        new_state = [chunk_decay[h] * state[h] + kv[head_block(h)] for h in range(RET_HEADS)]
        return (scores, cross, v), new_state

    def second_matmuls(c, scores, cross, v):
        rows = chunk_rows(c)
        g = g_ref[rows, :]
        decayed = [(scores[hd] * dmat_ref[hd]).astype(BF16) for hd in range(RET_HEADS)]
        for hd in range(RET_HEADS):
            sl = slice(hd * RET_DV, (hd + 1) * RET_DV)
            inner = jnp.dot(decayed[hd], v[:, sl], preferred_element_type=F32)
            o_ref[rows, sl] = _group_norm_gate(inner + cross[:, sl], g[:, sl]).astype(o_ref.dtype)

    n_chunks = SEQ // RET_CHUNK
    ahead, state = first_matmuls(0, [jnp.zeros((RET_DK, RET_DV), F32)] * RET_HEADS)
    for c in range(n_chunks):
        current = ahead
        if c + 1 < n_chunks:
            ahead, state = first_matmuls(c + 1, state)
        second_matmuls(c, *current)
    for hd in range(RET_HEADS):
        st_ref[hd] = state[hd]


def _ret_prompt(rq, rk, rv, rg):
    tabs = _ret_prompt_tables()
    seq = lambda w: pl.BlockSpec((SEQ, w), lambda b: (b, 0))
    const = lambda a: pl.BlockSpec(a.shape, lambda b: (0,) * a.ndim)
    return pl.pallas_call(
        _ret_prompt_body,
        grid=(BATCH,),
        in_specs=[seq(RET_QK), seq(RET_QK), seq(RET_WIDTH), seq(RET_WIDTH)] + [const(a) for a in tabs],
        out_specs=[seq(RET_WIDTH), pl.BlockSpec((None, RET_HEADS, RET_DK, RET_DV), lambda b: (b, 0, 0, 0))],
        out_shape=[jax.ShapeDtypeStruct((BATCH * SEQ, RET_WIDTH), BF16),
                   jax.ShapeDtypeStruct((BATCH, RET_HEADS, RET_DK, RET_DV), F32)],
        compiler_params=pltpu.CompilerParams(dimension_semantics=("arbitrary",),
                                             vmem_limit_bytes=VMEM_LIMIT),
        name="ret_prompt",
    )(rq, rk, rv, rg, *tabs)


def _ret_sample_tables():
    lg = _log_decay()
    t = np.arange(N_SAMPLE_TOK) % DEC_SEQ
    seq_id = np.arange(N_SAMPLE_TOK) // DEC_SEQ
    diff = (t[:, None] - t[None, :]).astype(np.float64)
    same_seq = seq_id[:, None] == seq_id[None, :]
    dmat = np.where(same_seq[None] & (diff >= 0)[None],
                    np.exp(np.maximum(diff, 0.0)[None] * lg[:, None, None]), 0.0)
    lane_head = np.arange(RET_QK) // RET_DK
    qd = np.exp((t[:, None] + 1.0) * lg[lane_head][None, :])
    kd = np.exp((DEC_SEQ - 1.0 - t)[:, None] * lg[lane_head][None, :])
    return [jnp.asarray(a, dtype=F32) for a in (dmat, np.concatenate([qd, kd], axis=1))]


def _ret_sample_body(q_ref, k_ref, v_ref, g_ref, st_ref, dmat_ref, qkd_ref, o_ref, sto_ref):
    lg = _log_decay()
    q = q_ref[...]
    k = k_ref[...]
    kb = k.astype(BF16)
    qdec = q * qkd_ref[:, :RET_QK]
    kdec = k * qkd_ref[:, RET_QK:]
    vb = v_ref[...].astype(BF16)
    g = g_ref[...]
    lane = lax.broadcasted_iota(jnp.int32, (N_SAMPLE_TOK, LANES), 1)
    lane_head = lax.broadcasted_iota(jnp.int32, (N_SAMPLE_TOK, RET_QK), 1) >> LOG2_RET_DK
    n_state_rows = DEC_BATCH * RET_DK
    own_seq = ((lax.broadcasted_iota(jnp.int32, (N_SAMPLE_TOK, n_state_rows), 0) >> LOG2_DEC_SEQ)
               == (lax.broadcasted_iota(jnp.int32, (N_SAMPLE_TOK, n_state_rows), 1) >> LOG2_RET_DK))
    for hd in range(RET_HEADS):
        sl = slice(hd * RET_DV, (hd + 1) * RET_DV)
        qm = jnp.where(lane_head == hd, q, 0.0).astype(BF16)
        s = lax.dot_general(qm, kb, _NT, preferred_element_type=F32) * dmat_ref[hd]
        inner = jnp.dot(s.astype(BF16), vb[:, sl], preferred_element_type=F32)

        def expand(z):
            slab = z[:, (hd // 2) * LANES:(hd // 2 + 1) * LANES]
            other = pltpu.roll(slab, RET_DK, 1)
            in_low = lane < RET_DK
            both = jnp.where(in_low, slab, other) if hd % 2 == 0 else jnp.where(in_low, other, slab)
            tiled = jnp.concatenate([both] * (n_state_rows // LANES), axis=1)
            return jnp.where(own_seq, tiled, 0.0).astype(BF16)

        st = st_ref[:, hd].reshape(n_state_rows, RET_DV)
        cross = jnp.dot(expand(qdec), st.astype(BF16), preferred_element_type=F32)
        o_ref[:, sl] = _group_norm_gate(inner + cross, g[:, sl])
        kv = lax.dot_general(expand(kdec), vb[:, sl], _TN, preferred_element_type=F32)
        new = float(np.exp(DEC_SEQ * lg[hd])) * st + kv
        sto_ref[:, hd] = new.reshape(DEC_BATCH, RET_DK, RET_DV)


def _ret_sample(rq, rk, rv, rg, state):
    tabs = _ret_sample_tables()
    full = lambda a: pl.BlockSpec(a.shape, lambda i: (0,) * a.ndim)
    args = (rq, rk, rv, rg, state, *tabs)
    return pl.pallas_call(
        _ret_sample_body,
        grid=(1,),
        in_specs=[full(a) for a in args],
        out_specs=[pl.BlockSpec((N_SAMPLE_TOK, RET_WIDTH), lambda i: (0, 0)),
                   pl.BlockSpec(state.shape, lambda i: (0, 0, 0, 0))],
        out_shape=[jax.ShapeDtypeStruct((N_SAMPLE_TOK, RET_WIDTH), F32),
                   jax.ShapeDtypeStruct(state.shape, F32)],
        compiler_params=pltpu.CompilerParams(dimension_semantics=("arbitrary",),
                                             vmem_limit_bytes=VMEM_LIMIT),
        name="ret_sample",
    )(*args)


def _moba_prompt_body(q_ref, k_ref, v_ref, o_ref):
    hd = pl.program_id(1)
    n_blocks = SEQ // MOBA_BLOCK
    exp2_scale = MOBA_HEAD_DIM ** -0.5 * LOG2_E
    k32 = k_ref[pl.ds(hd, SEQ, stride=MOBA_HEADS), :]
    kb = (k32 * exp2_scale).astype(BF16)
    vt = v_ref[pl.ds(hd, SEQ, stride=MOBA_HEADS), :].T.astype(BF16)
    kmean = jnp.sum(k32.reshape(n_blocks, MOBA_BLOCK, MOBA_HEAD_DIM), axis=1) * (1.0 / MOBA_BLOCK)
    kmb = kmean.astype(BF16)
    key_id = lax.broadcasted_iota(jnp.int32, (MOBA_BLOCK, MOBA_BLOCK), 0)
    qry_id = lax.broadcasted_iota(jnp.int32, (MOBA_BLOCK, MOBA_BLOCK), 1)
    causal = key_id <= qry_id

    blk = lambda n: slice(n * MOBA_BLOCK, (n + 1) * MOBA_BLOCK)

    def score_matmuls(i):
        qi = q_ref[blk(i), :]
        st = [lax.dot_general(kb[blk(n)], qi, _NT, preferred_element_type=F32) for n in range(i + 1)]
        st[i] = jnp.where(causal, st[i], NEG_INF)
        top = [jnp.max(s, axis=0, keepdims=True) for s in st]
        gt = lax.dot_general(kmb, qi, _NT, preferred_element_type=F32) if i > MOBA_TOPK else None
        return st, top, gt

    ahead = score_matmuls(n_blocks - 1)
    for i in reversed(range(n_blocks)):
        st, top, gt = ahead
        if i > 0:
            ahead = score_matmuls(i - 1)
        drop = [0.0] * (i + 1)
        if i > MOBA_TOPK:
            for n in range(i):
                beats = jnp.zeros((1, MOBA_BLOCK), F32)
                for mm in range(i):
                    if mm == n:
                        continue
                    win = (gt[mm:mm + 1] >= gt[n:n + 1]) if mm < n else (gt[mm:mm + 1] > gt[n:n + 1])
                    beats = beats + win.astype(F32)
                drop[n] = jnp.where(beats < MOBA_TOPK, 0.0, NEG_INF)
        m = functools.reduce(jnp.maximum, [top[n] + drop[n] for n in range(i + 1)])
        l = jnp.zeros((1, MOBA_BLOCK), F32)
        acc = jnp.zeros((MOBA_HEAD_DIM, MOBA_BLOCK), F32)
        for n in range(i + 1):
            e = jnp.exp2(st[n] - (m - drop[n]))
            l = l + jnp.sum(e, axis=0, keepdims=True)
            acc = acc + jnp.dot(vt[:, blk(n)], e.astype(BF16), preferred_element_type=F32)
        o_ref[blk(i), :] = (acc / l).T.astype(o_ref.dtype)


def _moba_prompt(mq, k2d, v2d):
    kv_spec = pl.BlockSpec((SEQ * MOBA_HEADS, LANES), lambda b, h: (b, 0))
    return pl.pallas_call(
        _moba_prompt_body,
        grid=(BATCH, MOBA_HEADS),
        in_specs=[pl.BlockSpec((SEQ, MOBA_HEAD_DIM), lambda b, h: (b, h)), kv_spec, kv_spec],
        out_specs=pl.BlockSpec((SEQ, MOBA_HEAD_DIM), lambda b, h: (b, h)),
        out_shape=jax.ShapeDtypeStruct((BATCH * SEQ, MOBA_WIDTH), BF16),
        compiler_params=pltpu.CompilerParams(dimension_semantics=("arbitrary", "arbitrary"),
                                             vmem_limit_bytes=VMEM_LIMIT),
        name="moba_prompt",
    )(mq, k2d, v2d)


class _SampleMoba:
    n_chunks = PAGES_PER_SEQ // CHUNK_PAGES
    n_rows = MOBA_HEADS * DEC_SEQ
    n_blocks = PAGES_PER_SEQ * PAGE_SIZE // MOBA_BLOCK
    pages_per_block = MOBA_BLOCK // PAGE_SIZE

    def __init__(self, pt_ref, seq, n_seqs, q_ref, kn_ref, vn_ref, kc_ref, vc_ref, o_ref, ring, sem, s_scr):
        self.pt_ref, self.seq, self.n_seqs = pt_ref, seq, n_seqs
        self.q_ref, self.kn_ref, self.vn_ref = q_ref, kn_ref, vn_ref
        self.kc_ref, self.vc_ref, self.o_ref = kc_ref, vc_ref, o_ref
        self.ring, self.sem, self.s_scr = ring, sem, s_scr

    def _page_copy(self, cache_ref, row0, slot):
        return pltpu.make_async_copy(cache_ref.at[pl.ds(row0, PAGE_ROWS)], self.ring.at[slot],
                                     self.sem.at[slot // CHUNK_PAGES])

    def _start_chunk(self, cache_ref, seq, chunk):
        for r in range(CHUNK_PAGES):
            page = self.pt_ref[seq * PAGES_PER_SEQ + chunk * CHUNK_PAGES + r]
            self._page_copy(cache_ref, pl.multiple_of(page * PAGE_ROWS, PAGE_ROWS),
                            (chunk % 2) * CHUNK_PAGES + r).start()

    def _wait_chunk(self, cache_ref, chunk):
        for r in range(CHUNK_PAGES):
            self._page_copy(cache_ref, 0, (chunk % 2) * CHUNK_PAGES + r).wait()

    def prologue(self):
        @pl.when(self.seq == 0)
        def _():
            self._start_chunk(self.kc_ref, self.seq, 0)
            self._start_chunk(self.kc_ref, self.seq, 1)

        self.q = jnp.concatenate(
            [self.q_ref[:, hd * MOBA_HEAD_DIM:(hd + 1) * MOBA_HEAD_DIM] for hd in range(MOBA_HEADS)], axis=0
        ).astype(BF16)
        row_head = lax.broadcasted_iota(jnp.int32, (self.n_rows, LANES), 0) >> LOG2_DEC_SEQ
        col_head = lax.broadcasted_iota(jnp.int32, (self.n_rows, LANES), 1) & (MOBA_HEADS - 1)
        self.same_head = row_head == col_head
        self.head_bias = jnp.where(self.same_head, 0.0, NEG_INF)
        self.block_sum, self.block_max = [], []

    @staticmethod
    def _slabs(x):
        return [x[:, j * LANES:(j + 1) * LANES] for j in range(PAGE_ROWS // LANES)]

    def wait_k(self, c):
        self._wait_chunk(self.kc_ref, c)

    def k_pages(self, c, first, last):
        assert first % self.pages_per_block == 0 and last % self.pages_per_block == 0
        for r0 in range(first, last, self.pages_per_block):
            tot = jnp.zeros((self.n_rows, LANES), F32)
            top = jnp.full((self.n_rows, LANES), NEG_INF, F32)
            for r in range(r0, r0 + self.pages_per_block):
                page = self.ring[(c % 2) * CHUNK_PAGES + r].astype(BF16)
                s = lax.dot_general(self.q, page, _NT, preferred_element_type=F32)
                self.s_scr[c * CHUNK_PAGES + r] = s
                for slab in self._slabs(s):
                    tot = tot + jnp.where(self.same_head, slab, 0.0)
                    top = jnp.maximum(top, slab + self.head_bias)
            self.block_sum.append(jnp.sum(tot, axis=-1, keepdims=True))
            self.block_max.append(jnp.max(top, axis=-1, keepdims=True))

    def refill_after_k(self, c):
        if c + 2 < self.n_chunks:
            self._start_chunk(self.kc_ref, self.seq, c + 2)
        else:
            self._start_chunk(self.vc_ref, self.seq, c + 2 - self.n_chunks)

    def select(self):
        n_rows, gs = self.n_rows, self.block_sum
        self.exp2_scale = MOBA_HEAD_DIM ** -0.5 * LOG2_E
        lane = lax.broadcasted_iota(jnp.int32, (n_rows, LANES), 1)
        g_all = jnp.full((n_rows, LANES), NEG_INF, F32)
        for n in range(self.n_blocks):
            g_all = jnp.where(lane == n, gs[n], g_all)
        self.keep_bias = []
        for n in range(self.n_blocks):
            wins = (g_all > gs[n]) | ((g_all == gs[n]) & (lane < n))
            beats = jnp.sum(wins.astype(F32), axis=-1, keepdims=True)
            self.keep_bias.append(jnp.where(beats < MOBA_TOPK, 0.0, NEG_INF))
        s_own = lax.dot_general(self.q, self.kn_ref[...].astype(BF16), _NT, preferred_element_type=F32)
        r_id = lax.broadcasted_iota(jnp.int32, (n_rows, n_rows), 0)
        c_id = lax.broadcasted_iota(jnp.int32, (n_rows, n_rows), 1)
        own_ok = ((c_id & (MOBA_HEADS - 1)) == (r_id >> LOG2_DEC_SEQ)) & ((c_id >> LOG2_MOBA_HEADS) <= (r_id & (DEC_SEQ - 1)))
        s_own = jnp.where(own_ok, s_own, NEG_INF)
        m = jnp.max(s_own, axis=-1, keepdims=True)
        for n in range(self.n_blocks):
            m = jnp.maximum(m, self.block_max[n] + self.keep_bias[n])
        self.m = m
        self.lsum = jnp.zeros((n_rows, LANES), F32)
        e_own = jnp.exp2((s_own - m) * self.exp2_scale)
        return (jnp.sum(e_own, axis=-1, keepdims=True),
                jnp.dot(e_own.astype(BF16), self.vn_ref[...].astype(BF16), preferred_element_type=F32))

    def wait_v(self, c):
        self._wait_chunk(self.vc_ref, c)

    def v_pages(self, c, first, last, acc):
        assert first % self.pages_per_block == 0 and last % self.pages_per_block == 0
        for r0 in range(first, last, self.pages_per_block):
            shift = self.head_bias + (self.keep_bias[(c * CHUNK_PAGES + r0) // self.pages_per_block] - self.m)
            for r in range(r0, r0 + self.pages_per_block):
                e = [jnp.exp2((slab + shift) * self.exp2_scale) for slab in self._slabs(self.s_scr[c * CHUNK_PAGES + r])]
                self.lsum = self.lsum + functools.reduce(jnp.add, e)
                page = self.ring[(c % 2) * CHUNK_PAGES + r].astype(BF16)
                acc = acc + jnp.dot(jnp.concatenate(e, axis=1).astype(BF16), page, preferred_element_type=F32)
        return acc

    def refill_after_v(self, c):
        if c + 2 < self.n_chunks:
            self._start_chunk(self.vc_ref, self.seq, c + 2)
        else:
            @pl.when(self.seq + 1 < self.n_seqs)
            def _():
                self._start_chunk(self.kc_ref, self.seq + 1, c + 2 - self.n_chunks)

    def finish(self, l_own, acc):
        out = acc / (l_own + jnp.sum(self.lsum, axis=-1, keepdims=True))
        for hd in range(MOBA_HEADS):
            self.o_ref[:, hd * MOBA_HEAD_DIM:(hd + 1) * MOBA_HEAD_DIM] = out[hd * DEC_SEQ:(hd + 1) * DEC_SEQ]


def _layer_norm(x, g, b):
    mu = jnp.mean(x, axis=-1, keepdims=True)
    d = x - mu
    var = jnp.mean(d * d, axis=-1, keepdims=True)
    return d * lax.rsqrt(var + LN_EPS) * g + b


def _out_ffn_body(ar_ref, am_ref, x_ref, ga_ref, shf_ref, scf_ref, gf_ref, wo_ref, g1_ref, b1_ref,
                  wu_ref, wd_ref, g2_ref, b2_ref, y_ref, x1_scr, h_scr, acc_scr):
    c = pl.program_id(0)

    @pl.when(c == 0)
    def _():
        mixed = (jnp.dot(ar_ref[...].astype(BF16), wo_ref[:RET_WIDTH, :], preferred_element_type=F32)
                 + jnp.dot(am_ref[...].astype(BF16), wo_ref[RET_WIDTH:, :], preferred_element_type=F32))
        x1 = _layer_norm(ALPHA * x_ref[...] + _modulation(ga_ref, True) * mixed, g1_ref[...], b1_ref[...])
        x1_scr[...] = x1
        h_scr[...] = (x1 * (1.0 + _modulation(scf_ref, True)) + _modulation(shf_ref, True)).astype(BF16)
        acc_scr[...] = jnp.zeros_like(acc_scr)

    u = jnp.maximum(jnp.dot(h_scr[...], wu_ref[...], preferred_element_type=F32), 0.0)
    acc_scr[...] += jnp.dot((u * u).astype(BF16), wd_ref[...], preferred_element_type=F32)

    @pl.when(c == pl.num_programs(0) - 1)
    def _():
        y_ref[...] = _layer_norm(ALPHA * x1_scr[...] + _modulation(gf_ref, True) * acc_scr[...],
                                 g2_ref[...], b2_ref[...])


def _out_ffn(a_ret, a_moba, x2d, mod, weights):
    w_o, ln1_g, ln1_b, w_up, w_down, ln2_g, ln2_b = weights
    t = x2d.shape[0]
    whole = lambda a: pl.BlockSpec(a.shape, lambda c: (0,) * a.ndim)
    return pl.pallas_call(
        _out_ffn_body,
        grid=(D_FF // D_MODEL,),
        in_specs=[whole(a_ret), whole(a_moba), whole(x2d)] + [_mod_spec(term) for term in (2, 3, 4, 5)]
        + [whole(w_o), whole(ln1_g), whole(ln1_b),
           pl.BlockSpec((D_MODEL, D_MODEL), lambda c: (0, c)), pl.BlockSpec((D_MODEL, D_MODEL), lambda c: (c, 0)),
           whole(ln2_g), whole(ln2_b)],
        out_specs=pl.BlockSpec((t, D_MODEL), lambda c: (0, 0)),
        out_shape=jax.ShapeDtypeStruct((t, D_MODEL), F32),
        scratch_shapes=[pltpu.VMEM((t, D_MODEL), F32), pltpu.VMEM((t, D_MODEL), BF16), pltpu.VMEM((t, D_MODEL), F32)],
        compiler_params=pltpu.CompilerParams(dimension_semantics=("arbitrary",),
                                             vmem_limit_bytes=VMEM_LIMIT),
        name="out_ffn",
    )(a_ret, a_moba, x2d, *([mod] * 4), *weights)


def _out_ffn_moba_body(pt_ref, ar_ref, am_ref, x_ref, ga_ref, shf_ref, scf_ref, gf_ref, wo_ref, g1_ref, b1_ref,
                       wu_ref, wd_ref, g2_ref, b2_ref, q_ref, kn_ref, vn_ref, kc_ref, vc_ref,
                       y_ref, o_ref, ring, sem, s_scr):
    n_seqs = pl.num_programs(0) * pl.num_programs(1)
    seq = pl.program_id(0) * pl.num_programs(1) + pl.program_id(1)
    sm = _SampleMoba(pt_ref, seq, n_seqs, q_ref, kn_ref, vn_ref, kc_ref, vc_ref, o_ref, ring, sem, s_scr)

    def up(c, h):
        u = jnp.maximum(jnp.dot(h, wu_ref[:, c * D_MODEL:(c + 1) * D_MODEL], preferred_element_type=F32), 0.0)
        return (u * u).astype(BF16)

    def down(c, u):
        return jnp.dot(u, wd_ref[c * D_MODEL:(c + 1) * D_MODEL, :], preferred_element_type=F32)

    half = CHUNK_PAGES // 2
    sm.prologue()

    sm.wait_k(0)
    sm.k_pages(0, 0, half)
    mixed = jnp.dot(ar_ref[...].astype(BF16), wo_ref[:RET_WIDTH, :], preferred_element_type=F32)
    sm.k_pages(0, half, CHUNK_PAGES)
    mixed = mixed + jnp.dot(am_ref[...].astype(BF16), wo_ref[RET_WIDTH:, :], preferred_element_type=F32)
    x1 = _layer_norm(ALPHA * x_ref[...] + _modulation(ga_ref, False) * mixed, g1_ref[...], b1_ref[...])
    h = (x1 * (1.0 + _modulation(scf_ref, False)) + _modulation(shf_ref, False)).astype(BF16)
    sm.refill_after_k(0)

    sm.wait_k(1)
    sm.k_pages(1, 0, half)
    u = up(0, h)
    sm.k_pages(1, half, CHUNK_PAGES)
    sm.refill_after_k(1)

    sm.wait_k(2)
    sm.k_pages(2, 0, half)
    acc = down(0, u)
    sm.k_pages(2, half, CHUNK_PAGES)
    sm.refill_after_k(2)

    sm.wait_k(3)
    sm.k_pages(3, 0, half)
    u = up(1, h)
    sm.k_pages(3, half, CHUNK_PAGES)
    sm.refill_after_k(3)

    l, acc_s = sm.select()
    acc = acc + down(1, u)

    sm.wait_v(0)
    acc_s = sm.v_pages(0, 0, half, acc_s)
    u = up(2, h)
    acc_s = sm.v_pages(0, half, CHUNK_PAGES, acc_s)
    sm.refill_after_v(0)

    sm.wait_v(1)
    acc_s = sm.v_pages(1, 0, half, acc_s)
    acc = acc + down(2, u)
    acc_s = sm.v_pages(1, half, CHUNK_PAGES, acc_s)
    sm.refill_after_v(1)

    sm.wait_v(2)
    acc_s = sm.v_pages(2, 0, half, acc_s)
    u = up(3, h)
    acc_s = sm.v_pages(2, half, CHUNK_PAGES, acc_s)
    sm.refill_after_v(2)

    sm.wait_v(3)
    acc_s = sm.v_pages(3, 0, half, acc_s)
    acc = acc + down(3, u)
    acc_s = sm.v_pages(3, half, CHUNK_PAGES, acc_s)
    sm.refill_after_v(3)
    y_ref[...] = _layer_norm(ALPHA * x1 + _modulation(gf_ref, False) * acc, g2_ref[...], b2_ref[...])
    sm.finish(l, acc_s)


def _out_ffn_moba(a_ret, a_moba, x2d, mod, weights, page_table, mq_s, kn2d, vn2d, cache_k2d, cache_v2d, tm):
    t = x2d.shape[0]
    nt = SEQ // tm
    assert BATCH * nt == DEC_BATCH
    n_rows = MOBA_HEADS * DEC_SEQ
    tile = lambda b, i, pt: (b * nt + i, 0)
    wide = lambda w: pl.BlockSpec((tm, w), tile)
    const = lambda a: pl.BlockSpec(a.shape, lambda b, i, pt: (0,) * a.ndim, pipeline_mode=pl.Buffered(1))
    seq_rows = lambda w, n: pl.BlockSpec((n, w), tile)
    hbm = pl.BlockSpec(memory_space=pl.ANY)
    grid_spec = pltpu.PrefetchScalarGridSpec(
        num_scalar_prefetch=1,
        grid=(BATCH, nt),
        in_specs=[wide(RET_WIDTH), wide(MOBA_WIDTH), wide(D_MODEL)] + [_mod_spec(term) for term in (2, 3, 4, 5)]
        + [const(a) for a in weights]
        + [seq_rows(MOBA_WIDTH, DEC_SEQ), seq_rows(LANES, n_rows), seq_rows(LANES, n_rows), hbm, hbm],
        out_specs=[wide(D_MODEL), seq_rows(MOBA_WIDTH, DEC_SEQ)],
        scratch_shapes=[pltpu.VMEM((RING_PAGES, PAGE_ROWS, LANES), F32),
                        pltpu.SemaphoreType.DMA((RING_PAGES // CHUNK_PAGES,)),
                        pltpu.VMEM((PAGES_PER_SEQ, n_rows, PAGE_ROWS), F32)],
    )
    return pl.pallas_call(
        _out_ffn_moba_body,
        grid_spec=grid_spec,
        out_shape=[jax.ShapeDtypeStruct((t, D_MODEL), F32),
                   jax.ShapeDtypeStruct((N_SAMPLE_TOK, MOBA_WIDTH), F32)],
        compiler_params=pltpu.CompilerParams(dimension_semantics=("arbitrary", "arbitrary"),
                                             vmem_limit_bytes=VMEM_LIMIT_FUSED),
        name="out_ffn_moba",
    )(page_table.reshape(-1), a_ret, a_moba, x2d, *([mod] * 4), *weights,
      mq_s, kn2d, vn2d, cache_k2d, cache_v2d)


def kernel(x_prompt, x_sample, cache_k, cache_v, state_ret, page_table, c_prompt, c_sample,
           w_ada, b_ada, w_in, w_o, ln1_g, ln1_b, w_up, w_down, ln2_g, ln2_b):
    n_prompt_tok = BATCH * SEQ
    past_len = page_table.shape[1] * PAGE_SIZE

    mod = _adaln(c_prompt, c_sample, w_ada[0], b_ada)

    tm = 512
    nt = SEQ // tm
    p_row = lambda b, i: (b * nt + i, 0)
    p_tab = pl.BlockSpec((tm, LANES), lambda b, i: (i, 0))
    s_row = lambda i: (0, 0)
    s_tab = pl.BlockSpec((N_SAMPLE_TOK, LANES), s_row)

    pos_p = np.arange(SEQ, dtype=np.int32)
    pos_s = np.tile(past_len + np.arange(DEC_SEQ, dtype=np.int32), DEC_BATCH)
    tabs_p = _rope_tables(pos_p, MOBA_HEAD_DIM) + _rope_tables(pos_p, RET_DK)
    tabs_s = _rope_tables(pos_s, MOBA_HEAD_DIM) + _rope_tables(pos_s, RET_DK)

    xp = x_prompt.reshape(n_prompt_tok, D_MODEL)
    rq, rk, rv, rg, mq, k_p, v_p, w_in_b, w_o_b, w_up_b, w_down_b = _inproj(
        xp, mod, False, [p_tab] * 4, tabs_p, w_in[0],
        (BATCH, nt), p_row, tm, BF16, casts=(w_in[0], w_o[0], w_up[0], w_down[0]))
    weights = (w_o_b, ln1_g, ln1_b, w_up_b, w_down_b, ln2_g, ln2_b)
    a_ret, state_p = _ret_prompt(rq, rk, rv, rg)
    a_moba = _moba_prompt(mq, k_p, v_p)

    xs = x_sample.reshape(N_SAMPLE_TOK, D_MODEL)
    rq_s, rk_s, rv_s, rg_s, mq_s, k_s, v_s = _inproj(
        xs, mod, True, [s_tab] * 4, tabs_s, w_in_b,
        (1,), s_row, N_SAMPLE_TOK, F32)
    a_ret_s, state_s = _ret_sample(rq_s, rk_s, rv_s, rg_s, state_ret[0])

    cache_rows = cache_k.shape[1] * PAGE_ROWS
    y_p, a_moba_s = _out_ffn_moba(a_ret, a_moba, xp, mod, weights, page_table, mq_s, k_s, v_s,
                                  cache_k.reshape(cache_rows, LANES), cache_v.reshape(cache_rows, LANES), tm)
    y_s = _out_ffn(a_ret_s, a_moba_s, xs, mod, weights)

    kv_p_shape = (DEPTH, BATCH, SEQ, MOBA_HEADS, MOBA_HEAD_DIM)
    kv_s_shape = (DEPTH, DEC_BATCH, DEC_SEQ, MOBA_HEADS, MOBA_HEAD_DIM)
    return (y_p.reshape(BATCH, SEQ, D_MODEL),
            y_s.reshape(DEC_BATCH, DEC_SEQ, D_MODEL),
            k_p.reshape(kv_p_shape), v_p.reshape(kv_p_shape), state_p[None],
            k_s.reshape(kv_s_shape), v_s.reshape(kv_s_shape), state_s[None])
```

```python
import functools

import numpy as np
import jax
import jax.numpy as jnp
from jax import lax
from jax.experimental import pallas as pl
from jax.experimental.pallas import tpu as pltpu

F32 = jnp.float32
BF16 = jnp.bfloat16

D_MODEL = 1024
BATCH = 8
SEQ = 2048
DEC_BATCH = 32
DEC_SEQ = 8
PAGE_SIZE = 128
RET_HEADS = 4
RET_DK = 64
RET_DV = 128
RET_CHUNK = 128
MOBA_HEADS = 4
MOBA_HEAD_DIM = 128
MOBA_BLOCK = 256
MOBA_TOPK = 3
D_FF = 4 * D_MODEL
ROPE_THETA = 10000.0
LN_EPS = 1e-5
GN_EPS = 1e-6
DEPTH = 1
ALPHA = (2 * DEPTH) ** 0.25
RET_QK = RET_HEADS * RET_DK
RET_WIDTH = RET_HEADS * RET_DV
MOBA_WIDTH = MOBA_HEADS * MOBA_HEAD_DIM
IN_WIDTH = 2 * RET_QK + 2 * RET_WIDTH + 3 * MOBA_WIDTH
OFF_RQ, OFF_RK, OFF_RV, OFF_RG = 0, 256, 512, 1024
OFF_MQ, OFF_MK, OFF_MV = 1536, 2048, 2560
LANES = 128
VMEM_BYTES_V7X = 64 * 1024 * 1024
VMEM_LIMIT = VMEM_BYTES_V7X * 3 // 4
VMEM_LIMIT_FUSED = VMEM_BYTES_V7X * 29 // 32
N_SAMPLE_TOK = DEC_BATCH * DEC_SEQ
PAGES_PER_SEQ = 64
CHUNK_PAGES = 16
RING_PAGES = 2 * CHUNK_PAGES
PAGE_ROWS = PAGE_SIZE * MOBA_HEADS
LOG2_RET_DK = RET_DK.bit_length() - 1
LOG2_DEC_SEQ = DEC_SEQ.bit_length() - 1
LOG2_MOBA_HEADS = MOBA_HEADS.bit_length() - 1
NEG_INF = float("-inf")
LOG2_E = 1.4426950408889634

_NT = (((1,), (1,)), ((), ()))
_TN = (((0,), (0,)), ((), ()))


def _log_decay():
    return np.log1p(-np.exp2(-5.0 - np.arange(RET_HEADS, dtype=np.float64)))


N_MOD = 6
MOD_ROWS = BATCH + DEC_BATCH


def _adaln_body(cp_ref, cs_ref, w_ref, b_ref, o_ref):
    c = jnp.concatenate([cp_ref[...], cs_ref[...]], axis=0)
    a = (c * jax.nn.sigmoid(c)).astype(BF16)
    o_ref[...] = jnp.dot(a, w_ref[...].astype(BF16), preferred_element_type=F32) + b_ref[...]


def _adaln(c_prompt, c_sample, w_ada, b_ada):
    return pl.pallas_call(
        _adaln_body,
        grid=(N_MOD,),
        in_specs=[pl.BlockSpec(c_prompt.shape, lambda j: (0, 0)),
                  pl.BlockSpec(c_sample.shape, lambda j: (0, 0)),
                  pl.BlockSpec((D_MODEL, D_MODEL), lambda j: (0, j)),
                  pl.BlockSpec((1, D_MODEL), lambda j: (0, j))],
        out_specs=pl.BlockSpec((None, MOD_ROWS, D_MODEL), lambda j: (j, 0, 0)),
        out_shape=jax.ShapeDtypeStruct((N_MOD, MOD_ROWS, D_MODEL), F32),
        name="adaln",
    )(c_prompt, c_sample, w_ada, b_ada)


def _modulation(ref, sample_group):
    if sample_group:
        rows = ref[BATCH:BATCH + DEC_BATCH, :]
        return jnp.broadcast_to(rows[:, None, :], (DEC_BATCH, DEC_SEQ, D_MODEL)).reshape(N_SAMPLE_TOK, D_MODEL)
    return ref[pl.ds(pl.program_id(0), 1), :]


def _mod_spec(term):
    return pl.BlockSpec((None, MOD_ROWS, D_MODEL), lambda *_: (term, 0, 0))


def _rope_tables(pos, head_dim):
    half = head_dim // 2
    inv_freq = np.power(ROPE_THETA, -np.arange(half, dtype=np.float64) / half)
    ang = pos.astype(np.float64)[:, None] * inv_freq[None, :]
    cos, sin = np.cos(ang), np.sin(ang)
    reps = LANES // head_dim
    cos_t = np.tile(np.concatenate([cos, cos], axis=-1), (1, reps))
    sin_t = np.tile(np.concatenate([-sin, sin], axis=-1), (1, reps))
    return jnp.asarray(cos_t, dtype=F32), jnp.asarray(sin_t, dtype=F32)


def _inproj_body(sample_group, n_casts, x_ref, sc_ref, sh_ref, w_ref, cm_ref, sm_ref, cr_ref, sr_ref, *refs):
    cast_in, refs = refs[:n_casts], refs[n_casts:]
    rq_ref, rk_ref, rv_ref, rg_ref, mq_ref, ko_ref, vo_ref = refs[:7]
    for src, dst in zip(cast_in, refs[7:]):
        dst[...] = src[...].astype(dst.dtype)
    tm = x_ref.shape[0]
    h = (x_ref[...] * (1.0 + _modulation(sc_ref, sample_group)) + _modulation(sh_ref, sample_group)).astype(BF16)

    def proj(lo, width):
        return jnp.dot(h, w_ref[:, lo:lo + width].astype(BF16), preferred_element_type=F32)

    lane = lax.broadcasted_iota(jnp.int32, (tm, LANES), 1)
    low_half = (lane & (RET_DK - 1)) < (RET_DK // 2)
    cr, sr = cr_ref[...], sr_ref[...]
    cm, sm = cm_ref[...], sm_ref[...]

    def rope_ret(z):
        rot = jnp.where(low_half, pltpu.roll(z, LANES - RET_DK // 2, 1), pltpu.roll(z, RET_DK // 2, 1))
        return z * cr + rot * sr

    def rope_moba(z):
        return z * cm + pltpu.roll(z, MOBA_HEAD_DIM // 2, 1) * sm

    zq = proj(OFF_RQ, RET_QK)
    zk = proj(OFF_RK, RET_QK)
    for s in range(RET_QK // LANES):
        sl = slice(s * LANES, (s + 1) * LANES)
        rq_ref[:, sl] = rope_ret(zq[:, sl])
        rk_ref[:, sl] = rope_ret(zk[:, sl]) * (RET_DK ** -0.5)
    rv_ref[...] = proj(OFF_RV, RET_WIDTH).astype(rv_ref.dtype)
    rg_ref[...] = proj(OFF_RG, RET_WIDTH)
    zq = proj(OFF_MQ, MOBA_WIDTH)
    zk = proj(OFF_MK, MOBA_WIDTH)
    zv = proj(OFF_MV, MOBA_WIDTH)
    for hd in range(MOBA_HEADS):
        sl = slice(hd * LANES, (hd + 1) * LANES)
        mq_ref[:, sl] = rope_moba(zq[:, sl]).astype(mq_ref.dtype)
        ko_ref[pl.ds(hd, tm, stride=MOBA_HEADS), :] = rope_moba(zk[:, sl])
        vo_ref[pl.ds(hd, tm, stride=MOBA_HEADS), :] = zv[:, sl]


def _inproj(x2d, mod, sample_group, tab_specs, tabs, w_in, grid, row_map, tm, act_dtype, casts=()):
    t = x2d.shape[0]
    n_steps = int(np.prod(grid))
    wide = lambda w: pl.BlockSpec((tm, w), row_map)
    cast_specs = [pl.BlockSpec((a.shape[0] // n_steps, a.shape[1]), row_map) for a in casts]
    w_spec = pl.BlockSpec((D_MODEL, IN_WIDTH), lambda *_: (0, 0), pipeline_mode=pl.Buffered(1))
    return pl.pallas_call(
        functools.partial(_inproj_body, sample_group, len(casts)),
        grid=grid,
        in_specs=[wide(D_MODEL), _mod_spec(1), _mod_spec(0), w_spec] + tab_specs + cast_specs,
        out_specs=[wide(RET_QK), wide(RET_QK), wide(RET_WIDTH), wide(RET_WIDTH), wide(MOBA_WIDTH),
                   pl.BlockSpec((tm * MOBA_HEADS, LANES), row_map),
                   pl.BlockSpec((tm * MOBA_HEADS, LANES), row_map)] + cast_specs,
        out_shape=[jax.ShapeDtypeStruct((t, RET_QK), F32), jax.ShapeDtypeStruct((t, RET_QK), F32),
                   jax.ShapeDtypeStruct((t, RET_WIDTH), act_dtype), jax.ShapeDtypeStruct((t, RET_WIDTH), F32),
                   jax.ShapeDtypeStruct((t, MOBA_WIDTH), act_dtype),
                   jax.ShapeDtypeStruct((t * MOBA_HEADS, LANES), F32),
                   jax.ShapeDtypeStruct((t * MOBA_HEADS, LANES), F32)]
        + [jax.ShapeDtypeStruct(a.shape, BF16) for a in casts],
        compiler_params=pltpu.CompilerParams(dimension_semantics=("arbitrary",) * len(grid),
                                             vmem_limit_bytes=VMEM_LIMIT),
        name="inproj",
    )(x2d, mod, mod, w_in, *tabs, *casts)


def _group_norm_gate(o, g):
    mu = jnp.mean(o, axis=-1, keepdims=True)
    d = o - mu
    var = jnp.mean(d * d, axis=-1, keepdims=True)
    return d * lax.rsqrt(var + GN_EPS) * (g * jax.nn.sigmoid(g))


def _ret_prompt_tables():
    lg = _log_decay()
    i = np.arange(RET_CHUNK, dtype=np.float64)
    diff = i[:, None] - i[None, :]
    dmat = np.where(diff >= 0, np.exp(np.maximum(diff, 0.0)[None] * lg[:, None, None]), 0.0)
    lane_head = np.arange(RET_QK) // RET_DK
    qd = np.exp((i[:, None] + 1.0) * lg[lane_head][None, :])
    kd = np.exp((RET_CHUNK - 1.0 - i)[:, None] * lg[lane_head][None, :])
    return [jnp.asarray(a, dtype=F32) for a in (dmat, np.concatenate([qd, kd], axis=1))]


def _ret_prompt_body(q_ref, k_ref, v_ref, g_ref, dmat_ref, qkd_ref, o_ref, st_ref):
    lane_head = lax.broadcasted_iota(jnp.int32, (RET_CHUNK, RET_QK), 1) >> LOG2_RET_DK
    chunk_rows = lambda c: slice(c * RET_CHUNK, (c + 1) * RET_CHUNK)
    chunk_decay = [float(np.exp(RET_CHUNK * lg)) for lg in _log_decay()]
    head_block = lambda h: (slice(h * RET_DK, (h + 1) * RET_DK), slice(h * RET_DV, (h + 1) * RET_DV))

    def block_diagonal(state):
        zero = jnp.zeros((RET_DK, RET_DV), BF16)
        return jnp.concatenate(
            [jnp.concatenate([state[h].astype(BF16) if c == h else zero for c in range(RET_HEADS)], axis=1)
             for h in range(RET_HEADS)], axis=0)

    def first_matmuls(c, state):
        rows = chunk_rows(c)
        q = q_ref[rows, :]
        k = k_ref[rows, :]
        v = v_ref[rows, :]
        kb = k.astype(BF16)
        scores = [lax.dot_general(jnp.where(lane_head == hd, q, 0.0).astype(BF16), kb, _NT,
                                  preferred_element_type=F32) for hd in range(RET_HEADS)]
        cross = jnp.dot((q * qkd_ref[:, :RET_QK]).astype(BF16), block_diagonal(state),
                        preferred_element_type=F32)
        kv = lax.dot_general((k * qkd_ref[:, RET_QK:]).astype(BF16), v, _TN, preferred_element_type=F32)
        new_state = [chunk_decay[h] * state[h] + kv[head_block(h)] for h in range(RET_HEADS)]
        return (scores, cross, v), new_state

    def second_matmuls(c, scores, cross, v):
        rows = chunk_rows(c)
        g = g_ref[rows, :]
        decayed = [(scores[hd] * dmat_ref[hd]).astype(BF16) for hd in range(RET_HEADS)]
        for hd in range(RET_HEADS):
            sl = slice(hd * RET_DV, (hd + 1) * RET_DV)
            inner = jnp.dot(decayed[hd], v[:, sl], preferred_element_type=F32)
            o_ref[rows, sl] = _group_norm_gate(inner + cross[:, sl], g[:, sl]).astype(o_ref.dtype)

    n_chunks = SEQ // RET_CHUNK
    ahead, state = first_matmuls(0, [jnp.zeros((RET_DK, RET_DV), F32)] * RET_HEADS)
    for c in range(n_chunks):
        current = ahead
        if c + 1 < n_chunks:
            ahead, state = first_matmuls(c + 1, state)
        second_matmuls(c, *current)
    for hd in range(RET_HEADS):
        st_ref[hd] = state[hd]


def _ret_prompt(rq, rk, rv, rg):
    tabs = _ret_prompt_tables()
    seq = lambda w: pl.BlockSpec((SEQ, w), lambda b: (b, 0))
    const = lambda a: pl.BlockSpec(a.shape, lambda b: (0,) * a.ndim)
    return pl.pallas_call(
        _ret_prompt_body,
        grid=(BATCH,),
        in_specs=[seq(RET_QK), seq(RET_QK), seq(RET_WIDTH), seq(RET_WIDTH)] + [const(a) for a in tabs],
        out_specs=[seq(RET_WIDTH), pl.BlockSpec((None, RET_HEADS, RET_DK, RET_DV), lambda b: (b, 0, 0, 0))],
        out_shape=[jax.ShapeDtypeStruct((BATCH * SEQ, RET_WIDTH), BF16),
                   jax.ShapeDtypeStruct((BATCH, RET_HEADS, RET_DK, RET_DV), F32)],
        compiler_params=pltpu.CompilerParams(dimension_semantics=("arbitrary",),
                                             vmem_limit_bytes=VMEM_LIMIT),
        name="ret_prompt",
    )(rq, rk, rv, rg, *tabs)


def _ret_sample_tables():
    lg = _log_decay()
    t = np.arange(N_SAMPLE_TOK) % DEC_SEQ
    seq_id = np.arange(N_SAMPLE_TOK) // DEC_SEQ
    diff = (t[:, None] - t[None, :]).astype(np.float64)
    same_seq = seq_id[:, None] == seq_id[None, :]
    dmat = np.where(same_seq[None] & (diff >= 0)[None],
                    np.exp(np.maximum(diff, 0.0)[None] * lg[:, None, None]), 0.0)
    lane_head = np.arange(RET_QK) // RET_DK
    qd = np.exp((t[:, None] + 1.0) * lg[lane_head][None, :])
    kd = np.exp((DEC_SEQ - 1.0 - t)[:, None] * lg[lane_head][None, :])
    return [jnp.asarray(a, dtype=F32) for a in (dmat, np.concatenate([qd, kd], axis=1))]


def _ret_sample_body(q_ref, k_ref, v_ref, g_ref, st_ref, dmat_ref, qkd_ref, o_ref, sto_ref):
    lg = _log_decay()
    q = q_ref[...]
    k = k_ref[...]
    kb = k.astype(BF16)
    qdec = q * qkd_ref[:, :RET_QK]
    kdec = k * qkd_ref[:, RET_QK:]
    vb = v_ref[...].astype(BF16)
    g = g_ref[...]
    lane = lax.broadcasted_iota(jnp.int32, (N_SAMPLE_TOK, LANES), 1)
    lane_head = lax.broadcasted_iota(jnp.int32, (N_SAMPLE_TOK, RET_QK), 1) >> LOG2_RET_DK
    n_state_rows = DEC_BATCH * RET_DK
    own_seq = ((lax.broadcasted_iota(jnp.int32, (N_SAMPLE_TOK, n_state_rows), 0) >> LOG2_DEC_SEQ)
               == (lax.broadcasted_iota(jnp.int32, (N_SAMPLE_TOK, n_state_rows), 1) >> LOG2_RET_DK))
    for hd in range(RET_HEADS):
        sl = slice(hd * RET_DV, (hd + 1) * RET_DV)
        qm = jnp.where(lane_head == hd, q, 0.0).astype(BF16)
        s = lax.dot_general(qm, kb, _NT, preferred_element_type=F32) * dmat_ref[hd]
        inner = jnp.dot(s.astype(BF16), vb[:, sl], preferred_element_type=F32)

        def expand(z):
            slab = z[:, (hd // 2) * LANES:(hd // 2 + 1) * LANES]
            other = pltpu.roll(slab, RET_DK, 1)
            in_low = lane < RET_DK
            both = jnp.where(in_low, slab, other) if hd % 2 == 0 else jnp.where(in_low, other, slab)
            tiled = jnp.concatenate([both] * (n_state_rows // LANES), axis=1)
            return jnp.where(own_seq, tiled, 0.0).astype(BF16)

        st = st_ref[:, hd].reshape(n_state_rows, RET_DV)
        cross = jnp.dot(expand(qdec), st.astype(BF16), preferred_element_type=F32)
        o_ref[:, sl] = _group_norm_gate(inner + cross, g[:, sl])
        kv = lax.dot_general(expand(kdec), vb[:, sl], _TN, preferred_element_type=F32)
        new = float(np.exp(DEC_SEQ * lg[hd])) * st + kv
        sto_ref[:, hd] = new.reshape(DEC_BATCH, RET_DK, RET_DV)


def _ret_sample(rq, rk, rv, rg, state):
    tabs = _ret_sample_tables()
    full = lambda a: pl.BlockSpec(a.shape, lambda i: (0,) * a.ndim)
    args = (rq, rk, rv, rg, state, *tabs)
    return pl.pallas_call(
        _ret_sample_body,
        grid=(1,),
        in_specs=[full(a) for a in args],
        out_specs=[pl.BlockSpec((N_SAMPLE_TOK, RET_WIDTH), lambda i: (0, 0)),
                   pl.BlockSpec(state.shape, lambda i: (0, 0, 0, 0))],
        out_shape=[jax.ShapeDtypeStruct((N_SAMPLE_TOK, RET_WIDTH), F32),
                   jax.ShapeDtypeStruct(state.shape, F32)],
        compiler_params=pltpu.CompilerParams(dimension_semantics=("arbitrary",),
                                             vmem_limit_bytes=VMEM_LIMIT),
        name="ret_sample",
    )(*args)


def _moba_prompt_body(q_ref, k_ref, v_ref, o_ref):
    hd = pl.program_id(1)
    n_blocks = SEQ // MOBA_BLOCK
    exp2_scale = MOBA_HEAD_DIM ** -0.5 * LOG2_E
    k32 = k_ref[pl.ds(hd, SEQ, stride=MOBA_HEADS), :]
    kb = (k32 * exp2_scale).astype(BF16)
    vt = v_ref[pl.ds(hd, SEQ, stride=MOBA_HEADS), :].T.astype(BF16)
    kmean = jnp.sum(k32.reshape(n_blocks, MOBA_BLOCK, MOBA_HEAD_DIM), axis=1) * (1.0 / MOBA_BLOCK)
    kmb = kmean.astype(BF16)
    key_id = lax.broadcasted_iota(jnp.int32, (MOBA_BLOCK, MOBA_BLOCK), 0)
    qry_id = lax.broadcasted_iota(jnp.int32, (MOBA_BLOCK, MOBA_BLOCK), 1)
    causal = key_id <= qry_id

    blk = lambda n: slice(n * MOBA_BLOCK, (n + 1) * MOBA_BLOCK)

    def score_matmuls(i):
        qi = q_ref[blk(i), :]
        st = [lax.dot_general(kb[blk(n)], qi, _NT, preferred_element_type=F32) for n in range(i + 1)]
        st[i] = jnp.where(causal, st[i], NEG_INF)
        top = [jnp.max(s, axis=0, keepdims=True) for s in st]
        gt = lax.dot_general(kmb, qi, _NT, preferred_element_type=F32) if i > MOBA_TOPK else None
        return st, top, gt

    ahead = score_matmuls(n_blocks - 1)
    for i in reversed(range(n_blocks)):
        st, top, gt = ahead
        if i > 0:
            ahead = score_matmuls(i - 1)
        drop = [0.0] * (i + 1)
        if i > MOBA_TOPK:
            for n in range(i):
                beats = jnp.zeros((1, MOBA_BLOCK), F32)
                for mm in range(i):
                    if mm == n:
                        continue
                    win = (gt[mm:mm + 1] >= gt[n:n + 1]) if mm < n else (gt[mm:mm + 1] > gt[n:n + 1])
                    beats = beats + win.astype(F32)
                drop[n] = jnp.where(beats < MOBA_TOPK, 0.0, NEG_INF)
        m = functools.reduce(jnp.maximum, [top[n] + drop[n] for n in range(i + 1)])
        l = jnp.zeros((1, MOBA_BLOCK), F32)
        acc = jnp.zeros((MOBA_HEAD_DIM, MOBA_BLOCK), F32)
        for n in range(i + 1):
            e = jnp.exp2(st[n] - (m - drop[n]))
            l = l + jnp.sum(e, axis=0, keepdims=True)
            acc = acc + jnp.dot(vt[:, blk(n)], e.astype(BF16), preferred_element_type=F32)
        o_ref[blk(i), :] = (acc / l).T.astype(o_ref.dtype)


def _moba_prompt(mq, k2d, v2d):
    kv_spec = pl.BlockSpec((SEQ * MOBA_HEADS, LANES), lambda b, h: (b, 0))
    return pl.pallas_call(
        _moba_prompt_body,
        grid=(BATCH, MOBA_HEADS),
        in_specs=[pl.BlockSpec((SEQ, MOBA_HEAD_DIM), lambda b, h: (b, h)), kv_spec, kv_spec],
        out_specs=pl.BlockSpec((SEQ, MOBA_HEAD_DIM), lambda b, h: (b, h)),
        out_shape=jax.ShapeDtypeStruct((BATCH * SEQ, MOBA_WIDTH), BF16),
        compiler_params=pltpu.CompilerParams(dimension_semantics=("arbitrary", "arbitrary"),
                                             vmem_limit_bytes=VMEM_LIMIT),
        name="moba_prompt",
    )(mq, k2d, v2d)


class _SampleMoba:
    n_chunks = PAGES_PER_SEQ // CHUNK_PAGES
    n_rows = MOBA_HEADS * DEC_SEQ
    n_blocks = PAGES_PER_SEQ * PAGE_SIZE // MOBA_BLOCK
    pages_per_block = MOBA_BLOCK // PAGE_SIZE

    def __init__(self, pt_ref, seq, n_seqs, q_ref, kn_ref, vn_ref, kc_ref, vc_ref, o_ref, ring, sem, s_scr):
        self.pt_ref, self.seq, self.n_seqs = pt_ref, seq, n_seqs
        self.q_ref, self.kn_ref, self.vn_ref = q_ref, kn_ref, vn_ref
        self.kc_ref, self.vc_ref, self.o_ref = kc_ref, vc_ref, o_ref
        self.ring, self.sem, self.s_scr = ring, sem, s_scr

    def _page_copy(self, cache_ref, row0, slot):
        return pltpu.make_async_copy(cache_ref.at[pl.ds(row0, PAGE_ROWS)], self.ring.at[slot],
                                     self.sem.at[slot // CHUNK_PAGES])

    def _start_chunk(self, cache_ref, seq, chunk):
        for r in range(CHUNK_PAGES):
            page = self.pt_ref[seq * PAGES_PER_SEQ + chunk * CHUNK_PAGES + r]
            self._page_copy(cache_ref, pl.multiple_of(page * PAGE_ROWS, PAGE_ROWS),
                            (chunk % 2) * CHUNK_PAGES + r).start()

    def _wait_chunk(self, cache_ref, chunk):
        for r in range(CHUNK_PAGES):
            self._page_copy(cache_ref, 0, (chunk % 2) * CHUNK_PAGES + r).wait()

    def prologue(self):
        @pl.when(self.seq == 0)
        def _():
            self._start_chunk(self.kc_ref, self.seq, 0)
            self._start_chunk(self.kc_ref, self.seq, 1)

        self.q = jnp.concatenate(
            [self.q_ref[:, hd * MOBA_HEAD_DIM:(hd + 1) * MOBA_HEAD_DIM] for hd in range(MOBA_HEADS)], axis=0
        ).astype(BF16)
        row_head = lax.broadcasted_iota(jnp.int32, (self.n_rows, LANES), 0) >> LOG2_DEC_SEQ
        col_head = lax.broadcasted_iota(jnp.int32, (self.n_rows, LANES), 1) & (MOBA_HEADS - 1)
        self.same_head = row_head == col_head
        self.head_bias = jnp.where(self.same_head, 0.0, NEG_INF)
        self.block_sum, self.block_max = [], []

    @staticmethod
    def _slabs(x):
        return [x[:, j * LANES:(j + 1) * LANES] for j in range(PAGE_ROWS // LANES)]

    def wait_k(self, c):
        self._wait_chunk(self.kc_ref, c)

    def k_pages(self, c, first, last):
        assert first % self.pages_per_block == 0 and last % self.pages_per_block == 0
        for r0 in range(first, last, self.pages_per_block):
            tot = jnp.zeros((self.n_rows, LANES), F32)
            top = jnp.full((self.n_rows, LANES), NEG_INF, F32)
            for r in range(r0, r0 + self.pages_per_block):
                page = self.ring[(c % 2) * CHUNK_PAGES + r].astype(BF16)
                s = lax.dot_general(self.q, page, _NT, preferred_element_type=F32)
                self.s_scr[c * CHUNK_PAGES + r] = s
                for slab in self._slabs(s):
                    tot = tot + jnp.where(self.same_head, slab, 0.0)
                    top = jnp.maximum(top, slab + self.head_bias)
            self.block_sum.append(jnp.sum(tot, axis=-1, keepdims=True))
            self.block_max.append(jnp.max(top, axis=-1, keepdims=True))

    def refill_after_k(self, c):
        if c + 2 < self.n_chunks:
            self._start_chunk(self.kc_ref, self.seq, c + 2)
        else:
            self._start_chunk(self.vc_ref, self.seq, c + 2 - self.n_chunks)

    def select(self):
        n_rows, gs = self.n_rows, self.block_sum
        self.exp2_scale = MOBA_HEAD_DIM ** -0.5 * LOG2_E
        lane = lax.broadcasted_iota(jnp.int32, (n_rows, LANES), 1)
        g_all = jnp.full((n_rows, LANES), NEG_INF, F32)
        for n in range(self.n_blocks):
            g_all = jnp.where(lane == n, gs[n], g_all)
        self.keep_bias = []
        for n in range(self.n_blocks):
            wins = (g_all > gs[n]) | ((g_all == gs[n]) & (lane < n))
            beats = jnp.sum(wins.astype(F32), axis=-1, keepdims=True)
            self.keep_bias.append(jnp.where(beats < MOBA_TOPK, 0.0, NEG_INF))
        s_own = lax.dot_general(self.q, self.kn_ref[...].astype(BF16), _NT, preferred_element_type=F32)
        r_id = lax.broadcasted_iota(jnp.int32, (n_rows, n_rows), 0)
        c_id = lax.broadcasted_iota(jnp.int32, (n_rows, n_rows), 1)
        own_ok = ((c_id & (MOBA_HEADS - 1)) == (r_id >> LOG2_DEC_SEQ)) & ((c_id >> LOG2_MOBA_HEADS) <= (r_id & (DEC_SEQ - 1)))
        s_own = jnp.where(own_ok, s_own, NEG_INF)
        m = jnp.max(s_own, axis=-1, keepdims=True)
        for n in range(self.n_blocks):
            m = jnp.maximum(m, self.block_max[n] + self.keep_bias[n])
        self.m = m
        self.lsum = jnp.zeros((n_rows, LANES), F32)
        e_own = jnp.exp2((s_own - m) * self.exp2_scale)
        return (jnp.sum(e_own, axis=-1, keepdims=True),
                jnp.dot(e_own.astype(BF16), self.vn_ref[...].astype(BF16), preferred_element_type=F32))

    def wait_v(self, c):
        self._wait_chunk(self.vc_ref, c)

    def v_pages(self, c, first, last, acc):
        assert first % self.pages_per_block == 0 and last % self.pages_per_block == 0
        for r0 in range(first, last, self.pages_per_block):
            shift = self.head_bias + (self.keep_bias[(c * CHUNK_PAGES + r0) // self.pages_per_block] - self.m)
            for r in range(r0, r0 + self.pages_per_block):
                e = [jnp.exp2((slab + shift) * self.exp2_scale) for slab in self._slabs(self.s_scr[c * CHUNK_PAGES + r])]
                self.lsum = self.lsum + functools.reduce(jnp.add, e)
                page = self.ring[(c % 2) * CHUNK_PAGES + r].astype(BF16)
                acc = acc + jnp.dot(jnp.concatenate(e, axis=1).astype(BF16), page, preferred_element_type=F32)
        return acc

    def refill_after_v(self, c):
        if c + 2 < self.n_chunks:
            self._start_chunk(self.vc_ref, self.seq, c + 2)
        else:
            @pl.when(self.seq + 1 < self.n_seqs)
            def _():
                self._start_chunk(self.kc_ref, self.seq + 1, c + 2 - self.n_chunks)

    def finish(self, l_own, acc):
        out = acc / (l_own + jnp.sum(self.lsum, axis=-1, keepdims=True))
        for hd in range(MOBA_HEADS):
            self.o_ref[:, hd * MOBA_HEAD_DIM:(hd + 1) * MOBA_HEAD_DIM] = out[hd * DEC_SEQ:(hd + 1) * DEC_SEQ]


def _layer_norm(x, g, b):
    mu = jnp.mean(x, axis=-1, keepdims=True)
    d = x - mu
    var = jnp.mean(d * d, axis=-1, keepdims=True)
    return d * lax.rsqrt(var + LN_EPS) * g + b


def _out_ffn_body(ar_ref, am_ref, x_ref, ga_ref, shf_ref, scf_ref, gf_ref, wo_ref, g1_ref, b1_ref,
                  wu_ref, wd_ref, g2_ref, b2_ref, y_ref, x1_scr, h_scr, acc_scr):
    c = pl.program_id(0)

    @pl.when(c == 0)
    def _():
        mixed = (jnp.dot(ar_ref[...].astype(BF16), wo_ref[:RET_WIDTH, :], preferred_element_type=F32)
                 + jnp.dot(am_ref[...].astype(BF16), wo_ref[RET_WIDTH:, :], preferred_element_type=F32))
        x1 = _layer_norm(ALPHA * x_ref[...] + _modulation(ga_ref, True) * mixed, g1_ref[...], b1_ref[...])
        x1_scr[...] = x1
        h_scr[...] = (x1 * (1.0 + _modulation(scf_ref, True)) + _modulation(shf_ref, True)).astype(BF16)
        acc_scr[...] = jnp.zeros_like(acc_scr)

    u = jnp.maximum(jnp.dot(h_scr[...], wu_ref[...], preferred_element_type=F32), 0.0)
    acc_scr[...] += jnp.dot((u * u).astype(BF16), wd_ref[...], preferred_element_type=F32)

    @pl.when(c == pl.num_programs(0) - 1)
    def _():
        y_ref[...] = _layer_norm(ALPHA * x1_scr[...] + _modulation(gf_ref, True) * acc_scr[...],
                                 g2_ref[...], b2_ref[...])


def _out_ffn(a_ret, a_moba, x2d, mod, weights):
    w_o, ln1_g, ln1_b, w_up, w_down, ln2_g, ln2_b = weights
    t = x2d.shape[0]
    whole = lambda a: pl.BlockSpec(a.shape, lambda c: (0,) * a.ndim)
    return pl.pallas_call(
        _out_ffn_body,
        grid=(D_FF // D_MODEL,),
        in_specs=[whole(a_ret), whole(a_moba), whole(x2d)] + [_mod_spec(term) for term in (2, 3, 4, 5)]
        + [whole(w_o), whole(ln1_g), whole(ln1_b),
           pl.BlockSpec((D_MODEL, D_MODEL), lambda c: (0, c)), pl.BlockSpec((D_MODEL, D_MODEL), lambda c: (c, 0)),
           whole(ln2_g), whole(ln2_b)],
        out_specs=pl.BlockSpec((t, D_MODEL), lambda c: (0, 0)),
        out_shape=jax.ShapeDtypeStruct((t, D_MODEL), F32),
        scratch_shapes=[pltpu.VMEM((t, D_MODEL), F32), pltpu.VMEM((t, D_MODEL), BF16), pltpu.VMEM((t, D_MODEL), F32)],
        compiler_params=pltpu.CompilerParams(dimension_semantics=("arbitrary",),
                                             vmem_limit_bytes=VMEM_LIMIT),
        name="out_ffn",
    )(a_ret, a_moba, x2d, *([mod] * 4), *weights)


def _out_ffn_moba_body(pt_ref, ar_ref, am_ref, x_ref, ga_ref, shf_ref, scf_ref, gf_ref, wo_ref, g1_ref, b1_ref,
                       wu_ref, wd_ref, g2_ref, b2_ref, q_ref, kn_ref, vn_ref, kc_ref, vc_ref,
                       y_ref, o_ref, ring, sem, s_scr):
    n_seqs = pl.num_programs(0) * pl.num_programs(1)
    seq = pl.program_id(0) * pl.num_programs(1) + pl.program_id(1)
    sm = _SampleMoba(pt_ref, seq, n_seqs, q_ref, kn_ref, vn_ref, kc_ref, vc_ref, o_ref, ring, sem, s_scr)

    def up(c, h):
        u = jnp.maximum(jnp.dot(h, wu_ref[:, c * D_MODEL:(c + 1) * D_MODEL], preferred_element_type=F32), 0.0)
        return (u * u).astype(BF16)

    def down(c, u):
        return jnp.dot(u, wd_ref[c * D_MODEL:(c + 1) * D_MODEL, :], preferred_element_type=F32)

    half = CHUNK_PAGES // 2
    sm.prologue()

    sm.wait_k(0)
    sm.k_pages(0, 0, half)
    mixed = jnp.dot(ar_ref[...].astype(BF16), wo_ref[:RET_WIDTH, :], preferred_element_type=F32)
    sm.k_pages(0, half, CHUNK_PAGES)
    mixed = mixed + jnp.dot(am_ref[...].astype(BF16), wo_ref[RET_WIDTH:, :], preferred_element_type=F32)
    x1 = _layer_norm(ALPHA * x_ref[...] + _modulation(ga_ref, False) * mixed, g1_ref[...], b1_ref[...])
    h = (x1 * (1.0 + _modulation(scf_ref, False)) + _modulation(shf_ref, False)).astype(BF16)
    sm.refill_after_k(0)

    sm.wait_k(1)
    sm.k_pages(1, 0, half)
    u = up(0, h)
    sm.k_pages(1, half, CHUNK_PAGES)
    sm.refill_after_k(1)

    sm.wait_k(2)
    sm.k_pages(2, 0, half)
    acc = down(0, u)
    sm.k_pages(2, half, CHUNK_PAGES)
    sm.refill_after_k(2)

    sm.wait_k(3)
    sm.k_pages(3, 0, half)
    u = up(1, h)
    sm.k_pages(3, half, CHUNK_PAGES)
    sm.refill_after_k(3)

    l, acc_s = sm.select()
    acc = acc + down(1, u)

    sm.wait_v(0)
    acc_s = sm.v_pages(0, 0, half, acc_s)
    u = up(2, h)
    acc_s = sm.v_pages(0, half, CHUNK_PAGES, acc_s)
    sm.refill_after_v(0)

    sm.wait_v(1)
    acc_s = sm.v_pages(1, 0, half, acc_s)
    acc = acc + down(2, u)
    acc_s = sm.v_pages(1, half, CHUNK_PAGES, acc_s)
    sm.refill_after_v(1)

    sm.wait_v(2)
    acc_s = sm.v_pages(2, 0, half, acc_s)
    u = up(3, h)
    acc_s = sm.v_pages(2, half, CHUNK_PAGES, acc_s)
    sm.refill_after_v(2)

    sm.wait_v(3)
    acc_s = sm.v_pages(3, 0, half, acc_s)
    acc = acc + down(3, u)
    acc_s = sm.v_pages(3, half, CHUNK_PAGES, acc_s)
    sm.refill_after_v(3)
    y_ref[...] = _layer_norm(ALPHA * x1 + _modulation(gf_ref, False) * acc, g2_ref[...], b2_ref[...])
    sm.finish(l, acc_s)


def _out_ffn_moba(a_ret, a_moba, x2d, mod, weights, page_table, mq_s, kn2d, vn2d, cache_k2d, cache_v2d, tm):
    t = x2d.shape[0]
    nt = SEQ // tm
    assert BATCH * nt == DEC_BATCH
    n_rows = MOBA_HEADS * DEC_SEQ
    tile = lambda b, i, pt: (b * nt + i, 0)
    wide = lambda w: pl.BlockSpec((tm, w), tile)
    const = lambda a: pl.BlockSpec(a.shape, lambda b, i, pt: (0,) * a.ndim, pipeline_mode=pl.Buffered(1))
    seq_rows = lambda w, n: pl.BlockSpec((n, w), tile)
    hbm = pl.BlockSpec(memory_space=pl.ANY)
    grid_spec = pltpu.PrefetchScalarGridSpec(
        num_scalar_prefetch=1,
        grid=(BATCH, nt),
        in_specs=[wide(RET_WIDTH), wide(MOBA_WIDTH), wide(D_MODEL)] + [_mod_spec(term) for term in (2, 3, 4, 5)]
        + [const(a) for a in weights]
        + [seq_rows(MOBA_WIDTH, DEC_SEQ), seq_rows(LANES, n_rows), seq_rows(LANES, n_rows), hbm, hbm],
        out_specs=[wide(D_MODEL), seq_rows(MOBA_WIDTH, DEC_SEQ)],
        scratch_shapes=[pltpu.VMEM((RING_PAGES, PAGE_ROWS, LANES), F32),
                        pltpu.SemaphoreType.DMA((RING_PAGES // CHUNK_PAGES,)),
                        pltpu.VMEM((PAGES_PER_SEQ, n_rows, PAGE_ROWS), F32)],
    )
    return pl.pallas_call(
        _out_ffn_moba_body,
        grid_spec=grid_spec,
        out_shape=[jax.ShapeDtypeStruct((t, D_MODEL), F32),
                   jax.ShapeDtypeStruct((N_SAMPLE_TOK, MOBA_WIDTH), F32)],
        compiler_params=pltpu.CompilerParams(dimension_semantics=("arbitrary", "arbitrary"),
                                             vmem_limit_bytes=VMEM_LIMIT_FUSED),
        name="out_ffn_moba",
    )(page_table.reshape(-1), a_ret, a_moba, x2d, *([mod] * 4), *weights,
      mq_s, kn2d, vn2d, cache_k2d, cache_v2d)


def kernel(x_prompt, x_sample, cache_k, cache_v, state_ret, page_table, c_prompt, c_sample,
           w_ada, b_ada, w_in, w_o, ln1_g, ln1_b, w_up, w_down, ln2_g, ln2_b):
    n_prompt_tok = BATCH * SEQ
    past_len = page_table.shape[1] * PAGE_SIZE

    mod = _adaln(c_prompt, c_sample, w_ada[0], b_ada)

    tm = 512
    nt = SEQ // tm
    p_row = lambda b, i: (b * nt + i, 0)
    p_tab = pl.BlockSpec((tm, LANES), lambda b, i: (i, 0))
    s_row = lambda i: (0, 0)
    s_tab = pl.BlockSpec((N_SAMPLE_TOK, LANES), s_row)

    pos_p = np.arange(SEQ, dtype=np.int32)
    pos_s = np.tile(past_len + np.arange(DEC_SEQ, dtype=np.int32), DEC_BATCH)
    tabs_p = _rope_tables(pos_p, MOBA_HEAD_DIM) + _rope_tables(pos_p, RET_DK)
    tabs_s = _rope_tables(pos_s, MOBA_HEAD_DIM) + _rope_tables(pos_s, RET_DK)

    xp = x_prompt.reshape(n_prompt_tok, D_MODEL)
    rq, rk, rv, rg, mq, k_p, v_p, w_o_b, w_up_b, w_down_b = _inproj(
        xp, mod, False, [p_tab] * 4, tabs_p, w_in[0],
        (BATCH, nt), p_row, tm, BF16, casts=(w_o[0], w_up[0], w_down[0]))
    weights = (w_o_b, ln1_g, ln1_b, w_up_b, w_down_b, ln2_g, ln2_b)
    a_ret, state_p = _ret_prompt(rq, rk, rv, rg)
    a_moba = _moba_prompt(mq, k_p, v_p)

    xs = x_sample.reshape(N_SAMPLE_TOK, D_MODEL)
    rq_s, rk_s, rv_s, rg_s, mq_s, k_s, v_s = _inproj(
        xs, mod, True, [s_tab] * 4, tabs_s, w_in[0],
        (1,), s_row, N_SAMPLE_TOK, F32)
    a_ret_s, state_s = _ret_sample(rq_s, rk_s, rv_s, rg_s, state_ret[0])

    cache_rows = cache_k.shape[1] * PAGE_ROWS
    y_p, a_moba_s = _out_ffn_moba(a_ret, a_moba, xp, mod, weights, page_table, mq_s, k_s, v_s,
                                  cache_k.reshape(cache_rows, LANES), cache_v.reshape(cache_rows, LANES), tm)
    y_s = _out_ffn(a_ret_s, a_moba_s, xs, mod, weights)

    kv_p_shape = (DEPTH, BATCH, SEQ, MOBA_HEADS, MOBA_HEAD_DIM)
    kv_s_shape = (DEPTH, DEC_BATCH, DEC_SEQ, MOBA_HEADS, MOBA_HEAD_DIM)
    return (y_p.reshape(BATCH, SEQ, D_MODEL),
            y_s.reshape(DEC_BATCH, DEC_SEQ, D_MODEL),
            k_p.reshape(kv_p_shape), v_p.reshape(kv_p_shape), state_p[None],
            k_s.reshape(kv_s_shape), v_s.reshape(kv_s_shape), state_s[None])
```

```python
import functools

import numpy as np
import jax
import jax.numpy as jnp
from jax import lax
from jax.experimental import pallas as pl
from jax.experimental.pallas import tpu as pltpu

F32 = jnp.float32
BF16 = jnp.bfloat16

D_MODEL = 1024
BATCH = 8
SEQ = 2048
DEC_BATCH = 32
DEC_SEQ = 8
PAGE_SIZE = 128
RET_HEADS = 4
RET_DK = 64
RET_DV = 128
RET_CHUNK = 128
MOBA_HEADS = 4
MOBA_HEAD_DIM = 128
MOBA_BLOCK = 256
MOBA_TOPK = 3
D_FF = 4 * D_MODEL
ROPE_THETA = 10000.0
LN_EPS = 1e-5
GN_EPS = 1e-6
DEPTH = 1
ALPHA = (2 * DEPTH) ** 0.25
RET_QK = RET_HEADS * RET_DK
RET_WIDTH = RET_HEADS * RET_DV
MOBA_WIDTH = MOBA_HEADS * MOBA_HEAD_DIM
IN_WIDTH = 2 * RET_QK + 2 * RET_WIDTH + 3 * MOBA_WIDTH
OFF_RQ, OFF_RK, OFF_RV, OFF_RG = 0, 256, 512, 1024
OFF_MQ, OFF_MK, OFF_MV = 1536, 2048, 2560
LANES = 128
VMEM_BYTES_V7X = 64 * 1024 * 1024
VMEM_LIMIT = VMEM_BYTES_V7X * 3 // 4
VMEM_LIMIT_FUSED = VMEM_BYTES_V7X * 29 // 32
N_SAMPLE_TOK = DEC_BATCH * DEC_SEQ
PAGES_PER_SEQ = 64
CHUNK_PAGES = 16
RING_PAGES = 2 * CHUNK_PAGES
PAGE_ROWS = PAGE_SIZE * MOBA_HEADS
LOG2_RET_DK = RET_DK.bit_length() - 1
LOG2_DEC_SEQ = DEC_SEQ.bit_length() - 1
LOG2_MOBA_HEADS = MOBA_HEADS.bit_length() - 1
NEG_INF = float("-inf")
LOG2_E = 1.4426950408889634

_NT = (((1,), (1,)), ((), ()))
_TN = (((0,), (0,)), ((), ()))


def _log_decay():
    return np.log1p(-np.exp2(-5.0 - np.arange(RET_HEADS, dtype=np.float64)))


N_MOD = 6
MOD_ROWS = BATCH + DEC_BATCH


def _adaln_body(cp_ref, cs_ref, w_ref, b_ref, o_ref):
    c = jnp.concatenate([cp_ref[...], cs_ref[...]], axis=0)
    a = (c * jax.nn.sigmoid(c)).astype(BF16)
    o_ref[...] = jnp.dot(a, w_ref[...].astype(BF16), preferred_element_type=F32) + b_ref[...]


def _adaln(c_prompt, c_sample, w_ada, b_ada):
    return pl.pallas_call(
        _adaln_body,
        grid=(N_MOD,),
        in_specs=[pl.BlockSpec(c_prompt.shape, lambda j: (0, 0)),
                  pl.BlockSpec(c_sample.shape, lambda j: (0, 0)),
                  pl.BlockSpec((D_MODEL, D_MODEL), lambda j: (0, j)),
                  pl.BlockSpec((1, D_MODEL), lambda j: (0, j))],
        out_specs=pl.BlockSpec((None, MOD_ROWS, D_MODEL), lambda j: (j, 0, 0)),
        out_shape=jax.ShapeDtypeStruct((N_MOD, MOD_ROWS, D_MODEL), F32),
        name="adaln",
    )(c_prompt, c_sample, w_ada, b_ada)


def _modulation(ref, sample_group):
    if sample_group:
        rows = ref[BATCH:BATCH + DEC_BATCH, :]
        return jnp.broadcast_to(rows[:, None, :], (DEC_BATCH, DEC_SEQ, D_MODEL)).reshape(N_SAMPLE_TOK, D_MODEL)
    return ref[pl.ds(pl.program_id(0), 1), :]


def _mod_spec(term):
    return pl.BlockSpec((None, MOD_ROWS, D_MODEL), lambda *_: (term, 0, 0))


def _rope_tables(pos, head_dim):
    half = head_dim // 2
    inv_freq = np.power(ROPE_THETA, -np.arange(half, dtype=np.float64) / half)
    ang = pos.astype(np.float64)[:, None] * inv_freq[None, :]
    cos, sin = np.cos(ang), np.sin(ang)
    reps = LANES // head_dim
    cos_t = np.tile(np.concatenate([cos, cos], axis=-1), (1, reps))
    sin_t = np.tile(np.concatenate([-sin, sin], axis=-1), (1, reps))
    return jnp.asarray(cos_t, dtype=F32), jnp.asarray(sin_t, dtype=F32)


def _inproj_body(sample_group, n_casts, x_ref, sc_ref, sh_ref, w_ref, cm_ref, sm_ref, cr_ref, sr_ref, *refs):
    cast_in, refs = refs[:n_casts], refs[n_casts:]
    rq_ref, rk_ref, rv_ref, rg_ref, mq_ref, ko_ref, vo_ref = refs[:7]
    for src, dst in zip(cast_in, refs[7:]):
        dst[...] = src[...].astype(dst.dtype)
    tm = x_ref.shape[0]
    h = (x_ref[...] * (1.0 + _modulation(sc_ref, sample_group)) + _modulation(sh_ref, sample_group)).astype(BF16)

    def proj(lo, width):
        return jnp.dot(h, w_ref[:, lo:lo + width].astype(BF16), preferred_element_type=F32)

    lane = lax.broadcasted_iota(jnp.int32, (tm, LANES), 1)
    low_half = (lane & (RET_DK - 1)) < (RET_DK // 2)
    cr, sr = cr_ref[...], sr_ref[...]
    cm, sm = cm_ref[...], sm_ref[...]

    def rope_ret(z):
        rot = jnp.where(low_half, pltpu.roll(z, LANES - RET_DK // 2, 1), pltpu.roll(z, RET_DK // 2, 1))
        return z * cr + rot * sr

    def rope_moba(z):
        return z * cm + pltpu.roll(z, MOBA_HEAD_DIM // 2, 1) * sm

    zq = proj(OFF_RQ, RET_QK)
    zk = proj(OFF_RK, RET_QK)
    for s in range(RET_QK // LANES):
        sl = slice(s * LANES, (s + 1) * LANES)
        rq_ref[:, sl] = rope_ret(zq[:, sl])
        rk_ref[:, sl] = rope_ret(zk[:, sl]) * (RET_DK ** -0.5)
    rv_ref[...] = proj(OFF_RV, RET_WIDTH).astype(rv_ref.dtype)
    rg_ref[...] = proj(OFF_RG, RET_WIDTH)
    zq = proj(OFF_MQ, MOBA_WIDTH)
    zk = proj(OFF_MK, MOBA_WIDTH)
    zv = proj(OFF_MV, MOBA_WIDTH)
    for hd in range(MOBA_HEADS):
        sl = slice(hd * LANES, (hd + 1) * LANES)
        mq_ref[:, sl] = rope_moba(zq[:, sl]).astype(mq_ref.dtype)
        ko_ref[pl.ds(hd, tm, stride=MOBA_HEADS), :] = rope_moba(zk[:, sl])
        vo_ref[pl.ds(hd, tm, stride=MOBA_HEADS), :] = zv[:, sl]


def _inproj(x2d, mod, sample_group, tab_specs, tabs, w_in, grid, row_map, tm, act_dtype, casts=()):
    t = x2d.shape[0]
    n_steps = int(np.prod(grid))
    wide = lambda w: pl.BlockSpec((tm, w), row_map)
    cast_specs = [pl.BlockSpec((a.shape[0] // n_steps, a.shape[1]), row_map) for a in casts]
    w_spec = pl.BlockSpec((D_MODEL, IN_WIDTH), lambda *_: (0, 0), pipeline_mode=pl.Buffered(1))
    return pl.pallas_call(
        functools.partial(_inproj_body, sample_group, len(casts)),
        grid=grid,
        in_specs=[wide(D_MODEL), _mod_spec(1), _mod_spec(0), w_spec] + tab_specs + cast_specs,
        out_specs=[wide(RET_QK), wide(RET_QK), wide(RET_WIDTH), wide(RET_WIDTH), wide(MOBA_WIDTH),
                   pl.BlockSpec((tm * MOBA_HEADS, LANES), row_map),
                   pl.BlockSpec((tm * MOBA_HEADS, LANES), row_map)] + cast_specs,
        out_shape=[jax.ShapeDtypeStruct((t, RET_QK), F32), jax.ShapeDtypeStruct((t, RET_QK), F32),
                   jax.ShapeDtypeStruct((t, RET_WIDTH), act_dtype), jax.ShapeDtypeStruct((t, RET_WIDTH), F32),
                   jax.ShapeDtypeStruct((t, MOBA_WIDTH), act_dtype),
                   jax.ShapeDtypeStruct((t * MOBA_HEADS, LANES), F32),
                   jax.ShapeDtypeStruct((t * MOBA_HEADS, LANES), F32)]
        + [jax.ShapeDtypeStruct(a.shape, BF16) for a in casts],
        compiler_params=pltpu.CompilerParams(dimension_semantics=("arbitrary",) * len(grid),
                                             vmem_limit_bytes=VMEM_LIMIT),
        name="inproj",
    )(x2d, mod, mod, w_in, *tabs, *casts)


def _group_norm_gate(o, g):
    mu = jnp.mean(o, axis=-1, keepdims=True)
    d = o - mu
    var = jnp.mean(d * d, axis=-1, keepdims=True)
    return d * lax.rsqrt(var + GN_EPS) * (g * jax.nn.sigmoid(g))


def _ret_prompt_tables():
    lg = _log_decay()
    i = np.arange(RET_CHUNK, dtype=np.float64)
    diff = i[:, None] - i[None, :]
    dmat = np.where(diff >= 0, np.exp(np.maximum(diff, 0.0)[None] * lg[:, None, None]), 0.0)
    lane_head = np.arange(RET_QK) // RET_DK
    qd = np.exp((i[:, None] + 1.0) * lg[lane_head][None, :])
    kd = np.exp((RET_CHUNK - 1.0 - i)[:, None] * lg[lane_head][None, :])
    return [jnp.asarray(a, dtype=F32) for a in (dmat, np.concatenate([qd, kd], axis=1))]


def _ret_prompt_body(q_ref, k_ref, v_ref, g_ref, dmat_ref, qkd_ref, o_ref, st_ref):
    lane_head = lax.broadcasted_iota(jnp.int32, (RET_CHUNK, RET_QK), 1) >> LOG2_RET_DK
    chunk_rows = lambda c: slice(c * RET_CHUNK, (c + 1) * RET_CHUNK)
    chunk_decay = [float(np.exp(RET_CHUNK * lg)) for lg in _log_decay()]
    head_block = lambda h: (slice(h * RET_DK, (h + 1) * RET_DK), slice(h * RET_DV, (h + 1) * RET_DV))

    def block_diagonal(state):
        zero = jnp.zeros((RET_DK, RET_DV), BF16)
        return jnp.concatenate(
            [jnp.concatenate([state[h].astype(BF16) if c == h else zero for c in range(RET_HEADS)], axis=1)
             for h in range(RET_HEADS)], axis=0)

    def first_matmuls(c, state):
        rows = chunk_rows(c)
        q = q_ref[rows, :]
        k = k_ref[rows, :]
        v = v_ref[rows, :]
        kb = k.astype(BF16)
        scores = [lax.dot_general(jnp.where(lane_head == hd, q, 0.0).astype(BF16), kb, _NT,
                                  preferred_element_type=F32) for hd in range(RET_HEADS)]
        cross = jnp.dot((q * qkd_ref[:, :RET_QK]).astype(BF16), block_diagonal(state),
                        preferred_element_type=F32)
        kv = lax.dot_general((k * qkd_ref[:, RET_QK:]).astype(BF16), v, _TN, preferred_element_type=F32)
        new_state = [chunk_decay[h] * state[h] + kv[head_block(h)] for h in range(RET_HEADS)]
        return (scores, cross, v), new_state

    def second_matmuls(c, scores, cross, v):
        rows = chunk_rows(c)
        g = g_ref[rows, :]
        decayed = [(scores[hd] * dmat_ref[hd]).astype(BF16) for hd in range(RET_HEADS)]
        for hd in range(RET_HEADS):
            sl = slice(hd * RET_DV, (hd + 1) * RET_DV)
            inner = jnp.dot(decayed[hd], v[:, sl], preferred_element_type=F32)
            o_ref[rows, sl] = _group_norm_gate(inner + cross[:, sl], g[:, sl]).astype(o_ref.dtype)

    n_chunks = SEQ // RET_CHUNK
    ahead, state = first_matmuls(0, [jnp.zeros((RET_DK, RET_DV), F32)] * RET_HEADS)
    for c in range(n_chunks):
        current = ahead
        if c + 1 < n_chunks:
            ahead, state = first_matmuls(c + 1, state)
        second_matmuls(c, *current)
    for hd in range(RET_HEADS):
        st_ref[hd] = state[hd]


def _ret_prompt(rq, rk, rv, rg):
    tabs = _ret_prompt_tables()
    seq = lambda w: pl.BlockSpec((SEQ, w), lambda b: (b, 0))
    const = lambda a: pl.BlockSpec(a.shape, lambda b: (0,) * a.ndim)
    return pl.pallas_call(
        _ret_prompt_body,
        grid=(BATCH,),
        in_specs=[seq(RET_QK), seq(RET_QK), seq(RET_WIDTH), seq(RET_WIDTH)] + [const(a) for a in tabs],
        out_specs=[seq(RET_WIDTH), pl.BlockSpec((None, RET_HEADS, RET_DK, RET_DV), lambda b: (b, 0, 0, 0))],
        out_shape=[jax.ShapeDtypeStruct((BATCH * SEQ, RET_WIDTH), BF16),
                   jax.ShapeDtypeStruct((BATCH, RET_HEADS, RET_DK, RET_DV), F32)],
        compiler_params=pltpu.CompilerParams(dimension_semantics=("arbitrary",),
                                             vmem_limit_bytes=VMEM_LIMIT),
        name="ret_prompt",
    )(rq, rk, rv, rg, *tabs)


def _ret_sample_tables():
    lg = _log_decay()
    t = np.arange(N_SAMPLE_TOK) % DEC_SEQ
    seq_id = np.arange(N_SAMPLE_TOK) // DEC_SEQ
    diff = (t[:, None] - t[None, :]).astype(np.float64)
    same_seq = seq_id[:, None] == seq_id[None, :]
    dmat = np.where(same_seq[None] & (diff >= 0)[None],
                    np.exp(np.maximum(diff, 0.0)[None] * lg[:, None, None]), 0.0)
    lane_head = np.arange(RET_QK) // RET_DK
    qd = np.exp((t[:, None] + 1.0) * lg[lane_head][None, :])
    kd = np.exp((DEC_SEQ - 1.0 - t)[:, None] * lg[lane_head][None, :])
    return [jnp.asarray(a, dtype=F32) for a in (dmat, np.concatenate([qd, kd], axis=1))]


def _ret_sample_body(q_ref, k_ref, v_ref, g_ref, st_ref, dmat_ref, qkd_ref, o_ref, sto_ref):
    lg = _log_decay()
    q = q_ref[...]
    k = k_ref[...]
    kb = k.astype(BF16)
    qdec = q * qkd_ref[:, :RET_QK]
    kdec = k * qkd_ref[:, RET_QK:]
    vb = v_ref[...].astype(BF16)
    g = g_ref[...]
    lane = lax.broadcasted_iota(jnp.int32, (N_SAMPLE_TOK, LANES), 1)
    lane_head = lax.broadcasted_iota(jnp.int32, (N_SAMPLE_TOK, RET_QK), 1) >> LOG2_RET_DK
    n_state_rows = DEC_BATCH * RET_DK
    own_seq = ((lax.broadcasted_iota(jnp.int32, (N_SAMPLE_TOK, n_state_rows), 0) >> LOG2_DEC_SEQ)
               == (lax.broadcasted_iota(jnp.int32, (N_SAMPLE_TOK, n_state_rows), 1) >> LOG2_RET_DK))
    for hd in range(RET_HEADS):
        sl = slice(hd * RET_DV, (hd + 1) * RET_DV)
        qm = jnp.where(lane_head == hd, q, 0.0).astype(BF16)
        s = lax.dot_general(qm, kb, _NT, preferred_element_type=F32) * dmat_ref[hd]
        inner = jnp.dot(s.astype(BF16), vb[:, sl], preferred_element_type=F32)

        def expand(z):
            slab = z[:, (hd // 2) * LANES:(hd // 2 + 1) * LANES]
            other = pltpu.roll(slab, RET_DK, 1)
            in_low = lane < RET_DK
            both = jnp.where(in_low, slab, other) if hd % 2 == 0 else jnp.where(in_low, other, slab)
            tiled = jnp.concatenate([both] * (n_state_rows // LANES), axis=1)
            return jnp.where(own_seq, tiled, 0.0).astype(BF16)

        st = st_ref[:, hd].reshape(n_state_rows, RET_DV)
        cross = jnp.dot(expand(qdec), st.astype(BF16), preferred_element_type=F32)
        o_ref[:, sl] = _group_norm_gate(inner + cross, g[:, sl])
        kv = lax.dot_general(expand(kdec), vb[:, sl], _TN, preferred_element_type=F32)
        new = float(np.exp(DEC_SEQ * lg[hd])) * st + kv
        sto_ref[:, hd] = new.reshape(DEC_BATCH, RET_DK, RET_DV)


def _ret_sample(rq, rk, rv, rg, state):
    tabs = _ret_sample_tables()
    full = lambda a: pl.BlockSpec(a.shape, lambda i: (0,) * a.ndim)
    args = (rq, rk, rv, rg, state, *tabs)
    return pl.pallas_call(
        _ret_sample_body,
        grid=(1,),
        in_specs=[full(a) for a in args],
        out_specs=[pl.BlockSpec((N_SAMPLE_TOK, RET_WIDTH), lambda i: (0, 0)),
                   pl.BlockSpec(state.shape, lambda i: (0, 0, 0, 0))],
        out_shape=[jax.ShapeDtypeStruct((N_SAMPLE_TOK, RET_WIDTH), F32),
                   jax.ShapeDtypeStruct(state.shape, F32)],
        compiler_params=pltpu.CompilerParams(dimension_semantics=("arbitrary",),
                                             vmem_limit_bytes=VMEM_LIMIT),
        name="ret_sample",
    )(*args)


def _moba_prompt_body(n_casts, q_ref, k_ref, v_ref, *refs):
    o_ref = refs[n_casts]
    for src, dst in zip(refs[:n_casts], refs[n_casts + 1:]):
        dst[...] = src[...].astype(dst.dtype)
    hd = pl.program_id(1)
    n_blocks = SEQ // MOBA_BLOCK
    exp2_scale = MOBA_HEAD_DIM ** -0.5 * LOG2_E
    k32 = k_ref[pl.ds(hd, SEQ, stride=MOBA_HEADS), :]
    kb = (k32 * exp2_scale).astype(BF16)
    vt = v_ref[pl.ds(hd, SEQ, stride=MOBA_HEADS), :].T.astype(BF16)
    kmean = jnp.sum(k32.reshape(n_blocks, MOBA_BLOCK, MOBA_HEAD_DIM), axis=1) * (1.0 / MOBA_BLOCK)
    kmb = kmean.astype(BF16)
    key_id = lax.broadcasted_iota(jnp.int32, (MOBA_BLOCK, MOBA_BLOCK), 0)
    qry_id = lax.broadcasted_iota(jnp.int32, (MOBA_BLOCK, MOBA_BLOCK), 1)
    causal = key_id <= qry_id

    blk = lambda n: slice(n * MOBA_BLOCK, (n + 1) * MOBA_BLOCK)

    def score_matmuls(i):
        qi = q_ref[blk(i), :]
        st = [lax.dot_general(kb[blk(n)], qi, _NT, preferred_element_type=F32) for n in range(i + 1)]
        st[i] = jnp.where(causal, st[i], NEG_INF)
        top = [jnp.max(s, axis=0, keepdims=True) for s in st]
        gt = lax.dot_general(kmb, qi, _NT, preferred_element_type=F32) if i > MOBA_TOPK else None
        return st, top, gt

    ahead = score_matmuls(n_blocks - 1)
    for i in reversed(range(n_blocks)):
        st, top, gt = ahead
        if i > 0:
            ahead = score_matmuls(i - 1)
        drop = [0.0] * (i + 1)
        if i > MOBA_TOPK:
            for n in range(i):
                beats = jnp.zeros((1, MOBA_BLOCK), F32)
                for mm in range(i):
                    if mm == n:
                        continue
                    win = (gt[mm:mm + 1] >= gt[n:n + 1]) if mm < n else (gt[mm:mm + 1] > gt[n:n + 1])
                    beats = beats + win.astype(F32)
                drop[n] = jnp.where(beats < MOBA_TOPK, 0.0, NEG_INF)
        m = functools.reduce(jnp.maximum, [top[n] + drop[n] for n in range(i + 1)])
        l = jnp.zeros((1, MOBA_BLOCK), F32)
        acc = jnp.zeros((MOBA_HEAD_DIM, MOBA_BLOCK), F32)
        for n in range(i + 1):
            e = jnp.exp2(st[n] - (m - drop[n]))
            l = l + jnp.sum(e, axis=0, keepdims=True)
            acc = acc + jnp.dot(vt[:, blk(n)], e.astype(BF16), preferred_element_type=F32)
        o_ref[blk(i), :] = (acc / l).T.astype(o_ref.dtype)


def _moba_prompt(mq, k2d, v2d, casts=()):
    kv_spec = pl.BlockSpec((SEQ * MOBA_HEADS, LANES), lambda b, h: (b, 0))
    n_steps = BATCH * MOBA_HEADS
    cast_specs = [pl.BlockSpec((a.shape[0] // n_steps, a.shape[1]), lambda b, h: (b * MOBA_HEADS + h, 0))
                  for a in casts]
    return pl.pallas_call(
        functools.partial(_moba_prompt_body, len(casts)),
        grid=(BATCH, MOBA_HEADS),
        in_specs=[pl.BlockSpec((SEQ, MOBA_HEAD_DIM), lambda b, h: (b, h)), kv_spec, kv_spec] + cast_specs,
        out_specs=[pl.BlockSpec((SEQ, MOBA_HEAD_DIM), lambda b, h: (b, h))] + cast_specs,
        out_shape=[jax.ShapeDtypeStruct((BATCH * SEQ, MOBA_WIDTH), BF16)]
        + [jax.ShapeDtypeStruct(a.shape, BF16) for a in casts],
        compiler_params=pltpu.CompilerParams(dimension_semantics=("arbitrary", "arbitrary"),
                                             vmem_limit_bytes=VMEM_LIMIT),
        name="moba_prompt",
    )(mq, k2d, v2d, *casts)


class _SampleMoba:
    n_chunks = PAGES_PER_SEQ // CHUNK_PAGES
    n_rows = MOBA_HEADS * DEC_SEQ
    n_blocks = PAGES_PER_SEQ * PAGE_SIZE // MOBA_BLOCK
    pages_per_block = MOBA_BLOCK // PAGE_SIZE

    def __init__(self, pt_ref, seq, n_seqs, q_ref, kn_ref, vn_ref, kc_ref, vc_ref, o_ref, ring, sem, s_scr):
        self.pt_ref, self.seq, self.n_seqs = pt_ref, seq, n_seqs
        self.q_ref, self.kn_ref, self.vn_ref = q_ref, kn_ref, vn_ref
        self.kc_ref, self.vc_ref, self.o_ref = kc_ref, vc_ref, o_ref
        self.ring, self.sem, self.s_scr = ring, sem, s_scr

    def _page_copy(self, cache_ref, row0, slot):
        return pltpu.make_async_copy(cache_ref.at[pl.ds(row0, PAGE_ROWS)], self.ring.at[slot],
                                     self.sem.at[slot // CHUNK_PAGES])

    def _start_chunk(self, cache_ref, seq, chunk):
        for r in range(CHUNK_PAGES):
            page = self.pt_ref[seq * PAGES_PER_SEQ + chunk * CHUNK_PAGES + r]
            self._page_copy(cache_ref, pl.multiple_of(page * PAGE_ROWS, PAGE_ROWS),
                            (chunk % 2) * CHUNK_PAGES + r).start()

    def _wait_chunk(self, cache_ref, chunk):
        for r in range(CHUNK_PAGES):
            self._page_copy(cache_ref, 0, (chunk % 2) * CHUNK_PAGES + r).wait()

    def prologue(self):
        @pl.when(self.seq == 0)
        def _():
            self._start_chunk(self.kc_ref, self.seq, 0)
            self._start_chunk(self.kc_ref, self.seq, 1)

        self.q = jnp.concatenate(
            [self.q_ref[:, hd * MOBA_HEAD_DIM:(hd + 1) * MOBA_HEAD_DIM] for hd in range(MOBA_HEADS)], axis=0
        ).astype(BF16)
        row_head = lax.broadcasted_iota(jnp.int32, (self.n_rows, LANES), 0) >> LOG2_DEC_SEQ
        col_head = lax.broadcasted_iota(jnp.int32, (self.n_rows, LANES), 1) & (MOBA_HEADS - 1)
        self.same_head = row_head == col_head
        self.head_bias = jnp.where(self.same_head, 0.0, NEG_INF)
        self.block_sum, self.block_max = [], []

    @staticmethod
    def _slabs(x):
        return [x[:, j * LANES:(j + 1) * LANES] for j in range(PAGE_ROWS // LANES)]

    def wait_k(self, c):
        self._wait_chunk(self.kc_ref, c)

    def k_pages(self, c, first, last):
        assert first % self.pages_per_block == 0 and last % self.pages_per_block == 0
        for r0 in range(first, last, self.pages_per_block):
            tot = jnp.zeros((self.n_rows, LANES), F32)
            top = jnp.full((self.n_rows, LANES), NEG_INF, F32)
            for r in range(r0, r0 + self.pages_per_block):
                page = self.ring[(c % 2) * CHUNK_PAGES + r].astype(BF16)
                s = lax.dot_general(self.q, page, _NT, preferred_element_type=F32)
                self.s_scr[c * CHUNK_PAGES + r] = s
                for slab in self._slabs(s):
                    tot = tot + jnp.where(self.same_head, slab, 0.0)
                    top = jnp.maximum(top, slab + self.head_bias)
            self.block_sum.append(jnp.sum(tot, axis=-1, keepdims=True))
            self.block_max.append(jnp.max(top, axis=-1, keepdims=True))

    def refill_after_k(self, c):
        if c + 2 < self.n_chunks:
            self._start_chunk(self.kc_ref, self.seq, c + 2)
        else:
            self._start_chunk(self.vc_ref, self.seq, c + 2 - self.n_chunks)

    def select(self):
        n_rows, gs = self.n_rows, self.block_sum
        self.exp2_scale = MOBA_HEAD_DIM ** -0.5 * LOG2_E
        lane = lax.broadcasted_iota(jnp.int32, (n_rows, LANES), 1)
        g_all = jnp.full((n_rows, LANES), NEG_INF, F32)
        for n in range(self.n_blocks):
            g_all = jnp.where(lane == n, gs[n], g_all)
        self.keep_bias = []
        for n in range(self.n_blocks):
            wins = (g_all > gs[n]) | ((g_all == gs[n]) & (lane < n))
            beats = jnp.sum(wins.astype(F32), axis=-1, keepdims=True)
            self.keep_bias.append(jnp.where(beats < MOBA_TOPK, 0.0, NEG_INF))
        s_own = lax.dot_general(self.q, self.kn_ref[...].astype(BF16), _NT, preferred_element_type=F32)
        r_id = lax.broadcasted_iota(jnp.int32, (n_rows, n_rows), 0)
        c_id = lax.broadcasted_iota(jnp.int32, (n_rows, n_rows), 1)
        own_ok = ((c_id & (MOBA_HEADS - 1)) == (r_id >> LOG2_DEC_SEQ)) & ((c_id >> LOG2_MOBA_HEADS) <= (r_id & (DEC_SEQ - 1)))
        s_own = jnp.where(own_ok, s_own, NEG_INF)
        m = jnp.max(s_own, axis=-1, keepdims=True)
        for n in range(self.n_blocks):
            m = jnp.maximum(m, self.block_max[n] + self.keep_bias[n])
        self.m = m
        self.lsum = jnp.zeros((n_rows, LANES), F32)
        e_own = jnp.exp2((s_own - m) * self.exp2_scale)
        return (jnp.sum(e_own, axis=-1, keepdims=True),
                jnp.dot(e_own.astype(BF16), self.vn_ref[...].astype(BF16), preferred_element_type=F32))

    def wait_v(self, c):
        self._wait_chunk(self.vc_ref, c)

    def v_pages(self, c, first, last, acc):
        assert first % self.pages_per_block == 0 and last % self.pages_per_block == 0
        for r0 in range(first, last, self.pages_per_block):
            shift = self.head_bias + (self.keep_bias[(c * CHUNK_PAGES + r0) // self.pages_per_block] - self.m)
            for r in range(r0, r0 + self.pages_per_block):
                e = [jnp.exp2((slab + shift) * self.exp2_scale) for slab in self._slabs(self.s_scr[c * CHUNK_PAGES + r])]
                self.lsum = self.lsum + functools.reduce(jnp.add, e)
                page = self.ring[(c % 2) * CHUNK_PAGES + r].astype(BF16)
                acc = acc + jnp.dot(jnp.concatenate(e, axis=1).astype(BF16), page, preferred_element_type=F32)
        return acc

    def refill_after_v(self, c):
        if c + 2 < self.n_chunks:
            self._start_chunk(self.vc_ref, self.seq, c + 2)
        else:
            @pl.when(self.seq + 1 < self.n_seqs)
            def _():
                self._start_chunk(self.kc_ref, self.seq + 1, c + 2 - self.n_chunks)

    def finish(self, l_own, acc):
        out = acc / (l_own + jnp.sum(self.lsum, axis=-1, keepdims=True))
        for hd in range(MOBA_HEADS):
            self.o_ref[:, hd * MOBA_HEAD_DIM:(hd + 1) * MOBA_HEAD_DIM] = out[hd * DEC_SEQ:(hd + 1) * DEC_SEQ]


def _layer_norm(x, g, b):
    mu = jnp.mean(x, axis=-1, keepdims=True)
    d = x - mu
    var = jnp.mean(d * d, axis=-1, keepdims=True)
    return d * lax.rsqrt(var + LN_EPS) * g + b


def _out_ffn_body(ar_ref, am_ref, x_ref, ga_ref, shf_ref, scf_ref, gf_ref, wo_ref, g1_ref, b1_ref,
                  wu_ref, wd_ref, g2_ref, b2_ref, y_ref, x1_scr, h_scr, acc_scr):
    c = pl.program_id(0)

    @pl.when(c == 0)
    def _():
        mixed = (jnp.dot(ar_ref[...].astype(BF16), wo_ref[:RET_WIDTH, :], preferred_element_type=F32)
                 + jnp.dot(am_ref[...].astype(BF16), wo_ref[RET_WIDTH:, :], preferred_element_type=F32))
        x1 = _layer_norm(ALPHA * x_ref[...] + _modulation(ga_ref, True) * mixed, g1_ref[...], b1_ref[...])
        x1_scr[...] = x1
        h_scr[...] = (x1 * (1.0 + _modulation(scf_ref, True)) + _modulation(shf_ref, True)).astype(BF16)
        acc_scr[...] = jnp.zeros_like(acc_scr)

    u = jnp.maximum(jnp.dot(h_scr[...], wu_ref[...], preferred_element_type=F32), 0.0)
    acc_scr[...] += jnp.dot((u * u).astype(BF16), wd_ref[...], preferred_element_type=F32)

    @pl.when(c == pl.num_programs(0) - 1)
    def _():
        y_ref[...] = _layer_norm(ALPHA * x1_scr[...] + _modulation(gf_ref, True) * acc_scr[...],
                                 g2_ref[...], b2_ref[...])


def _out_ffn(a_ret, a_moba, x2d, mod, weights):
    w_o, ln1_g, ln1_b, w_up, w_down, ln2_g, ln2_b = weights
    t = x2d.shape[0]
    whole = lambda a: pl.BlockSpec(a.shape, lambda c: (0,) * a.ndim)
    return pl.pallas_call(
        _out_ffn_body,
        grid=(D_FF // D_MODEL,),
        in_specs=[whole(a_ret), whole(a_moba), whole(x2d)] + [_mod_spec(term) for term in (2, 3, 4, 5)]
        + [whole(w_o), whole(ln1_g), whole(ln1_b),
           pl.BlockSpec((D_MODEL, D_MODEL), lambda c: (0, c)), pl.BlockSpec((D_MODEL, D_MODEL), lambda c: (c, 0)),
           whole(ln2_g), whole(ln2_b)],
        out_specs=pl.BlockSpec((t, D_MODEL), lambda c: (0, 0)),
        out_shape=jax.ShapeDtypeStruct((t, D_MODEL), F32),
        scratch_shapes=[pltpu.VMEM((t, D_MODEL), F32), pltpu.VMEM((t, D_MODEL), BF16), pltpu.VMEM((t, D_MODEL), F32)],
        compiler_params=pltpu.CompilerParams(dimension_semantics=("arbitrary",),
                                             vmem_limit_bytes=VMEM_LIMIT),
        name="out_ffn",
    )(a_ret, a_moba, x2d, *([mod] * 4), *weights)


def _out_ffn_moba_body(pt_ref, ar_ref, am_ref, x_ref, ga_ref, shf_ref, scf_ref, gf_ref, wo_ref, g1_ref, b1_ref,
                       wu_ref, wd_ref, g2_ref, b2_ref, q_ref, kn_ref, vn_ref, kc_ref, vc_ref,
                       y_ref, o_ref, ring, sem, s_scr):
    n_seqs = pl.num_programs(0) * pl.num_programs(1)
    seq = pl.program_id(0) * pl.num_programs(1) + pl.program_id(1)
    sm = _SampleMoba(pt_ref, seq, n_seqs, q_ref, kn_ref, vn_ref, kc_ref, vc_ref, o_ref, ring, sem, s_scr)

    def up(c, h):
        u = jnp.maximum(jnp.dot(h, wu_ref[:, c * D_MODEL:(c + 1) * D_MODEL], preferred_element_type=F32), 0.0)
        return (u * u).astype(BF16)

    def down(c, u):
        return jnp.dot(u, wd_ref[c * D_MODEL:(c + 1) * D_MODEL, :], preferred_element_type=F32)

    half = CHUNK_PAGES // 2
    sm.prologue()

    sm.wait_k(0)
    sm.k_pages(0, 0, half)
    mixed = jnp.dot(ar_ref[...].astype(BF16), wo_ref[:RET_WIDTH, :], preferred_element_type=F32)
    sm.k_pages(0, half, CHUNK_PAGES)
    mixed = mixed + jnp.dot(am_ref[...].astype(BF16), wo_ref[RET_WIDTH:, :], preferred_element_type=F32)
    x1 = _layer_norm(ALPHA * x_ref[...] + _modulation(ga_ref, False) * mixed, g1_ref[...], b1_ref[...])
    h = (x1 * (1.0 + _modulation(scf_ref, False)) + _modulation(shf_ref, False)).astype(BF16)
    sm.refill_after_k(0)

    sm.wait_k(1)
    sm.k_pages(1, 0, half)
    u = up(0, h)
    sm.k_pages(1, half, CHUNK_PAGES)
    sm.refill_after_k(1)

    sm.wait_k(2)
    sm.k_pages(2, 0, half)
    acc = down(0, u)
    sm.k_pages(2, half, CHUNK_PAGES)
    sm.refill_after_k(2)

    sm.wait_k(3)
    sm.k_pages(3, 0, half)
    u = up(1, h)
    sm.k_pages(3, half, CHUNK_PAGES)
    sm.refill_after_k(3)

    l, acc_s = sm.select()
    acc = acc + down(1, u)

    sm.wait_v(0)
    acc_s = sm.v_pages(0, 0, half, acc_s)
    u = up(2, h)
    acc_s = sm.v_pages(0, half, CHUNK_PAGES, acc_s)
    sm.refill_after_v(0)

    sm.wait_v(1)
    acc_s = sm.v_pages(1, 0, half, acc_s)
    acc = acc + down(2, u)
    acc_s = sm.v_pages(1, half, CHUNK_PAGES, acc_s)
    sm.refill_after_v(1)

    sm.wait_v(2)
    acc_s = sm.v_pages(2, 0, half, acc_s)
    u = up(3, h)
    acc_s = sm.v_pages(2, half, CHUNK_PAGES, acc_s)
    sm.refill_after_v(2)

    sm.wait_v(3)
    acc_s = sm.v_pages(3, 0, half, acc_s)
    acc = acc + down(3, u)
    acc_s = sm.v_pages(3, half, CHUNK_PAGES, acc_s)
    sm.refill_after_v(3)
    y_ref[...] = _layer_norm(ALPHA * x1 + _modulation(gf_ref, False) * acc, g2_ref[...], b2_ref[...])
    sm.finish(l, acc_s)


def _out_ffn_moba(a_ret, a_moba, x2d, mod, weights, page_table, mq_s, kn2d, vn2d, cache_k2d, cache_v2d, tm):
    t = x2d.shape[0]
    nt = SEQ // tm
    assert BATCH * nt == DEC_BATCH
    n_rows = MOBA_HEADS * DEC_SEQ
    tile = lambda b, i, pt: (b * nt + i, 0)
    wide = lambda w: pl.BlockSpec((tm, w), tile)
    const = lambda a: pl.BlockSpec(a.shape, lambda b, i, pt: (0,) * a.ndim, pipeline_mode=pl.Buffered(1))
    seq_rows = lambda w, n: pl.BlockSpec((n, w), tile)
    hbm = pl.BlockSpec(memory_space=pl.ANY)
    grid_spec = pltpu.PrefetchScalarGridSpec(
        num_scalar_prefetch=1,
        grid=(BATCH, nt),
        in_specs=[wide(RET_WIDTH), wide(MOBA_WIDTH), wide(D_MODEL)] + [_mod_spec(term) for term in (2, 3, 4, 5)]
        + [const(a) for a in weights]
        + [seq_rows(MOBA_WIDTH, DEC_SEQ), seq_rows(LANES, n_rows), seq_rows(LANES, n_rows), hbm, hbm],
        out_specs=[wide(D_MODEL), seq_rows(MOBA_WIDTH, DEC_SEQ)],
        scratch_shapes=[pltpu.VMEM((RING_PAGES, PAGE_ROWS, LANES), F32),
                        pltpu.SemaphoreType.DMA((RING_PAGES // CHUNK_PAGES,)),
                        pltpu.VMEM((PAGES_PER_SEQ, n_rows, PAGE_ROWS), F32)],
    )
    return pl.pallas_call(
        _out_ffn_moba_body,
        grid_spec=grid_spec,
        out_shape=[jax.ShapeDtypeStruct((t, D_MODEL), F32),
                   jax.ShapeDtypeStruct((N_SAMPLE_TOK, MOBA_WIDTH), F32)],
        compiler_params=pltpu.CompilerParams(dimension_semantics=("arbitrary", "arbitrary"),
                                             vmem_limit_bytes=VMEM_LIMIT_FUSED),
        name="out_ffn_moba",
    )(page_table.reshape(-1), a_ret, a_moba, x2d, *([mod] * 4), *weights,
      mq_s, kn2d, vn2d, cache_k2d, cache_v2d)


def kernel(x_prompt, x_sample, cache_k, cache_v, state_ret, page_table, c_prompt, c_sample,
           w_ada, b_ada, w_in, w_o, ln1_g, ln1_b, w_up, w_down, ln2_g, ln2_b):
    n_prompt_tok = BATCH * SEQ
    past_len = page_table.shape[1] * PAGE_SIZE

    mod = _adaln(c_prompt, c_sample, w_ada[0], b_ada)

    tm = 512
    nt = SEQ // tm
    p_row = lambda b, i: (b * nt + i, 0)
    p_tab = pl.BlockSpec((tm, LANES), lambda b, i: (i, 0))
    s_row = lambda i: (0, 0)
    s_tab = pl.BlockSpec((N_SAMPLE_TOK, LANES), s_row)

    pos_p = np.arange(SEQ, dtype=np.int32)
    pos_s = np.tile(past_len + np.arange(DEC_SEQ, dtype=np.int32), DEC_BATCH)
    tabs_p = _rope_tables(pos_p, MOBA_HEAD_DIM) + _rope_tables(pos_p, RET_DK)
    tabs_s = _rope_tables(pos_s, MOBA_HEAD_DIM) + _rope_tables(pos_s, RET_DK)

    xp = x_prompt.reshape(n_prompt_tok, D_MODEL)
    rq, rk, rv, rg, mq, k_p, v_p = _inproj(
        xp, mod, False, [p_tab] * 4, tabs_p, w_in[0],
        (BATCH, nt), p_row, tm, BF16)
    a_ret, state_p = _ret_prompt(rq, rk, rv, rg)
    a_moba, w_o_b, w_up_b, w_down_b = _moba_prompt(mq, k_p, v_p, casts=(w_o[0], w_up[0], w_down[0]))
    weights = (w_o_b, ln1_g, ln1_b, w_up_b, w_down_b, ln2_g, ln2_b)

    xs = x_sample.reshape(N_SAMPLE_TOK, D_MODEL)
    rq_s, rk_s, rv_s, rg_s, mq_s, k_s, v_s = _inproj(
        xs, mod, True, [s_tab] * 4, tabs_s, w_in[0],
        (1,), s_row, N_SAMPLE_TOK, F32)
    a_ret_s, state_s = _ret_sample(rq_s, rk_s, rv_s, rg_s, state_ret[0])

    cache_rows = cache_k.shape[1] * PAGE_ROWS
    y_p, a_moba_s = _out_ffn_moba(a_ret, a_moba, xp, mod, weights, page_table, mq_s, k_s, v_s,
                                  cache_k.reshape(cache_rows, LANES), cache_v.reshape(cache_rows, LANES), tm)
    y_s = _out_ffn(a_ret_s, a_moba_s, xs, mod, weights)

    kv_p_shape = (DEPTH, BATCH, SEQ, MOBA_HEADS, MOBA_HEAD_DIM)
    kv_s_shape = (DEPTH, DEC_BATCH, DEC_SEQ, MOBA_HEADS, MOBA_HEAD_DIM)
    return (y_p.reshape(BATCH, SEQ, D_MODEL),
            y_s.reshape(DEC_BATCH, DEC_SEQ, D_MODEL),
            k_p.reshape(kv_p_shape), v_p.reshape(kv_p_shape), state_p[None],
            k_s.reshape(kv_s_shape), v_s.reshape(kv_s_shape), state_s[None])
```
